```python
import math
import numpy as np
import jax
import jax.numpy as jnp
from jax import lax


D_MODEL = 1024
BATCH = 8
SEQ = 2048
DEPTH = 4

D_FF = 2752
HEAD_DIM = 64
HEAD_SLOTS = 16
NUM_BUCKETS = 32
T5_MAX_EXACT = 16
T5_MAX_DIST = 128
Q_BLOCK = 128
GATHER_Q_BLOCK = 32
RMS_EPS = 1e-6
NEG_INF = -1e30
MLA_HEADS = 8
MLA_NOPE = 64
MLA_ROPE = 32
MLA_V = 64
MLA_Q_LORA = 256
MLA_KV_LORA = 128
ROPE_THETA = 10000.0
NSA_HEADS = 8
NSA_GROUPS = 2
NSA_HPG = NSA_HEADS // NSA_GROUPS
NSA_CMP_LEN = 32
NSA_CMP_STRIDE = 16
NSA_CMP_HID = 256
NSA_SLC_BLOCK = 64
NSA_SLC_TOP = 8
NSA_WINDOW = 512
NSA_FORCED = 1e6
DIL_PAIRS = ((128, 1), (512, 4), (2048, 16))
DIL_HPG = 4
DIL_SLOTS = len(DIL_PAIRS) * DIL_HPG
MOBA_HEADS = 4
MOBA_BLOCK = 256
MOBA_TOP = 3
EV_SPLITS = (MLA_Q_LORA, MLA_KV_LORA, MLA_ROPE, NSA_HEADS * HEAD_DIM) + (NSA_GROUPS * HEAD_DIM,) * 6 + (3 * NSA_HEADS,)
EV_COLS = sum(EV_SPLITS)
EV_OUT = MLA_HEADS * MLA_V + NSA_HEADS * HEAD_DIM
OD_SPLITS = (DIL_SLOTS * HEAD_DIM,) * 3 + (MOBA_HEADS * HEAD_DIM,) * 3
OD_COLS = sum(OD_SPLITS)
OD_OUT = DIL_HPG * HEAD_DIM + MOBA_HEADS * HEAD_DIM
N_EVEN = (DEPTH + 1) // 2
N_ODD = DEPTH // 2

kernel_name = 'hybrid_mla_nsa_dilated_moba_macaron'


def split_cols(u, sizes):
    return jnp.split(u, np.cumsum(sizes)[:-1].tolist(), axis=-1)


def rms_norm(x, g):
    x32 = x.astype(jnp.float32)
    y = x32 * lax.rsqrt(jnp.mean(x32 * x32, axis=-1, keepdims=True) + RMS_EPS)
    return (y * g.astype(jnp.float32)).astype(x.dtype)


def masked_softmax(logits, mask):
    l = jnp.where(mask, logits, NEG_INF)
    m = jnp.max(l, axis=-1, keepdims=True)
    e = jnp.where(mask, jnp.exp(l - m), 0.0)
    den = jnp.maximum(jnp.sum(e, axis=-1, keepdims=True), 1e-30)
    return e / den, (m + jnp.log(den))[..., 0]


def t5_bucket(dist):
    n = jnp.maximum(jnp.asarray(dist, jnp.int32), 0)
    nf = jnp.maximum(n, 1).astype(jnp.float32)
    large = T5_MAX_EXACT + (jnp.log(nf / T5_MAX_EXACT) / math.log(T5_MAX_DIST / T5_MAX_EXACT)
                            * (NUM_BUCKETS - T5_MAX_EXACT)).astype(jnp.int32)
    return jnp.where(n < T5_MAX_EXACT, n, jnp.minimum(large, NUM_BUCKETS - 1))


def from_blocks(a):
    n, b, qb = a.shape[:3]
    return jnp.moveaxis(a, 0, 1).reshape(b, n * qb, *a.shape[3:])


def rope_tables(s):
    inv = ROPE_THETA ** (-jnp.arange(0, MLA_ROPE, 2, dtype=jnp.float32) / MLA_ROPE)
    ang = jnp.arange(s, dtype=jnp.float32)[:, None] * inv[None, :]
    return jnp.cos(ang), jnp.sin(ang)


def apply_rope(x, cos, sin):
    x32 = x.astype(jnp.float32)
    half = x.shape[-1] // 2
    x1, x2 = x32[..., :half], x32[..., half:]
    return jnp.concatenate([x1 * cos - x2 * sin, x2 * cos + x1 * sin], axis=-1).astype(x.dtype)


def swiglu(y, w_in, w_out):
    a, b = jnp.split(y @ w_in, 2, axis=-1)
    return (jax.nn.silu(a) * b) @ w_out


def modulate(x, g, shift, scale):
    return rms_norm(x, g) * (1.0 + scale[:, None, :]) + shift[:, None, :]


def causal_dense(q, k, v):
    b, s = q.shape[:2]
    scale = q.shape[-1] ** -0.5
    kpos = jnp.arange(s)

    def block(blk):
        qs = blk * Q_BLOCK
        qb = lax.dynamic_slice_in_dim(q, qs, Q_BLOCK, axis=1)
        qpos = qs + jnp.arange(Q_BLOCK)
        logits = jnp.einsum('bqhd,bkhd->bhqk', qb, k).astype(jnp.float32) * scale
        p, _ = masked_softmax(logits, kpos[None, :] <= qpos[:, None])
        return jnp.einsum('bhqk,bkhd->bqhd', p.astype(v.dtype), v)

    return from_blocks(lax.map(block, jnp.arange(s // Q_BLOCK)))


def band_attn(q, k, v, bias_tab, w, r):
    b, s, g, p_, dh = q.shape
    scale = dh ** -0.5
    nk = w // r + 1
    dist = r * np.arange(nk)
    bias = jnp.transpose(bias_tab[t5_bucket(dist)], (1, 2, 0))[:, :, None, :]
    kp = jnp.pad(k, ((0, 0), (w, 0), (0, 0), (0, 0)))
    vp = jnp.pad(v, ((0, 0), (w, 0), (0, 0), (0, 0)))
    gidx = np.arange(Q_BLOCK)[:, None] + w - dist[None, :]
    dist_j = jnp.asarray(dist)

    def block(blk):
        qs = blk * Q_BLOCK
        qb = lax.dynamic_slice_in_dim(q, qs, Q_BLOCK, axis=1)
        kg = lax.dynamic_slice_in_dim(kp, qs, w + Q_BLOCK, axis=1)[:, gidx]
        vg = lax.dynamic_slice_in_dim(vp, qs, w + Q_BLOCK, axis=1)[:, gidx]
        ok = (qs + jnp.arange(Q_BLOCK)[:, None] - dist_j[None, :]) >= 0
        logits = jnp.einsum('bqgpd,bqkgd->bgpqk', qb, kg).astype(jnp.float32) * scale + bias
        pr, lse = masked_softmax(logits, ok)
        out = jnp.einsum('bgpqk,bqkgd->bqgpd', pr.astype(vg.dtype), vg)
        return out, lse

    outs, lses = lax.map(block, jnp.arange(s // Q_BLOCK))
    lse = jnp.transpose(lses, (1, 0, 4, 2, 3)).reshape(b, s, g, p_)
    return from_blocks(outs), lse


def mla_mixer(cq, ckv, kr, q_norm_g, kv_norm_g, w_uq, w_ukv, qk_g):
    b, s = cq.shape[:2]
    q = (rms_norm(cq, q_norm_g) @ w_uq).reshape(b, s, MLA_HEADS, MLA_NOPE + MLA_ROPE)
    kv = (rms_norm(ckv, kv_norm_g) @ w_ukv).reshape(b, s, MLA_HEADS, MLA_NOPE + MLA_V)
    cos, sin = rope_tables(s)
    q_nope = rms_norm(q[..., :MLA_NOPE], qk_g[0, :MLA_NOPE])
    q_rope = apply_rope(rms_norm(q[..., MLA_NOPE:], qk_g[0, MLA_NOPE:]), cos[:, None, :], sin[:, None, :])
    k_nope = rms_norm(kv[..., :MLA_NOPE], qk_g[1, :MLA_NOPE])
    k_rope = apply_rope(rms_norm(kr, qk_g[1, MLA_NOPE:]), cos, sin)
    qf = jnp.concatenate([q_nope, q_rope], axis=-1)
    kf = jnp.concatenate([k_nope, jnp.broadcast_to(k_rope[:, :, None, :], (b, s, MLA_HEADS, MLA_ROPE))], axis=-1)
    out = causal_dense(qf, kf, kv[..., MLA_NOPE:])
    return out.reshape(b, s, MLA_HEADS * MLA_V)


def nsa_mixer(q, kc, vc, ks, vs, kw, vw, gate_logits, cmp_pe, cmp_w1, cmp_w2, qk_g, bias_tab):
    b, s = q.shape[:2]
    g, p_, dh = NSA_GROUPS, NSA_HPG, HEAD_DIM
    scale = dh ** -0.5
    q = rms_norm(q.reshape(b, s, g, p_, dh), qk_g[0])
    kc, vc, ks, vs, kw, vw = [a.reshape(b, s, g, dh) for a in (kc, vc, ks, vs, kw, vw)]
    ks = rms_norm(ks, qk_g[1])
    kw = rms_norm(kw, qk_g[1])
    t = jnp.arange(s)

    n_cmp = (s - NSA_CMP_LEN) // NSA_CMP_STRIDE + 1
    cstart = np.arange(n_cmp) * NSA_CMP_STRIDE
    cidx = cstart[:, None] + np.arange(NSA_CMP_LEN)[None, :]

    def compress(a, j):
        blk = a[:, cidx] + cmp_pe[j][None, None, :, None, :]
        flat = jnp.moveaxis(blk, 3, 2).reshape(b, n_cmp, g, NSA_CMP_LEN * dh)
        return jax.nn.silu(flat @ cmp_w1[j]) @ cmp_w2[j]

    k_cmp = rms_norm(compress(kc, 0), qk_g[1])
    v_cmp = compress(vc, 1)
    cend = jnp.asarray(cstart + NSA_CMP_LEN - 1)
    mask_c = cend[None, :] <= t[:, None]
    bias_c = jnp.transpose(bias_tab[t5_bucket(t[:, None] - cend[None, :])], (2, 3, 0, 1))
    logits_c = jnp.einsum('bsgpd,bngd->bgpsn', q, k_cmp).astype(jnp.float32) * scale + bias_c
    p_c, _ = masked_softmax(logits_c, mask_c)
    o_c = jnp.einsum('bgpsn,bngd->bsgpd', p_c.astype(v_cmp.dtype), v_cmp)

    n_slc = s // NSA_SLC_BLOCK
    sstart = np.arange(n_slc) * NSA_SLC_BLOCK
    overlap = np.clip(np.minimum(cstart[:, None] + NSA_CMP_LEN, sstart[None, :] + NSA_SLC_BLOCK)
                      - np.maximum(cstart[:, None], sstart[None, :]), 0, None).astype(np.float32) / NSA_CMP_LEN
    imp = jnp.einsum('bgpsn,nm->bgsm', p_c, jnp.asarray(overlap))
    cur = t // NSA_SLC_BLOCK
    ids = jnp.arange(n_slc)[None, :]
    forced = (ids == 0) | (ids == cur[:, None]) | (ids == cur[:, None] - 1)
    score = jnp.where(forced, NSA_FORCED, jnp.where(ids <= cur[:, None], imp, NEG_INF))
    n_sel = min(NSA_SLC_TOP, n_slc)
    top_s, top_i = lax.top_k(score, n_sel)
    sel_ok = top_s > 0.5 * NEG_INF
    k_blk = jnp.transpose(ks.reshape(b, n_slc, NSA_SLC_BLOCK, g, dh), (0, 3, 1, 2, 4))
    v_blk = jnp.transpose(vs.reshape(b, n_slc, NSA_SLC_BLOCK, g, dh), (0, 3, 1, 2, 4))
    bi = jnp.arange(b)[:, None, None, None]
    gi = jnp.arange(g)[None, :, None, None]
    qbn = GATHER_Q_BLOCK

    def sel_block(blk):
        qs = blk * qbn
        qb = lax.dynamic_slice_in_dim(q, qs, qbn, axis=1)
        ib = lax.dynamic_slice_in_dim(top_i, qs, qbn, axis=2)
        okb = lax.dynamic_slice_in_dim(sel_ok, qs, qbn, axis=2)
        kg = k_blk[bi, gi, ib].reshape(b, g, qbn, n_sel * NSA_SLC_BLOCK, dh)
        vg = v_blk[bi, gi, ib].reshape(b, g, qbn, n_sel * NSA_SLC_BLOCK, dh)
        pos = (ib[..., None] * NSA_SLC_BLOCK + jnp.arange(NSA_SLC_BLOCK)).reshape(b, g, qbn, -1)
        dist = (qs + jnp.arange(qbn))[None, None, :, None] - pos
        mask = (jnp.repeat(okb, NSA_SLC_BLOCK, axis=-1) & (dist >= 0))[:, :, None]
        bias = jnp.moveaxis(bias_tab[t5_bucket(dist), gi], -1, 2)
        logits = jnp.einsum('bqgpd,bgqkd->bgpqk', qb, kg).astype(jnp.float32) * scale + bias
        pr, _ = masked_softmax(logits, mask)
        return jnp.einsum('bgpqk,bgqkd->bqgpd', pr.astype(vg.dtype), vg)

    o_s = from_blocks(lax.map(sel_block, jnp.arange(s // qbn)))

    o_w, _ = band_attn(q, kw, vw, bias_tab, NSA_WINDOW - 1, 1)

    gt = jax.nn.sigmoid(gate_logits.astype(jnp.float32)).reshape(b, s, g, p_, 3).astype(q.dtype)
    out = gt[..., 0:1] * o_c + gt[..., 1:2] * o_s + gt[..., 2:3] * o_w
    return out.reshape(b, s, NSA_HEADS * dh)


def dilated_mixer(q, k, v, qk_g, t5_bias):
    b, s = q.shape[:2]
    ng = len(DIL_PAIRS)
    q = rms_norm(q.reshape(b, s, ng, DIL_HPG, HEAD_DIM), qk_g[0])
    k = rms_norm(k.reshape(b, s, ng, DIL_HPG, HEAD_DIM), qk_g[1])
    v = v.reshape(b, s, ng, DIL_HPG, HEAD_DIM)
    outs, lses = [], []
    for gidx, (w, r) in enumerate(DIL_PAIRS):
        tab = t5_bias[:, gidx * DIL_HPG:(gidx + 1) * DIL_HPG][:, :, None]
        o, lse = band_attn(q[:, :, gidx, :, None], k[:, :, gidx], v[:, :, gidx], tab, w, r)
        outs.append(o[:, :, :, 0])
        lses.append(lse[..., 0])
    wts = jax.nn.softmax(jnp.stack(lses), axis=0)
    out = jnp.sum(wts[..., None].astype(q.dtype) * jnp.stack(outs), axis=0)
    return out.reshape(b, s, DIL_HPG * HEAD_DIM)


def moba_mixer(q, k, v, qk_g, bias_tab):
    b, s = q.shape[:2]
    h, dh = MOBA_HEADS, HEAD_DIM
    scale = dh ** -0.5
    q = rms_norm(q.reshape(b, s, h, dh), qk_g[0])
    k = rms_norm(k.reshape(b, s, h, dh), qk_g[1])
    v = v.reshape(b, s, h, dh)
    nb = -(-s // MOBA_BLOCK)
    pad = nb * MOBA_BLOCK - s
    kp = jnp.pad(k, ((0, 0), (0, pad), (0, 0), (0, 0)))
    vp = jnp.pad(v, ((0, 0), (0, pad), (0, 0), (0, 0)))
    k_blk = jnp.transpose(kp.reshape(b, nb, MOBA_BLOCK, h, dh), (0, 3, 1, 2, 4))
    v_blk = jnp.transpose(vp.reshape(b, nb, MOBA_BLOCK, h, dh), (0, 3, 1, 2, 4))
    t = jnp.arange(s)
    own = t // MOBA_BLOCK
    n_top = min(MOBA_TOP, nb - 1)
    if n_top > 0:
        k_mean = jnp.mean(k_blk.astype(jnp.float32), axis=3)
        gate = jnp.einsum('bshd,bhnd->bhsn', q.astype(jnp.float32), k_mean)
        past = jnp.arange(nb)[None, :] < own[:, None]
        top_s, top_i = lax.top_k(jnp.where(past, gate, NEG_INF), n_top)
        sel_ok = top_s > 0.5 * NEG_INF
    bi = jnp.arange(b)[:, None, None, None]
    hi = jnp.arange(h)[None, :, None, None]
    qbn = GATHER_Q_BLOCK

    def block(blk):
        qs = blk * qbn
        qb = lax.dynamic_slice_in_dim(q, qs, qbn, axis=1)
        qpos = qs + jnp.arange(qbn)
        ob = qs // MOBA_BLOCK
        ko = lax.dynamic_slice_in_dim(kp, ob * MOBA_BLOCK, MOBA_BLOCK, axis=1)
        vo = lax.dynamic_slice_in_dim(vp, ob * MOBA_BLOCK, MOBA_BLOCK, axis=1)
        d_own = qpos[:, None] - (ob * MOBA_BLOCK + jnp.arange(MOBA_BLOCK))[None, :]
        logits_own = (jnp.einsum('bqhd,bkhd->bhqk', qb, ko).astype(jnp.float32) * scale
                      + jnp.transpose(bias_tab[t5_bucket(d_own)], (2, 0, 1)))
        mask_own = jnp.broadcast_to(d_own >= 0, logits_own.shape)
        if n_top == 0:
            pr, _ = masked_softmax(logits_own, mask_own)
            return jnp.einsum('bhqk,bkhd->bqhd', pr.astype(vo.dtype), vo)
        ib = lax.dynamic_slice_in_dim(top_i, qs, qbn, axis=2)
        okb = lax.dynamic_slice_in_dim(sel_ok, qs, qbn, axis=2)
        nsel = n_top * MOBA_BLOCK
        kg = k_blk[bi, hi, ib].reshape(b, h, qbn, nsel, dh)
        vg = v_blk[bi, hi, ib].reshape(b, h, qbn, nsel, dh)
        pos = (ib[..., None] * MOBA_BLOCK + jnp.arange(MOBA_BLOCK)).reshape(b, h, qbn, nsel)
        logits_sel = (jnp.einsum('bqhd,bhqkd->bhqk', qb, kg).astype(jnp.float32) * scale
                      + bias_tab[t5_bucket(qpos[None, None, :, None] - pos), hi])
        mask_sel = jnp.repeat(okb, MOBA_BLOCK, axis=-1)
        pr, _ = masked_softmax(jnp.concatenate([logits_sel, logits_own], axis=-1),
                               jnp.concatenate([mask_sel, mask_own], axis=-1))
        pr = pr.astype(v.dtype)
        return (jnp.einsum('bhqk,bhqkd->bqhd', pr[..., :nsel], vg)
                + jnp.einsum('bhqk,bkhd->bqhd', pr[..., nsel:], vo))

    return from_blocks(lax.map(block, jnp.arange(s // qbn))).reshape(b, s, h * dh)


def even_mixer(y, w_in, w_out, q_norm_g, kv_norm_g, w_uq, w_ukv, mla_qk_g, cmp_pe, cmp_w1, cmp_w2, nsa_qk_g, t5_bias):
    cq, ckv, kr, qn, kc, vc, ks, vs, kw, vw, gl = split_cols(y @ w_in, EV_SPLITS)
    o_a = mla_mixer(cq, ckv, kr, q_norm_g, kv_norm_g, w_uq, w_ukv, mla_qk_g)
    tab = t5_bias[:, MLA_HEADS:MLA_HEADS + NSA_HEADS].reshape(NUM_BUCKETS, NSA_GROUPS, NSA_HPG)
    o_b = nsa_mixer(qn, kc, vc, ks, vs, kw, vw, gl, cmp_pe, cmp_w1, cmp_w2, nsa_qk_g, tab)
    return jnp.concatenate([o_a, o_b], axis=-1) @ w_out


def odd_mixer(y, w_in, w_out, dil_qk_g, moba_qk_g, t5_bias):
    qd, kd, vd, qm, km, vm = split_cols(y @ w_in, OD_SPLITS)
    o_c = dilated_mixer(qd, kd, vd, dil_qk_g, t5_bias)
    o_d = moba_mixer(qm, km, vm, moba_qk_g, t5_bias[:, DIL_SLOTS:DIL_SLOTS + MOBA_HEADS])
    return jnp.concatenate([o_c, o_d], axis=-1) @ w_out


def setup_inputs(seed: int = 0) -> dict:
    key = jax.random.key(seed)
    keys = iter(jax.random.split(key, 32))

    def nrm(shape, scale):
        return jax.random.normal(next(keys), shape, jnp.float32) * scale

    return {
        'x': nrm((BATCH, SEQ, D_MODEL), 1.0),
        'c': nrm((BATCH, D_MODEL), 1.0),
        't5_bias': nrm((NUM_BUCKETS, HEAD_SLOTS), 0.5),
        'ada_w': nrm((DEPTH, D_MODEL, 9 * D_MODEL), D_MODEL ** -0.5),
        'ada_b': nrm((DEPTH, 9 * D_MODEL), 0.02),
        'norm_g': 1.0 + nrm((DEPTH, 3, D_MODEL), 0.05),
        'ffn_w_in': nrm((DEPTH, 2, D_MODEL, 2 * D_FF), D_MODEL ** -0.5),
        'ffn_w_out': nrm((DEPTH, 2, D_FF, D_MODEL), D_FF ** -0.5),
        'ev_w_in': nrm((N_EVEN, D_MODEL, EV_COLS), D_MODEL ** -0.5),
        'ev_w_out': nrm((N_EVEN, EV_OUT, D_MODEL), EV_OUT ** -0.5),
        'mla_q_norm_g': 1.0 + nrm((N_EVEN, MLA_Q_LORA), 0.05),
        'mla_kv_norm_g': 1.0 + nrm((N_EVEN, MLA_KV_LORA), 0.05),
        'mla_w_uq': nrm((N_EVEN, MLA_Q_LORA, MLA_HEADS * (MLA_NOPE + MLA_ROPE)), MLA_Q_LORA ** -0.5),
        'mla_w_ukv': nrm((N_EVEN, MLA_KV_LORA, MLA_HEADS * (MLA_NOPE + MLA_V)), MLA_KV_LORA ** -0.5),
        'mla_qk_g': 1.0 + nrm((N_EVEN, 2, MLA_NOPE + MLA_ROPE), 0.05),
        'nsa_cmp_pe': nrm((N_EVEN, 2, NSA_CMP_LEN, HEAD_DIM), 0.1),
        'nsa_cmp_w1': nrm((N_EVEN, 2, NSA_CMP_LEN * HEAD_DIM, NSA_CMP_HID), (NSA_CMP_LEN * HEAD_DIM) ** -0.5),
        'nsa_cmp_w2': nrm((N_EVEN, 2, NSA_CMP_HID, HEAD_DIM), NSA_CMP_HID ** -0.5),
        'nsa_qk_g': 1.0 + nrm((N_EVEN, 2, HEAD_DIM), 0.05),
        'od_w_in': nrm((N_ODD, D_MODEL, OD_COLS), D_MODEL ** -0.5),
        'od_w_out': nrm((N_ODD, OD_OUT, D_MODEL), OD_OUT ** -0.5),
        'dil_qk_g': 1.0 + nrm((N_ODD, 2, HEAD_DIM), 0.05),
        'moba_qk_g': 1.0 + nrm((N_ODD, 2, HEAD_DIM), 0.05),
    }


def reference(x, c, t5_bias, ada_w, ada_b, norm_g, ffn_w_in, ffn_w_out, ev_w_in, ev_w_out,
              mla_q_norm_g, mla_kv_norm_g, mla_w_uq, mla_w_ukv, mla_qk_g, nsa_cmp_pe, nsa_cmp_w1,
              nsa_cmp_w2, nsa_qk_g, od_w_in, od_w_out, dil_qk_g, moba_qk_g):
    b, d = x.shape[0], x.shape[-1]
    c_act = jax.nn.silu(c)
    for i in range(DEPTH):
        mod = (c_act @ ada_w[i] + ada_b[i]).reshape(b, 3, 3, d)
        y = modulate(x, norm_g[i, 0], mod[:, 0, 0], mod[:, 0, 1])
        x = x + 0.5 * mod[:, 0, 2][:, None, :] * swiglu(y, ffn_w_in[i, 0], ffn_w_out[i, 0])
        y = modulate(x, norm_g[i, 1], mod[:, 1, 0], mod[:, 1, 1])
        j = i // 2
        if i % 2 == 0:
            m = even_mixer(y, ev_w_in[j], ev_w_out[j], mla_q_norm_g[j], mla_kv_norm_g[j], mla_w_uq[j],
                           mla_w_ukv[j], mla_qk_g[j], nsa_cmp_pe[j], nsa_cmp_w1[j], nsa_cmp_w2[j],
                           nsa_qk_g[j], t5_bias)
        else:
            m = odd_mixer(y, od_w_in[j], od_w_out[j], dil_qk_g[j], moba_qk_g[j], t5_bias)
        x = x + mod[:, 1, 2][:, None, :] * m
        y = modulate(x, norm_g[i, 2], mod[:, 2, 0], mod[:, 2, 1])
        x = x + 0.5 * mod[:, 2, 2][:, None, :] * swiglu(y, ffn_w_in[i, 1], ffn_w_out[i, 1])
    return x
```

```python
import functools
import math

import numpy as np
import jax
import jax.numpy as jnp
from jax import lax
from jax.experimental import pallas as pl
from jax.experimental.pallas import tpu as pltpu

F32 = jnp.float32
BF16 = jnp.bfloat16

D_MODEL = 1024
DEPTH = 4
D_FF = 2752
HEAD_DIM = 64
NUM_BUCKETS = 32
T5_MAX_EXACT = 16
T5_MAX_DIST = 128
RMS_EPS = 1e-6
NEG_INF = -1e30
MLA_HEADS = 8
MLA_NOPE = 64
MLA_ROPE = 32
MLA_V = 64
MLA_Q_LORA = 256
MLA_KV_LORA = 128
ROPE_THETA = 10000.0
NSA_HEADS = 8
NSA_GROUPS = 2
NSA_HPG = 4
NSA_CMP_LEN = 32
NSA_CMP_STRIDE = 16
NSA_CMP_HID = 256
NSA_SLC_BLOCK = 64
NSA_SLC_TOP = 8
NSA_WINDOW = 512
NSA_FORCED = 1e6
DIL_PAIRS = ((128, 1), (512, 4), (2048, 16))
DIL_HPG = 4
DIL_SLOTS = len(DIL_PAIRS) * DIL_HPG
MOBA_HEADS = 4
MOBA_BLOCK = 256
MOBA_TOP = 3

LANES = 128
TM = 512
TQ = 256
TK = 256
FF_PAD = 2816
FF_CHUNK = 256
VMEM_LIMIT = 56 * 1024 * 1024


def _dot(a, b):
    return jnp.dot(a, b, preferred_element_type=F32)


def _dot_nt(a, b):
    return lax.dot_general(a, b, (((1,), (1,)), ((), ())), preferred_element_type=F32)


def _split(a):
    hi = a.astype(BF16)
    lo = (a - hi.astype(F32)).astype(BF16)
    return hi, lo


def _dot_hilo(a, b):
    hi, lo = _split(a)
    return _dot(hi, b) + _dot(lo, b)


def _sigmoid(x):
    return 1.0 / (1.0 + jnp.exp(-x))


def _modulated_norm(x, g, shift, scale):
    ms = jnp.mean(x * x, axis=-1, keepdims=True)
    y = x * lax.rsqrt(ms + RMS_EPS) * g
    return y * (1.0 + scale) + shift


def _params(*sem):
    return pltpu.CompilerParams(dimension_semantics=sem, vmem_limit_bytes=VMEM_LIMIT)


def _resident(shape):
    nd = len(shape)
    return pl.BlockSpec(shape, lambda *_: (0,) * nd, pipeline_mode=pl.Buffered(1))


def _ada_kernel(c_ref, w_ref, b_ref, o_ref):
    c = c_ref[...]
    ca = c * _sigmoid(c)
    o_ref[...] = jnp.dot(ca, w_ref[...], preferred_element_type=F32,
                         precision=lax.Precision.HIGHEST) + b_ref[...]


def _ada(c, ada_w, ada_b):
    depth, d, n = ada_w.shape
    b = c.shape[0]
    tn = 1152
    return pl.pallas_call(
        _ada_kernel,
        grid=(depth, n // tn),
        in_specs=[pl.BlockSpec((b, d), lambda l, j: (0, 0)),
                  pl.BlockSpec((None, d, tn), lambda l, j: (l, 0, j)),
                  pl.BlockSpec((None, 1, tn), lambda l, j: (l, 0, j))],
        out_specs=pl.BlockSpec((None, b, tn), lambda l, j: (l, 0, j)),
        out_shape=jax.ShapeDtypeStruct((depth, b, n), F32),
        compiler_params=_params("parallel", "parallel"),
        name="ada",
    )(c, ada_w, ada_b.reshape(depth, 1, n))


def _ffn_kernel(x_ref, g_ref, sh_ref, sc_ref, gt_ref, wa_ref, wb_ref, wo_ref, o_ref, y_ref, acc_ref):
    y_ref[...] = _modulated_norm(x_ref[...], g_ref[...], sh_ref[...], sc_ref[...]).astype(BF16)
    for c in range(FF_PAD // FF_CHUNK):
        sl = slice(c * FF_CHUNK, (c + 1) * FF_CHUNK)
        a = _dot(y_ref[...], wa_ref[:, sl])
        b = _dot(y_ref[...], wb_ref[:, sl])
        u = (a * _sigmoid(a) * b).astype(BF16)
        contrib = _dot(u, wo_ref[sl, :])
        if c == 0:
            acc_ref[...] = contrib
        else:
            acc_ref[...] += contrib
    o_ref[...] = x_ref[...] + 0.5 * gt_ref[...] * acc_ref[...]


def _mod_spec(tiles_per_batch, d):
    return pl.BlockSpec((None, 1, d), lambda i: (i // tiles_per_batch, 0, 0))


def _ffn(x, g, shift, scale, gate, wa, wb, wo, s):
    n, d = x.shape
    tpb = s // TM
    return pl.pallas_call(
        _ffn_kernel,
        grid=(n // TM,),
        in_specs=[pl.BlockSpec((TM, d), lambda i: (i, 0)),
                  _resident((1, d)),
                  _mod_spec(tpb, d), _mod_spec(tpb, d), _mod_spec(tpb, d),
                  _resident(wa.shape), _resident(wb.shape), _resident(wo.shape)],
        out_specs=pl.BlockSpec((TM, d), lambda i: (i, 0)),
        out_shape=jax.ShapeDtypeStruct((n, d), F32),
        scratch_shapes=[pltpu.VMEM((TM, d), BF16), pltpu.VMEM((TM, d), F32)],
        compiler_params=_params("parallel"),
        name="ffn",
    )(x, g, shift, scale, gate, wa, wb, wo)


def _proj_kernel(meta, n_out, x_ref, g_ref, sh_ref, sc_ref, w_ref, gain_ref, gm_ref, *rest):
    outs = rest[:n_out]
    y_ref = rest[n_out]
    y_ref[...] = _modulated_norm(x_ref[...], g_ref[...], sh_ref[...], sc_ref[...]).astype(BF16)
    for c, (oi, off, normed) in enumerate(meta):
        sl = slice(c * LANES, (c + 1) * LANES)
        z = _dot(y_ref[...], w_ref[:, sl])
        if normed:
            msq = _dot_hilo(z * z, gm_ref[...])
            z = z * lax.rsqrt(msq + RMS_EPS) * gain_ref[:, sl]
        outs[oi][:, off:off + LANES] = z.astype(outs[oi].dtype)


def _proj(x, g, shift, scale, w, gain, gm, meta, out_defs, s):
    n, d = x.shape
    tpb = s // TM
    return pl.pallas_call(
        functools.partial(_proj_kernel, meta, len(out_defs)),
        grid=(n // TM,),
        in_specs=[pl.BlockSpec((TM, d), lambda i: (i, 0)),
                  _resident((1, d)),
                  _mod_spec(tpb, d), _mod_spec(tpb, d),
                  _resident(w.shape), _resident(gain.shape), _resident(gm.shape)],
        out_specs=[pl.BlockSpec((TM, wd), lambda i: (i, 0)) for wd, _ in out_defs],
        out_shape=[jax.ShapeDtypeStruct((n, wd), dt) for wd, dt in out_defs],
        scratch_shapes=[pltpu.VMEM((TM, d), BF16)],
        compiler_params=_params("parallel"),
        name="proj",
    )(x, g, shift, scale, w, gain, gm)


def _mla_prep_kernel(in_ref, qg_ref, kvg_ref, wuq_ref, wuk_ref, wuv_ref, gm_ref, rot_ref,
                     gq_ref, gkn_ref, gkr_ref, cos_ref, sin_ref, q_out, k_out, v_out):
    def rms(z, g):
        return z * lax.rsqrt(jnp.mean(z * z, axis=-1, keepdims=True) + RMS_EPS) * g

    cqn = rms(in_ref[:, 0:MLA_Q_LORA], qg_ref[...]).astype(BF16)
    ckvn = rms(in_ref[:, MLA_Q_LORA:MLA_Q_LORA + MLA_KV_LORA], kvg_ref[...]).astype(BF16)
    c3 = in_ref[:, 3 * LANES:4 * LANES]
    cos = cos_ref[...]
    sin = sin_ref[...]
    gm = gm_ref[...]
    rot = rot_ref[...]

    def norm_rope(z, gain):
        msq = _dot_hilo(z * z, gm)
        z = z * lax.rsqrt(msq + RMS_EPS) * gain
        return z * cos + _dot_hilo(z, rot) * sin

    kr = norm_rope(c3, gkr_ref[...])
    for h in range(MLA_HEADS):
        sl = slice(h * LANES, (h + 1) * LANES)
        q_out[:, sl] = norm_rope(_dot(cqn, wuq_ref[:, sl]), gq_ref[...]).astype(BF16)
        k_out[:, sl] = (norm_rope(_dot(ckvn, wuk_ref[:, sl]), gkn_ref[...]) + kr).astype(BF16)
        v_out[:, sl] = _dot(ckvn, wuv_ref[:, sl]).astype(BF16)


def _mla_prep(mla_in, qg, kvg, wuq, wuk, wuv, gm, rot, gq, gkn, gkr, cos_t, sin_t, s):
    n = mla_in.shape[0]
    tpb = s // TM
    hw = MLA_HEADS * LANES
    tab = pl.BlockSpec((TM, LANES), lambda i: (i % tpb, 0))
    consts = [qg, kvg, wuq, wuk, wuv, gm, rot, gq, gkn, gkr]
    return pl.pallas_call(
        _mla_prep_kernel,
        grid=(n // TM,),
        in_specs=[pl.BlockSpec((TM, 4 * LANES), lambda i: (i, 0))] + [_resident(a.shape) for a in consts] + [tab, tab],
        out_specs=[pl.BlockSpec((TM, hw), lambda i: (i, 0))] * 3,
        out_shape=[jax.ShapeDtypeStruct((n, hw), BF16)] * 3,
        compiler_params=_params("parallel"),
        name="mla_prep",
    )(mla_in, *consts, cos_t, sin_t)


def _attn_kernel(cfg, q_ref, k_ref, v_ref, tab_ref, *rest):
    n_sub, n_tab, scale, backs, sel_cfg, want_lse = cfg
    if sel_cfg is not None:
        sel_ref, rest = rest[0], rest[1:]
        sel_bpt, sel_stride, sel_pair_mul = sel_cfg
    o_ref = rest[0]
    blk = pl.program_id(1)
    qi = pl.program_id(2)
    q = q_ref[...]
    lane = lax.broadcasted_iota(jnp.int32, (1, LANES), 1)
    if n_sub == 2:
        zero = jnp.zeros_like(q)
        qs = [jnp.where(lane < HEAD_DIM, q, zero), jnp.where(lane >= HEAD_DIM, q, zero)]
    else:
        qs = [q]
    if backs is None:
        lo = 0
    else:
        back = jnp.int32(backs[-1])
        for bi in range(len(backs) - 2, -1, -1):
            back = jnp.where(blk == bi, jnp.int32(backs[bi]), back)
        lo = jnp.maximum(qi - back, 0)
    if sel_cfg is not None:
        selv = sel_ref[...]
        e_row = lax.broadcasted_iota(jnp.int32, (LANES, TK), 0)
        e_col = lax.broadcasted_iota(jnp.int32, (LANES, TK), 1) >> int(math.log2(TK // sel_bpt))

    def body(j, carry):
        start = pl.multiple_of(j * TK, TK)
        kj = k_ref[pl.ds(start, TK), :]
        vj = v_ref[pl.ds(start, TK), :]
        d = jnp.minimum(qi - j, n_tab - 1)
        new = []
        for s in range(n_sub):
            m, l, acc = carry[s]
            lg = _dot_nt(qs[s], kj) * scale + tab_ref[s, d]
            if sel_cfg is not None:
                off = sel_stride * (s + sel_pair_mul * blk) + sel_bpt * j
                expand = jnp.where(e_row == e_col + off, 1.0, 0.0).astype(BF16)
                lg = lg + (_dot(selv, expand) - 1.0) * (-NEG_INF)
            m_new = jnp.maximum(m, jnp.max(lg, axis=-1, keepdims=True))
            alpha = jnp.exp(m - m_new)
            p = jnp.exp(lg - m_new)
            l = alpha * l + jnp.sum(p, axis=-1, keepdims=True)
            acc = alpha * acc + _dot(p.astype(BF16), vj)
            new.append((m_new, l, acc))
        return tuple(new)

    init = tuple((jnp.full((TQ, 1), NEG_INF, F32), jnp.zeros((TQ, 1), F32), jnp.zeros((TQ, LANES), F32))
                 for _ in range(n_sub))
    res = lax.fori_loop(lo, qi + 1, body, init)
    outs = [acc / l for (_, l, acc) in res]
    if n_sub == 2:
        o_ref[...] = jnp.where(lane < HEAD_DIM, outs[0], outs[1]).astype(o_ref.dtype)
    else:
        o_ref[...] = outs[0].astype(o_ref.dtype)
    if want_lse:
        lses = [m + jnp.log(l) for (m, l, _) in res]
        rest[1][...] = jnp.where(lane < HEAD_DIM, lses[0], lses[1]) if n_sub == 2 else jnp.broadcast_to(lses[0], (TQ, LANES))


def _attn(q, k, v, tab, sel, *, qc0, kc0, vc0, n_blk, n_sub, scale, backs=None, k_shared=False,
          tab_shared=False, sel_cfg=None, out_dtype=F32, want_lse=False):
    b, s = q.shape[:2]
    n_tab = tab.shape[2]
    cfg = (n_sub, n_tab, scale, backs, sel_cfg, want_lse)
    kidx = (lambda bb, h, i: (bb, 0, kc0)) if k_shared else (lambda bb, h, i: (bb, 0, kc0 + h))
    vidx = (lambda bb, h, i: (bb, 0, vc0)) if k_shared else (lambda bb, h, i: (bb, 0, vc0 + h))
    tidx = (lambda bb, h, i: (0, 0, 0, 0, 0)) if tab_shared else (lambda bb, h, i: (h, 0, 0, 0, 0))
    in_specs = [pl.BlockSpec((None, TQ, LANES), lambda bb, h, i: (bb, i, qc0 + h)),
                pl.BlockSpec((None, s, LANES), kidx),
                pl.BlockSpec((None, s, LANES), vidx),
                pl.BlockSpec((None, n_sub, n_tab, TQ, TK), tidx)]
    args = [q, k, v, tab]
    if sel_cfg is not None:
        in_specs.append(pl.BlockSpec((None, TQ, LANES), lambda bb, h, i: (bb, i, 0)))
        args.append(sel)
    ospec = pl.BlockSpec((None, TQ, LANES), lambda bb, h, i: (bb, i, h))
    out_specs = [ospec]
    out_shape = [jax.ShapeDtypeStruct((b, s, n_blk * LANES), out_dtype)]
    if want_lse:
        out_specs.append(ospec)
        out_shape.append(jax.ShapeDtypeStruct((b, s, n_blk * LANES), F32))
    return pl.pallas_call(
        functools.partial(_attn_kernel, cfg),
        grid=(b, n_blk, s // TQ),
        in_specs=in_specs,
        out_specs=out_specs,
        out_shape=out_shape,
        compiler_params=_params("parallel", "parallel", "arbitrary"),
        name="attn",
    )(*args)


def _nsa_cmp_kernel(kc_ref, vc_ref, pe_ref, wlo_ref, whi_ref, w2_ref, gm_ref, gain_ref, kcmp_ref, vcmp_ref):
    def compress(c, j):
        lo = _dot((c + pe_ref[j, 0]).astype(BF16), wlo_ref[j])
        hi = _dot((c + pe_ref[j, 1]).astype(BF16), whi_ref[j])
        h = lo + pltpu.roll(hi, hi.shape[0] - 1, 0)
        h = h * _sigmoid(h)
        return _dot(h.astype(BF16), w2_ref[j])

    kz = compress(kc_ref[...], 0)
    msq = _dot_hilo(kz * kz, gm_ref[...])
    kcmp_ref[...] = (kz * lax.rsqrt(msq + RMS_EPS) * gain_ref[...]).astype(BF16)
    vcmp_ref[...] = compress(vc_ref[...], 1).astype(BF16)


def _nsa_cmp(kc2, vc2, pe, wlo, whi, w2, gm, gain):
    b, nch, width = kc2.shape
    consts = [pe, wlo, whi, w2, gm, gain]
    blk = pl.BlockSpec((None, nch, width), lambda i: (i, 0, 0))
    oblk = pl.BlockSpec((None, nch, LANES), lambda i: (i, 0, 0))
    return pl.pallas_call(
        _nsa_cmp_kernel,
        grid=(b,),
        in_specs=[blk, blk] + [_resident(a.shape) for a in consts],
        out_specs=[oblk, oblk],
        out_shape=[jax.ShapeDtypeStruct((b, nch, LANES), BF16)] * 2,
        compiler_params=_params("parallel"),
        name="nsa_cmp",
    )(kc2, vc2, *consts)


def _nsa_sel_kernel(q_ref, kcmp_ref, vcmp_ref, bias_ref, ovl_ref, oc_ref, sel_ref):
    qi = pl.program_id(1)
    scale = HEAD_DIM ** -0.5
    lane = lax.broadcasted_iota(jnp.int32, (1, LANES), 1)
    t = qi * TQ + lax.broadcasted_iota(jnp.int32, (TQ, 1), 0)
    mask_c = (NSA_CMP_STRIDE * lane + NSA_CMP_LEN - 1) <= t
    kcmp = kcmp_ref[...]
    vcmp = vcmp_ref[...]
    imp = jnp.zeros((TQ, LANES), F32)
    for p in range(NSA_HPG):
        qp = q_ref[:, p * LANES:(p + 1) * LANES]
        zero = jnp.zeros_like(qp)
        ocs = []
        for g in range(NSA_GROUPS):
            qs = jnp.where((lane >> 6) == g, qp, zero)
            lg = _dot_nt(qs, kcmp) * scale + bias_ref[p * NSA_GROUPS + g]
            lg = jnp.where(mask_c, lg, NEG_INF)
            m = jnp.max(lg, axis=-1, keepdims=True)
            e = jnp.where(mask_c, jnp.exp(lg - m), 0.0)
            den = jnp.maximum(jnp.sum(e, axis=-1, keepdims=True), 1e-30)
            pc = e / den
            ocs.append(_dot(pc.astype(BF16), vcmp))
            imp = imp + _dot_hilo(pc, ovl_ref[g])
        oc_ref[:, p * LANES:(p + 1) * LANES] = jnp.where(lane < HEAD_DIM, ocs[0], ocs[1])

    n_slc = 32
    ids = lane & (n_slc - 1)
    cur = t >> 6
    forced = (ids == 0) | (ids == cur) | (ids == cur - 1)
    score = jnp.where(forced, NSA_FORCED, jnp.where(ids <= cur, imp, NEG_INF))
    cnt = jnp.zeros((TQ, LANES), jnp.int32)
    for mp in range(n_slc):
        colb = jnp.where(lane < n_slc, score[:, mp:mp + 1], score[:, n_slc + mp:n_slc + mp + 1])
        tie = jnp.where(mp < ids, 1, 0)
        cnt = cnt + jnp.where(colb > score, 1, jnp.where(colb == score, tie, 0))
    keep = jnp.where(cnt < NSA_SLC_TOP, jnp.where(score > 0.5 * NEG_INF, 1.0, 0.0), 0.0)
    sel_ref[...] = jnp.where(lane < 2 * n_slc, keep, 0.0).astype(BF16)


def _nsa_sel(q, kcmp, vcmp, bias_c, ovl):
    b, s, w = q.shape
    return pl.pallas_call(
        _nsa_sel_kernel,
        grid=(b, s // TQ),
        in_specs=[pl.BlockSpec((None, TQ, w), lambda bb, i: (bb, i, 0)),
                  pl.BlockSpec((None, LANES, LANES), lambda bb, i: (bb, 0, 0)),
                  pl.BlockSpec((None, LANES, LANES), lambda bb, i: (bb, 0, 0)),
                  pl.BlockSpec((NSA_HEADS, TQ, LANES), lambda bb, i: (0, i, 0)),
                  _resident(ovl.shape)],
        out_specs=[pl.BlockSpec((None, TQ, w), lambda bb, i: (bb, i, 0)),
                   pl.BlockSpec((None, TQ, LANES), lambda bb, i: (bb, i, 0))],
        out_shape=[jax.ShapeDtypeStruct((b, s, w), F32), jax.ShapeDtypeStruct((b, s, LANES), BF16)],
        compiler_params=_params("parallel", "parallel"),
        name="nsa_sel",
    )(q, kcmp, vcmp, bias_c, ovl)


def _moba_gate_kernel(q_ref, k_ref, avg_ref, sel_ref):
    qi = pl.program_id(1)
    nb = 8
    kmean = _dot(avg_ref[...], k_ref[...])
    row = lax.broadcasted_iota(jnp.int32, kmean.shape, 0)
    col = lax.broadcasted_iota(jnp.int32, kmean.shape, 1)
    kmean = jnp.where((row >> 3) == (col >> 6), kmean, 0.0)
    kh, kl = _split(kmean)
    q = q_ref[...]
    gate = _dot_nt(q, kh) + _dot_nt(q, kl)
    lane = lax.broadcasted_iota(jnp.int32, (1, LANES), 1)
    ids = lane & (nb - 1)
    head = lane >> 3
    past = (head < MOBA_HEADS) & (ids < qi)
    score = jnp.where(past, gate, NEG_INF)
    cnt = jnp.zeros((TQ, LANES), jnp.int32)
    for mp in range(nb):
        colb = score[:, (MOBA_HEADS - 1) * nb + mp:(MOBA_HEADS - 1) * nb + mp + 1]
        for h in range(MOBA_HEADS - 2, -1, -1):
            colb = jnp.where(head == h, score[:, h * nb + mp:h * nb + mp + 1], colb)
        tie = jnp.where(mp < ids, 1, 0)
        cnt = cnt + jnp.where(colb > score, 1, jnp.where(colb == score, tie, 0))
    keep = jnp.where(past, jnp.where(cnt < MOBA_TOP, 1.0, 0.0), 0.0)
    keep = jnp.where(head < MOBA_HEADS, jnp.where(ids == qi, 1.0, keep), 0.0)
    sel_ref[...] = jnp.broadcast_to(keep, (TQ, LANES)).astype(BF16)


def _moba_gate(p_arr, avg, qc, kc):
    b, s, _ = p_arr.shape
    w = MOBA_HEADS * HEAD_DIM
    return pl.pallas_call(
        _moba_gate_kernel,
        grid=(b, s // TQ),
        in_specs=[pl.BlockSpec((None, TQ, w), lambda bb, i: (bb, i, qc)),
                  pl.BlockSpec((None, s, w), lambda bb, i: (bb, 0, kc)),
                  _resident(avg.shape)],
        out_specs=pl.BlockSpec((None, TQ, LANES), lambda bb, i: (bb, i, 0)),
        out_shape=jax.ShapeDtypeStruct((b, s, LANES), BF16),
        compiler_params=_params("parallel", "parallel"),
        name="moba_gate",
    )(p_arr, p_arr, avg)


def _even_out_kernel(x_ref, oa_ref, oc_ref, os_ref, ow_ref, gl_ref, eg_ref, wa_ref, wb_ref, gt_ref, o_ref):
    sg = _sigmoid(gl_ref[...])
    hi, lo = _split(sg)
    nsa = None
    for br, src in enumerate((oc_ref, os_ref, ow_ref)):
        gexp = _dot(hi, eg_ref[br]) + _dot(lo, eg_ref[br])
        term = gexp * src[...]
        nsa = term if nsa is None else nsa + term
    m = _dot(oa_ref[...], wa_ref[...]) + _dot(nsa.astype(BF16), wb_ref[...])
    o_ref[...] = x_ref[...] + gt_ref[...] * m


def _even_out(x, o_a, o_c, o_s, o_w, mla_in, eg, wa, wb, gate, s):
    n, d = x.shape
    tpb = s // TM
    row = lambda wd: pl.BlockSpec((TM, wd), lambda i: (i, 0))
    return pl.pallas_call(
        _even_out_kernel,
        grid=(n // TM,),
        in_specs=[row(d), row(o_a.shape[1]), row(o_c.shape[1]), row(o_s.shape[1]), row(o_w.shape[1]),
                  pl.BlockSpec((TM, LANES), lambda i: (i, 3)),
                  _resident(eg.shape), _resident(wa.shape), _resident(wb.shape), _mod_spec(tpb, d)],
        out_specs=row(d),
        out_shape=jax.ShapeDtypeStruct((n, d), F32),
        compiler_params=_params("parallel"),
        name="even_out",
    )(x, o_a, o_c, o_s, o_w, mla_in, eg, wa, wb, gate)


def _odd_out_kernel(x_ref, od_ref, lse_ref, om_ref, wd_ref, wm_ref, gt_ref, o_ref):
    w = DIL_HPG * HEAD_DIM
    ls = [lse_ref[:, g * w:(g + 1) * w] for g in range(len(DIL_PAIRS))]
    mx = jnp.maximum(jnp.maximum(ls[0], ls[1]), ls[2])
    es = [jnp.exp(l - mx) for l in ls]
    den = es[0] + es[1] + es[2]
    merged = None
    for g in range(len(DIL_PAIRS)):
        term = (es[g] / den) * od_ref[:, g * w:(g + 1) * w]
        merged = term if merged is None else merged + term
    m = _dot(merged.astype(BF16), wd_ref[...]) + _dot(om_ref[...], wm_ref[...])
    o_ref[...] = x_ref[...] + gt_ref[...] * m


def _odd_out(x, o_d, lse_d, o_m, wd, wm, gate, s):
    n, d = x.shape
    tpb = s // TM
    row = lambda wd_: pl.BlockSpec((TM, wd_), lambda i: (i, 0))
    return pl.pallas_call(
        _odd_out_kernel,
        grid=(n // TM,),
        in_specs=[row(d), row(o_d.shape[1]), row(lse_d.shape[1]), row(o_m.shape[1]),
                  _resident(wd.shape), _resident(wm.shape), _mod_spec(tpb, d)],
        out_specs=row(d),
        out_shape=jax.ShapeDtypeStruct((n, d), F32),
        compiler_params=_params("parallel"),
        name="odd_out",
    )(x, o_d, lse_d, o_m, wd, wm, gate)


def _t5_bucket(dist):
    n = jnp.maximum(jnp.asarray(dist, jnp.int32), 0)
    nf = jnp.maximum(n, 1).astype(F32)
    large = T5_MAX_EXACT + (jnp.log(nf / T5_MAX_EXACT) / math.log(T5_MAX_DIST / T5_MAX_EXACT)
                            * (NUM_BUCKETS - T5_MAX_EXACT)).astype(jnp.int32)
    return jnp.where(n < T5_MAX_EXACT, n, jnp.minimum(large, NUM_BUCKETS - 1))


def _tile_dist(n_tab):
    r = np.arange(TQ)[:, None]
    c = np.arange(TK)[None, :]
    return np.stack([d * TQ + r - c for d in range(n_tab)])


def _bias_tiles(t5_cols, ok):
    dist = _tile_dist(ok.shape[0])
    bias = jnp.moveaxis(t5_cols[_t5_bucket(dist)], -1, 0)
    return jnp.where(jnp.asarray(ok)[None], bias, NEG_INF)


def _pair(tiles):
    h = tiles.shape[0]
    return tiles.reshape(h // 2, 2, *tiles.shape[1:])


def _take_cols(w, idx):
    idx = np.asarray(idx)
    out = jnp.take(w, jnp.asarray(np.maximum(idx, 0)), axis=-1)
    return jnp.where(jnp.asarray(idx >= 0), out, 0.0)


def _take_rows(w, idx):
    return jnp.swapaxes(_take_cols(jnp.swapaxes(w, -1, -2), idx), -1, -2)


def _group_mean_matrix(sizes):
    gm = np.zeros((LANES, LANES), np.float32)
    o = 0
    for sz in sizes:
        gm[o:o + sz, o:o + sz] = 1.0 / sz
        o += sz
    return jnp.asarray(gm, BF16)


def _even_layout():
    idx = -np.ones(14 * LANES, np.int64)
    idx[0:256] = np.arange(0, 256)
    idx[256:384] = np.arange(256, 384)
    idx[384:384 + 24] = np.arange(1696, 1720)
    idx[384 + 64:384 + 96] = np.arange(384, 416)
    for p in range(NSA_HPG):
        for g in range(NSA_GROUPS):
            dst = 512 + p * LANES + g * HEAD_DIM
            idx[dst:dst + HEAD_DIM] = 416 + (g * NSA_HPG + p) * HEAD_DIM + np.arange(HEAD_DIM)
    for i in range(6):
        idx[1024 + i * LANES:1024 + (i + 1) * LANES] = 928 + i * LANES + np.arange(LANES)
    meta = [(0, 0, False), (0, 128, False), (0, 256, False), (0, 384, False),
            (1, 0, True), (1, 128, True), (1, 256, True), (1, 384, True),
            (2, 0, False), (3, 0, False), (4, 0, True), (5, 0, False), (6, 0, True), (7, 0, False)]
    out_defs = [(512, F32), (512, BF16), (128, F32), (128, F32), (128, BF16), (128, BF16), (128, BF16), (128, BF16)]
    return idx, tuple(meta), out_defs


def kernel(x, c, t5_bias, ada_w, ada_b, norm_g, ffn_w_in, ffn_w_out, ev_w_in, ev_w_out, mla_q_norm_g,
           mla_kv_norm_g, mla_w_uq, mla_w_ukv, mla_qk_g, nsa_cmp_pe, nsa_cmp_w1, nsa_cmp_w2, nsa_qk_g,
           od_w_in, od_w_out, dil_qk_g, moba_qk_g):
    b, s, d = x.shape
    assert (s, d) == (2048, D_MODEL) and s % TM == 0 and TQ == MOBA_BLOCK
    n = b * s
    hd = HEAD_DIM

    mod = _ada(c, ada_w, ada_b).reshape(DEPTH, b, 3, 3, 1, d)

    dist3 = _tile_dist(3)
    causal = dist3 >= 0
    gm64 = _group_mean_matrix((hd, hd))

    padc = FF_PAD - D_FF
    wa_all = jnp.pad(ffn_w_in[..., :D_FF], ((0, 0), (0, 0), (0, 0), (0, padc))).astype(BF16)
    wb_all = jnp.pad(ffn_w_in[..., D_FF:], ((0, 0), (0, 0), (0, 0), (0, padc))).astype(BF16)
    wo_all = jnp.pad(ffn_w_out, ((0, 0), (0, 0), (0, padc), (0, 0))).astype(BF16)

    ev_idx, ev_meta, ev_outs = _even_layout()
    nsa_tab = t5_bias[:, MLA_HEADS:MLA_HEADS + NSA_HEADS].reshape(NUM_BUCKETS, NSA_GROUPS, NSA_HPG)
    nsa_cols = jnp.transpose(nsa_tab, (0, 2, 1)).reshape(NUM_BUCKETS, NSA_HEADS)
    tab_sel = _pair(_bias_tiles(nsa_cols, causal))
    tab_win = _pair(_bias_tiles(nsa_cols, causal & (dist3 <= NSA_WINDOW - 1)))
    mla_ok = _tile_dist(2) >= 0
    tab_mla = jnp.where(jnp.asarray(mla_ok), 0.0, NEG_INF).astype(F32)[None, None]
    n_cmp_pad = s // NSA_CMP_STRIDE
    cend = NSA_CMP_STRIDE * np.arange(n_cmp_pad) + NSA_CMP_LEN - 1
    bias_c = jnp.moveaxis(nsa_cols[_t5_bucket(np.arange(s)[:, None] - cend[None, :])], -1, 0)
    n_cmp = (s - NSA_CMP_LEN) // NSA_CMP_STRIDE + 1
    cstart = np.arange(n_cmp) * NSA_CMP_STRIDE
    sstart = np.arange(s // NSA_SLC_BLOCK) * NSA_SLC_BLOCK
    overlap = np.clip(np.minimum(cstart[:, None] + NSA_CMP_LEN, sstart[None, :] + NSA_SLC_BLOCK)
                      - np.maximum(cstart[:, None], sstart[None, :]), 0, None).astype(np.float32) / NSA_CMP_LEN
    ovl = np.zeros((NSA_GROUPS, LANES, LANES), np.float32)
    for g in range(NSA_GROUPS):
        ovl[g, :n_cmp, 32 * g:32 * g + 32] = overlap
    ovl = jnp.asarray(ovl, BF16)
    eg = np.zeros((3, LANES, NSA_HEADS * hd), np.float32)
    for g in range(NSA_GROUPS):
        for p in range(NSA_HPG):
            for br in range(3):
                eg[br, (g * NSA_HPG + p) * 3 + br, p * LANES + g * hd:p * LANES + (g + 1) * hd] = 1.0
    eg = jnp.asarray(eg, BF16)
    gm_mla = _group_mean_matrix((MLA_NOPE, MLA_ROPE))
    rot = np.zeros((LANES, LANES), np.float32)
    half = MLA_ROPE // 2
    for i in range(half):
        rot[MLA_NOPE + half + i, MLA_NOPE + i] = -1.0
        rot[MLA_NOPE + i, MLA_NOPE + half + i] = 1.0
    rot = jnp.asarray(rot, BF16)
    inv = ROPE_THETA ** (-jnp.arange(0, MLA_ROPE, 2, dtype=F32) / MLA_ROPE)
    ang = jnp.arange(s, dtype=F32)[:, None] * inv[None, :]
    ones = jnp.ones((s, MLA_NOPE), F32)
    cos_t = jnp.concatenate([ones, jnp.cos(ang), jnp.cos(ang), jnp.ones((s, LANES - MLA_NOPE - MLA_ROPE), F32)], axis=1)
    sin_t = jnp.concatenate([0 * ones, jnp.sin(ang), jnp.sin(ang), jnp.zeros((s, LANES - MLA_NOPE - MLA_ROPE), F32)], axis=1)
    uq_idx = -np.ones(MLA_HEADS * LANES, np.int64)
    uk_idx = -np.ones(MLA_HEADS * LANES, np.int64)
    uv_idx = -np.ones(MLA_HEADS * LANES, np.int64)
    oa_idx = -np.ones(MLA_HEADS * LANES, np.int64)
    for h in range(MLA_HEADS):
        uq_idx[h * LANES:h * LANES + 96] = h * 96 + np.arange(96)
        uk_idx[h * LANES:h * LANES + 64] = h * 128 + np.arange(64)
        uv_idx[h * LANES:h * LANES + 64] = h * 128 + 64 + np.arange(64)
        oa_idx[h * LANES:h * LANES + 64] = h * 64 + np.arange(64)
    ob_idx = np.zeros(NSA_HEADS * hd, np.int64)
    for p in range(NSA_HPG):
        for g in range(NSA_GROUPS):
            ob_idx[p * LANES + g * hd:p * LANES + (g + 1) * hd] = MLA_HEADS * MLA_V + (g * NSA_HPG + p) * hd + np.arange(hd)

    dil_tabs = []
    for gi, (w, r) in enumerate(DIL_PAIRS):
        ok = causal & (dist3 <= w) & (dist3 % r == 0)
        dil_tabs.append(_bias_tiles(t5_bias[:, gi * DIL_HPG:(gi + 1) * DIL_HPG], ok))
    tab_dil = _pair(jnp.concatenate(dil_tabs, axis=0))
    dil_backs = (1, 1, 2, 2, None, None)
    dil_backs = tuple(bk if bk is not None else s // TK for bk in dil_backs)
    tab_moba = _pair(_bias_tiles(t5_bias[:, DIL_SLOTS:DIL_SLOTS + MOBA_HEADS], causal))
    avg = np.zeros((LANES, s), np.float32)
    for h in range(MOBA_HEADS):
        for m in range(s // MOBA_BLOCK):
            avg[8 * h + m, m * MOBA_BLOCK:(m + 1) * MOBA_BLOCK] = 1.0 / MOBA_BLOCK
    avg = jnp.asarray(avg, BF16)
    od_meta = tuple((0, cidx * LANES, (cidx < 12) or (18 <= cidx < 22)) for cidx in range(24))

    xf = x.reshape(n, d)
    for i in range(DEPTH):
        j = i // 2
        g_i = norm_g[i].reshape(3, 1, d)
        xf = _ffn(xf, g_i[0], mod[i, :, 0, 0], mod[i, :, 0, 1], mod[i, :, 0, 2],
                  wa_all[i, 0], wb_all[i, 0], wo_all[i, 0], s)
        if i % 2 == 0:
            w_in = _take_cols(ev_w_in[j], ev_idx).astype(BF16)
            gain = jnp.ones((14 * LANES,), F32)
            gain = gain.at[512:1024].set(jnp.tile(nsa_qk_g[j, 0], 8))
            gain = gain.at[1280:1408].set(jnp.tile(nsa_qk_g[j, 1], 2))
            gain = gain.at[1536:1664].set(jnp.tile(nsa_qk_g[j, 1], 2))
            mla_in, nsa_q, kc, vc, ks, vs, kw, vw = _proj(
                xf, g_i[1], mod[i, :, 1, 0], mod[i, :, 1, 1], w_in, gain[None], gm64, ev_meta, ev_outs, s)
            zpad = jnp.zeros((LANES - MLA_NOPE - MLA_ROPE,), F32)
            gq = jnp.concatenate([mla_qk_g[j, 0], zpad])[None]
            gkn = jnp.concatenate([mla_qk_g[j, 1, :MLA_NOPE], jnp.zeros((LANES - MLA_NOPE,), F32)])[None]
            gkr = jnp.concatenate([jnp.zeros((MLA_NOPE,), F32), mla_qk_g[j, 1, MLA_NOPE:], zpad])[None]
            qf, kf, vf = _mla_prep(
                mla_in, mla_q_norm_g[j][None], mla_kv_norm_g[j][None],
                _take_cols(mla_w_uq[j], uq_idx).astype(BF16), _take_cols(mla_w_ukv[j], uk_idx).astype(BF16),
                _take_cols(mla_w_ukv[j], uv_idx).astype(BF16), gm_mla, rot, gq, gkn, gkr, cos_t, sin_t, s)
            sh3 = lambda a: a.reshape(b, s, a.shape[-1])
            (o_a,) = _attn(sh3(qf), sh3(kf), sh3(vf), tab_mla, None, qc0=0, kc0=0, vc0=0, n_blk=MLA_HEADS, n_sub=1,
                           scale=(MLA_NOPE + MLA_ROPE) ** -0.5, tab_shared=True, out_dtype=BF16)
            pe = nsa_cmp_pe[j]
            pe2 = jnp.broadcast_to(pe.reshape(2, 2, 16, 1, hd), (2, 2, 16, NSA_GROUPS, hd)).reshape(2, 2, 1, 16 * LANES)
            w1 = nsa_cmp_w1[j].reshape(2, 2, 16, hd, NSA_CMP_HID)
            eye = jnp.eye(NSA_GROUPS, dtype=F32)
            w1x = jnp.einsum('jaldc,gh->jalgdhc', w1, eye).reshape(2, 2, 16 * LANES, NSA_GROUPS * NSA_CMP_HID).astype(BF16)
            w2x = jnp.einsum('jcd,gh->jgchd', nsa_cmp_w2[j], eye).reshape(2, NSA_GROUPS * NSA_CMP_HID, LANES).astype(BF16)
            kcmp, vcmp = _nsa_cmp(kc.reshape(b, n_cmp_pad, 16 * LANES), vc.reshape(b, n_cmp_pad, 16 * LANES),
                                  pe2, w1x[:, 0], w1x[:, 1], w2x, gm64, jnp.tile(nsa_qk_g[j, 1], 2)[None])
            o_c, sel = _nsa_sel(sh3(nsa_q), kcmp, vcmp, bias_c, ovl)
            (o_s,) = _attn(sh3(nsa_q), sh3(ks), sh3(vs), tab_sel, sel, qc0=0, kc0=0, vc0=0, n_blk=NSA_HPG, n_sub=2,
                           scale=hd ** -0.5, k_shared=True, sel_cfg=(TK // NSA_SLC_BLOCK, 32, 0))
            (o_w,) = _attn(sh3(nsa_q), sh3(kw), sh3(vw), tab_win, None, qc0=0, kc0=0, vc0=0, n_blk=NSA_HPG, n_sub=2,
                           scale=hd ** -0.5, k_shared=True, backs=(2, 2, 2, 2))
            wa_o = _take_rows(ev_w_out[j], oa_idx).astype(BF16)
            wb_o = _take_rows(ev_w_out[j], ob_idx).astype(BF16)
            xf = _even_out(xf, o_a.reshape(n, -1), o_c.reshape(n, -1), o_s.reshape(n, -1), o_w.reshape(n, -1),
                           mla_in, eg, wa_o, wb_o, mod[i, :, 1, 2], s)
        else:
            gain = jnp.concatenate([jnp.tile(dil_qk_g[j, 0], 12), jnp.tile(dil_qk_g[j, 1], 12), jnp.ones((768,), F32),
                                    jnp.tile(moba_qk_g[j, 0], 4), jnp.tile(moba_qk_g[j, 1], 4), jnp.ones((256,), F32)])
            (pr,) = _proj(xf, g_i[1], mod[i, :, 1, 0], mod[i, :, 1, 1], od_w_in[j].astype(BF16), gain[None], gm64,
                          od_meta, [(24 * LANES, BF16)], s)
            pr3 = pr.reshape(b, s, 24 * LANES)
            o_d, lse_d = _attn(pr3, pr3, pr3, tab_dil, None, qc0=0, kc0=6, vc0=12, n_blk=6, n_sub=2, scale=hd ** -0.5,
                               backs=dil_backs, want_lse=True)
            selm = _moba_gate(pr3, avg, 9, 10)
            (o_m,) = _attn(pr3, pr3, pr3, tab_moba, selm, qc0=18, kc0=20, vc0=22, n_blk=2, n_sub=2, scale=hd ** -0.5,
                           sel_cfg=(1, 8, 2), out_dtype=BF16)
            xf = _odd_out(xf, o_d.reshape(n, -1), lse_d.reshape(n, -1), o_m.reshape(n, -1),
                          od_w_out[j, :DIL_HPG * hd].astype(BF16), od_w_out[j, DIL_HPG * hd:].astype(BF16),
                          mod[i, :, 1, 2], s)
        xf = _ffn(xf, g_i[2], mod[i, :, 2, 0], mod[i, :, 2, 1], mod[i, :, 2, 2],
                  wa_all[i, 1], wb_all[i, 1], wo_all[i, 1], s)
    return xf.reshape(b, s, d)
```

```python
import functools
import math

import numpy as np
import jax
import jax.numpy as jnp
from jax import lax
from jax.experimental import pallas as pl
from jax.experimental.pallas import tpu as pltpu

F32 = jnp.float32
BF16 = jnp.bfloat16

D_MODEL = 1024
DEPTH = 4
D_FF = 2752
HEAD_DIM = 64
NUM_BUCKETS = 32
T5_MAX_EXACT = 16
T5_MAX_DIST = 128
RMS_EPS = 1e-6
NEG_INF = -1e30
MLA_HEADS = 8
MLA_NOPE = 64
MLA_ROPE = 32
MLA_V = 64
MLA_Q_LORA = 256
MLA_KV_LORA = 128
ROPE_THETA = 10000.0
NSA_HEADS = 8
NSA_GROUPS = 2
NSA_HPG = 4
NSA_CMP_LEN = 32
NSA_CMP_STRIDE = 16
NSA_CMP_HID = 256
NSA_SLC_BLOCK = 64
NSA_SLC_TOP = 8
NSA_WINDOW = 512
NSA_FORCED = 1e6
DIL_PAIRS = ((128, 1), (512, 4), (2048, 16))
DIL_HPG = 4
DIL_SLOTS = len(DIL_PAIRS) * DIL_HPG
MOBA_HEADS = 4
MOBA_BLOCK = 256
MOBA_TOP = 3

LANES = 128
TM = 512
TQ = 256
TK = 256
FF_PAD = 2816
FF_CHUNK = 256
VMEM_LIMIT = 56 * 1024 * 1024
LOG2E = math.log2(math.e)


def _dot(a, b):
    return jnp.dot(a, b, preferred_element_type=F32)


def _dot_nt(a, b):
    return lax.dot_general(a, b, (((1,), (1,)), ((), ())), preferred_element_type=F32)


def _split(a):
    hi = a.astype(BF16)
    lo = (a - hi.astype(F32)).astype(BF16)
    return hi, lo


def _dot_hilo(a, b):
    hi, lo = _split(a)
    return _dot(hi, b) + _dot(lo, b)


def _sigmoid(x):
    return 1.0 / (1.0 + jnp.exp(-x))


def _modulated_norm(x, g, shift, scale):
    ms = jnp.mean(x * x, axis=-1, keepdims=True)
    y = x * lax.rsqrt(ms + RMS_EPS) * g
    return y * (1.0 + scale) + shift


def _params(*sem):
    return pltpu.CompilerParams(dimension_semantics=sem, vmem_limit_bytes=VMEM_LIMIT)


def _resident(shape):
    nd = len(shape)
    return pl.BlockSpec(shape, lambda *_: (0,) * nd, pipeline_mode=pl.Buffered(1))


def _ada_kernel(c_ref, w_ref, b_ref, o_ref):
    c = c_ref[...]
    ca = c * _sigmoid(c)
    o_ref[...] = jnp.dot(ca, w_ref[...], preferred_element_type=F32,
                         precision=lax.Precision.HIGHEST) + b_ref[...]


def _ada(c, ada_w, ada_b):
    depth, d, n = ada_w.shape
    b = c.shape[0]
    tn = 1152
    return pl.pallas_call(
        _ada_kernel,
        grid=(depth, n // tn),
        in_specs=[pl.BlockSpec((b, d), lambda l, j: (0, 0)),
                  pl.BlockSpec((None, d, tn), lambda l, j: (l, 0, j)),
                  pl.BlockSpec((None, 1, tn), lambda l, j: (l, 0, j))],
        out_specs=pl.BlockSpec((None, b, tn), lambda l, j: (l, 0, j)),
        out_shape=jax.ShapeDtypeStruct((depth, b, n), F32),
        compiler_params=_params("parallel", "parallel"),
        name="ada",
    )(c, ada_w, ada_b.reshape(depth, 1, n))


def _ffn_kernel(x_ref, g_ref, sh_ref, sc_ref, gt_ref, wa_ref, wb_ref, wo_ref, o_ref, y_ref, acc_ref):
    y_ref[...] = _modulated_norm(x_ref[...], g_ref[...], sh_ref[...], sc_ref[...]).astype(BF16)
    for c in range(FF_PAD // FF_CHUNK):
        sl = slice(c * FF_CHUNK, (c + 1) * FF_CHUNK)
        a = _dot(y_ref[...], wa_ref[:, sl])
        b = _dot(y_ref[...], wb_ref[:, sl])
        u = (a * _sigmoid(a) * b).astype(BF16)
        contrib = _dot(u, wo_ref[sl, :])
        if c == 0:
            acc_ref[...] = contrib
        else:
            acc_ref[...] += contrib
    o_ref[...] = x_ref[...] + 0.5 * gt_ref[...] * acc_ref[...]


def _mod_spec(tiles_per_batch, d):
    return pl.BlockSpec((None, 1, d), lambda i: (i // tiles_per_batch, 0, 0))


def _ffn(x, g, shift, scale, gate, wa, wb, wo, s):
    n, d = x.shape
    tpb = s // TM
    return pl.pallas_call(
        _ffn_kernel,
        grid=(n // TM,),
        in_specs=[pl.BlockSpec((TM, d), lambda i: (i, 0)),
                  _resident((1, d)),
                  _mod_spec(tpb, d), _mod_spec(tpb, d), _mod_spec(tpb, d),
                  _resident(wa.shape), _resident(wb.shape), _resident(wo.shape)],
        out_specs=pl.BlockSpec((TM, d), lambda i: (i, 0)),
        out_shape=jax.ShapeDtypeStruct((n, d), F32),
        scratch_shapes=[pltpu.VMEM((TM, d), BF16), pltpu.VMEM((TM, d), F32)],
        compiler_params=_params("parallel"),
        name="ffn",
    )(x, g, shift, scale, gate, wa, wb, wo)


def _proj_kernel(meta, n_out, x_ref, g_ref, sh_ref, sc_ref, w_ref, gain_ref, gm_ref, *rest):
    outs = rest[:n_out]
    y_ref = rest[n_out]
    y_ref[...] = _modulated_norm(x_ref[...], g_ref[...], sh_ref[...], sc_ref[...]).astype(BF16)
    for c, (oi, off, normed) in enumerate(meta):
        sl = slice(c * LANES, (c + 1) * LANES)
        z = _dot(y_ref[...], w_ref[:, sl])
        if normed:
            msq = _dot_hilo(z * z, gm_ref[...])
            z = z * lax.rsqrt(msq + RMS_EPS) * gain_ref[:, sl]
        outs[oi][:, off:off + LANES] = z.astype(outs[oi].dtype)


def _proj(x, g, shift, scale, w, gain, gm, meta, out_defs, s):
    n, d = x.shape
    tpb = s // TM
    return pl.pallas_call(
        functools.partial(_proj_kernel, meta, len(out_defs)),
        grid=(n // TM,),
        in_specs=[pl.BlockSpec((TM, d), lambda i: (i, 0)),
                  _resident((1, d)),
                  _mod_spec(tpb, d), _mod_spec(tpb, d),
                  _resident(w.shape), _resident(gain.shape), _resident(gm.shape)],
        out_specs=[pl.BlockSpec((TM, wd), lambda i: (i, 0)) for wd, _ in out_defs],
        out_shape=[jax.ShapeDtypeStruct((n, wd), dt) for wd, dt in out_defs],
        scratch_shapes=[pltpu.VMEM((TM, d), BF16)],
        compiler_params=_params("parallel"),
        name="proj",
    )(x, g, shift, scale, w, gain, gm)


def _mla_prep_kernel(in_ref, qg_ref, kvg_ref, wuq_ref, wuk_ref, wuv_ref, gm_ref, rot_ref,
                     gq_ref, gkn_ref, gkr_ref, cos_ref, sin_ref, q_out, k_out, v_out):
    def rms(z, g):
        return z * lax.rsqrt(jnp.mean(z * z, axis=-1, keepdims=True) + RMS_EPS) * g

    cqn = rms(in_ref[:, 0:MLA_Q_LORA], qg_ref[...]).astype(BF16)
    ckvn = rms(in_ref[:, MLA_Q_LORA:MLA_Q_LORA + MLA_KV_LORA], kvg_ref[...]).astype(BF16)
    c3 = in_ref[:, 3 * LANES:4 * LANES]
    cos = cos_ref[...]
    sin = sin_ref[...]
    gm = gm_ref[...]
    rot = rot_ref[...]

    def norm_rope(z, gain):
        msq = _dot_hilo(z * z, gm)
        z = z * lax.rsqrt(msq + RMS_EPS) * gain
        return z * cos + _dot_hilo(z, rot) * sin

    kr = norm_rope(c3, gkr_ref[...])
    for h in range(MLA_HEADS):
        sl = slice(h * LANES, (h + 1) * LANES)
        q_out[:, sl] = norm_rope(_dot(cqn, wuq_ref[:, sl]), gq_ref[...]).astype(BF16)
        k_out[:, sl] = (norm_rope(_dot(ckvn, wuk_ref[:, sl]), gkn_ref[...]) + kr).astype(BF16)
    v_out[...] = _dot(ckvn, wuv_ref[...]).astype(BF16)


def _mla_prep(mla_in, qg, kvg, wuq, wuk, wuv, gm, rot, gq, gkn, gkr, cos_t, sin_t, s):
    n = mla_in.shape[0]
    tpb = s // TM
    hw = MLA_HEADS * LANES
    vw = MLA_HEADS * MLA_V
    tab = pl.BlockSpec((TM, LANES), lambda i: (i % tpb, 0))
    consts = [qg, kvg, wuq, wuk, wuv, gm, rot, gq, gkn, gkr]
    return pl.pallas_call(
        _mla_prep_kernel,
        grid=(n // TM,),
        in_specs=[pl.BlockSpec((TM, 4 * LANES), lambda i: (i, 0))] + [_resident(a.shape) for a in consts] + [tab, tab],
        out_specs=[pl.BlockSpec((TM, hw), lambda i: (i, 0)), pl.BlockSpec((TM, hw), lambda i: (i, 0)),
                   pl.BlockSpec((TM, vw), lambda i: (i, 0))],
        out_shape=[jax.ShapeDtypeStruct((n, hw), BF16), jax.ShapeDtypeStruct((n, hw), BF16),
                   jax.ShapeDtypeStruct((n, vw), BF16)],
        compiler_params=_params("parallel"),
        name="mla_prep",
    )(mla_in, *consts, cos_t, sin_t)


def _attn_kernel(cfg, q_ref, k_ref, vt_ref, tab_ref, *rest):
    n_tab, qw, scale2, backs, sel_cfg, want_lse = cfg
    if sel_cfg is not None:
        sel_ref, rest = rest[0], rest[1:]
        sel_bpt, sel_stride, sel_pair_mul = sel_cfg
    o_ref = rest[0]
    blk = pl.program_id(1)
    qi = pl.program_id(2)
    lane = lax.broadcasted_iota(jnp.int32, (1, LANES), 1)
    if qw == 1:
        q = q_ref[...]
        zero = jnp.zeros_like(q)
        qs = [jnp.where(lane < HEAD_DIM, q, zero), jnp.where(lane >= HEAD_DIM, q, zero)]
    else:
        qs = [q_ref[:, 0:LANES], q_ref[:, LANES:2 * LANES]]
    if backs is None:
        lo = 0
    else:
        back = jnp.int32(backs[-1])
        for bi in range(len(backs) - 2, -1, -1):
            back = jnp.where(blk == bi, jnp.int32(backs[bi]), back)
        lo = jnp.maximum(qi - back, 0)
    if sel_cfg is not None:
        selv = sel_ref[...]
        e_key = lax.broadcasted_iota(jnp.int32, (TK, LANES), 0) >> int(math.log2(TK // sel_bpt))
        e_col = lax.broadcasted_iota(jnp.int32, (TK, LANES), 1)

    def body(j, carry):
        start = pl.multiple_of(j * TK, TK)
        d = jnp.minimum(qi - j, n_tab - 1)
        sts = []
        for s in range(2):
            kj = k_ref[pl.ds(start, TK), :] if qw == 1 else k_ref[pl.ds(start, TK), s * LANES:(s + 1) * LANES]
            st = _dot_nt(kj, qs[s])
            if sel_cfg is not None:
                off = sel_stride * (s + sel_pair_mul * blk) + sel_bpt * j
                expand = jnp.where(e_col == e_key + off, 1.0, 0.0).astype(BF16)
                st = st * scale2 + (_dot_nt(expand, selv) - 1.0) * (-NEG_INF)
            else:
                st = st * scale2
            sts.append(st)
        ps = []
        for s in range(2):
            m, l, _ = carry[s]
            st = sts[s] + tab_ref[s, d]
            m_new = jnp.maximum(m, jnp.max(st, axis=0, keepdims=True))
            alpha = jnp.exp2(m - m_new)
            p = jnp.exp2(st - m_new)
            l = alpha * l + jnp.sum(p, axis=0, keepdims=True)
            ps.append((m_new, l, alpha, p.astype(BF16)))
        new = []
        for s in range(2):
            m_new, l, alpha, p = ps[s]
            vt = vt_ref[s * HEAD_DIM:(s + 1) * HEAD_DIM, pl.ds(start, TK)]
            new.append((m_new, l, alpha * carry[s][2] + _dot(vt, p)))
        return tuple(new)

    init = tuple((jnp.full((1, TQ), NEG_INF, F32), jnp.zeros((1, TQ), F32), jnp.zeros((HEAD_DIM, TQ), F32))
                 for _ in range(2))
    res = lax.fori_loop(lo, qi + 1, body, init)
    out_t = jnp.concatenate([acc / l for (_, l, acc) in res], axis=0)
    o_ref[...] = out_t.T.astype(o_ref.dtype)
    if want_lse:
        lse_t = jnp.concatenate([jnp.broadcast_to((m + jnp.log2(l)) * math.log(2.0), (HEAD_DIM, TQ))
                                 for (m, l, _) in res], axis=0)
        rest[1][...] = lse_t.T


def _attn(q, k, vt, tab, sel, *, qc0, kc0, vb0, n_blk, qw, scale, backs=None, kv_shared=False,
          tab_shared=False, sel_cfg=None, out_dtype=F32, want_lse=False):
    b, s = q.shape[:2]
    n_tab = tab.shape[2]
    cfg = (n_tab, qw, scale * LOG2E, backs, sel_cfg, want_lse)
    kidx = (lambda bb, h, i: (bb, 0, kc0)) if kv_shared else (lambda bb, h, i: (bb, 0, kc0 + h))
    vidx = (lambda bb, h, i: (bb, vb0, 0)) if kv_shared else (lambda bb, h, i: (bb, vb0 + h, 0))
    tidx = (lambda bb, h, i: (0, 0, 0, 0, 0)) if tab_shared else (lambda bb, h, i: (h, 0, 0, 0, 0))
    in_specs = [pl.BlockSpec((None, TQ, qw * LANES), lambda bb, h, i: (bb, i, qc0 + h)),
                pl.BlockSpec((None, s, qw * LANES), kidx),
                pl.BlockSpec((None, 2 * HEAD_DIM, s), vidx),
                pl.BlockSpec((None, 2, n_tab, TK, TQ), tidx)]
    args = [q, k, vt, tab]
    if sel_cfg is not None:
        in_specs.append(pl.BlockSpec((None, TQ, LANES), lambda bb, h, i: (bb, i, 0)))
        args.append(sel)
    ospec = pl.BlockSpec((None, TQ, LANES), lambda bb, h, i: (bb, i, h))
    out_specs = [ospec]
    out_shape = [jax.ShapeDtypeStruct((b, s, n_blk * LANES), out_dtype)]
    if want_lse:
        out_specs.append(ospec)
        out_shape.append(jax.ShapeDtypeStruct((b, s, n_blk * LANES), F32))
    return pl.pallas_call(
        functools.partial(_attn_kernel, cfg),
        grid=(b, n_blk, s // TQ),
        in_specs=in_specs,
        out_specs=out_specs,
        out_shape=out_shape,
        compiler_params=_params("parallel", "parallel", "arbitrary"),
        name="attn",
    )(*args)


def _nsa_cmp_kernel(kc_ref, vc_ref, pe_ref, wlo_ref, whi_ref, w2_ref, gm_ref, gain_ref, kcmp_ref, vcmp_ref):
    def compress(c, j):
        lo = _dot((c + pe_ref[j, 0]).astype(BF16), wlo_ref[j])
        hi = _dot((c + pe_ref[j, 1]).astype(BF16), whi_ref[j])
        h = lo + pltpu.roll(hi, hi.shape[0] - 1, 0)
        h = h * _sigmoid(h)
        return _dot(h.astype(BF16), w2_ref[j])

    kz = compress(kc_ref[...], 0)
    msq = _dot_hilo(kz * kz, gm_ref[...])
    kcmp_ref[...] = (kz * lax.rsqrt(msq + RMS_EPS) * gain_ref[...]).astype(BF16)
    vcmp_ref[...] = compress(vc_ref[...], 1).astype(BF16)


def _nsa_cmp(kc2, vc2, pe, wlo, whi, w2, gm, gain):
    b, nch, width = kc2.shape
    consts = [pe, wlo, whi, w2, gm, gain]
    blk = pl.BlockSpec((None, nch, width), lambda i: (i, 0, 0))
    oblk = pl.BlockSpec((None, nch, LANES), lambda i: (i, 0, 0))
    return pl.pallas_call(
        _nsa_cmp_kernel,
        grid=(b,),
        in_specs=[blk, blk] + [_resident(a.shape) for a in consts],
        out_specs=[oblk, oblk],
        out_shape=[jax.ShapeDtypeStruct((b, nch, LANES), BF16)] * 2,
        compiler_params=_params("parallel"),
        name="nsa_cmp",
    )(kc2, vc2, *consts)


def _nsa_sel_kernel(q_ref, kcmp_ref, vcmp_ref, bias_ref, ovl_ref, oc_ref, sel_ref):
    qi = pl.program_id(1)
    scale = HEAD_DIM ** -0.5
    lane = lax.broadcasted_iota(jnp.int32, (1, LANES), 1)
    t = qi * TQ + lax.broadcasted_iota(jnp.int32, (TQ, 1), 0)
    mask_c = (NSA_CMP_STRIDE * lane + NSA_CMP_LEN - 1) <= t
    kcmp = kcmp_ref[...]
    vcmp = vcmp_ref[...]
    imp = jnp.zeros((TQ, LANES), F32)
    for p in range(NSA_HPG):
        qp = q_ref[:, p * LANES:(p + 1) * LANES]
        zero = jnp.zeros_like(qp)
        ocs = []
        for g in range(NSA_GROUPS):
            qs = jnp.where((lane >> 6) == g, qp, zero)
            lg = _dot_nt(qs, kcmp) * scale + bias_ref[p * NSA_GROUPS + g]
            lg = jnp.where(mask_c, lg, NEG_INF)
            m = jnp.max(lg, axis=-1, keepdims=True)
            e = jnp.where(mask_c, jnp.exp(lg - m), 0.0)
            den = jnp.maximum(jnp.sum(e, axis=-1, keepdims=True), 1e-30)
            pc = e / den
            ocs.append(_dot(pc.astype(BF16), vcmp))
            imp = imp + _dot_hilo(pc, ovl_ref[g])
        oc_ref[:, p * LANES:(p + 1) * LANES] = jnp.where(lane < HEAD_DIM, ocs[0], ocs[1])

    n_slc = 32
    ids = lane & (n_slc - 1)
    cur = t >> 6
    forced = (ids == 0) | (ids == cur) | (ids == cur - 1)
    score = jnp.where(forced, NSA_FORCED, jnp.where(ids <= cur, imp, NEG_INF))
    cnt = jnp.zeros((TQ, LANES), jnp.int32)
    for mp in range(n_slc):
        colb = jnp.where(lane < n_slc, score[:, mp:mp + 1], score[:, n_slc + mp:n_slc + mp + 1])
        tie = jnp.where(mp < ids, 1, 0)
        cnt = cnt + jnp.where(colb > score, 1, jnp.where(colb == score, tie, 0))
    keep = jnp.where(cnt < NSA_SLC_TOP, jnp.where(score > 0.5 * NEG_INF, 1.0, 0.0), 0.0)
    sel_ref[...] = jnp.where(lane < 2 * n_slc, keep, 0.0).astype(BF16)


def _nsa_sel(q, kcmp, vcmp, bias_c, ovl):
    b, s, w = q.shape
    return pl.pallas_call(
        _nsa_sel_kernel,
        grid=(b, s // TQ),
        in_specs=[pl.BlockSpec((None, TQ, w), lambda bb, i: (bb, i, 0)),
                  pl.BlockSpec((None, LANES, LANES), lambda bb, i: (bb, 0, 0)),
                  pl.BlockSpec((None, LANES, LANES), lambda bb, i: (bb, 0, 0)),
                  pl.BlockSpec((NSA_HEADS, TQ, LANES), lambda bb, i: (0, i, 0)),
                  _resident(ovl.shape)],
        out_specs=[pl.BlockSpec((None, TQ, w), lambda bb, i: (bb, i, 0)),
                   pl.BlockSpec((None, TQ, LANES), lambda bb, i: (bb, i, 0))],
        out_shape=[jax.ShapeDtypeStruct((b, s, w), F32), jax.ShapeDtypeStruct((b, s, LANES), BF16)],
        compiler_params=_params("parallel", "parallel"),
        name="nsa_sel",
    )(q, kcmp, vcmp, bias_c, ovl)


def _moba_gate_kernel(q_ref, k_ref, avg_ref, sel_ref):
    qi = pl.program_id(1)
    nb = 8
    kmean = _dot(avg_ref[...], k_ref[...])
    row = lax.broadcasted_iota(jnp.int32, kmean.shape, 0)
    col = lax.broadcasted_iota(jnp.int32, kmean.shape, 1)
    kmean = jnp.where((row >> 3) == (col >> 6), kmean, 0.0)
    kh, kl = _split(kmean)
    q = q_ref[...]
    gate = _dot_nt(q, kh) + _dot_nt(q, kl)
    lane = lax.broadcasted_iota(jnp.int32, (1, LANES), 1)
    ids = lane & (nb - 1)
    head = lane >> 3
    past = (head < MOBA_HEADS) & (ids < qi)
    score = jnp.where(past, gate, NEG_INF)
    cnt = jnp.zeros((TQ, LANES), jnp.int32)
    for mp in range(nb):
        colb = score[:, (MOBA_HEADS - 1) * nb + mp:(MOBA_HEADS - 1) * nb + mp + 1]
        for h in range(MOBA_HEADS - 2, -1, -1):
            colb = jnp.where(head == h, score[:, h * nb + mp:h * nb + mp + 1], colb)
        tie = jnp.where(mp < ids, 1, 0)
        cnt = cnt + jnp.where(colb > score, 1, jnp.where(colb == score, tie, 0))
    keep = jnp.where(past, jnp.where(cnt < MOBA_TOP, 1.0, 0.0), 0.0)
    keep = jnp.where(head < MOBA_HEADS, jnp.where(ids == qi, 1.0, keep), 0.0)
    sel_ref[...] = jnp.broadcast_to(keep, (TQ, LANES)).astype(BF16)


def _moba_gate(p_arr, avg, qc, kc):
    b, s, _ = p_arr.shape
    w = MOBA_HEADS * HEAD_DIM
    return pl.pallas_call(
        _moba_gate_kernel,
        grid=(b, s // TQ),
        in_specs=[pl.BlockSpec((None, TQ, w), lambda bb, i: (bb, i, qc)),
                  pl.BlockSpec((None, s, w), lambda bb, i: (bb, 0, kc)),
                  _resident(avg.shape)],
        out_specs=pl.BlockSpec((None, TQ, LANES), lambda bb, i: (bb, i, 0)),
        out_shape=jax.ShapeDtypeStruct((b, s, LANES), BF16),
        compiler_params=_params("parallel", "parallel"),
        name="moba_gate",
    )(p_arr, p_arr, avg)


def _even_out_kernel(x_ref, oa_ref, oc_ref, os_ref, ow_ref, gl_ref, eg_ref, wa_ref, wb_ref, gt_ref, o_ref):
    sg = _sigmoid(gl_ref[...])
    hi, lo = _split(sg)
    nsa = None
    for br, src in enumerate((oc_ref, os_ref, ow_ref)):
        gexp = _dot(hi, eg_ref[br]) + _dot(lo, eg_ref[br])
        term = gexp * src[...]
        nsa = term if nsa is None else nsa + term
    m = _dot(oa_ref[...], wa_ref[...]) + _dot(nsa.astype(BF16), wb_ref[...])
    o_ref[...] = x_ref[...] + gt_ref[...] * m


def _even_out(x, o_a, o_c, o_s, o_w, mla_in, eg, wa, wb, gate, s):
    n, d = x.shape
    tpb = s // TM
    row = lambda wd: pl.BlockSpec((TM, wd), lambda i: (i, 0))
    return pl.pallas_call(
        _even_out_kernel,
        grid=(n // TM,),
        in_specs=[row(d), row(o_a.shape[1]), row(o_c.shape[1]), row(o_s.shape[1]), row(o_w.shape[1]),
                  pl.BlockSpec((TM, LANES), lambda i: (i, 3)),
                  _resident(eg.shape), _resident(wa.shape), _resident(wb.shape), _mod_spec(tpb, d)],
        out_specs=row(d),
        out_shape=jax.ShapeDtypeStruct((n, d), F32),
        compiler_params=_params("parallel"),
        name="even_out",
    )(x, o_a, o_c, o_s, o_w, mla_in, eg, wa, wb, gate)


def _odd_out_kernel(x_ref, od_ref, lse_ref, om_ref, wd_ref, wm_ref, gt_ref, o_ref):
    w = DIL_HPG * HEAD_DIM
    ls = [lse_ref[:, g * w:(g + 1) * w] for g in range(len(DIL_PAIRS))]
    mx = jnp.maximum(jnp.maximum(ls[0], ls[1]), ls[2])
    es = [jnp.exp(l - mx) for l in ls]
    den = es[0] + es[1] + es[2]
    merged = None
    for g in range(len(DIL_PAIRS)):
        term = (es[g] / den) * od_ref[:, g * w:(g + 1) * w]
        merged = term if merged is None else merged + term
    m = _dot(merged.astype(BF16), wd_ref[...]) + _dot(om_ref[...], wm_ref[...])
    o_ref[...] = x_ref[...] + gt_ref[...] * m


def _odd_out(x, o_d, lse_d, o_m, wd, wm, gate, s):
    n, d = x.shape
    tpb = s // TM
    row = lambda wd_: pl.BlockSpec((TM, wd_), lambda i: (i, 0))
    return pl.pallas_call(
        _odd_out_kernel,
        grid=(n // TM,),
        in_specs=[row(d), row(o_d.shape[1]), row(lse_d.shape[1]), row(o_m.shape[1]),
                  _resident(wd.shape), _resident(wm.shape), _mod_spec(tpb, d)],
        out_specs=row(d),
        out_shape=jax.ShapeDtypeStruct((n, d), F32),
        compiler_params=_params("parallel"),
        name="odd_out",
    )(x, o_d, lse_d, o_m, wd, wm, gate)


def _t5_bucket(dist):
    n = jnp.maximum(jnp.asarray(dist, jnp.int32), 0)
    nf = jnp.maximum(n, 1).astype(F32)
    large = T5_MAX_EXACT + (jnp.log(nf / T5_MAX_EXACT) / math.log(T5_MAX_DIST / T5_MAX_EXACT)
                            * (NUM_BUCKETS - T5_MAX_EXACT)).astype(jnp.int32)
    return jnp.where(n < T5_MAX_EXACT, n, jnp.minimum(large, NUM_BUCKETS - 1))


TOEP_PERIOD = 4 * TQ


def _toeplitz_dist():
    j = np.arange(TOEP_PERIOD)
    return np.where(j < 3 * TQ, j, j - TOEP_PERIOD)


def _toeplitz_tiles(u, n_tab):
    h = u.shape[0]
    m = jnp.tile(u, (1, TK))[:, :TK * (TOEP_PERIOD - 1)].reshape(h, TK, TOEP_PERIOD - 1)[:, :, :n_tab * TQ]
    return jnp.transpose(m.reshape(h, TK, n_tab, TQ), (0, 2, 1, 3))


def _bias_tiles(t5_cols, ok, n_tab=3):
    dist = _toeplitz_dist()
    bias = jnp.transpose(t5_cols[_t5_bucket(dist)])
    u = jnp.where(jnp.asarray(ok)[None], bias, NEG_INF) * LOG2E
    return _toeplitz_tiles(u, n_tab)


def _pair(tiles):
    h = tiles.shape[0]
    return tiles.reshape(h // 2, 2, *tiles.shape[1:])


def _take_cols(w, idx):
    idx = np.asarray(idx)
    out = jnp.take(w, jnp.asarray(np.maximum(idx, 0)), axis=-1)
    return jnp.where(jnp.asarray(idx >= 0), out, 0.0)


def _take_rows(w, idx):
    return jnp.swapaxes(_take_cols(jnp.swapaxes(w, -1, -2), idx), -1, -2)


def _group_mean_matrix(sizes):
    gm = np.zeros((LANES, LANES), np.float32)
    o = 0
    for sz in sizes:
        gm[o:o + sz, o:o + sz] = 1.0 / sz
        o += sz
    return jnp.asarray(gm, BF16)


def _even_layout():
    idx = -np.ones(14 * LANES, np.int64)
    idx[0:256] = np.arange(0, 256)
    idx[256:384] = np.arange(256, 384)
    idx[384:384 + 24] = np.arange(1696, 1720)
    idx[384 + 64:384 + 96] = np.arange(384, 416)
    for p in range(NSA_HPG):
        for g in range(NSA_GROUPS):
            dst = 512 + p * LANES + g * HEAD_DIM
            idx[dst:dst + HEAD_DIM] = 416 + (g * NSA_HPG + p) * HEAD_DIM + np.arange(HEAD_DIM)
    for i in range(6):
        idx[1024 + i * LANES:1024 + (i + 1) * LANES] = 928 + i * LANES + np.arange(LANES)
    meta = [(0, 0, False), (0, 128, False), (0, 256, False), (0, 384, False),
            (1, 0, True), (1, 128, True), (1, 256, True), (1, 384, True),
            (2, 0, False), (3, 0, False), (4, 0, True), (5, 0, False), (6, 0, True), (7, 0, False)]
    out_defs = [(512, F32), (512, BF16), (128, F32), (128, F32), (128, BF16), (128, BF16), (128, BF16), (128, BF16)]
    return idx, tuple(meta), out_defs


def kernel(x, c, t5_bias, ada_w, ada_b, norm_g, ffn_w_in, ffn_w_out, ev_w_in, ev_w_out, mla_q_norm_g,
           mla_kv_norm_g, mla_w_uq, mla_w_ukv, mla_qk_g, nsa_cmp_pe, nsa_cmp_w1, nsa_cmp_w2, nsa_qk_g,
           od_w_in, od_w_out, dil_qk_g, moba_qk_g):
    b, s, d = x.shape
    assert (s, d) == (2048, D_MODEL) and s % TM == 0 and TQ == MOBA_BLOCK and TQ == TK
    n = b * s
    hd = HEAD_DIM

    mod = _ada(c, ada_w, ada_b).reshape(DEPTH, b, 3, 3, 1, d)

    dist = _toeplitz_dist()
    causal = dist >= 0
    gm64 = _group_mean_matrix((hd, hd))

    padc = FF_PAD - D_FF
    wa_all = jnp.pad(ffn_w_in[..., :D_FF], ((0, 0), (0, 0), (0, 0), (0, padc))).astype(BF16)
    wb_all = jnp.pad(ffn_w_in[..., D_FF:], ((0, 0), (0, 0), (0, 0), (0, padc))).astype(BF16)
    wo_all = jnp.pad(ffn_w_out, ((0, 0), (0, 0), (0, padc), (0, 0))).astype(BF16)

    ev_idx, ev_meta, ev_outs = _even_layout()
    nsa_tab = t5_bias[:, MLA_HEADS:MLA_HEADS + NSA_HEADS].reshape(NUM_BUCKETS, NSA_GROUPS, NSA_HPG)
    nsa_cols = jnp.transpose(nsa_tab, (0, 2, 1)).reshape(NUM_BUCKETS, NSA_HEADS)
    tab_sel = _pair(_bias_tiles(nsa_cols, causal))
    tab_win = _pair(_bias_tiles(nsa_cols, causal & (dist <= NSA_WINDOW - 1)))
    tab_mla = _toeplitz_tiles(jnp.where(jnp.asarray(causal), 0.0, NEG_INF).astype(F32)[None], 2)
    tab_mla = jnp.broadcast_to(tab_mla[None], (1, 2) + tab_mla.shape[1:])
    n_cmp_pad = s // NSA_CMP_STRIDE
    jj = np.arange(2 * n_cmp_pad)
    kk = np.where(jj < n_cmp_pad, jj, jj - 2 * n_cmp_pad)
    dc = NSA_CMP_STRIDE * (-kk)[None, :] + np.arange(NSA_CMP_STRIDE)[:, None] - (NSA_CMP_LEN - 1)
    ub = jnp.moveaxis(nsa_cols[_t5_bucket(dc)], -1, 0)
    mb = jnp.tile(ub, (1, 1, n_cmp_pad))[:, :, :n_cmp_pad * (2 * n_cmp_pad - 1)]
    mb = mb.reshape(NSA_HEADS, NSA_CMP_STRIDE, n_cmp_pad, 2 * n_cmp_pad - 1)[..., :n_cmp_pad]
    bias_c = jnp.transpose(mb, (0, 2, 1, 3)).reshape(NSA_HEADS, s, n_cmp_pad)
    n_cmp = (s - NSA_CMP_LEN) // NSA_CMP_STRIDE + 1
    cstart = np.arange(n_cmp) * NSA_CMP_STRIDE
    sstart = np.arange(s // NSA_SLC_BLOCK) * NSA_SLC_BLOCK
    overlap = np.clip(np.minimum(cstart[:, None] + NSA_CMP_LEN, sstart[None, :] + NSA_SLC_BLOCK)
                      - np.maximum(cstart[:, None], sstart[None, :]), 0, None).astype(np.float32) / NSA_CMP_LEN
    ovl = np.zeros((NSA_GROUPS, LANES, LANES), np.float32)
    for g in range(NSA_GROUPS):
        ovl[g, :n_cmp, 32 * g:32 * g + 32] = overlap
    ovl = jnp.asarray(ovl, BF16)
    eg = np.zeros((3, LANES, NSA_HEADS * hd), np.float32)
    for g in range(NSA_GROUPS):
        for p in range(NSA_HPG):
            for br in range(3):
                eg[br, (g * NSA_HPG + p) * 3 + br, p * LANES + g * hd:p * LANES + (g + 1) * hd] = 1.0
    eg = jnp.asarray(eg, BF16)
    gm_mla = _group_mean_matrix((MLA_NOPE, MLA_ROPE))
    rot = np.zeros((LANES, LANES), np.float32)
    half = MLA_ROPE // 2
    for i in range(half):
        rot[MLA_NOPE + half + i, MLA_NOPE + i] = -1.0
        rot[MLA_NOPE + i, MLA_NOPE + half + i] = 1.0
    rot = jnp.asarray(rot, BF16)
    inv = ROPE_THETA ** (-jnp.arange(0, MLA_ROPE, 2, dtype=F32) / MLA_ROPE)
    ang = jnp.arange(s, dtype=F32)[:, None] * inv[None, :]
    ones = jnp.ones((s, MLA_NOPE), F32)
    cos_t = jnp.concatenate([ones, jnp.cos(ang), jnp.cos(ang), jnp.ones((s, LANES - MLA_NOPE - MLA_ROPE), F32)], axis=1)
    sin_t = jnp.concatenate([0 * ones, jnp.sin(ang), jnp.sin(ang), jnp.zeros((s, LANES - MLA_NOPE - MLA_ROPE), F32)], axis=1)
    uq_idx = -np.ones(MLA_HEADS * LANES, np.int64)
    uk_idx = -np.ones(MLA_HEADS * LANES, np.int64)
    uv_idx = np.zeros(MLA_HEADS * MLA_V, np.int64)
    for h in range(MLA_HEADS):
        uq_idx[h * LANES:h * LANES + 96] = h * 96 + np.arange(96)
        uk_idx[h * LANES:h * LANES + 64] = h * 128 + np.arange(64)
        uv_idx[h * MLA_V:(h + 1) * MLA_V] = h * 128 + 64 + np.arange(64)
    ob_idx = np.zeros(NSA_HEADS * hd, np.int64)
    for p in range(NSA_HPG):
        for g in range(NSA_GROUPS):
            ob_idx[p * LANES + g * hd:p * LANES + (g + 1) * hd] = MLA_HEADS * MLA_V + (g * NSA_HPG + p) * hd + np.arange(hd)

    dil_tabs = []
    for gi, (w, r) in enumerate(DIL_PAIRS):
        ok = causal & (dist <= w) & (dist % r == 0)
        dil_tabs.append(_bias_tiles(t5_bias[:, gi * DIL_HPG:(gi + 1) * DIL_HPG], ok))
    tab_dil = _pair(jnp.concatenate(dil_tabs, axis=0))
    dil_backs = (1, 1, 2, 2, s // TK, s // TK)
    tab_moba = _pair(_bias_tiles(t5_bias[:, DIL_SLOTS:DIL_SLOTS + MOBA_HEADS], causal))
    avg = np.zeros((LANES, s), np.float32)
    for h in range(MOBA_HEADS):
        for m in range(s // MOBA_BLOCK):
            avg[8 * h + m, m * MOBA_BLOCK:(m + 1) * MOBA_BLOCK] = 1.0 / MOBA_BLOCK
    avg = jnp.asarray(avg, BF16)
    od_meta = tuple((0, cidx * LANES, (cidx < 12) or (18 <= cidx < 22)) for cidx in range(24))

    sh3 = lambda a: a.reshape(b, s, a.shape[-1])
    tr3 = lambda a: jnp.swapaxes(sh3(a), 1, 2)
    xf = x.reshape(n, d)
    for i in range(DEPTH):
        j = i // 2
        g_i = norm_g[i].reshape(3, 1, d)
        xf = _ffn(xf, g_i[0], mod[i, :, 0, 0], mod[i, :, 0, 1], mod[i, :, 0, 2],
                  wa_all[i, 0], wb_all[i, 0], wo_all[i, 0], s)
        if i % 2 == 0:
            w_in = _take_cols(ev_w_in[j], ev_idx).astype(BF16)
            gain = jnp.ones((14 * LANES,), F32)
            gain = gain.at[512:1024].set(jnp.tile(nsa_qk_g[j, 0], 8))
            gain = gain.at[1280:1408].set(jnp.tile(nsa_qk_g[j, 1], 2))
            gain = gain.at[1536:1664].set(jnp.tile(nsa_qk_g[j, 1], 2))
            mla_in, nsa_q, kc, vc, ks, vs, kw, vw = _proj(
                xf, g_i[1], mod[i, :, 1, 0], mod[i, :, 1, 1], w_in, gain[None], gm64, ev_meta, ev_outs, s)
            zpad = jnp.zeros((LANES - MLA_NOPE - MLA_ROPE,), F32)
            gq = jnp.concatenate([mla_qk_g[j, 0], zpad])[None]
            gkn = jnp.concatenate([mla_qk_g[j, 1, :MLA_NOPE], jnp.zeros((LANES - MLA_NOPE,), F32)])[None]
            gkr = jnp.concatenate([jnp.zeros((MLA_NOPE,), F32), mla_qk_g[j, 1, MLA_NOPE:], zpad])[None]
            qf, kf, vf = _mla_prep(
                mla_in, mla_q_norm_g[j][None], mla_kv_norm_g[j][None],
                _take_cols(mla_w_uq[j], uq_idx).astype(BF16), _take_cols(mla_w_ukv[j], uk_idx).astype(BF16),
                _take_cols(mla_w_ukv[j], uv_idx).astype(BF16), gm_mla, rot, gq, gkn, gkr, cos_t, sin_t, s)
            (o_a,) = _attn(sh3(qf), sh3(kf), tr3(vf), tab_mla, None, qc0=0, kc0=0, vb0=0, n_blk=MLA_HEADS // 2, qw=2,
                           scale=(MLA_NOPE + MLA_ROPE) ** -0.5, tab_shared=True, out_dtype=BF16)
            pe = nsa_cmp_pe[j]
            pe2 = jnp.broadcast_to(pe.reshape(2, 2, 16, 1, hd), (2, 2, 16, NSA_GROUPS, hd)).reshape(2, 2, 1, 16 * LANES)
            w1 = nsa_cmp_w1[j].reshape(2, 2, 16, hd, NSA_CMP_HID)
            eye = jnp.eye(NSA_GROUPS, dtype=F32)
            w1x = jnp.einsum('jaldc,gh->jalgdhc', w1, eye).reshape(2, 2, 16 * LANES, NSA_GROUPS * NSA_CMP_HID).astype(BF16)
            w2x = jnp.einsum('jcd,gh->jgchd', nsa_cmp_w2[j], eye).reshape(2, NSA_GROUPS * NSA_CMP_HID, LANES).astype(BF16)
            kcmp, vcmp = _nsa_cmp(kc.reshape(b, n_cmp_pad, 16 * LANES), vc.reshape(b, n_cmp_pad, 16 * LANES),
                                  pe2, w1x[:, 0], w1x[:, 1], w2x, gm64, jnp.tile(nsa_qk_g[j, 1], 2)[None])
            o_c, sel = _nsa_sel(sh3(nsa_q), kcmp, vcmp, bias_c, ovl)
            (o_s,) = _attn(sh3(nsa_q), sh3(ks), tr3(vs), tab_sel, sel, qc0=0, kc0=0, vb0=0, n_blk=NSA_HPG, qw=1,
                           scale=hd ** -0.5, kv_shared=True, sel_cfg=(TK // NSA_SLC_BLOCK, 32, 0))
            (o_w,) = _attn(sh3(nsa_q), sh3(kw), tr3(vw), tab_win, None, qc0=0, kc0=0, vb0=0, n_blk=NSA_HPG, qw=1,
                           scale=hd ** -0.5, kv_shared=True, backs=(2, 2, 2, 2))
            wa_o = ev_w_out[j, :MLA_HEADS * MLA_V].astype(BF16)
            wb_o = _take_rows(ev_w_out[j], ob_idx).astype(BF16)
            xf = _even_out(xf, o_a.reshape(n, -1), o_c.reshape(n, -1), o_s.reshape(n, -1), o_w.reshape(n, -1),
                           mla_in, eg, wa_o, wb_o, mod[i, :, 1, 2], s)
        else:
            gain = jnp.concatenate([jnp.tile(dil_qk_g[j, 0], 12), jnp.tile(dil_qk_g[j, 1], 12), jnp.ones((768,), F32),
                                    jnp.tile(moba_qk_g[j, 0], 4), jnp.tile(moba_qk_g[j, 1], 4), jnp.ones((256,), F32)])
            (pr,) = _proj(xf, g_i[1], mod[i, :, 1, 0], mod[i, :, 1, 1], od_w_in[j].astype(BF16), gain[None], gm64,
                          od_meta, [(24 * LANES, BF16)], s)
            pr3 = pr.reshape(b, s, 24 * LANES)
            vdt = jnp.swapaxes(pr3[:, :, 12 * LANES:18 * LANES], 1, 2)
            vmt = jnp.swapaxes(pr3[:, :, 22 * LANES:24 * LANES], 1, 2)
            o_d, lse_d = _attn(pr3, pr3, vdt, tab_dil, None, qc0=0, kc0=6, vb0=0, n_blk=6, qw=1, scale=hd ** -0.5,
                               backs=dil_backs, want_lse=True)
            selm = _moba_gate(pr3, avg, 9, 10)
            (o_m,) = _attn(pr3, pr3, vmt, tab_moba, selm, qc0=18, kc0=20, vb0=0, n_blk=2, qw=1, scale=hd ** -0.5,
                           sel_cfg=(1, 8, 2), out_dtype=BF16)
            xf = _odd_out(xf, o_d.reshape(n, -1), lse_d.reshape(n, -1), o_m.reshape(n, -1),
                          od_w_out[j, :DIL_HPG * hd].astype(BF16), od_w_out[j, DIL_HPG * hd:].astype(BF16),
                          mod[i, :, 1, 2], s)
        xf = _ffn(xf, g_i[2], mod[i, :, 2, 0], mod[i, :, 2, 1], mod[i, :, 2, 2],
                  wa_all[i, 1], wb_all[i, 1], wo_all[i, 1], s)
    return xf.reshape(b, s, d)
```

```python
import functools
import math

import numpy as np
import jax
import jax.numpy as jnp
from jax import lax
from jax.experimental import pallas as pl
from jax.experimental.pallas import tpu as pltpu

F32 = jnp.float32
BF16 = jnp.bfloat16

D_MODEL = 1024
DEPTH = 4
D_FF = 2752
HEAD_DIM = 64
NUM_BUCKETS = 32
T5_MAX_EXACT = 16
T5_MAX_DIST = 128
RMS_EPS = 1e-6
NEG_INF = -1e30
MLA_HEADS = 8
MLA_NOPE = 64
MLA_ROPE = 32
MLA_V = 64
MLA_Q_LORA = 256
MLA_KV_LORA = 128
ROPE_THETA = 10000.0
NSA_HEADS = 8
NSA_GROUPS = 2
NSA_HPG = 4
NSA_CMP_LEN = 32
NSA_CMP_STRIDE = 16
NSA_CMP_HID = 256
NSA_SLC_BLOCK = 64
NSA_SLC_TOP = 8
NSA_WINDOW = 512
NSA_FORCED = 1e6
DIL_PAIRS = ((128, 1), (512, 4), (2048, 16))
DIL_HPG = 4
DIL_SLOTS = len(DIL_PAIRS) * DIL_HPG
MOBA_HEADS = 4
MOBA_BLOCK = 256
MOBA_TOP = 3

LANES = 128
TM = 512
TQ = 256
TK = 256
FF_PAD = 2816
FF_CHUNK = 256
PROJ_CHUNK = 256
VMEM_LIMIT = 56 * 1024 * 1024
LOG2E = math.log2(math.e)


def _dot(a, b):
    return jnp.dot(a, b, preferred_element_type=F32)


def _dot_nt(a, b):
    return lax.dot_general(a, b, (((1,), (1,)), ((), ())), preferred_element_type=F32)


def _split(a):
    hi = a.astype(BF16)
    lo = (a - hi.astype(F32)).astype(BF16)
    return hi, lo


def _dot_hilo(a, b):
    hi, lo = _split(a)
    return _dot(hi, b) + _dot(lo, b)


def _sigmoid(x):
    return 1.0 / (1.0 + jnp.exp(-x))


def _modulated_norm(x, g, shift, scale):
    ms = jnp.mean(x * x, axis=-1, keepdims=True)
    y = x * lax.rsqrt(ms + RMS_EPS) * g
    return y * (1.0 + scale) + shift


def _params(*sem):
    return pltpu.CompilerParams(dimension_semantics=sem, vmem_limit_bytes=VMEM_LIMIT)


def _resident(shape):
    nd = len(shape)
    return pl.BlockSpec(shape, lambda *_: (0,) * nd, pipeline_mode=pl.Buffered(1))


def _ada_kernel(c_ref, w_ref, b_ref, o_ref):
    c = c_ref[...]
    ca = c * _sigmoid(c)
    o_ref[...] = jnp.dot(ca, w_ref[...], preferred_element_type=F32,
                         precision=lax.Precision.HIGHEST) + b_ref[...]


def _ada(c, ada_w, ada_b):
    depth, d, n = ada_w.shape
    b = c.shape[0]
    tn = 1152
    return pl.pallas_call(
        _ada_kernel,
        grid=(depth, n // tn),
        in_specs=[pl.BlockSpec((b, d), lambda l, j: (0, 0)),
                  pl.BlockSpec((None, d, tn), lambda l, j: (l, 0, j)),
                  pl.BlockSpec((None, 1, tn), lambda l, j: (l, 0, j))],
        out_specs=pl.BlockSpec((None, b, tn), lambda l, j: (l, 0, j)),
        out_shape=jax.ShapeDtypeStruct((depth, b, n), F32),
        compiler_params=_params("parallel", "parallel"),
        name="ada",
    )(c, ada_w, ada_b.reshape(depth, 1, n))


def _ffn_kernel(x_ref, g_ref, sh_ref, sc_ref, gt_ref, wa_ref, wb_ref, wo_ref, o_ref, y_ref, acc_ref):
    y_ref[...] = _modulated_norm(x_ref[...], g_ref[...], sh_ref[...], sc_ref[...]).astype(BF16)
    for c in range(FF_PAD // FF_CHUNK):
        sl = slice(c * FF_CHUNK, (c + 1) * FF_CHUNK)
        a = _dot(y_ref[...], wa_ref[:, sl])
        b = _dot(y_ref[...], wb_ref[:, sl])
        u = (a * _sigmoid(a) * b).astype(BF16)
        contrib = _dot(u, wo_ref[sl, :])
        if c == 0:
            acc_ref[...] = contrib
        else:
            acc_ref[...] += contrib
    o_ref[...] = x_ref[...] + 0.5 * gt_ref[...] * acc_ref[...]


def _mod_spec(tiles_per_batch, d):
    return pl.BlockSpec((None, 1, d), lambda i: (i // tiles_per_batch, 0, 0))


def _ffn(x, g, shift, scale, gate, wa, wb, wo, s):
    n, d = x.shape
    tpb = s // TM
    return pl.pallas_call(
        _ffn_kernel,
        grid=(n // TM,),
        in_specs=[pl.BlockSpec((TM, d), lambda i: (i, 0)),
                  _resident((1, d)),
                  _mod_spec(tpb, d), _mod_spec(tpb, d), _mod_spec(tpb, d),
                  _resident(wa.shape), _resident(wb.shape), _resident(wo.shape)],
        out_specs=pl.BlockSpec((TM, d), lambda i: (i, 0)),
        out_shape=jax.ShapeDtypeStruct((n, d), F32),
        scratch_shapes=[pltpu.VMEM((TM, d), BF16), pltpu.VMEM((TM, d), F32)],
        compiler_params=_params("parallel"),
        name="ffn",
    )(x, g, shift, scale, gate, wa, wb, wo)


def _proj_kernel(meta, n_out, x_ref, g_ref, sh_ref, sc_ref, w_ref, gain_ref, gm_ref, *rest):
    outs = rest[:n_out]
    y_ref = rest[n_out]
    y_ref[...] = _modulated_norm(x_ref[...], g_ref[...], sh_ref[...], sc_ref[...]).astype(BF16)
    for c, halves in enumerate(meta):
        sl = slice(c * PROJ_CHUNK, (c + 1) * PROJ_CHUNK)
        z = _dot(y_ref[...], w_ref[:, sl])
        if any(normed for _, _, normed in halves):
            msq = _dot_hilo(z * z, gm_ref[...])
            zn = z * lax.rsqrt(msq + RMS_EPS) * gain_ref[:, sl]
        for hf, (oi, off, normed) in enumerate(halves):
            src = zn if normed else z
            outs[oi][:, off:off + LANES] = src[:, hf * LANES:(hf + 1) * LANES].astype(outs[oi].dtype)


def _proj(x, g, shift, scale, w, gain, gm, meta, out_defs, s):
    n, d = x.shape
    tpb = s // TM
    return pl.pallas_call(
        functools.partial(_proj_kernel, meta, len(out_defs)),
        grid=(n // TM,),
        in_specs=[pl.BlockSpec((TM, d), lambda i: (i, 0)),
                  _resident((1, d)),
                  _mod_spec(tpb, d), _mod_spec(tpb, d),
                  _resident(w.shape), _resident(gain.shape), _resident(gm.shape)],
        out_specs=[pl.BlockSpec((TM, wd), lambda i: (i, 0)) for wd, _ in out_defs],
        out_shape=[jax.ShapeDtypeStruct((n, wd), dt) for wd, dt in out_defs],
        scratch_shapes=[pltpu.VMEM((TM, d), BF16)],
        compiler_params=_params("parallel"),
        name="proj",
    )(x, g, shift, scale, w, gain, gm)


def _mla_prep_kernel(in_ref, qg_ref, kvg_ref, wuq_ref, wuk_ref, wuv_ref, gm_ref, rot_ref,
                     gq_ref, gkn_ref, gkr_ref, cos_ref, sin_ref, q_out, k_out, v_out):
    def rms(z, g):
        return z * lax.rsqrt(jnp.mean(z * z, axis=-1, keepdims=True) + RMS_EPS) * g

    cqn = rms(in_ref[:, 0:MLA_Q_LORA], qg_ref[...]).astype(BF16)
    ckvn = rms(in_ref[:, MLA_Q_LORA:MLA_Q_LORA + MLA_KV_LORA], kvg_ref[...]).astype(BF16)
    c3 = in_ref[:, 3 * LANES:4 * LANES]
    cos = cos_ref[...]
    sin = sin_ref[...]
    gm = gm_ref[...]
    rot = rot_ref[...]

    def norm_rope(z, gain):
        msq = _dot_hilo(z * z, gm)
        z = z * lax.rsqrt(msq + RMS_EPS) * gain
        return z * cos + _dot_hilo(z, rot) * sin

    kr = norm_rope(jnp.concatenate([c3, c3], axis=1), gkr_ref[...])
    for h in range(MLA_HEADS // 2):
        sl = slice(h * 2 * LANES, (h + 1) * 2 * LANES)
        q_out[:, sl] = norm_rope(_dot(cqn, wuq_ref[:, sl]), gq_ref[...]).astype(BF16)
        k_out[:, sl] = (norm_rope(_dot(ckvn, wuk_ref[:, sl]), gkn_ref[...]) + kr).astype(BF16)
    v_out[...] = _dot(ckvn, wuv_ref[...]).astype(BF16)


def _mla_prep(mla_in, qg, kvg, wuq, wuk, wuv, gm, rot, gq, gkn, gkr, cos_t, sin_t, s):
    n = mla_in.shape[0]
    tpb = s // TM
    hw = MLA_HEADS * LANES
    vw = MLA_HEADS * MLA_V
    tab = pl.BlockSpec((TM, 2 * LANES), lambda i: (i % tpb, 0))
    consts = [qg, kvg, wuq, wuk, wuv, gm, rot, gq, gkn, gkr]
    return pl.pallas_call(
        _mla_prep_kernel,
        grid=(n // TM,),
        in_specs=[pl.BlockSpec((TM, 4 * LANES), lambda i: (i, 0))] + [_resident(a.shape) for a in consts] + [tab, tab],
        out_specs=[pl.BlockSpec((TM, hw), lambda i: (i, 0)), pl.BlockSpec((TM, hw), lambda i: (i, 0)),
                   pl.BlockSpec((TM, vw), lambda i: (i, 0))],
        out_shape=[jax.ShapeDtypeStruct((n, hw), BF16), jax.ShapeDtypeStruct((n, hw), BF16),
                   jax.ShapeDtypeStruct((n, vw), BF16)],
        compiler_params=_params("parallel"),
        name="mla_prep",
    )(mla_in, *consts, cos_t, sin_t)


def _attn_kernel_per_tile(cfg, q_ref, k_ref, vt_ref, tab_ref, *rest):
    n_tab, qw, scale2, backs, sel_cfg, want_lse = cfg
    if sel_cfg is not None:
        sel_ref, rest = rest[0], rest[1:]
        sel_bpt, sel_stride, sel_pair_mul = sel_cfg
    o_ref = rest[0]
    blk = pl.program_id(1)
    qi = pl.program_id(2)
    lane = lax.broadcasted_iota(jnp.int32, (1, LANES), 1)
    if qw == 1:
        q = q_ref[...]
        zero = jnp.zeros_like(q)
        qs = [jnp.where(lane < HEAD_DIM, q, zero), jnp.where(lane >= HEAD_DIM, q, zero)]
    else:
        qs = [q_ref[:, 0:LANES], q_ref[:, LANES:2 * LANES]]
    if backs is None:
        lo = 0
    else:
        back = jnp.int32(backs[-1])
        for bi in range(len(backs) - 2, -1, -1):
            back = jnp.where(blk == bi, jnp.int32(backs[bi]), back)
        lo = jnp.maximum(qi - back, 0)
    if sel_cfg is not None:
        selv = sel_ref[...]
        e_key = lax.broadcasted_iota(jnp.int32, (TK, LANES), 0) >> int(math.log2(TK // sel_bpt))
        e_col = lax.broadcasted_iota(jnp.int32, (TK, LANES), 1)

    def logits(j):
        start = pl.multiple_of(j * TK, TK)
        sts = []
        for s in range(2):
            kj = k_ref[pl.ds(start, TK), :] if qw == 1 else k_ref[pl.ds(start, TK), s * LANES:(s + 1) * LANES]
            st = _dot_nt(kj, qs[s]) * scale2
            if sel_cfg is not None:
                off = sel_stride * (s + sel_pair_mul * blk) + sel_bpt * j
                expand = jnp.where(e_col == e_key + off, 1.0, 0.0).astype(BF16)
                st = st + (_dot_nt(expand, selv) - 1.0) * (-NEG_INF)
            sts.append(st)
        return tuple(sts)

    def accumulate(j, pend, accs):
        start = pl.multiple_of(j * TK, TK)
        new = []
        for s in range(2):
            alpha, p = pend[s]
            vt = vt_ref[s * HEAD_DIM:(s + 1) * HEAD_DIM, pl.ds(start, TK)]
            new.append(alpha * accs[s] + _dot(vt, p))
        return tuple(new)

    def body(j, carry):
        stats, accs, st_cur, pend = carry
        st_next = logits(jnp.minimum(j + 1, qi))
        accs = accumulate(jnp.maximum(j - 1, lo), pend, accs)
        d = jnp.minimum(qi - j, n_tab - 1)
        new_stats, new_pend = [], []
        for s in range(2):
            m, l = stats[s]
            st = st_cur[s] + tab_ref[s, d]
            m_new = jnp.maximum(m, jnp.max(st, axis=0, keepdims=True))
            alpha = jnp.exp2(m - m_new)
            p = jnp.exp2(st - m_new)
            new_stats.append((m_new, alpha * l + jnp.sum(p, axis=0, keepdims=True)))
            new_pend.append((alpha, p.astype(BF16)))
        return tuple(new_stats), accs, st_next, tuple(new_pend)

    lo = jnp.int32(lo)
    init = (tuple((jnp.full((1, TQ), NEG_INF, F32), jnp.zeros((1, TQ), F32)) for _ in range(2)),
            tuple(jnp.zeros((HEAD_DIM, TQ), F32) for _ in range(2)),
            logits(lo),
            tuple((jnp.ones((1, TQ), F32), jnp.zeros((TK, TQ), BF16)) for _ in range(2)))
    stats, accs, _, pend = lax.fori_loop(lo, qi + 1, body, init)
    accs = accumulate(qi, pend, accs)
    out_t = jnp.concatenate([acc / l for acc, (_, l) in zip(accs, stats)], axis=0)
    o_ref[...] = out_t.T.astype(o_ref.dtype)
    if want_lse:
        lse_t = jnp.concatenate([jnp.broadcast_to((m + jnp.log2(l)) * math.log(2.0), (HEAD_DIM, TQ))
                                 for (m, l) in stats], axis=0)
        rest[1][...] = lse_t.T


def _attn_per_tile(q, k, vt, tab, sel, *, qc0, kc0, vb0, n_blk, qw, scale, backs=None, kv_shared=False,
                   tab_shared=False, sel_cfg=None, out_dtype=F32, want_lse=False):
    b, s = q.shape[:2]
    n_tab = tab.shape[2]
    cfg = (n_tab, qw, scale * LOG2E, backs, sel_cfg, want_lse)
    kidx = (lambda bb, h, i: (bb, 0, kc0)) if kv_shared else (lambda bb, h, i: (bb, 0, kc0 + h))
    vidx = (lambda bb, h, i: (bb, vb0, 0)) if kv_shared else (lambda bb, h, i: (bb, vb0 + h, 0))
    tidx = (lambda bb, h, i: (0, 0, 0, 0, 0)) if tab_shared else (lambda bb, h, i: (h, 0, 0, 0, 0))
    in_specs = [pl.BlockSpec((None, TQ, qw * LANES), lambda bb, h, i: (bb, i, qc0 + h)),
                pl.BlockSpec((None, s, qw * LANES), kidx),
                pl.BlockSpec((None, 2 * HEAD_DIM, s), vidx),
                pl.BlockSpec((None, 2, n_tab, TK, TQ), tidx)]
    args = [q, k, vt, tab]
    if sel_cfg is not None:
        in_specs.append(pl.BlockSpec((None, TQ, LANES), lambda bb, h, i: (bb, i, 0)))
        args.append(sel)
    ospec = pl.BlockSpec((None, TQ, LANES), lambda bb, h, i: (bb, i, h))
    out_specs = [ospec]
    out_shape = [jax.ShapeDtypeStruct((b, s, n_blk * LANES), out_dtype)]
    if want_lse:
        out_specs.append(ospec)
        out_shape.append(jax.ShapeDtypeStruct((b, s, n_blk * LANES), F32))
    return pl.pallas_call(
        functools.partial(_attn_kernel, cfg),
        grid=(b, n_blk, s // TQ),
        in_specs=in_specs,
        out_specs=out_specs,
        out_shape=out_shape,
        compiler_params=_params("parallel", "parallel", "arbitrary"),
        name="attn",
    )(*args)


def _attn_kernel(cfg, q_ref, k_ref, vt_ref, tab_ref, *rest):
    n_tab, qw, scale2, backs, sel_cfg, want_lse, n_qt = cfg
    if sel_cfg is not None:
        sel_ref, rest = rest[0], rest[1:]
        sel_bpt, sel_stride, sel_pair_mul = sel_cfg
    o_ref = rest[0]
    lse_ref = rest[1] if want_lse else None
    st_scr, p_scr = rest[-2], rest[-1]
    blk = pl.program_id(1)
    lane = lax.broadcasted_iota(jnp.int32, (1, LANES), 1)

    counts = tuple(sum(min(t, bk) + 1 for t in range(n_qt)) for bk in backs)
    back = jnp.int32(backs[-1])
    n_tiles = jnp.int32(counts[-1])
    for bi in range(len(backs) - 2, -1, -1):
        back = jnp.where(blk == bi, jnp.int32(backs[bi]), back)
        n_tiles = jnp.where(blk == bi, jnp.int32(counts[bi]), n_tiles)

    if sel_cfg is not None:
        e_key = lax.broadcasted_iota(jnp.int32, (TK, LANES), 0) >> int(math.log2(TK // sel_bpt))
        e_col = lax.broadcasted_iota(jnp.int32, (TK, LANES), 1)

    def first_key_tile(qi):
        return jnp.maximum(qi - back, 0)

    def logits_to(slot, qi, j):
        qrow = pl.multiple_of(qi * TQ, TQ)
        krow = pl.multiple_of(j * TK, TK)
        if qw == 1:
            q = q_ref[pl.ds(qrow, TQ), :]
            zero = jnp.zeros_like(q)
            qs = [jnp.where(lane < HEAD_DIM, q, zero), jnp.where(lane >= HEAD_DIM, q, zero)]
        else:
            qs = [q_ref[pl.ds(qrow, TQ), s * LANES:(s + 1) * LANES] for s in range(2)]
        if sel_cfg is not None:
            selv = sel_ref[pl.ds(qrow, TQ), :]
        for s in range(2):
            kj = k_ref[pl.ds(krow, TK), :] if qw == 1 else k_ref[pl.ds(krow, TK), s * LANES:(s + 1) * LANES]
            st = _dot_nt(kj, qs[s]) * scale2
            if sel_cfg is not None:
                off = sel_stride * (s + sel_pair_mul * blk) + sel_bpt * j
                expand = jnp.where(e_col == e_key + off, 1.0, 0.0).astype(BF16)
                st = st + (_dot_nt(expand, selv) - 1.0) * (-NEG_INF)
            st_scr[slot, s] = st

    def accumulate(slot, j, alphas, accs):
        krow = pl.multiple_of(j * TK, TK)
        new = []
        for s in range(2):
            vt = vt_ref[s * HEAD_DIM:(s + 1) * HEAD_DIM, pl.ds(krow, TK)]
            new.append(alphas[s] * accs[s] + _dot(vt, p_scr[slot, s]))
        return tuple(new)

    def finalize(qi, stats, accs):
        qrow = pl.multiple_of(qi * TQ, TQ)
        out_t = jnp.concatenate([acc / l for acc, (_, l) in zip(accs, stats)], axis=0)
        o_ref[pl.ds(qrow, TQ), :] = out_t.T.astype(o_ref.dtype)
        if want_lse:
            lse_t = jnp.concatenate([jnp.broadcast_to((m + jnp.log2(l)) * math.log(2.0), (HEAD_DIM, TQ))
                                     for (m, l) in stats], axis=0)
            lse_ref[pl.ds(qrow, TQ), :] = lse_t.T

    def advance(q, j):
        last = j == q
        at_end = jnp.logical_and(last, q == n_qt - 1)
        starts = jnp.logical_and(last, jnp.logical_not(at_end))
        qn = jnp.where(starts, q + 1, q)
        jn = jnp.where(at_end, j, jnp.where(last, first_key_tile(q + 1), j + 1))
        return qn, jn, starts, at_end

    def softmax(slot, tile, stats, accs):
        q, j, is_first, filler = tile
        keep = jnp.where(is_first, 0.0, 1.0)
        d = jnp.where(filler, n_tab, jnp.minimum(q - j, n_tab - 1))
        new_stats, alphas, new_accs = [], [], []
        for s in range(2):
            m, l = stats[s]
            m = jnp.where(is_first, NEG_INF, m)
            st = st_scr[slot, s] + tab_ref[s, d]
            m_new = jnp.maximum(m, jnp.max(st, axis=0, keepdims=True))
            alpha = jnp.exp2(m - m_new)
            p = jnp.exp2(st - m_new)
            new_stats.append((m_new, alpha * (l * keep) + jnp.sum(p, axis=0, keepdims=True)))
            alphas.append(alpha)
            new_accs.append(accs[s] * keep)
            p_scr[slot, s] = p.astype(BF16)
        return tuple(new_stats), tuple(alphas), tuple(new_accs)

    def body(u, carry):
        tile_a, (qp, jp), stats, alphas, accs = carry
        qa, ja, first_a, _ = tile_a
        tile_b = advance(qa, ja)
        qb, jb, first_b, _ = tile_b
        tile_c = advance(qb, jb)
        logits_to(1, qb, jb)
        accs_p = accumulate(1, jp, alphas, accs)
        stats_a, alphas_a, accs = softmax(0, tile_a, stats, accs_p)
        logits_to(0, tile_c[0], tile_c[1])
        accs_a = accumulate(0, ja, alphas_a, accs)
        stats_b, alphas_b, accs = softmax(1, tile_b, stats_a, accs_a)

        @pl.when(jnp.logical_and(first_a, u > 0))
        def _():
            finalize(qp, stats, accs_p)

        @pl.when(first_b)
        def _():
            finalize(qa, stats_a, accs_a)

        return tile_c, (qb, jb), stats_b, alphas_b, accs

    zero_i = jnp.int32(0)
    logits_to(0, zero_i, zero_i)
    p_scr[1] = jnp.zeros(p_scr.shape[1:], BF16)
    init = ((zero_i, zero_i, zero_i == 0, zero_i != 0), (zero_i, zero_i),
            tuple((jnp.full((1, TQ), NEG_INF, F32), jnp.zeros((1, TQ), F32)) for _ in range(2)),
            tuple(jnp.ones((1, TQ), F32) for _ in range(2)),
            tuple(jnp.zeros((HEAD_DIM, TQ), F32) for _ in range(2)))
    _, (qp, jp), stats, alphas, accs = lax.fori_loop(0, (n_tiles + 1) // 2, body, init)
    accs = accumulate(1, jp, alphas, accs)
    finalize(qp, stats, accs)


def _attn(q, k, vt, tab, sel, *, qc0, kc0, vb0, n_blk, qw, scale, backs=None, kv_shared=False,
          tab_shared=False, sel_cfg=None, out_dtype=F32, want_lse=False):
    b, s = q.shape[:2]
    n_tab = tab.shape[2] - 1
    n_qt = s // TQ
    if backs is None:
        backs = (n_qt,)
    cfg = (n_tab, qw, scale * LOG2E, backs, sel_cfg, want_lse, n_qt)
    kidx = (lambda bb, h: (bb, 0, kc0)) if kv_shared else (lambda bb, h: (bb, 0, kc0 + h))
    vidx = (lambda bb, h: (bb, vb0, 0)) if kv_shared else (lambda bb, h: (bb, vb0 + h, 0))
    tidx = (lambda bb, h: (0, 0, 0, 0, 0)) if tab_shared else (lambda bb, h: (h, 0, 0, 0, 0))
    in_specs = [pl.BlockSpec((None, s, qw * LANES), lambda bb, h: (bb, 0, qc0 + h)),
                pl.BlockSpec((None, s, qw * LANES), kidx),
                pl.BlockSpec((None, 2 * HEAD_DIM, s), vidx),
                pl.BlockSpec((None, 2, n_tab + 1, TK, TQ), tidx)]
    args = [q, k, vt, tab]
    if sel_cfg is not None:
        in_specs.append(pl.BlockSpec((None, s, LANES), lambda bb, h: (bb, 0, 0)))
        args.append(sel)
    ospec = pl.BlockSpec((None, s, LANES), lambda bb, h: (bb, 0, h))
    out_specs = [ospec]
    out_shape = [jax.ShapeDtypeStruct((b, s, n_blk * LANES), out_dtype)]
    if want_lse:
        out_specs.append(ospec)
        out_shape.append(jax.ShapeDtypeStruct((b, s, n_blk * LANES), F32))
    return pl.pallas_call(
        functools.partial(_attn_kernel, cfg),
        grid=(b, n_blk),
        in_specs=in_specs,
        out_specs=out_specs,
        out_shape=out_shape,
        scratch_shapes=[pltpu.VMEM((2, 2, TK, TQ), F32), pltpu.VMEM((2, 2, TK, TQ), BF16)],
        compiler_params=_params("parallel", "parallel"),
        name="attn",
    )(*args)


def _nsa_cmp_kernel(kc_ref, vc_ref, pe_ref, wlo_ref, whi_ref, w2_ref, gm_ref, gain_ref, kcmp_ref, vcmp_ref):
    def compress(c, j):
        lo = _dot((c + pe_ref[j, 0]).astype(BF16), wlo_ref[j])
        hi = _dot((c + pe_ref[j, 1]).astype(BF16), whi_ref[j])
        h = lo + pltpu.roll(hi, hi.shape[0] - 1, 0)
        h = h * _sigmoid(h)
        return _dot(h.astype(BF16), w2_ref[j])

    kz = compress(kc_ref[...], 0)
    msq = _dot_hilo(kz * kz, gm_ref[...])
    kcmp_ref[...] = (kz * lax.rsqrt(msq + RMS_EPS) * gain_ref[...]).astype(BF16)
    vcmp_ref[...] = compress(vc_ref[...], 1).astype(BF16)


def _nsa_cmp(kc2, vc2, pe, wlo, whi, w2, gm, gain):
    b, nch, width = kc2.shape
    consts = [pe, wlo, whi, w2, gm, gain]
    blk = pl.BlockSpec((None, nch, width), lambda i: (i, 0, 0))
    oblk = pl.BlockSpec((None, nch, LANES), lambda i: (i, 0, 0))
    return pl.pallas_call(
        _nsa_cmp_kernel,
        grid=(b,),
        in_specs=[blk, blk] + [_resident(a.shape) for a in consts],
        out_specs=[oblk, oblk],
        out_shape=[jax.ShapeDtypeStruct((b, nch, LANES), BF16)] * 2,
        compiler_params=_params("parallel"),
        name="nsa_cmp",
    )(kc2, vc2, *consts)


def _nsa_sel_kernel(q_ref, kcmp_ref, vcmp_ref, bias_ref, ovl_ref, oc_ref, sel_ref):
    qi = pl.program_id(1)
    scale = HEAD_DIM ** -0.5
    lane = lax.broadcasted_iota(jnp.int32, (1, LANES), 1)
    t = qi * TQ + lax.broadcasted_iota(jnp.int32, (TQ, 1), 0)
    mask_c = (NSA_CMP_STRIDE * lane + NSA_CMP_LEN - 1) <= t
    kcmp = kcmp_ref[...]
    vcmp = vcmp_ref[...]
    heads = [(p, g) for p in range(NSA_HPG) for g in range(NSA_GROUPS)]
    raw = []
    for p, g in heads:
        qp = q_ref[:, p * LANES:(p + 1) * LANES]
        raw.append(_dot_nt(jnp.where((lane >> 6) == g, qp, jnp.zeros_like(qp)), kcmp))
    pcs = []
    for (p, g), r in zip(heads, raw):
        lg = jnp.where(mask_c, r * scale + bias_ref[p * NSA_GROUPS + g], NEG_INF)
        m = jnp.max(lg, axis=-1, keepdims=True)
        e = jnp.where(mask_c, jnp.exp(lg - m), 0.0)
        den = jnp.maximum(jnp.sum(e, axis=-1, keepdims=True), 1e-30)
        pcs.append(e / den)
    imp = jnp.zeros((TQ, LANES), F32)
    ocs = []
    for (p, g), pc in zip(heads, pcs):
        ocs.append(_dot(pc.astype(BF16), vcmp))
        imp = imp + _dot_hilo(pc, ovl_ref[g])
    for p in range(NSA_HPG):
        oc_ref[:, p * LANES:(p + 1) * LANES] = jnp.where(lane < HEAD_DIM, ocs[2 * p], ocs[2 * p + 1])

    n_slc = 32
    ids = lane & (n_slc - 1)
    cur = t >> 6
    forced = (ids == 0) | (ids == cur) | (ids == cur - 1)
    score = jnp.where(forced, NSA_FORCED, jnp.where(ids <= cur, imp, NEG_INF))
    cnt = jnp.zeros((TQ, LANES), jnp.int32)
    for mp in range(n_slc):
        colb = jnp.where(lane < n_slc, score[:, mp:mp + 1], score[:, n_slc + mp:n_slc + mp + 1])
        tie = jnp.where(mp < ids, 1, 0)
        cnt = cnt + jnp.where(colb > score, 1, jnp.where(colb == score, tie, 0))
    keep = jnp.where(cnt < NSA_SLC_TOP, jnp.where(score > 0.5 * NEG_INF, 1.0, 0.0), 0.0)
    sel_ref[...] = jnp.where(lane < 2 * n_slc, keep, 0.0).astype(BF16)


def _nsa_sel(q, kcmp, vcmp, bias_c, ovl):
    b, s, w = q.shape
    return pl.pallas_call(
        _nsa_sel_kernel,
        grid=(b, s // TQ),
        in_specs=[pl.BlockSpec((None, TQ, w), lambda bb, i: (bb, i, 0)),
                  pl.BlockSpec((None, LANES, LANES), lambda bb, i: (bb, 0, 0)),
                  pl.BlockSpec((None, LANES, LANES), lambda bb, i: (bb, 0, 0)),
                  pl.BlockSpec((NSA_HEADS, TQ, LANES), lambda bb, i: (0, i, 0)),
                  _resident(ovl.shape)],
        out_specs=[pl.BlockSpec((None, TQ, w), lambda bb, i: (bb, i, 0)),
                   pl.BlockSpec((None, TQ, LANES), lambda bb, i: (bb, i, 0))],
        out_shape=[jax.ShapeDtypeStruct((b, s, w), F32), jax.ShapeDtypeStruct((b, s, LANES), BF16)],
        compiler_params=_params("parallel", "parallel"),
        name="nsa_sel",
    )(q, kcmp, vcmp, bias_c, ovl)


def _moba_gate_kernel(q_ref, k_ref, avg_ref, sel_ref):
    qi = pl.program_id(1)
    nb = 8
    kmean = _dot(avg_ref[...], k_ref[...])
    row = lax.broadcasted_iota(jnp.int32, kmean.shape, 0)
    col = lax.broadcasted_iota(jnp.int32, kmean.shape, 1)
    kmean = jnp.where((row >> 3) == (col >> 6), kmean, 0.0)
    kh, kl = _split(kmean)
    q = q_ref[...]
    gate = _dot_nt(q, kh) + _dot_nt(q, kl)
    lane = lax.broadcasted_iota(jnp.int32, (1, LANES), 1)
    ids = lane & (nb - 1)
    head = lane >> 3
    past = (head < MOBA_HEADS) & (ids < qi)
    score = jnp.where(past, gate, NEG_INF)
    cnt = jnp.zeros((TQ, LANES), jnp.int32)
    for mp in range(nb):
        colb = score[:, (MOBA_HEADS - 1) * nb + mp:(MOBA_HEADS - 1) * nb + mp + 1]
        for h in range(MOBA_HEADS - 2, -1, -1):
            colb = jnp.where(head == h, score[:, h * nb + mp:h * nb + mp + 1], colb)
        tie = jnp.where(mp < ids, 1, 0)
        cnt = cnt + jnp.where(colb > score, 1, jnp.where(colb == score, tie, 0))
    keep = jnp.where(past, jnp.where(cnt < MOBA_TOP, 1.0, 0.0), 0.0)
    keep = jnp.where(head < MOBA_HEADS, jnp.where(ids == qi, 1.0, keep), 0.0)
    sel_ref[...] = jnp.broadcast_to(keep, (TQ, LANES)).astype(BF16)


def _moba_gate(p_arr, avg, qc, kc):
    b, s, _ = p_arr.shape
    w = MOBA_HEADS * HEAD_DIM
    return pl.pallas_call(
        _moba_gate_kernel,
        grid=(b, s // TQ),
        in_specs=[pl.BlockSpec((None, TQ, w), lambda bb, i: (bb, i, qc)),
                  pl.BlockSpec((None, s, w), lambda bb, i: (bb, 0, kc)),
                  _resident(avg.shape)],
        out_specs=pl.BlockSpec((None, TQ, LANES), lambda bb, i: (bb, i, 0)),
        out_shape=jax.ShapeDtypeStruct((b, s, LANES), BF16),
        compiler_params=_params("parallel", "parallel"),
        name="moba_gate",
    )(p_arr, p_arr, avg)


def _even_out_kernel(x_ref, oa_ref, oc_ref, os_ref, ow_ref, gl_ref, eg_ref, wa_ref, wb_ref, gt_ref, o_ref):
    sg = _sigmoid(gl_ref[...])
    hi, lo = _split(sg)
    nsa = None
    for br, src in enumerate((oc_ref, os_ref, ow_ref)):
        gexp = _dot(hi, eg_ref[br]) + _dot(lo, eg_ref[br])
        term = gexp * src[...]
        nsa = term if nsa is None else nsa + term
    m = _dot(oa_ref[...], wa_ref[...]) + _dot(nsa.astype(BF16), wb_ref[...])
    o_ref[...] = x_ref[...] + gt_ref[...] * m


def _even_out(x, o_a, o_c, o_s, o_w, mla_in, eg, wa, wb, gate, s):
    n, d = x.shape
    tpb = s // TM
    row = lambda wd: pl.BlockSpec((TM, wd), lambda i: (i, 0))
    return pl.pallas_call(
        _even_out_kernel,
        grid=(n // TM,),
        in_specs=[row(d), row(o_a.shape[1]), row(o_c.shape[1]), row(o_s.shape[1]), row(o_w.shape[1]),
                  pl.BlockSpec((TM, LANES), lambda i: (i, 3)),
                  _resident(eg.shape), _resident(wa.shape), _resident(wb.shape), _mod_spec(tpb, d)],
        out_specs=row(d),
        out_shape=jax.ShapeDtypeStruct((n, d), F32),
        compiler_params=_params("parallel"),
        name="even_out",
    )(x, o_a, o_c, o_s, o_w, mla_in, eg, wa, wb, gate)


def _odd_out_kernel(x_ref, od_ref, lse_ref, om_ref, wd_ref, wm_ref, gt_ref, o_ref):
    w = DIL_HPG * HEAD_DIM
    ls = [lse_ref[:, g * w:(g + 1) * w] for g in range(len(DIL_PAIRS))]
    mx = jnp.maximum(jnp.maximum(ls[0], ls[1]), ls[2])
    es = [jnp.exp(l - mx) for l in ls]
    den = es[0] + es[1] + es[2]
    merged = None
    for g in range(len(DIL_PAIRS)):
        term = (es[g] / den) * od_ref[:, g * w:(g + 1) * w]
        merged = term if merged is None else merged + term
    m = _dot(merged.astype(BF16), wd_ref[...]) + _dot(om_ref[...], wm_ref[...])
    o_ref[...] = x_ref[...] + gt_ref[...] * m


def _odd_out(x, o_d, lse_d, o_m, wd, wm, gate, s):
    n, d = x.shape
    tpb = s // TM
    row = lambda wd_: pl.BlockSpec((TM, wd_), lambda i: (i, 0))
    return pl.pallas_call(
        _odd_out_kernel,
        grid=(n // TM,),
        in_specs=[row(d), row(o_d.shape[1]), row(lse_d.shape[1]), row(o_m.shape[1]),
                  _resident(wd.shape), _resident(wm.shape), _mod_spec(tpb, d)],
        out_specs=row(d),
        out_shape=jax.ShapeDtypeStruct((n, d), F32),
        compiler_params=_params("parallel"),
        name="odd_out",
    )(x, o_d, lse_d, o_m, wd, wm, gate)


def _t5_bucket(dist):
    n = jnp.maximum(jnp.asarray(dist, jnp.int32), 0)
    nf = jnp.maximum(n, 1).astype(F32)
    large = T5_MAX_EXACT + (jnp.log(nf / T5_MAX_EXACT) / math.log(T5_MAX_DIST / T5_MAX_EXACT)
                            * (NUM_BUCKETS - T5_MAX_EXACT)).astype(jnp.int32)
    return jnp.where(n < T5_MAX_EXACT, n, jnp.minimum(large, NUM_BUCKETS - 1))


TOEP_PERIOD = 4 * TQ


def _toeplitz_dist():
    j = np.arange(TOEP_PERIOD)
    return np.where(j < 3 * TQ, j, j - TOEP_PERIOD)


def _toeplitz_tiles(u, n_tab):
    h = u.shape[0]
    m = jnp.tile(u, (1, TK))[:, :TK * (TOEP_PERIOD - 1)].reshape(h, TK, TOEP_PERIOD - 1)[:, :, :n_tab * TQ]
    tiles = jnp.transpose(m.reshape(h, TK, n_tab, TQ), (0, 2, 1, 3))
    filler = jnp.full((h, 1, TK, TQ), NEG_INF * LOG2E, F32)
    return jnp.concatenate([tiles, filler], axis=1)


def _bias_tiles(t5_cols, ok, n_tab=3):
    dist = _toeplitz_dist()
    bias = jnp.transpose(t5_cols[_t5_bucket(dist)])
    u = jnp.where(jnp.asarray(ok)[None], bias, NEG_INF) * LOG2E
    return _toeplitz_tiles(u, n_tab)


def _pair(tiles):
    h = tiles.shape[0]
    return tiles.reshape(h // 2, 2, *tiles.shape[1:])


def _take_cols(w, idx):
    idx = np.asarray(idx)
    out = jnp.take(w, jnp.asarray(np.maximum(idx, 0)), axis=-1)
    return jnp.where(jnp.asarray(idx >= 0), out, 0.0)


def _take_rows(w, idx):
    return jnp.swapaxes(_take_cols(jnp.swapaxes(w, -1, -2), idx), -1, -2)


def _group_mean_matrix(sizes, width=LANES):
    gm = np.zeros((width, width), np.float32)
    o = 0
    for sz in sizes:
        gm[o:o + sz, o:o + sz] = 1.0 / sz
        o += sz
    return jnp.asarray(gm, BF16)


def _even_layout():
    idx = -np.ones(14 * LANES, np.int64)
    idx[0:256] = np.arange(0, 256)
    idx[256:384] = np.arange(256, 384)
    idx[384:384 + 24] = np.arange(1696, 1720)
    idx[384 + 64:384 + 96] = np.arange(384, 416)
    for p in range(NSA_HPG):
        for g in range(NSA_GROUPS):
            dst = 512 + p * LANES + g * HEAD_DIM
            idx[dst:dst + HEAD_DIM] = 416 + (g * NSA_HPG + p) * HEAD_DIM + np.arange(HEAD_DIM)
    for i in range(6):
        idx[1024 + i * LANES:1024 + (i + 1) * LANES] = 928 + i * LANES + np.arange(LANES)
    meta = [(0, 0, False), (0, 128, False), (0, 256, False), (0, 384, False),
            (1, 0, True), (1, 128, True), (1, 256, True), (1, 384, True),
            (2, 0, False), (3, 0, False), (4, 0, True), (5, 0, False), (6, 0, True), (7, 0, False)]
    out_defs = [(512, F32), (512, BF16), (128, F32), (128, F32), (128, BF16), (128, BF16), (128, BF16), (128, BF16)]
    return idx, tuple(meta), out_defs


def kernel(x, c, t5_bias, ada_w, ada_b, norm_g, ffn_w_in, ffn_w_out, ev_w_in, ev_w_out, mla_q_norm_g,
           mla_kv_norm_g, mla_w_uq, mla_w_ukv, mla_qk_g, nsa_cmp_pe, nsa_cmp_w1, nsa_cmp_w2, nsa_qk_g,
           od_w_in, od_w_out, dil_qk_g, moba_qk_g):
    b, s, d = x.shape
    assert (s, d) == (2048, D_MODEL) and s % TM == 0 and TQ == MOBA_BLOCK and TQ == TK
    n = b * s
    hd = HEAD_DIM

    mod = _ada(c, ada_w, ada_b).reshape(DEPTH, b, 3, 3, 1, d)

    dist = _toeplitz_dist()
    causal = dist >= 0
    gm64 = _group_mean_matrix((hd, hd))
    gm_proj = _group_mean_matrix((hd,) * (PROJ_CHUNK // hd), PROJ_CHUNK)
    per_chunk = PROJ_CHUNK // LANES
    chunked = lambda m: tuple(tuple(m[i:i + per_chunk]) for i in range(0, len(m), per_chunk))

    padc = FF_PAD - D_FF
    wa_all = jnp.pad(ffn_w_in[..., :D_FF], ((0, 0), (0, 0), (0, 0), (0, padc))).astype(BF16)
    wb_all = jnp.pad(ffn_w_in[..., D_FF:], ((0, 0), (0, 0), (0, 0), (0, padc))).astype(BF16)
    wo_all = jnp.pad(ffn_w_out, ((0, 0), (0, 0), (0, padc), (0, 0))).astype(BF16)

    ev_idx, ev_meta, ev_outs = _even_layout()
    nsa_tab = t5_bias[:, MLA_HEADS:MLA_HEADS + NSA_HEADS].reshape(NUM_BUCKETS, NSA_GROUPS, NSA_HPG)
    nsa_cols = jnp.transpose(nsa_tab, (0, 2, 1)).reshape(NUM_BUCKETS, NSA_HEADS)
    tab_sel = _pair(_bias_tiles(nsa_cols, causal))
    tab_win = _pair(_bias_tiles(nsa_cols, causal & (dist <= NSA_WINDOW - 1)))
    tab_mla = _toeplitz_tiles(jnp.where(jnp.asarray(causal), 0.0, NEG_INF).astype(F32)[None], 2)
    tab_mla = jnp.broadcast_to(tab_mla[None], (1, 2) + tab_mla.shape[1:])
    n_cmp_pad = s // NSA_CMP_STRIDE
    jj = np.arange(2 * n_cmp_pad)
    kk = np.where(jj < n_cmp_pad, jj, jj - 2 * n_cmp_pad)
    dc = NSA_CMP_STRIDE * (-kk)[None, :] + np.arange(NSA_CMP_STRIDE)[:, None] - (NSA_CMP_LEN - 1)
    ub = jnp.moveaxis(nsa_cols[_t5_bucket(dc)], -1, 0)
    mb = jnp.tile(ub, (1, 1, n_cmp_pad))[:, :, :n_cmp_pad * (2 * n_cmp_pad - 1)]
    mb = mb.reshape(NSA_HEADS, NSA_CMP_STRIDE, n_cmp_pad, 2 * n_cmp_pad - 1)[..., :n_cmp_pad]
    bias_c = jnp.transpose(mb, (0, 2, 1, 3)).reshape(NSA_HEADS, s, n_cmp_pad)
    n_cmp = (s - NSA_CMP_LEN) // NSA_CMP_STRIDE + 1
    cstart = np.arange(n_cmp) * NSA_CMP_STRIDE
    sstart = np.arange(s // NSA_SLC_BLOCK) * NSA_SLC_BLOCK
    overlap = np.clip(np.minimum(cstart[:, None] + NSA_CMP_LEN, sstart[None, :] + NSA_SLC_BLOCK)
                      - np.maximum(cstart[:, None], sstart[None, :]), 0, None).astype(np.float32) / NSA_CMP_LEN
    ovl = np.zeros((NSA_GROUPS, LANES, LANES), np.float32)
    for g in range(NSA_GROUPS):
        ovl[g, :n_cmp, 32 * g:32 * g + 32] = overlap
    ovl = jnp.asarray(ovl, BF16)
    eg = np.zeros((3, LANES, NSA_HEADS * hd), np.float32)
    for g in range(NSA_GROUPS):
        for p in range(NSA_HPG):
            for br in range(3):
                eg[br, (g * NSA_HPG + p) * 3 + br, p * LANES + g * hd:p * LANES + (g + 1) * hd] = 1.0
    eg = jnp.asarray(eg, BF16)
    gm_mla = _group_mean_matrix((MLA_NOPE, MLA_ROPE))
    eye2 = jnp.eye(2, dtype=BF16)
    rot = np.zeros((LANES, LANES), np.float32)
    half = MLA_ROPE // 2
    for i in range(half):
        rot[MLA_NOPE + half + i, MLA_NOPE + i] = -1.0
        rot[MLA_NOPE + i, MLA_NOPE + half + i] = 1.0
    rot = jnp.asarray(rot, BF16)
    inv = ROPE_THETA ** (-jnp.arange(0, MLA_ROPE, 2, dtype=F32) / MLA_ROPE)
    ang = jnp.arange(s, dtype=F32)[:, None] * inv[None, :]
    ones = jnp.ones((s, MLA_NOPE), F32)
    cos_t = jnp.concatenate([ones, jnp.cos(ang), jnp.cos(ang), jnp.ones((s, LANES - MLA_NOPE - MLA_ROPE), F32)], axis=1)
    sin_t = jnp.concatenate([0 * ones, jnp.sin(ang), jnp.sin(ang), jnp.zeros((s, LANES - MLA_NOPE - MLA_ROPE), F32)], axis=1)
    uq_idx = -np.ones(MLA_HEADS * LANES, np.int64)
    uk_idx = -np.ones(MLA_HEADS * LANES, np.int64)
    uv_idx = np.zeros(MLA_HEADS * MLA_V, np.int64)
    for h in range(MLA_HEADS):
        uq_idx[h * LANES:h * LANES + 96] = h * 96 + np.arange(96)
        uk_idx[h * LANES:h * LANES + 64] = h * 128 + np.arange(64)
        uv_idx[h * MLA_V:(h + 1) * MLA_V] = h * 128 + 64 + np.arange(64)
    ob_idx = np.zeros(NSA_HEADS * hd, np.int64)
    for p in range(NSA_HPG):
        for g in range(NSA_GROUPS):
            ob_idx[p * LANES + g * hd:p * LANES + (g + 1) * hd] = MLA_HEADS * MLA_V + (g * NSA_HPG + p) * hd + np.arange(hd)

    dil_tabs = []
    for gi, (w, r) in enumerate(DIL_PAIRS):
        ok = causal & (dist <= w) & (dist % r == 0)
        dil_tabs.append(_bias_tiles(t5_bias[:, gi * DIL_HPG:(gi + 1) * DIL_HPG], ok))
    tab_dil = _pair(jnp.concatenate(dil_tabs, axis=0))
    dil_backs = (1, 1, 2, 2, s // TK, s // TK)
    tab_moba = _pair(_bias_tiles(t5_bias[:, DIL_SLOTS:DIL_SLOTS + MOBA_HEADS], causal))
    avg = np.zeros((LANES, s), np.float32)
    for h in range(MOBA_HEADS):
        for m in range(s // MOBA_BLOCK):
            avg[8 * h + m, m * MOBA_BLOCK:(m + 1) * MOBA_BLOCK] = 1.0 / MOBA_BLOCK
    avg = jnp.asarray(avg, BF16)
    od_meta = tuple((0, cidx * LANES, (cidx < 12) or (18 <= cidx < 22)) for cidx in range(24))

    sh3 = lambda a: a.reshape(b, s, a.shape[-1])
    tr3 = lambda a: jnp.swapaxes(sh3(a), 1, 2)
    xf = x.reshape(n, d)
    for i in range(DEPTH):
        j = i // 2
        g_i = norm_g[i].reshape(3, 1, d)
        xf = _ffn(xf, g_i[0], mod[i, :, 0, 0], mod[i, :, 0, 1], mod[i, :, 0, 2],
                  wa_all[i, 0], wb_all[i, 0], wo_all[i, 0], s)
        if i % 2 == 0:
            w_in = _take_cols(ev_w_in[j], ev_idx).astype(BF16)
            gain = jnp.ones((14 * LANES,), F32)
            gain = gain.at[512:1024].set(jnp.tile(nsa_qk_g[j, 0], 8))
            gain = gain.at[1280:1408].set(jnp.tile(nsa_qk_g[j, 1], 2))
            gain = gain.at[1536:1664].set(jnp.tile(nsa_qk_g[j, 1], 2))
            mla_in, nsa_q, kc, vc, ks, vs, kw, vw = _proj(
                xf, g_i[1], mod[i, :, 1, 0], mod[i, :, 1, 1], w_in, gain[None], gm_proj, chunked(ev_meta), ev_outs, s)
            zpad = jnp.zeros((LANES - MLA_NOPE - MLA_ROPE,), F32)
            gq = jnp.concatenate([mla_qk_g[j, 0], zpad])[None]
            gkn = jnp.concatenate([mla_qk_g[j, 1, :MLA_NOPE], jnp.zeros((LANES - MLA_NOPE,), F32)])[None]
            gkr = jnp.concatenate([jnp.zeros((MLA_NOPE,), F32), mla_qk_g[j, 1, MLA_NOPE:], zpad])[None]
            qf, kf, vf = _mla_prep(
                mla_in, mla_q_norm_g[j][None], mla_kv_norm_g[j][None],
                _take_cols(mla_w_uq[j], uq_idx).astype(BF16), _take_cols(mla_w_ukv[j], uk_idx).astype(BF16),
                _take_cols(mla_w_ukv[j], uv_idx).astype(BF16), jnp.kron(eye2, gm_mla), jnp.kron(eye2, rot),
                jnp.tile(gq, (1, 2)), jnp.tile(gkn, (1, 2)), jnp.tile(gkr, (1, 2)),
                jnp.tile(cos_t, (1, 2)), jnp.tile(sin_t, (1, 2)), s)
            (o_a,) = _attn(sh3(qf), sh3(kf), tr3(vf), tab_mla, None, qc0=0, kc0=0, vb0=0, n_blk=MLA_HEADS // 2, qw=2,
                           scale=(MLA_NOPE + MLA_ROPE) ** -0.5, tab_shared=True, out_dtype=BF16)
            pe = nsa_cmp_pe[j]
            pe2 = jnp.broadcast_to(pe.reshape(2, 2, 16, 1, hd), (2, 2, 16, NSA_GROUPS, hd)).reshape(2, 2, 1, 16 * LANES)
            w1 = nsa_cmp_w1[j].reshape(2, 2, 16, hd, NSA_CMP_HID)
            eye = jnp.eye(NSA_GROUPS, dtype=F32)
            w1x = jnp.einsum('jaldc,gh->jalgdhc', w1, eye).reshape(2, 2, 16 * LANES, NSA_GROUPS * NSA_CMP_HID).astype(BF16)
            w2x = jnp.einsum('jcd,gh->jgchd', nsa_cmp_w2[j], eye).reshape(2, NSA_GROUPS * NSA_CMP_HID, LANES).astype(BF16)
            kcmp, vcmp = _nsa_cmp(kc.reshape(b, n_cmp_pad, 16 * LANES), vc.reshape(b, n_cmp_pad, 16 * LANES),
                                  pe2, w1x[:, 0], w1x[:, 1], w2x, gm64, jnp.tile(nsa_qk_g[j, 1], 2)[None])
            o_c, sel = _nsa_sel(sh3(nsa_q), kcmp, vcmp, bias_c, ovl)
            (o_s,) = _attn(sh3(nsa_q), sh3(ks), tr3(vs), tab_sel, sel, qc0=0, kc0=0, vb0=0, n_blk=NSA_HPG, qw=1,
                           scale=hd ** -0.5, kv_shared=True, sel_cfg=(TK // NSA_SLC_BLOCK, 32, 0))
            (o_w,) = _attn(sh3(nsa_q), sh3(kw), tr3(vw), tab_win, None, qc0=0, kc0=0, vb0=0, n_blk=NSA_HPG, qw=1,
                           scale=hd ** -0.5, kv_shared=True, backs=(2, 2, 2, 2))
            wa_o = ev_w_out[j, :MLA_HEADS * MLA_V].astype(BF16)
            wb_o = _take_rows(ev_w_out[j], ob_idx).astype(BF16)
            xf = _even_out(xf, o_a.reshape(n, -1), o_c.reshape(n, -1), o_s.reshape(n, -1), o_w.reshape(n, -1),
                           mla_in, eg, wa_o, wb_o, mod[i, :, 1, 2], s)
        else:
            gain = jnp.concatenate([jnp.tile(dil_qk_g[j, 0], 12), jnp.tile(dil_qk_g[j, 1], 12), jnp.ones((768,), F32),
                                    jnp.tile(moba_qk_g[j, 0], 4), jnp.tile(moba_qk_g[j, 1], 4), jnp.ones((256,), F32)])
            (pr,) = _proj(xf, g_i[1], mod[i, :, 1, 0], mod[i, :, 1, 1], od_w_in[j].astype(BF16), gain[None], gm_proj,
                          chunked(od_meta), [(24 * LANES, BF16)], s)
            pr3 = pr.reshape(b, s, 24 * LANES)
            vdt = jnp.swapaxes(pr3[:, :, 12 * LANES:18 * LANES], 1, 2)
            vmt = jnp.swapaxes(pr3[:, :, 22 * LANES:24 * LANES], 1, 2)
            o_d, lse_d = _attn(pr3, pr3, vdt, tab_dil, None, qc0=0, kc0=6, vb0=0, n_blk=6, qw=1, scale=hd ** -0.5,
                               backs=dil_backs, want_lse=True)
            selm = _moba_gate(pr3, avg, 9, 10)
            (o_m,) = _attn(pr3, pr3, vmt, tab_moba, selm, qc0=18, kc0=20, vb0=0, n_blk=2, qw=1, scale=hd ** -0.5,
                           sel_cfg=(1, 8, 2), out_dtype=BF16)
            xf = _odd_out(xf, o_d.reshape(n, -1), lse_d.reshape(n, -1), o_m.reshape(n, -1),
                          od_w_out[j, :DIL_HPG * hd].astype(BF16), od_w_out[j, DIL_HPG * hd:].astype(BF16),
                          mod[i, :, 1, 2], s)
        xf = _ffn(xf, g_i[2], mod[i, :, 2, 0], mod[i, :, 2, 1], mod[i, :, 2, 2],
                  wa_all[i, 1], wb_all[i, 1], wo_all[i, 1], s)
    return xf.reshape(b, s, d)
```

```python
import functools
import math

import numpy as np
import jax
import jax.numpy as jnp
from jax import lax
from jax.experimental import pallas as pl
from jax.experimental.pallas import tpu as pltpu

F32 = jnp.float32
BF16 = jnp.bfloat16

D_MODEL = 1024
DEPTH = 4
D_FF = 2752
HEAD_DIM = 64
NUM_BUCKETS = 32
T5_MAX_EXACT = 16
T5_MAX_DIST = 128
RMS_EPS = 1e-6
NEG_INF = -1e30
MLA_HEADS = 8
MLA_NOPE = 64
MLA_ROPE = 32
MLA_V = 64
MLA_Q_LORA = 256
MLA_KV_LORA = 128
ROPE_THETA = 10000.0
NSA_HEADS = 8
NSA_GROUPS = 2
NSA_HPG = 4
NSA_CMP_LEN = 32
NSA_CMP_STRIDE = 16
NSA_CMP_HID = 256
NSA_SLC_BLOCK = 64
NSA_SLC_TOP = 8
NSA_WINDOW = 512
NSA_FORCED = 1e6
DIL_PAIRS = ((128, 1), (512, 4), (2048, 16))
DIL_HPG = 4
DIL_SLOTS = len(DIL_PAIRS) * DIL_HPG
MOBA_HEADS = 4
MOBA_BLOCK = 256
MOBA_TOP = 3

LANES = 128
V7X_VMEM_BYTES = 64 * 1024 * 1024
VMEM_LIMIT = V7X_VMEM_BYTES * 7 // 8
TM = 512
TQ = 256
TK = 256
FF_CHUNK = 256
FF_PAD = -(-D_FF // FF_CHUNK) * FF_CHUNK
PROJ_CHUNK = 256
LOG2E = math.log2(math.e)
LN2 = math.log(2.0)


def _dot(a, b):
    return jnp.dot(a, b, preferred_element_type=F32)


def _dot_nt(a, b):
    return lax.dot_general(a, b, (((1,), (1,)), ((), ())), preferred_element_type=F32)


def _split(a):
    hi = a.astype(BF16)
    lo = (a - hi.astype(F32)).astype(BF16)
    return hi, lo


def _dot_hilo(a, b):
    hi, lo = _split(a)
    return _dot(hi, b) + _dot(lo, b)


def _sigmoid(x):
    return 1.0 / (1.0 + jnp.exp(-x))


def _modulated_norm(x, g, shift, scale):
    ms = jnp.mean(x * x, axis=-1, keepdims=True)
    y = x * lax.rsqrt(ms + RMS_EPS) * g
    return y * (1.0 + scale) + shift


def _params(*sem):
    return pltpu.CompilerParams(dimension_semantics=sem, vmem_limit_bytes=VMEM_LIMIT)


def _resident(shape):
    nd = len(shape)
    return pl.BlockSpec(shape, lambda *_: (0,) * nd, pipeline_mode=pl.Buffered(1))


def _ada_kernel(c_ref, w_ref, b_ref, o_ref):
    c = c_ref[...]
    ca = c * _sigmoid(c)
    o_ref[...] = jnp.dot(ca, w_ref[...], preferred_element_type=F32,
                         precision=lax.Precision.HIGHEST) + b_ref[...]


def _ada(c, ada_w, ada_b):
    depth, d, n = ada_w.shape
    b = c.shape[0]
    tn = 9 * LANES
    return pl.pallas_call(
        _ada_kernel,
        grid=(depth, n // tn),
        in_specs=[pl.BlockSpec((b, d), lambda l, j: (0, 0)),
                  pl.BlockSpec((None, d, tn), lambda l, j: (l, 0, j)),
                  pl.BlockSpec((None, 1, tn), lambda l, j: (l, 0, j))],
        out_specs=pl.BlockSpec((None, b, tn), lambda l, j: (l, 0, j)),
        out_shape=jax.ShapeDtypeStruct((depth, b, n), F32),
        compiler_params=_params("parallel", "parallel"),
        name="ada",
    )(c, ada_w, ada_b.reshape(depth, 1, n))


def _ffn_kernel(x_ref, g_ref, sh_ref, sc_ref, gt_ref, wa_ref, wb_ref, wo_ref, o_ref, y_ref, acc_ref):
    y_ref[...] = _modulated_norm(x_ref[...], g_ref[...], sh_ref[...], sc_ref[...]).astype(BF16)
    for c in range(FF_PAD // FF_CHUNK):
        sl = slice(c * FF_CHUNK, (c + 1) * FF_CHUNK)
        a = _dot(y_ref[...], wa_ref[:, sl])
        b = _dot(y_ref[...], wb_ref[:, sl])
        u = (a * _sigmoid(a) * b).astype(BF16)
        contrib = _dot(u, wo_ref[sl, :])
        if c == 0:
            acc_ref[...] = contrib
        else:
            acc_ref[...] += contrib
    o_ref[...] = x_ref[...] + 0.5 * gt_ref[...] * acc_ref[...]


def _mod_spec(tiles_per_batch, d):
    return pl.BlockSpec((None, 1, d), lambda i: (i // tiles_per_batch, 0, 0))


def _ffn(x, g, shift, scale, gate, wa, wb, wo, s):
    n, d = x.shape
    tpb = s // TM
    return pl.pallas_call(
        _ffn_kernel,
        grid=(n // TM,),
        in_specs=[pl.BlockSpec((TM, d), lambda i: (i, 0)),
                  _resident((1, d)),
                  _mod_spec(tpb, d), _mod_spec(tpb, d), _mod_spec(tpb, d),
                  _resident(wa.shape), _resident(wb.shape), _resident(wo.shape)],
        out_specs=pl.BlockSpec((TM, d), lambda i: (i, 0)),
        out_shape=jax.ShapeDtypeStruct((n, d), F32),
        scratch_shapes=[pltpu.VMEM((TM, d), BF16), pltpu.VMEM((TM, d), F32)],
        compiler_params=_params("parallel"),
        name="ffn",
    )(x, g, shift, scale, gate, wa, wb, wo)


def _proj_kernel(meta, n_out, x_ref, g_ref, sh_ref, sc_ref, w_ref, gain_ref, gm_ref, *rest):
    outs = rest[:n_out]
    y_ref = rest[n_out]
    y_ref[...] = _modulated_norm(x_ref[...], g_ref[...], sh_ref[...], sc_ref[...]).astype(BF16)
    for c, halves in enumerate(meta):
        sl = slice(c * PROJ_CHUNK, (c + 1) * PROJ_CHUNK)
        z = _dot(y_ref[...], w_ref[:, sl])
        if any(normed for _, _, normed in halves):
            msq = _dot_hilo(z * z, gm_ref[...])
            zn = z * lax.rsqrt(msq + RMS_EPS) * gain_ref[:, sl]
        for hf, (oi, off, normed) in enumerate(halves):
            src = zn if normed else z
            outs[oi][:, off:off + LANES] = src[:, hf * LANES:(hf + 1) * LANES].astype(outs[oi].dtype)


def _proj(x, g, shift, scale, w, gain, gm, meta, out_defs, s):
    n, d = x.shape
    tpb = s // TM
    return pl.pallas_call(
        functools.partial(_proj_kernel, meta, len(out_defs)),
        grid=(n // TM,),
        in_specs=[pl.BlockSpec((TM, d), lambda i: (i, 0)),
                  _resident((1, d)),
                  _mod_spec(tpb, d), _mod_spec(tpb, d),
                  _resident(w.shape), _resident(gain.shape), _resident(gm.shape)],
        out_specs=[pl.BlockSpec((TM, wd), lambda i: (i, 0)) for wd, _ in out_defs],
        out_shape=[jax.ShapeDtypeStruct((n, wd), dt) for wd, dt in out_defs],
        scratch_shapes=[pltpu.VMEM((TM, d), BF16)],
        compiler_params=_params("parallel"),
        name="proj",
    )(x, g, shift, scale, w, gain, gm)


def _mla_prep_kernel(in_ref, qg_ref, kvg_ref, wuq_ref, wuk_ref, wuv_ref, gm_ref, rot_ref,
                     gq_ref, gkn_ref, gkr_ref, cos_ref, sin_ref, q_out, k_out, v_out):
    def rms(z, g):
        return z * lax.rsqrt(jnp.mean(z * z, axis=-1, keepdims=True) + RMS_EPS) * g

    cqn = rms(in_ref[:, 0:MLA_Q_LORA], qg_ref[...]).astype(BF16)
    ckvn = rms(in_ref[:, MLA_Q_LORA:MLA_Q_LORA + MLA_KV_LORA], kvg_ref[...]).astype(BF16)
    c3 = in_ref[:, 3 * LANES:4 * LANES]
    cos = cos_ref[...]
    sin = sin_ref[...]
    gm = gm_ref[...]
    rot = rot_ref[...]

    def norm_rope(z, gain):
        msq = _dot_hilo(z * z, gm)
        z = z * lax.rsqrt(msq + RMS_EPS) * gain
        return z * cos + _dot_hilo(z, rot) * sin

    kr = norm_rope(jnp.concatenate([c3, c3], axis=1), gkr_ref[...])
    for h in range(MLA_HEADS // 2):
        sl = slice(h * 2 * LANES, (h + 1) * 2 * LANES)
        q_out[:, sl] = norm_rope(_dot(cqn, wuq_ref[:, sl]), gq_ref[...]).astype(BF16)
        k_out[:, sl] = (norm_rope(_dot(ckvn, wuk_ref[:, sl]), gkn_ref[...]) + kr).astype(BF16)
    v_out[...] = _dot(ckvn, wuv_ref[...]).astype(BF16)


def _mla_prep(mla_in, qg, kvg, wuq, wuk, wuv, gm, rot, gq, gkn, gkr, cos_t, sin_t, s):
    n = mla_in.shape[0]
    tpb = s // TM
    hw = MLA_HEADS * LANES
    vw = MLA_HEADS * MLA_V
    tab = pl.BlockSpec((TM, 2 * LANES), lambda i: (i % tpb, 0))
    consts = [qg, kvg, wuq, wuk, wuv, gm, rot, gq, gkn, gkr]
    return pl.pallas_call(
        _mla_prep_kernel,
        grid=(n // TM,),
        in_specs=[pl.BlockSpec((TM, 4 * LANES), lambda i: (i, 0))] + [_resident(a.shape) for a in consts] + [tab, tab],
        out_specs=[pl.BlockSpec((TM, hw), lambda i: (i, 0)), pl.BlockSpec((TM, hw), lambda i: (i, 0)),
                   pl.BlockSpec((TM, vw), lambda i: (i, 0))],
        out_shape=[jax.ShapeDtypeStruct((n, hw), BF16), jax.ShapeDtypeStruct((n, hw), BF16),
                   jax.ShapeDtypeStruct((n, vw), BF16)],
        compiler_params=_params("parallel"),
        name="mla_prep",
    )(mla_in, *consts, cos_t, sin_t)


def _attn_kernel(cfg, q_ref, k_ref, vt_ref, tab_ref, *rest):
    n_tab, qw, backs, sel_cfg, want_lse, n_qt = cfg
    if sel_cfg is not None:
        sel_ref, rest = rest[0], rest[1:]
        sel_bpt, sel_stride, sel_pair_mul = sel_cfg
    o_ref = rest[0]
    lse_ref = rest[1] if want_lse else None
    st_scr, p_scr = rest[-2], rest[-1]
    blk = pl.program_id(1)
    lane = lax.broadcasted_iota(jnp.int32, (1, LANES), 1)

    counts = tuple(sum(min(t, bk) + 1 for t in range(n_qt)) for bk in backs)
    back = jnp.int32(backs[-1])
    n_tiles = jnp.int32(counts[-1])
    for bi in range(len(backs) - 2, -1, -1):
        back = jnp.where(blk == bi, jnp.int32(backs[bi]), back)
        n_tiles = jnp.where(blk == bi, jnp.int32(counts[bi]), n_tiles)

    def first_key_tile(qi):
        return jnp.maximum(qi - back, 0)

    def logits_to(slot, qi, j):
        qrow = pl.multiple_of(qi * TQ, TQ)
        krow = pl.multiple_of(j * TK, TK)
        if qw == 1:
            q = q_ref[pl.ds(qrow, TQ), :]
            zero = jnp.zeros_like(q)
            qs = [jnp.where(lane < HEAD_DIM, q, zero), jnp.where(lane >= HEAD_DIM, q, zero)]
        else:
            qs = [q_ref[pl.ds(qrow, TQ), s * LANES:(s + 1) * LANES] for s in range(2)]
        for s in range(2):
            kj = k_ref[pl.ds(krow, TK), :] if qw == 1 else k_ref[pl.ds(krow, TK), s * LANES:(s + 1) * LANES]
            st = _dot_nt(kj, qs[s])
            if sel_cfg is not None:
                off = sel_stride * (s + sel_pair_mul * blk) + sel_bpt * j
                kb = TK // sel_bpt
                st = jnp.concatenate(
                    [st[bk * kb:(bk + 1) * kb] + sel_ref[pl.ds(off + bk, 1), pl.ds(qrow, TQ)]
                     for bk in range(sel_bpt)], axis=0)
            st_scr[slot, s] = st

    def accumulate(slot, j, alphas, accs):
        krow = pl.multiple_of(j * TK, TK)
        new = []
        for s in range(2):
            vt = vt_ref[s * HEAD_DIM:(s + 1) * HEAD_DIM, pl.ds(krow, TK)]
            new.append(alphas[s] * accs[s] + _dot(vt, p_scr[slot, s]))
        return tuple(new)

    def finalize(qi, stats, accs):
        qrow = pl.multiple_of(qi * TQ, TQ)
        out_t = jnp.concatenate([acc / l for acc, (_, l) in zip(accs, stats)], axis=0)
        o_ref[pl.ds(qrow, TQ), :] = out_t.T.astype(o_ref.dtype)
        if want_lse:
            lse_t = jnp.concatenate([jnp.broadcast_to((m + jnp.log2(l)) * LN2, (HEAD_DIM, TQ))
                                     for (m, l) in stats], axis=0)
            lse_ref[pl.ds(qrow, TQ), :] = lse_t.T

    def advance(q, j):
        last = j == q
        at_end = jnp.logical_and(last, q == n_qt - 1)
        starts = jnp.logical_and(last, jnp.logical_not(at_end))
        qn = jnp.where(starts, q + 1, q)
        jn = jnp.where(at_end, j, jnp.where(last, first_key_tile(q + 1), j + 1))
        return qn, jn, starts, at_end

    def softmax(slot, tile, stats, accs):
        q, j, is_first, filler = tile
        keep = jnp.where(is_first, 0.0, 1.0)
        d = jnp.where(filler, n_tab, jnp.minimum(q - j, n_tab - 1))
        new_stats, alphas, new_accs = [], [], []
        for s in range(2):
            m, l = stats[s]
            m = jnp.where(is_first, NEG_INF, m)
            st = st_scr[slot, s] + tab_ref[s, d]
            m_new = jnp.maximum(m, jnp.max(st, axis=0, keepdims=True))
            alpha = jnp.exp2(m - m_new)
            p = jnp.exp2(st - m_new)
            new_stats.append((m_new, alpha * (l * keep) + jnp.sum(p, axis=0, keepdims=True)))
            alphas.append(alpha)
            new_accs.append(accs[s] * keep)
            p_scr[slot, s] = p.astype(BF16)
        return tuple(new_stats), tuple(alphas), tuple(new_accs)

    def body(u, carry):
        tile_a, (qp, jp), stats, alphas, accs = carry
        qa, ja, first_a, _ = tile_a
        tile_b = advance(qa, ja)
        qb, jb, first_b, _ = tile_b
        tile_c = advance(qb, jb)
        logits_to(1, qb, jb)
        accs_p = accumulate(1, jp, alphas, accs)
        stats_a, alphas_a, accs = softmax(0, tile_a, stats, accs_p)
        logits_to(0, tile_c[0], tile_c[1])
        accs_a = accumulate(0, ja, alphas_a, accs)
        stats_b, alphas_b, accs = softmax(1, tile_b, stats_a, accs_a)

        @pl.when(jnp.logical_and(first_a, u > 0))
        def _():
            finalize(qp, stats, accs_p)

        @pl.when(first_b)
        def _():
            finalize(qa, stats_a, accs_a)

        return tile_c, (qb, jb), stats_b, alphas_b, accs

    zero_i = jnp.int32(0)
    logits_to(0, zero_i, zero_i)
    p_scr[1] = jnp.zeros(p_scr.shape[1:], BF16)
    init = ((zero_i, zero_i, zero_i == 0, zero_i != 0), (zero_i, zero_i),
            tuple((jnp.full((1, TQ), NEG_INF, F32), jnp.zeros((1, TQ), F32)) for _ in range(2)),
            tuple(jnp.ones((1, TQ), F32) for _ in range(2)),
            tuple(jnp.zeros((HEAD_DIM, TQ), F32) for _ in range(2)))
    _, (qp, jp), stats, alphas, accs = lax.fori_loop(0, (n_tiles + 1) // 2, body, init)
    accs = accumulate(1, jp, alphas, accs)
    finalize(qp, stats, accs)


def _attn(q, k, vt, tab, sel, *, qc0, kc0, vb0, n_blk, qw, backs=None, kv_shared=False,
          tab_shared=False, sel_cfg=None, out_dtype=F32, want_lse=False):
    b, s = q.shape[:2]
    n_tab = tab.shape[2] - 1
    n_qt = s // TQ
    if backs is None:
        backs = (n_qt,)
    cfg = (n_tab, qw, backs, sel_cfg, want_lse, n_qt)
    kidx = (lambda bb, h: (bb, 0, kc0)) if kv_shared else (lambda bb, h: (bb, 0, kc0 + h))
    vidx = (lambda bb, h: (bb, vb0, 0)) if kv_shared else (lambda bb, h: (bb, vb0 + h, 0))
    tidx = (lambda bb, h: (0, 0, 0, 0, 0)) if tab_shared else (lambda bb, h: (h, 0, 0, 0, 0))
    in_specs = [pl.BlockSpec((None, s, qw * LANES), lambda bb, h: (bb, 0, qc0 + h)),
                pl.BlockSpec((None, s, qw * LANES), kidx),
                pl.BlockSpec((None, 2 * HEAD_DIM, s), vidx),
                pl.BlockSpec((None, 2, n_tab + 1, TK, TQ), tidx)]
    args = [q, k, vt, tab]
    if sel_cfg is not None:
        in_specs.append(pl.BlockSpec((None, LANES, s), lambda bb, h: (bb, 0, 0)))
        args.append(sel)
    ospec = pl.BlockSpec((None, s, LANES), lambda bb, h: (bb, 0, h))
    out_specs = [ospec]
    out_shape = [jax.ShapeDtypeStruct((b, s, n_blk * LANES), out_dtype)]
    if want_lse:
        out_specs.append(ospec)
        out_shape.append(jax.ShapeDtypeStruct((b, s, n_blk * LANES), F32))
    return pl.pallas_call(
        functools.partial(_attn_kernel, cfg),
        grid=(b, n_blk),
        in_specs=in_specs,
        out_specs=out_specs,
        out_shape=out_shape,
        scratch_shapes=[pltpu.VMEM((2, 2, TK, TQ), F32), pltpu.VMEM((2, 2, TK, TQ), BF16)],
        compiler_params=_params("parallel", "parallel"),
        name="attn",
    )(*args)


def _nsa_cmp_kernel(kc_ref, vc_ref, pe_ref, wlo_ref, whi_ref, w2_ref, gm_ref, gain_ref, kcmp_ref, vcmpt_ref):
    def hidden(c, j):
        lo = _dot((c + pe_ref[j, 0]).astype(BF16), wlo_ref[j])
        hi = _dot((c + pe_ref[j, 1]).astype(BF16), whi_ref[j])
        h = lo + pltpu.roll(hi, hi.shape[0] - 1, 0)
        return (h * _sigmoid(h)).astype(BF16)

    kz = _dot_nt(hidden(kc_ref[...], 0), w2_ref[0])
    msq = _dot_hilo(kz * kz, gm_ref[...])
    kcmp_ref[...] = (kz * lax.rsqrt(msq + RMS_EPS) * gain_ref[...]).astype(BF16)
    vcmpt_ref[...] = _dot_nt(w2_ref[1], hidden(vc_ref[...], 1)).astype(BF16)


def _nsa_cmp(kc2, vc2, pe, wlo, whi, w2, gm, gain):
    b, nch, width = kc2.shape
    consts = [pe, wlo, whi, w2, gm, gain]
    blk = pl.BlockSpec((None, nch, width), lambda i: (i, 0, 0))
    oblk = pl.BlockSpec((None, nch, LANES), lambda i: (i, 0, 0))
    return pl.pallas_call(
        _nsa_cmp_kernel,
        grid=(b,),
        in_specs=[blk, blk] + [_resident(a.shape) for a in consts],
        out_specs=[oblk, oblk],
        out_shape=[jax.ShapeDtypeStruct((b, nch, LANES), BF16)] * 2,
        compiler_params=_params("parallel"),
        name="nsa_cmp",
    )(kc2, vc2, *consts)


def _rank_keep(score, ids, top):
    cnt = jnp.zeros(score.shape, jnp.int32)
    for mp in range(score.shape[0]):
        other = score[mp:mp + 1, :]
        tie = jnp.where(mp < ids, 1, 0)
        cnt = cnt + jnp.where(other > score, 1, jnp.where(other == score, tie, 0))
    return cnt < top


def _nsa_sel_kernel(q_ref, kcmp_ref, vcmpt_ref, bias_ref, ovl_ref, oc_ref, sel_ref):
    qi = pl.program_id(1)
    lane = lax.broadcasted_iota(jnp.int32, (1, LANES), 1)
    row = lax.broadcasted_iota(jnp.int32, (LANES, 1), 0)
    t = qi * TQ + lax.broadcasted_iota(jnp.int32, (1, TQ), 1)
    mask_c = (NSA_CMP_STRIDE * row + NSA_CMP_LEN - 1) <= t
    kcmp = kcmp_ref[...]
    heads = [(p, g) for p in range(NSA_HPG) for g in range(NSA_GROUPS)]
    raw = []
    for p, g in heads:
        qp = q_ref[:, p * LANES:(p + 1) * LANES]
        raw.append(_dot_nt(kcmp, jnp.where((lane >> 6) == g, qp, jnp.zeros_like(qp))))
    pcs = []
    for (p, g), r in zip(heads, raw):
        lg = jnp.where(mask_c, r * LN2 + bias_ref[p * NSA_GROUPS + g], NEG_INF)
        m = jnp.max(lg, axis=0, keepdims=True)
        e = jnp.where(mask_c, jnp.exp(lg - m), 0.0)
        den = jnp.maximum(jnp.sum(e, axis=0, keepdims=True), 1e-30)
        pcs.append(e / den)
    imp = jnp.zeros((LANES, TQ), F32)
    ocs = []
    for (p, g), pc in zip(heads, pcs):
        hi, lo = _split(pc)
        ocs.append(_dot(vcmpt_ref[g * HEAD_DIM:(g + 1) * HEAD_DIM, :], hi))
        imp = imp + _dot(ovl_ref[g], hi) + _dot(ovl_ref[g], lo)
    for p in range(NSA_HPG):
        oc_ref[:, p * LANES:(p + 1) * LANES] = jnp.concatenate([ocs[2 * p], ocs[2 * p + 1]], axis=0).T

    n_slc = 32
    ids = row[0:n_slc]
    cur = t >> 6
    forced = (ids == 0) | (ids == cur) | (ids == cur - 1)
    masks = []
    for g in range(NSA_GROUPS):
        score = jnp.where(forced, NSA_FORCED, jnp.where(ids <= cur, imp[g * n_slc:(g + 1) * n_slc], NEG_INF))
        keep = _rank_keep(score, ids, NSA_SLC_TOP) & (score > 0.5 * NEG_INF)
        masks.append(jnp.where(keep, 0.0, NEG_INF))
    masks.append(jnp.full((LANES - NSA_GROUPS * n_slc, TQ), NEG_INF, F32))
    sel_ref[...] = jnp.concatenate(masks, axis=0)


def _nsa_sel(q, kcmp, vcmpt, bias_c, ovl):
    b, s, w = q.shape
    return pl.pallas_call(
        _nsa_sel_kernel,
        grid=(b, s // TQ),
        in_specs=[pl.BlockSpec((None, TQ, w), lambda bb, i: (bb, i, 0)),
                  pl.BlockSpec((None, LANES, LANES), lambda bb, i: (bb, 0, 0)),
                  pl.BlockSpec((None, LANES, LANES), lambda bb, i: (bb, 0, 0)),
                  pl.BlockSpec((NSA_HEADS, LANES, TQ), lambda bb, i: (0, 0, i)),
                  _resident(ovl.shape)],
        out_specs=[pl.BlockSpec((None, TQ, w), lambda bb, i: (bb, i, 0)),
                   pl.BlockSpec((None, LANES, TQ), lambda bb, i: (bb, 0, i))],
        out_shape=[jax.ShapeDtypeStruct((b, s, w), F32), jax.ShapeDtypeStruct((b, LANES, s), F32)],
        compiler_params=_params("parallel", "parallel"),
        name="nsa_sel",
    )(q, kcmp, vcmpt, bias_c, ovl)


def _moba_gate_kernel(q_ref, k_ref, avg_ref, sel_ref):
    qi = pl.program_id(1)
    nb = 8
    kmean = _dot(avg_ref[...], k_ref[...])
    r2 = lax.broadcasted_iota(jnp.int32, kmean.shape, 0)
    c2 = lax.broadcasted_iota(jnp.int32, kmean.shape, 1)
    kmean = jnp.where((r2 >> 3) == (c2 >> 6), kmean, 0.0)
    kh, kl = _split(kmean)
    q = q_ref[...]
    gate = _dot_nt(kh, q) + _dot_nt(kl, q)
    ids = lax.broadcasted_iota(jnp.int32, (nb, 1), 0)
    past = ids < qi
    masks = []
    for h in range(MOBA_HEADS):
        score = jnp.where(past, gate[h * nb:(h + 1) * nb], NEG_INF)
        keep = (_rank_keep(score, ids, MOBA_TOP) & past) | (ids == qi)
        masks.append(jnp.where(keep, 0.0, NEG_INF))
    masks.append(jnp.full((LANES - MOBA_HEADS * nb, TQ), NEG_INF, F32))
    sel_ref[...] = jnp.concatenate(masks, axis=0)


def _moba_gate(p_arr, avg, qc, kc):
    b, s, _ = p_arr.shape
    w = MOBA_HEADS * HEAD_DIM
    return pl.pallas_call(
        _moba_gate_kernel,
        grid=(b, s // TQ),
        in_specs=[pl.BlockSpec((None, TQ, w), lambda bb, i: (bb, i, qc)),
                  pl.BlockSpec((None, s, w), lambda bb, i: (bb, 0, kc)),
                  _resident(avg.shape)],
        out_specs=pl.BlockSpec((None, LANES, TQ), lambda bb, i: (bb, 0, i)),
        out_shape=jax.ShapeDtypeStruct((b, LANES, s), F32),
        compiler_params=_params("parallel", "parallel"),
        name="moba_gate",
    )(p_arr, p_arr, avg)


def _even_out_kernel(x_ref, oa_ref, oc_ref, os_ref, ow_ref, gl_ref, eg_ref, wa_ref, wb_ref, gt_ref, o_ref):
    sg = _sigmoid(gl_ref[...])
    hi, lo = _split(sg)
    nsa = None
    for br, src in enumerate((oc_ref, os_ref, ow_ref)):
        gexp = _dot(hi, eg_ref[br]) + _dot(lo, eg_ref[br])
        term = gexp * src[...]
        nsa = term if nsa is None else nsa + term
    m = _dot(oa_ref[...], wa_ref[...]) + _dot(nsa.astype(BF16), wb_ref[...])
    o_ref[...] = x_ref[...] + gt_ref[...] * m


def _even_out(x, o_a, o_c, o_s, o_w, mla_in, eg, wa, wb, gate, s):
    n, d = x.shape
    tpb = s // TM
    row = lambda wd: pl.BlockSpec((TM, wd), lambda i: (i, 0))
    return pl.pallas_call(
        _even_out_kernel,
        grid=(n // TM,),
        in_specs=[row(d), row(o_a.shape[1]), row(o_c.shape[1]), row(o_s.shape[1]), row(o_w.shape[1]),
                  pl.BlockSpec((TM, LANES), lambda i: (i, 3)),
                  _resident(eg.shape), _resident(wa.shape), _resident(wb.shape), _mod_spec(tpb, d)],
        out_specs=row(d),
        out_shape=jax.ShapeDtypeStruct((n, d), F32),
        compiler_params=_params("parallel"),
        name="even_out",
    )(x, o_a, o_c, o_s, o_w, mla_in, eg, wa, wb, gate)


def _odd_out_kernel(x_ref, od_ref, lse_ref, om_ref, wd_ref, wm_ref, gt_ref, o_ref):
    w = DIL_HPG * HEAD_DIM
    ls = [lse_ref[:, g * w:(g + 1) * w] for g in range(len(DIL_PAIRS))]
    mx = jnp.maximum(jnp.maximum(ls[0], ls[1]), ls[2])
    es = [jnp.exp(l - mx) for l in ls]
    den = es[0] + es[1] + es[2]
    merged = None
    for g in range(len(DIL_PAIRS)):
        term = (es[g] / den) * od_ref[:, g * w:(g + 1) * w]
        merged = term if merged is None else merged + term
    m = _dot(merged.astype(BF16), wd_ref[...]) + _dot(om_ref[...], wm_ref[...])
    o_ref[...] = x_ref[...] + gt_ref[...] * m


def _odd_out(x, o_d, lse_d, o_m, wd, wm, gate, s):
    n, d = x.shape
    tpb = s // TM
    row = lambda wd_: pl.BlockSpec((TM, wd_), lambda i: (i, 0))
    return pl.pallas_call(
        _odd_out_kernel,
        grid=(n // TM,),
        in_specs=[row(d), row(o_d.shape[1]), row(lse_d.shape[1]), row(o_m.shape[1]),
                  _resident(wd.shape), _resident(wm.shape), _mod_spec(tpb, d)],
        out_specs=row(d),
        out_shape=jax.ShapeDtypeStruct((n, d), F32),
        compiler_params=_params("parallel"),
        name="odd_out",
    )(x, o_d, lse_d, o_m, wd, wm, gate)


def _t5_bucket(dist):
    n = jnp.maximum(jnp.asarray(dist, jnp.int32), 0)
    nf = jnp.maximum(n, 1).astype(F32)
    large = T5_MAX_EXACT + (jnp.log(nf / T5_MAX_EXACT) / math.log(T5_MAX_DIST / T5_MAX_EXACT)
                            * (NUM_BUCKETS - T5_MAX_EXACT)).astype(jnp.int32)
    return jnp.where(n < T5_MAX_EXACT, n, jnp.minimum(large, NUM_BUCKETS - 1))


TOEP_PERIOD = 4 * TQ


def _toeplitz_dist():
    j = np.arange(TOEP_PERIOD)
    return np.where(j < 3 * TQ, j, j - TOEP_PERIOD)


def _toeplitz_tiles(u, n_tab):
    h = u.shape[0]
    m = jnp.tile(u, (1, TK))[:, :TK * (TOEP_PERIOD - 1)].reshape(h, TK, TOEP_PERIOD - 1)[:, :, :n_tab * TQ]
    tiles = jnp.transpose(m.reshape(h, TK, n_tab, TQ), (0, 2, 1, 3))
    filler = jnp.full((h, 1, TK, TQ), NEG_INF * LOG2E, F32)
    return jnp.concatenate([tiles, filler], axis=1)


def _bias_tiles(t5_cols, ok, n_tab=3):
    dist = _toeplitz_dist()
    bias = jnp.transpose(t5_cols[_t5_bucket(dist)])
    u = jnp.where(jnp.asarray(ok)[None], bias, NEG_INF) * LOG2E
    return _toeplitz_tiles(u, n_tab)


def _pair(tiles):
    h = tiles.shape[0]
    return tiles.reshape(h // 2, 2, *tiles.shape[1:])


def _group_mean_np(sizes, width=LANES):
    gm = np.zeros((width, width), np.float32)
    o = 0
    for sz in sizes:
        gm[o:o + sz, o:o + sz] = 1.0 / sz
        o += sz
    return gm


def _group_mean_matrix(sizes, width=LANES):
    return jnp.asarray(_group_mean_np(sizes, width), BF16)


EV_META = ((0, 0, False), (0, 128, False), (0, 256, False), (0, 384, False),
           (1, 0, True), (1, 128, True), (1, 256, True), (1, 384, True),
           (2, 0, False), (3, 0, False), (4, 0, True), (5, 0, False), (6, 0, True), (7, 0, False))
EV_OUTS = [(512, F32), (512, BF16), (128, F32), (128, F32), (128, BF16), (128, BF16), (128, BF16), (128, BF16)]
OD_META = tuple((0, cidx * LANES, (cidx < 12) or (18 <= cidx < 22)) for cidx in range(24))


def _even_w_in(w):
    jn, d, _ = w.shape
    z = lambda n_: jnp.zeros((jn, d, n_), w.dtype)
    nq = w[:, :, 416:928].reshape(jn, d, NSA_GROUPS, NSA_HPG, HEAD_DIM)
    nq = jnp.transpose(nq, (0, 1, 3, 2, 4)).reshape(jn, d, NSA_HEADS * HEAD_DIM)
    chunk3 = jnp.concatenate([w[:, :, 1696:1720], z(HEAD_DIM - 24), w[:, :, 384:416], z(LANES - 96)], axis=-1)
    return jnp.concatenate([w[:, :, 0:384], chunk3, nq, w[:, :, 928:1696]], axis=-1)


def kernel(x, c, t5_bias, ada_w, ada_b, norm_g, ffn_w_in, ffn_w_out, ev_w_in, ev_w_out, mla_q_norm_g,
           mla_kv_norm_g, mla_w_uq, mla_w_ukv, mla_qk_g, nsa_cmp_pe, nsa_cmp_w1, nsa_cmp_w2, nsa_qk_g,
           od_w_in, od_w_out, dil_qk_g, moba_qk_g):
    b, s, d = x.shape
    assert (s, d) == (2048, D_MODEL) and s % TM == 0 and TQ == MOBA_BLOCK and TQ == TK
    n = b * s
    hd = HEAD_DIM
    n_even = ev_w_in.shape[0]
    n_odd = od_w_in.shape[0]
    c64 = hd ** -0.5 * LOG2E
    c96 = (MLA_NOPE + MLA_ROPE) ** -0.5 * LOG2E

    mod = _ada(c, ada_w, ada_b).reshape(DEPTH, b, 3, 3, 1, d)

    dist = _toeplitz_dist()
    causal = dist >= 0
    gm64 = _group_mean_matrix((hd, hd))
    gm_proj = _group_mean_matrix((hd,) * (PROJ_CHUNK // hd), PROJ_CHUNK)
    per_chunk = PROJ_CHUNK // LANES
    chunked = lambda m: tuple(tuple(m[i:i + per_chunk]) for i in range(0, len(m), per_chunk))

    padc = FF_PAD - D_FF
    wa_all = jnp.pad(ffn_w_in[..., :D_FF], ((0, 0), (0, 0), (0, 0), (0, padc))).astype(BF16)
    wb_all = jnp.pad(ffn_w_in[..., D_FF:], ((0, 0), (0, 0), (0, 0), (0, padc))).astype(BF16)
    wo_all = jnp.pad(ffn_w_out, ((0, 0), (0, 0), (0, padc), (0, 0))).astype(BF16)

    nsa_tab = t5_bias[:, MLA_HEADS:MLA_HEADS + NSA_HEADS].reshape(NUM_BUCKETS, NSA_GROUPS, NSA_HPG)
    nsa_cols = jnp.transpose(nsa_tab, (0, 2, 1)).reshape(NUM_BUCKETS, NSA_HEADS)
    tab_sel = _pair(_bias_tiles(nsa_cols, causal))
    tab_win = _pair(_bias_tiles(nsa_cols, causal & (dist <= NSA_WINDOW - 1)))
    tab_mla = _toeplitz_tiles(jnp.where(jnp.asarray(causal), 0.0, NEG_INF).astype(F32)[None], 2)
    tab_mla = jnp.broadcast_to(tab_mla[None], (1, 2) + tab_mla.shape[1:])
    n_cmp_pad = s // NSA_CMP_STRIDE
    jj = np.arange(2 * n_cmp_pad)
    kk = np.where(jj < n_cmp_pad, jj, jj - 2 * n_cmp_pad)
    dc = NSA_CMP_STRIDE * (-kk)[None, :] + np.arange(NSA_CMP_STRIDE)[:, None] - (NSA_CMP_LEN - 1)
    ub = jnp.moveaxis(nsa_cols[_t5_bucket(dc)], -1, 0)
    mb = jnp.tile(ub, (1, 1, n_cmp_pad))[:, :, :n_cmp_pad * (2 * n_cmp_pad - 1)]
    mb = mb.reshape(NSA_HEADS, NSA_CMP_STRIDE, n_cmp_pad, 2 * n_cmp_pad - 1)[..., :n_cmp_pad]
    bias_c = jnp.transpose(mb, (0, 3, 2, 1)).reshape(NSA_HEADS, n_cmp_pad, s)
    n_cmp = (s - NSA_CMP_LEN) // NSA_CMP_STRIDE + 1
    cstart = np.arange(n_cmp) * NSA_CMP_STRIDE
    sstart = np.arange(s // NSA_SLC_BLOCK) * NSA_SLC_BLOCK
    overlap = np.clip(np.minimum(cstart[:, None] + NSA_CMP_LEN, sstart[None, :] + NSA_SLC_BLOCK)
                      - np.maximum(cstart[:, None], sstart[None, :]), 0, None).astype(np.float32) / NSA_CMP_LEN
    ovl = np.zeros((NSA_GROUPS, LANES, LANES), np.float32)
    for g in range(NSA_GROUPS):
        ovl[g, 32 * g:32 * g + 32, :n_cmp] = overlap.T
    ovl = jnp.asarray(ovl, BF16)
    eg = np.zeros((3, LANES, NSA_HEADS * hd), np.float32)
    for g in range(NSA_GROUPS):
        for p in range(NSA_HPG):
            for br in range(3):
                eg[br, (g * NSA_HPG + p) * 3 + br, p * LANES + g * hd:p * LANES + (g + 1) * hd] = 1.0
    eg = jnp.asarray(eg, BF16)
    gm_mla = jnp.asarray(np.kron(np.eye(2, dtype=np.float32), _group_mean_np((MLA_NOPE, MLA_ROPE))), BF16)
    rot = np.zeros((LANES, LANES), np.float32)
    half = MLA_ROPE // 2
    for i in range(half):
        rot[MLA_NOPE + half + i, MLA_NOPE + i] = -1.0
        rot[MLA_NOPE + i, MLA_NOPE + half + i] = 1.0
    rot = jnp.asarray(np.kron(np.eye(2, dtype=np.float32), rot), BF16)
    inv = ROPE_THETA ** (-jnp.arange(0, MLA_ROPE, 2, dtype=F32) / MLA_ROPE)
    ang = jnp.arange(s, dtype=F32)[:, None] * inv[None, :]
    ones = jnp.ones((s, MLA_NOPE), F32)
    tail = LANES - MLA_NOPE - MLA_ROPE
    cos_t = jnp.tile(jnp.concatenate([ones, jnp.cos(ang), jnp.cos(ang), jnp.ones((s, tail), F32)], axis=1), (1, 2))
    sin_t = jnp.tile(jnp.concatenate([0 * ones, jnp.sin(ang), jnp.sin(ang), jnp.zeros((s, tail), F32)], axis=1), (1, 2))

    ev_w = _even_w_in(ev_w_in).astype(BF16)
    ev_gain = jnp.ones((n_even, 14 * LANES), F32)
    ev_gain = ev_gain.at[:, 512:1024].set(jnp.tile(nsa_qk_g[:, 0], (1, 8)) * c64)
    ev_gain = ev_gain.at[:, 1280:1408].set(jnp.tile(nsa_qk_g[:, 1], (1, 2)))
    ev_gain = ev_gain.at[:, 1536:1664].set(jnp.tile(nsa_qk_g[:, 1], (1, 2)))
    gain_kc = jnp.tile(nsa_qk_g[:, 1], (1, 2))
    wuq = jnp.pad(mla_w_uq.reshape(n_even, MLA_Q_LORA, MLA_HEADS, MLA_NOPE + MLA_ROPE),
                  ((0, 0), (0, 0), (0, 0), (0, tail))).reshape(n_even, MLA_Q_LORA, MLA_HEADS * LANES).astype(BF16)
    ukv = mla_w_ukv.reshape(n_even, MLA_KV_LORA, MLA_HEADS, MLA_NOPE + MLA_V)
    wuk = jnp.pad(ukv[..., :MLA_NOPE], ((0, 0), (0, 0), (0, 0), (0, LANES - MLA_NOPE))
                  ).reshape(n_even, MLA_KV_LORA, MLA_HEADS * LANES).astype(BF16)
    wuv = ukv[..., MLA_NOPE:].reshape(n_even, MLA_KV_LORA, MLA_HEADS * MLA_V).astype(BF16)
    zt = jnp.zeros((n_even, tail), F32)
    gq = jnp.tile(jnp.concatenate([mla_qk_g[:, 0] * c96, zt], axis=1), (1, 2))
    gkn = jnp.tile(jnp.concatenate([mla_qk_g[:, 1, :MLA_NOPE], jnp.zeros((n_even, LANES - MLA_NOPE), F32)], axis=1), (1, 2))
    gkr = jnp.tile(jnp.concatenate([jnp.zeros((n_even, MLA_NOPE), F32), mla_qk_g[:, 1, MLA_NOPE:], zt], axis=1), (1, 2))
    pe2 = jnp.broadcast_to(nsa_cmp_pe.reshape(n_even, 2, 2, 16, 1, hd), (n_even, 2, 2, 16, NSA_GROUPS, hd)
                           ).reshape(n_even, 2, 2, 1, 16 * LANES)
    eye = jnp.eye(NSA_GROUPS, dtype=F32)
    w1 = nsa_cmp_w1.reshape(n_even, 2, 2, 16, hd, NSA_CMP_HID)
    w1x = jnp.einsum('ijaldc,gh->ijalgdhc', w1, eye).reshape(n_even, 2, 2, 16 * LANES, NSA_GROUPS * NSA_CMP_HID).astype(BF16)
    w2x = jnp.einsum('ijcd,gh->ijhdgc', nsa_cmp_w2, eye).reshape(n_even, 2, LANES, NSA_GROUPS * NSA_CMP_HID).astype(BF16)
    wa_o = ev_w_out[:, :MLA_HEADS * MLA_V].astype(BF16)
    wb_o = jnp.transpose(ev_w_out[:, MLA_HEADS * MLA_V:].reshape(n_even, NSA_GROUPS, NSA_HPG, hd, d),
                         (0, 2, 1, 3, 4)).reshape(n_even, NSA_HEADS * hd, d).astype(BF16)

    dil_tabs = []
    for gi, (w, r) in enumerate(DIL_PAIRS):
        ok = causal & (dist <= w) & (dist % r == 0)
        dil_tabs.append(_bias_tiles(t5_bias[:, gi * DIL_HPG:(gi + 1) * DIL_HPG], ok))
    tab_dil = _pair(jnp.concatenate(dil_tabs, axis=0))
    dil_backs = (1, 1, 2, 2, s // TK, s // TK)
    tab_moba = _pair(_bias_tiles(t5_bias[:, DIL_SLOTS:DIL_SLOTS + MOBA_HEADS], causal))
    avg = np.zeros((LANES, s), np.float32)
    for h in range(MOBA_HEADS):
        for m in range(s // MOBA_BLOCK):
            avg[8 * h + m, m * MOBA_BLOCK:(m + 1) * MOBA_BLOCK] = 1.0 / MOBA_BLOCK
    avg = jnp.asarray(avg, BF16)
    od_w = od_w_in.astype(BF16)
    od_gain = jnp.concatenate([jnp.tile(dil_qk_g[:, 0], (1, 12)) * c64, jnp.tile(dil_qk_g[:, 1], (1, 12)),
                               jnp.ones((n_odd, 768), F32), jnp.tile(moba_qk_g[:, 0], (1, 4)) * c64,
                               jnp.tile(moba_qk_g[:, 1], (1, 4)), jnp.ones((n_odd, 256), F32)], axis=1)
    wd_o = od_w_out[:, :DIL_HPG * hd].astype(BF16)
    wm_o = od_w_out[:, DIL_HPG * hd:].astype(BF16)

    sh3 = lambda a: a.reshape(b, s, a.shape[-1])
    tr3 = lambda a: jnp.swapaxes(sh3(a), 1, 2)
    xf = x.reshape(n, d)
    for i in range(DEPTH):
        j = i // 2
        g_i = norm_g[i].reshape(3, 1, d)
        xf = _ffn(xf, g_i[0], mod[i, :, 0, 0], mod[i, :, 0, 1], mod[i, :, 0, 2],
                  wa_all[i, 0], wb_all[i, 0], wo_all[i, 0], s)
        if i % 2 == 0:
            mla_in, nsa_q, kc, vc, ks, vs, kw, vw = _proj(
                xf, g_i[1], mod[i, :, 1, 0], mod[i, :, 1, 1], ev_w[j], ev_gain[j][None], gm_proj,
                chunked(EV_META), EV_OUTS, s)
            qf, kf, vf = _mla_prep(mla_in, mla_q_norm_g[j][None], mla_kv_norm_g[j][None], wuq[j], wuk[j], wuv[j],
                                   gm_mla, rot, gq[j][None], gkn[j][None], gkr[j][None], cos_t, sin_t, s)
            (o_a,) = _attn(sh3(qf), sh3(kf), tr3(vf), tab_mla, None, qc0=0, kc0=0, vb0=0, n_blk=MLA_HEADS // 2, qw=2,
                           tab_shared=True, out_dtype=BF16)
            kcmp, vcmpt = _nsa_cmp(kc.reshape(b, n_cmp_pad, 16 * LANES), vc.reshape(b, n_cmp_pad, 16 * LANES),
                                   pe2[j], w1x[j, :, 0], w1x[j, :, 1], w2x[j], gm64, gain_kc[j][None])
            o_c, sel = _nsa_sel(sh3(nsa_q), kcmp, vcmpt, bias_c, ovl)
            (o_s,) = _attn(sh3(nsa_q), sh3(ks), tr3(vs), tab_sel, sel, qc0=0, kc0=0, vb0=0, n_blk=NSA_HPG, qw=1,
                           kv_shared=True, sel_cfg=(TK // NSA_SLC_BLOCK, 32, 0))
            (o_w,) = _attn(sh3(nsa_q), sh3(kw), tr3(vw), tab_win, None, qc0=0, kc0=0, vb0=0, n_blk=NSA_HPG, qw=1,
                           kv_shared=True, backs=(2, 2, 2, 2))
            xf = _even_out(xf, o_a.reshape(n, -1), o_c.reshape(n, -1), o_s.reshape(n, -1), o_w.reshape(n, -1),
                           mla_in, eg, wa_o[j], wb_o[j], mod[i, :, 1, 2], s)
        else:
            (pr,) = _proj(xf, g_i[1], mod[i, :, 1, 0], mod[i, :, 1, 1], od_w[j], od_gain[j][None], gm_proj,
                          chunked(OD_META), [(24 * LANES, BF16)], s)
            pr3 = pr.reshape(b, s, 24 * LANES)
            vdt = jnp.swapaxes(pr3[:, :, 12 * LANES:18 * LANES], 1, 2)
            vmt = jnp.swapaxes(pr3[:, :, 22 * LANES:24 * LANES], 1, 2)
            o_d, lse_d = _attn(pr3, pr3, vdt, tab_dil, None, qc0=0, kc0=6, vb0=0, n_blk=6, qw=1,
                               backs=dil_backs, want_lse=True)
            selm = _moba_gate(pr3, avg, 9, 10)
            (o_m,) = _attn(pr3, pr3, vmt, tab_moba, selm, qc0=18, kc0=20, vb0=0, n_blk=2, qw=1,
                           sel_cfg=(1, 8, 2), out_dtype=BF16)
            xf = _odd_out(xf, o_d.reshape(n, -1), lse_d.reshape(n, -1), o_m.reshape(n, -1),
                          wd_o[j], wm_o[j], mod[i, :, 1, 2], s)
        xf = _ffn(xf, g_i[2], mod[i, :, 2, 0], mod[i, :, 2, 1], mod[i, :, 2, 2],
                  wa_all[i, 1], wb_all[i, 1], wo_all[i, 1], s)
    return xf.reshape(b, s, d)
```

```python
import functools
import math

import numpy as np
import jax
import jax.numpy as jnp
from jax import lax
from jax.experimental import pallas as pl
from jax.experimental.pallas import tpu as pltpu

F32 = jnp.float32
BF16 = jnp.bfloat16

D_MODEL = 1024
DEPTH = 4
D_FF = 2752
HEAD_DIM = 64
NUM_BUCKETS = 32
T5_MAX_EXACT = 16
T5_MAX_DIST = 128
RMS_EPS = 1e-6
NEG_INF = -1e30
MLA_HEADS = 8
MLA_NOPE = 64
MLA_ROPE = 32
MLA_V = 64
MLA_Q_LORA = 256
MLA_KV_LORA = 128
ROPE_THETA = 10000.0
NSA_HEADS = 8
NSA_GROUPS = 2
NSA_HPG = 4
NSA_CMP_LEN = 32
NSA_CMP_STRIDE = 16
NSA_CMP_HID = 256
NSA_SLC_BLOCK = 64
NSA_SLC_TOP = 8
NSA_WINDOW = 512
NSA_FORCED = 1e6
DIL_PAIRS = ((128, 1), (512, 4), (2048, 16))
DIL_HPG = 4
DIL_SLOTS = len(DIL_PAIRS) * DIL_HPG
MOBA_HEADS = 4
MOBA_BLOCK = 256
MOBA_TOP = 3

LANES = 128
V7X_VMEM_BYTES = 64 * 1024 * 1024
VMEM_LIMIT = V7X_VMEM_BYTES * 7 // 8
TM = 512
TQ = 256
TK = 256
FF_CHUNK = 256
FF_PAD = -(-D_FF // FF_CHUNK) * FF_CHUNK
PROJ_CHUNK = 256
LOG2E = math.log2(math.e)
LN2 = math.log(2.0)


def _dot(a, b):
    return jnp.dot(a, b, preferred_element_type=F32)


def _dot_nt(a, b):
    return lax.dot_general(a, b, (((1,), (1,)), ((), ())), preferred_element_type=F32)


def _split(a):
    hi = a.astype(BF16)
    lo = (a - hi.astype(F32)).astype(BF16)
    return hi, lo


def _dot_hilo(a, b):
    hi, lo = _split(a)
    return _dot(hi, b) + _dot(lo, b)


def _sigmoid(x):
    return 1.0 / (1.0 + jnp.exp(-x))


def _modulated_norm(x, g, shift, scale):
    ms = jnp.mean(x * x, axis=-1, keepdims=True)
    y = x * lax.rsqrt(ms + RMS_EPS) * g
    return y * (1.0 + scale) + shift


def _params(*sem):
    return pltpu.CompilerParams(dimension_semantics=sem, vmem_limit_bytes=VMEM_LIMIT)


def _resident(shape):
    nd = len(shape)
    return pl.BlockSpec(shape, lambda *_: (0,) * nd, pipeline_mode=pl.Buffered(1))


def _ada_kernel(c_ref, w_ref, b_ref, o_ref):
    c = c_ref[...]
    ca = c * _sigmoid(c)
    o_ref[...] = jnp.dot(ca, w_ref[...], preferred_element_type=F32,
                         precision=lax.Precision.HIGHEST) + b_ref[...]


def _ada(c, ada_w, ada_b):
    depth, d, n = ada_w.shape
    b = c.shape[0]
    tn = 9 * LANES
    return pl.pallas_call(
        _ada_kernel,
        grid=(depth, n // tn),
        in_specs=[pl.BlockSpec((b, d), lambda l, j: (0, 0)),
                  pl.BlockSpec((None, d, tn), lambda l, j: (l, 0, j)),
                  pl.BlockSpec((None, 1, tn), lambda l, j: (l, 0, j))],
        out_specs=pl.BlockSpec((None, b, tn), lambda l, j: (l, 0, j)),
        out_shape=jax.ShapeDtypeStruct((depth, b, n), F32),
        compiler_params=_params("parallel", "parallel"),
        name="ada",
    )(c, ada_w, ada_b.reshape(depth, 1, n))


def _ffn_kernel(x_ref, g_ref, sh_ref, sc_ref, gt_ref, wa_ref, wb_ref, wo_ref, o_ref, y_ref, acc_ref):
    y_ref[...] = _modulated_norm(x_ref[...], g_ref[...], sh_ref[...], sc_ref[...]).astype(BF16)
    for c in range(FF_PAD // FF_CHUNK):
        sl = slice(c * FF_CHUNK, (c + 1) * FF_CHUNK)
        a = _dot(y_ref[...], wa_ref[:, sl])
        b = _dot(y_ref[...], wb_ref[:, sl])
        u = (a * _sigmoid(a) * b).astype(BF16)
        contrib = _dot(u, wo_ref[sl, :])
        if c == 0:
            acc_ref[...] = contrib
        else:
            acc_ref[...] += contrib
    o_ref[...] = x_ref[...] + 0.5 * gt_ref[...] * acc_ref[...]


def _mod_spec(tiles_per_batch, d):
    return pl.BlockSpec((None, 1, d), lambda i: (i // tiles_per_batch, 0, 0))


def _ffn(x, g, shift, scale, gate, wa, wb, wo, s):
    n, d = x.shape
    tpb = s // TM
    return pl.pallas_call(
        _ffn_kernel,
        grid=(n // TM,),
        in_specs=[pl.BlockSpec((TM, d), lambda i: (i, 0)),
                  _resident((1, d)),
                  _mod_spec(tpb, d), _mod_spec(tpb, d), _mod_spec(tpb, d),
                  _resident(wa.shape), _resident(wb.shape), _resident(wo.shape)],
        out_specs=pl.BlockSpec((TM, d), lambda i: (i, 0)),
        out_shape=jax.ShapeDtypeStruct((n, d), F32),
        scratch_shapes=[pltpu.VMEM((TM, d), BF16), pltpu.VMEM((TM, d), F32)],
        compiler_params=_params("parallel"),
        name="ffn",
    )(x, g, shift, scale, gate, wa, wb, wo)


def _proj_kernel(meta, n_out, x_ref, g_ref, sh_ref, sc_ref, w_ref, gain_ref, gm_ref, *rest):
    outs = rest[:n_out]
    y_ref = rest[n_out]
    y_ref[...] = _modulated_norm(x_ref[...], g_ref[...], sh_ref[...], sc_ref[...]).astype(BF16)
    for c, halves in enumerate(meta):
        sl = slice(c * PROJ_CHUNK, (c + 1) * PROJ_CHUNK)
        z = _dot(y_ref[...], w_ref[:, sl])
        if any(normed for _, _, normed in halves):
            msq = _dot_hilo(z * z, gm_ref[...])
            zn = z * lax.rsqrt(msq + RMS_EPS) * gain_ref[:, sl]
        for hf, (oi, off, normed) in enumerate(halves):
            src = zn if normed else z
            outs[oi][:, off:off + LANES] = src[:, hf * LANES:(hf + 1) * LANES].astype(outs[oi].dtype)


def _proj(x, g, shift, scale, w, gain, gm, meta, out_defs, s):
    n, d = x.shape
    tpb = s // TM
    return pl.pallas_call(
        functools.partial(_proj_kernel, meta, len(out_defs)),
        grid=(n // TM,),
        in_specs=[pl.BlockSpec((TM, d), lambda i: (i, 0)),
                  _resident((1, d)),
                  _mod_spec(tpb, d), _mod_spec(tpb, d),
                  _resident(w.shape), _resident(gain.shape), _resident(gm.shape)],
        out_specs=[pl.BlockSpec((TM, wd), lambda i: (i, 0)) for wd, _ in out_defs],
        out_shape=[jax.ShapeDtypeStruct((n, wd), dt) for wd, dt in out_defs],
        scratch_shapes=[pltpu.VMEM((TM, d), BF16)],
        compiler_params=_params("parallel"),
        name="proj",
    )(x, g, shift, scale, w, gain, gm)


def _mla_prep_kernel(in_ref, qg_ref, kvg_ref, wuq_ref, wuk_ref, wuv_ref, gm_ref, rot_ref,
                     gq_ref, gkn_ref, gkr_ref, cos_ref, sin_ref, q_out, k_out, v_out):
    def rms(z, g):
        return z * lax.rsqrt(jnp.mean(z * z, axis=-1, keepdims=True) + RMS_EPS) * g

    cqn = rms(in_ref[:, 0:MLA_Q_LORA], qg_ref[...]).astype(BF16)
    ckvn = rms(in_ref[:, MLA_Q_LORA:MLA_Q_LORA + MLA_KV_LORA], kvg_ref[...]).astype(BF16)
    c3 = in_ref[:, 3 * LANES:4 * LANES]
    cos = cos_ref[...]
    sin = sin_ref[...]
    gm = gm_ref[...]
    rot = rot_ref[...]

    def norm_rope(z, gain):
        msq = _dot_hilo(z * z, gm)
        z = z * lax.rsqrt(msq + RMS_EPS) * gain
        return z * cos + _dot_hilo(z, rot) * sin

    kr = norm_rope(jnp.concatenate([c3, c3], axis=1), gkr_ref[...])
    for h in range(MLA_HEADS // 2):
        sl = slice(h * 2 * LANES, (h + 1) * 2 * LANES)
        q_out[:, sl] = norm_rope(_dot(cqn, wuq_ref[:, sl]), gq_ref[...]).astype(BF16)
        k_out[:, sl] = (norm_rope(_dot(ckvn, wuk_ref[:, sl]), gkn_ref[...]) + kr).astype(BF16)
    v_out[...] = _dot(ckvn, wuv_ref[...]).astype(BF16)


def _mla_prep(mla_in, qg, kvg, wuq, wuk, wuv, gm, rot, gq, gkn, gkr, cos_t, sin_t, s):
    n = mla_in.shape[0]
    tpb = s // TM
    hw = MLA_HEADS * LANES
    vw = MLA_HEADS * MLA_V
    tab = pl.BlockSpec((TM, 2 * LANES), lambda i: (i % tpb, 0))
    consts = [qg, kvg, wuq, wuk, wuv, gm, rot, gq, gkn, gkr]
    return pl.pallas_call(
        _mla_prep_kernel,
        grid=(n // TM,),
        in_specs=[pl.BlockSpec((TM, 4 * LANES), lambda i: (i, 0))] + [_resident(a.shape) for a in consts] + [tab, tab],
        out_specs=[pl.BlockSpec((TM, hw), lambda i: (i, 0)), pl.BlockSpec((TM, hw), lambda i: (i, 0)),
                   pl.BlockSpec((TM, vw), lambda i: (i, 0))],
        out_shape=[jax.ShapeDtypeStruct((n, hw), BF16), jax.ShapeDtypeStruct((n, hw), BF16),
                   jax.ShapeDtypeStruct((n, vw), BF16)],
        compiler_params=_params("parallel"),
        name="mla_prep",
    )(mla_in, *consts, cos_t, sin_t)


def _attn_kernel(cfg, q_ref, k_ref, vt_ref, tab_ref, *rest):
    n_tab, qw, backs, sel_cfg, want_lse, n_qt, stride = cfg
    n_qs = n_qt // stride
    if sel_cfg is not None:
        sel_ref, rest = rest[0], rest[1:]
        sel_bpt, sel_stride, sel_pair_mul = sel_cfg
    o_ref = rest[0]
    lse_ref = rest[1] if want_lse else None
    st_scr, p_scr = rest[-4:-2], rest[-2:]
    blk = pl.program_id(1)
    lane = lax.broadcasted_iota(jnp.int32, (1, LANES), 1)

    counts = tuple(sum(min(t % n_qs, bk) + 1 for t in range(n_qt)) for bk in backs)
    back = jnp.int32(backs[-1])
    n_tiles = jnp.int32(counts[-1])
    for bi in range(len(backs) - 2, -1, -1):
        back = jnp.where(blk == bi, jnp.int32(backs[bi]), back)
        n_tiles = jnp.where(blk == bi, jnp.int32(counts[bi]), n_tiles)

    def first_key_tile(qi):
        return qi - jnp.minimum(qi & (n_qs - 1), back)

    def rows(t, size):
        return pl.ds(pl.multiple_of(t * size, size), size)

    def out_rows(t, size):
        if stride == 1:
            return rows(t, size)
        return pl.ds((t >> int(math.log2(n_qs))) + stride * size * (t & (n_qs - 1)), size, stride=stride)

    def logits_to(slot, qi, j):
        if qw == 1:
            q = q_ref[rows(qi, TQ), :]
            zero = jnp.zeros_like(q)
            qs = [jnp.where(lane < HEAD_DIM, q, zero), jnp.where(lane >= HEAD_DIM, q, zero)]
        else:
            qs = [q_ref[rows(qi, TQ), s * LANES:(s + 1) * LANES] for s in range(2)]
        for s in range(2):
            kj = k_ref[rows(j, TK), :] if qw == 1 else k_ref[rows(j, TK), s * LANES:(s + 1) * LANES]
            st = _dot_nt(kj, qs[s])
            if sel_cfg is not None:
                off = sel_stride * (s + sel_pair_mul * blk) + sel_bpt * j
                kb = TK // sel_bpt
                qcol = pl.ds(pl.multiple_of(qi * TQ, TQ), TQ)
                st = jnp.concatenate(
                    [st[bk * kb:(bk + 1) * kb] + sel_ref[pl.ds(off + bk, 1), qcol]
                     for bk in range(sel_bpt)], axis=0)
            st_scr[slot][s] = st

    def accumulate(slot, j, alphas, accs):
        krow = pl.multiple_of(j * TK, TK)
        new = []
        for s in range(2):
            vt = vt_ref[s * HEAD_DIM:(s + 1) * HEAD_DIM, pl.ds(krow, TK)]
            new.append(alphas[s] * accs[s] + _dot(vt, p_scr[slot][s]))
        return tuple(new)

    def finalize(qi, stats, accs):
        out_t = jnp.concatenate([acc / l for acc, (_, l) in zip(accs, stats)], axis=0)
        o_ref[out_rows(qi, TQ), :] = out_t.T.astype(o_ref.dtype)
        if want_lse:
            lse_t = jnp.concatenate([jnp.broadcast_to((m + jnp.log2(l)) * LN2, (HEAD_DIM, TQ))
                                     for (m, l) in stats], axis=0)
            lse_ref[out_rows(qi, TQ), :] = lse_t.T

    def advance(q, j):
        last = j == q
        at_end = jnp.logical_and(last, q == n_qt - 1)
        starts = jnp.logical_and(last, jnp.logical_not(at_end))
        qn = jnp.where(starts, q + 1, q)
        jn = jnp.where(at_end, j, jnp.where(last, first_key_tile(q + 1), j + 1))
        return qn, jn, starts, at_end

    def softmax(slot, tile, stats):
        q, j, is_first, filler = tile
        keep = jnp.where(is_first, 0.0, 1.0)
        d = jnp.where(filler, n_tab, jnp.minimum(q - j, n_tab - 1))
        new_stats, alphas = [], []
        for s in range(2):
            m, l = stats[s]
            m = jnp.where(is_first, NEG_INF, m)
            st = st_scr[slot][s] + tab_ref[s, d]
            m_new = jnp.maximum(m, jnp.max(st, axis=0, keepdims=True))
            alpha = jnp.exp2(m - m_new) * keep
            p = jnp.exp2(st - m_new)
            new_stats.append((m_new, alpha * l + jnp.sum(p, axis=0, keepdims=True)))
            alphas.append(alpha)
            p_scr[slot][s] = p.astype(BF16)
        return tuple(new_stats), tuple(alphas)

    def body(u, carry):
        tile_a, (q2, j2), (q1, j1, first1), stats2, stats1, alphas2, alphas1, accs = carry
        qa, ja, first_a, _ = tile_a
        tile_b = advance(qa, ja)
        qb, jb, first_b, _ = tile_b
        tile_c = advance(qb, jb)
        logits_to(1, qb, jb)
        accs_x = accumulate(0, j2, alphas2, accs)
        accs_y = accumulate(1, j1, alphas1, accs_x)
        stats_a, alphas_a = softmax(0, tile_a, stats1)
        logits_to(0, tile_c[0], tile_c[1])
        stats_b, alphas_b = softmax(1, tile_b, stats_a)

        @pl.when(jnp.logical_and(first1, u > 0))
        def _():
            finalize(q2, stats2, accs_x)

        @pl.when(jnp.logical_and(first_a, u > 0))
        def _():
            finalize(q1, stats1, accs_y)

        return tile_c, (qa, ja), (qb, jb, first_b), stats_a, stats_b, alphas_a, alphas_b, accs_y

    zero_i = jnp.int32(0)
    logits_to(0, zero_i, zero_i)
    for p_slot in p_scr:
        p_slot[...] = jnp.zeros(p_slot.shape, BF16)
    stats0 = tuple((jnp.full((1, TQ), NEG_INF, F32), jnp.zeros((1, TQ), F32)) for _ in range(2))
    ones = tuple(jnp.ones((1, TQ), F32) for _ in range(2))
    init = ((zero_i, zero_i, zero_i == 0, zero_i != 0), (zero_i, zero_i), (zero_i, zero_i, zero_i != 0),
            stats0, stats0, ones, ones, tuple(jnp.zeros((HEAD_DIM, TQ), F32) for _ in range(2)))
    _, (q2, j2), (q1, j1, first1), stats2, stats1, alphas2, alphas1, accs = lax.fori_loop(
        0, (n_tiles + 1) // 2, body, init)
    accs_x = accumulate(0, j2, alphas2, accs)

    @pl.when(first1)
    def _():
        finalize(q2, stats2, accs_x)

    finalize(q1, stats1, accumulate(1, j1, alphas1, accs_x))


def _attn(q, k, vt, tab, sel, *, qc0, kc0, vb0, n_blk, qw, backs=None, kv_shared=False,
          tab_shared=False, sel_cfg=None, out_dtype=F32, want_lse=False, stride=1):
    b, s = q.shape[:2]
    n_tab = tab.shape[2] - 1
    n_qt = s // TQ
    if backs is None:
        backs = (n_qt,)
    cfg = (n_tab, qw, backs, sel_cfg, want_lse, n_qt, stride)
    kidx = (lambda bb, h: (bb, 0, kc0)) if kv_shared else (lambda bb, h: (bb, 0, kc0 + h))
    vidx = (lambda bb, h: (bb, vb0, 0)) if kv_shared else (lambda bb, h: (bb, vb0 + h, 0))
    tidx = (lambda bb, h: (0, 0, 0, 0, 0)) if tab_shared else (lambda bb, h: (h, 0, 0, 0, 0))
    in_specs = [pl.BlockSpec((None, s, qw * LANES), lambda bb, h: (bb, 0, qc0 + h)),
                pl.BlockSpec((None, s, qw * LANES), kidx),
                pl.BlockSpec((None, 2 * HEAD_DIM, s), vidx),
                pl.BlockSpec((None, 2, n_tab + 1, TK, TQ), tidx)]
    args = [q, k, vt, tab]
    if sel_cfg is not None:
        in_specs.append(pl.BlockSpec((None, LANES, s), lambda bb, h: (bb, 0, 0)))
        args.append(sel)
    ospec = pl.BlockSpec((None, s, LANES), lambda bb, h: (bb, 0, h))
    out_specs = [ospec]
    out_shape = [jax.ShapeDtypeStruct((b, s, n_blk * LANES), out_dtype)]
    if want_lse:
        out_specs.append(ospec)
        out_shape.append(jax.ShapeDtypeStruct((b, s, n_blk * LANES), F32))
    return pl.pallas_call(
        functools.partial(_attn_kernel, cfg),
        grid=(b, n_blk),
        in_specs=in_specs,
        out_specs=out_specs,
        out_shape=out_shape,
        scratch_shapes=[pltpu.VMEM((2, TK, TQ), F32)] * 2 + [pltpu.VMEM((2, TK, TQ), BF16)] * 2,
        compiler_params=_params("parallel", "parallel"),
        name="attn",
    )(*args)


def _nsa_cmp_kernel(kc_ref, vc_ref, pe_ref, wlo_ref, whi_ref, w2_ref, gm_ref, gain_ref, kcmp_ref, vcmpt_ref):
    def hidden(c, j):
        lo = _dot((c + pe_ref[j, 0]).astype(BF16), wlo_ref[j])
        hi = _dot((c + pe_ref[j, 1]).astype(BF16), whi_ref[j])
        h = lo + pltpu.roll(hi, hi.shape[0] - 1, 0)
        return (h * _sigmoid(h)).astype(BF16)

    kz = _dot_nt(hidden(kc_ref[...], 0), w2_ref[0])
    msq = _dot_hilo(kz * kz, gm_ref[...])
    kcmp_ref[...] = (kz * lax.rsqrt(msq + RMS_EPS) * gain_ref[...]).astype(BF16)
    vcmpt_ref[...] = _dot_nt(w2_ref[1], hidden(vc_ref[...], 1)).astype(BF16)


def _nsa_cmp(kc2, vc2, pe, wlo, whi, w2, gm, gain):
    b, nch, width = kc2.shape
    consts = [pe, wlo, whi, w2, gm, gain]
    blk = pl.BlockSpec((None, nch, width), lambda i: (i, 0, 0))
    oblk = pl.BlockSpec((None, nch, LANES), lambda i: (i, 0, 0))
    return pl.pallas_call(
        _nsa_cmp_kernel,
        grid=(b,),
        in_specs=[blk, blk] + [_resident(a.shape) for a in consts],
        out_specs=[oblk, oblk],
        out_shape=[jax.ShapeDtypeStruct((b, nch, LANES), BF16)] * 2,
        compiler_params=_params("parallel"),
        name="nsa_cmp",
    )(kc2, vc2, *consts)


def _rank_keep(score, ids, top):
    cnt = jnp.zeros(score.shape, jnp.int32)
    for mp in range(score.shape[0]):
        other = score[mp:mp + 1, :]
        tie = jnp.where(mp < ids, 1, 0)
        cnt = cnt + jnp.where(other > score, 1, jnp.where(other == score, tie, 0))
    return cnt < top


def _nsa_sel_kernel(q_ref, kcmp_ref, vcmpt_ref, bias_ref, ovl_ref, oc_ref, sel_ref):
    qi = pl.program_id(1)
    lane = lax.broadcasted_iota(jnp.int32, (1, LANES), 1)
    row = lax.broadcasted_iota(jnp.int32, (LANES, 1), 0)
    t = qi * TQ + lax.broadcasted_iota(jnp.int32, (1, TQ), 1)
    mask_c = (NSA_CMP_STRIDE * row + NSA_CMP_LEN - 1) <= t
    kcmp = kcmp_ref[...]
    heads = [(p, g) for p in range(NSA_HPG) for g in range(NSA_GROUPS)]
    raw = []
    for p, g in heads:
        qp = q_ref[:, p * LANES:(p + 1) * LANES]
        raw.append(_dot_nt(kcmp, jnp.where((lane >> 6) == g, qp, jnp.zeros_like(qp))))
    pcs = []
    for (p, g), r in zip(heads, raw):
        lg = jnp.where(mask_c, r * LN2 + bias_ref[p * NSA_GROUPS + g], NEG_INF)
        m = jnp.max(lg, axis=0, keepdims=True)
        e = jnp.where(mask_c, jnp.exp(lg - m), 0.0)
        den = jnp.maximum(jnp.sum(e, axis=0, keepdims=True), 1e-30)
        pcs.append(e / den)
    imp = jnp.zeros((LANES, TQ), F32)
    ocs = []
    for (p, g), pc in zip(heads, pcs):
        hi, lo = _split(pc)
        ocs.append(_dot(vcmpt_ref[g * HEAD_DIM:(g + 1) * HEAD_DIM, :], hi))
        imp = imp + _dot(ovl_ref[g], hi) + _dot(ovl_ref[g], lo)
    for p in range(NSA_HPG):
        oc_ref[:, p * LANES:(p + 1) * LANES] = jnp.concatenate([ocs[2 * p], ocs[2 * p + 1]], axis=0).T

    n_slc = 32
    ids = row[0:n_slc]
    cur = t >> 6
    forced = (ids == 0) | (ids == cur) | (ids == cur - 1)
    masks = []
    for g in range(NSA_GROUPS):
        score = jnp.where(forced, NSA_FORCED, jnp.where(ids <= cur, imp[g * n_slc:(g + 1) * n_slc], NEG_INF))
        keep = _rank_keep(score, ids, NSA_SLC_TOP) & (score > 0.5 * NEG_INF)
        masks.append(jnp.where(keep, 0.0, NEG_INF))
    masks.append(jnp.full((LANES - NSA_GROUPS * n_slc, TQ), NEG_INF, F32))
    sel_ref[...] = jnp.concatenate(masks, axis=0)


def _nsa_sel(q, kcmp, vcmpt, bias_c, ovl):
    b, s, w = q.shape
    return pl.pallas_call(
        _nsa_sel_kernel,
        grid=(b, s // TQ),
        in_specs=[pl.BlockSpec((None, TQ, w), lambda bb, i: (bb, i, 0)),
                  pl.BlockSpec((None, LANES, LANES), lambda bb, i: (bb, 0, 0)),
                  pl.BlockSpec((None, LANES, LANES), lambda bb, i: (bb, 0, 0)),
                  pl.BlockSpec((NSA_HEADS, LANES, TQ), lambda bb, i: (0, 0, i)),
                  _resident(ovl.shape)],
        out_specs=[pl.BlockSpec((None, TQ, w), lambda bb, i: (bb, i, 0)),
                   pl.BlockSpec((None, LANES, TQ), lambda bb, i: (bb, 0, i))],
        out_shape=[jax.ShapeDtypeStruct((b, s, w), F32), jax.ShapeDtypeStruct((b, LANES, s), F32)],
        compiler_params=_params("parallel", "parallel"),
        name="nsa_sel",
    )(q, kcmp, vcmpt, bias_c, ovl)


def _moba_gate_kernel(q_ref, k_ref, avg_ref, sel_ref):
    qi = pl.program_id(1)
    nb = 8
    kmean = _dot(avg_ref[...], k_ref[...])
    r2 = lax.broadcasted_iota(jnp.int32, kmean.shape, 0)
    c2 = lax.broadcasted_iota(jnp.int32, kmean.shape, 1)
    kmean = jnp.where((r2 >> 3) == (c2 >> 6), kmean, 0.0)
    kh, kl = _split(kmean)
    q = q_ref[...]
    gate = _dot_nt(kh, q) + _dot_nt(kl, q)
    ids = lax.broadcasted_iota(jnp.int32, (nb, 1), 0)
    past = ids < qi
    masks = []
    for h in range(MOBA_HEADS):
        score = jnp.where(past, gate[h * nb:(h + 1) * nb], NEG_INF)
        keep = (_rank_keep(score, ids, MOBA_TOP) & past) | (ids == qi)
        masks.append(jnp.where(keep, 0.0, NEG_INF))
    masks.append(jnp.full((LANES - MOBA_HEADS * nb, TQ), NEG_INF, F32))
    sel_ref[...] = jnp.concatenate(masks, axis=0)


def _moba_gate(p_arr, avg, qc, kc):
    b, s, _ = p_arr.shape
    w = MOBA_HEADS * HEAD_DIM
    return pl.pallas_call(
        _moba_gate_kernel,
        grid=(b, s // TQ),
        in_specs=[pl.BlockSpec((None, TQ, w), lambda bb, i: (bb, i, qc)),
                  pl.BlockSpec((None, s, w), lambda bb, i: (bb, 0, kc)),
                  _resident(avg.shape)],
        out_specs=pl.BlockSpec((None, LANES, TQ), lambda bb, i: (bb, 0, i)),
        out_shape=jax.ShapeDtypeStruct((b, LANES, s), F32),
        compiler_params=_params("parallel", "parallel"),
        name="moba_gate",
    )(p_arr, p_arr, avg)


def _even_out_kernel(x_ref, oa_ref, oc_ref, os_ref, ow_ref, gl_ref, eg_ref, wa_ref, wb_ref, gt_ref, o_ref):
    sg = _sigmoid(gl_ref[...])
    hi, lo = _split(sg)
    nsa = None
    for br, src in enumerate((oc_ref, os_ref, ow_ref)):
        gexp = _dot(hi, eg_ref[br]) + _dot(lo, eg_ref[br])
        term = gexp * src[...]
        nsa = term if nsa is None else nsa + term
    m = _dot(oa_ref[...], wa_ref[...]) + _dot(nsa.astype(BF16), wb_ref[...])
    o_ref[...] = x_ref[...] + gt_ref[...] * m


def _even_out(x, o_a, o_c, o_s, o_w, mla_in, eg, wa, wb, gate, s):
    n, d = x.shape
    tpb = s // TM
    row = lambda wd: pl.BlockSpec((TM, wd), lambda i: (i, 0))
    return pl.pallas_call(
        _even_out_kernel,
        grid=(n // TM,),
        in_specs=[row(d), row(o_a.shape[1]), row(o_c.shape[1]), row(o_s.shape[1]), row(o_w.shape[1]),
                  pl.BlockSpec((TM, LANES), lambda i: (i, 3)),
                  _resident(eg.shape), _resident(wa.shape), _resident(wb.shape), _mod_spec(tpb, d)],
        out_specs=row(d),
        out_shape=jax.ShapeDtypeStruct((n, d), F32),
        compiler_params=_params("parallel"),
        name="even_out",
    )(x, o_a, o_c, o_s, o_w, mla_in, eg, wa, wb, gate)


def _odd_out_kernel(x_ref, od0_ref, od1_ref, od2_ref, ls0_ref, ls1_ref, ls2_ref, om_ref, wd_ref, wm_ref, gt_ref, o_ref):
    ods = (od0_ref, od1_ref, od2_ref)
    ls = [r[...] for r in (ls0_ref, ls1_ref, ls2_ref)]
    mx = jnp.maximum(jnp.maximum(ls[0], ls[1]), ls[2])
    es = [jnp.exp(l - mx) for l in ls]
    den = es[0] + es[1] + es[2]
    merged = None
    for g in range(len(DIL_PAIRS)):
        term = (es[g] / den) * ods[g][...]
        merged = term if merged is None else merged + term
    m = _dot(merged.astype(BF16), wd_ref[...]) + _dot(om_ref[...], wm_ref[...])
    o_ref[...] = x_ref[...] + gt_ref[...] * m


def _odd_out(x, o_ds, lse_ds, o_m, wd, wm, gate, s):
    n, d = x.shape
    tpb = s // TM
    row = lambda wd_: pl.BlockSpec((TM, wd_), lambda i: (i, 0))
    return pl.pallas_call(
        _odd_out_kernel,
        grid=(n // TM,),
        in_specs=[row(d)] + [row(a.shape[1]) for a in (*o_ds, *lse_ds)] + [row(o_m.shape[1]),
                  _resident(wd.shape), _resident(wm.shape), _mod_spec(tpb, d)],
        out_specs=row(d),
        out_shape=jax.ShapeDtypeStruct((n, d), F32),
        compiler_params=_params("parallel"),
        name="odd_out",
    )(x, *o_ds, *lse_ds, o_m, wd, wm, gate)


def _t5_bucket(dist):
    n = jnp.maximum(jnp.asarray(dist, jnp.int32), 0)
    nf = jnp.maximum(n, 1).astype(F32)
    large = T5_MAX_EXACT + (jnp.log(nf / T5_MAX_EXACT) / math.log(T5_MAX_DIST / T5_MAX_EXACT)
                            * (NUM_BUCKETS - T5_MAX_EXACT)).astype(jnp.int32)
    return jnp.where(n < T5_MAX_EXACT, n, jnp.minimum(large, NUM_BUCKETS - 1))


TOEP_PERIOD = 4 * TQ


def _toeplitz_dist():
    j = np.arange(TOEP_PERIOD)
    return np.where(j < 3 * TQ, j, j - TOEP_PERIOD)


def _toeplitz_tiles(u, n_tab):
    h = u.shape[0]
    m = jnp.tile(u, (1, TK))[:, :TK * (TOEP_PERIOD - 1)].reshape(h, TK, TOEP_PERIOD - 1)[:, :, :n_tab * TQ]
    tiles = jnp.transpose(m.reshape(h, TK, n_tab, TQ), (0, 2, 1, 3))
    filler = jnp.full((h, 1, TK, TQ), NEG_INF * LOG2E, F32)
    return jnp.concatenate([tiles, filler], axis=1)


def _bias_tiles(t5_cols, ok, n_tab=3, dist_scale=1):
    dist = _toeplitz_dist()
    bias = jnp.transpose(t5_cols[_t5_bucket(dist * dist_scale)])
    u = jnp.where(jnp.asarray(ok)[None], bias, NEG_INF) * LOG2E
    return _toeplitz_tiles(u, n_tab)


def _pair(tiles):
    h = tiles.shape[0]
    return tiles.reshape(h // 2, 2, *tiles.shape[1:])


def _group_mean_np(sizes, width=LANES):
    gm = np.zeros((width, width), np.float32)
    o = 0
    for sz in sizes:
        gm[o:o + sz, o:o + sz] = 1.0 / sz
        o += sz
    return gm


def _group_mean_matrix(sizes, width=LANES):
    return jnp.asarray(_group_mean_np(sizes, width), BF16)


EV_META = ((0, 0, False), (0, 128, False), (0, 256, False), (0, 384, False),
           (1, 0, True), (1, 128, True), (1, 256, True), (1, 384, True),
           (2, 0, False), (3, 0, False), (4, 0, True), (5, 0, False), (6, 0, True), (7, 0, False))
EV_OUTS = [(512, F32), (512, BF16), (128, F32), (128, F32), (128, BF16), (128, BF16), (128, BF16), (128, BF16)]
OD_META = tuple((0, cidx * LANES, (cidx < 12) or (18 <= cidx < 22)) for cidx in range(24))


def _even_w_in(w):
    jn, d, _ = w.shape
    z = lambda n_: jnp.zeros((jn, d, n_), w.dtype)
    nq = w[:, :, 416:928].reshape(jn, d, NSA_GROUPS, NSA_HPG, HEAD_DIM)
    nq = jnp.transpose(nq, (0, 1, 3, 2, 4)).reshape(jn, d, NSA_HEADS * HEAD_DIM)
    chunk3 = jnp.concatenate([w[:, :, 1696:1720], z(HEAD_DIM - 24), w[:, :, 384:416], z(LANES - 96)], axis=-1)
    return jnp.concatenate([w[:, :, 0:384], chunk3, nq, w[:, :, 928:1696]], axis=-1)


def kernel(x, c, t5_bias, ada_w, ada_b, norm_g, ffn_w_in, ffn_w_out, ev_w_in, ev_w_out, mla_q_norm_g,
           mla_kv_norm_g, mla_w_uq, mla_w_ukv, mla_qk_g, nsa_cmp_pe, nsa_cmp_w1, nsa_cmp_w2, nsa_qk_g,
           od_w_in, od_w_out, dil_qk_g, moba_qk_g):
    b, s, d = x.shape
    assert (s, d) == (2048, D_MODEL) and s % TM == 0 and TQ == MOBA_BLOCK and TQ == TK
    n = b * s
    hd = HEAD_DIM
    n_even = ev_w_in.shape[0]
    n_odd = od_w_in.shape[0]
    c64 = hd ** -0.5 * LOG2E
    c96 = (MLA_NOPE + MLA_ROPE) ** -0.5 * LOG2E

    mod = _ada(c, ada_w, ada_b).reshape(DEPTH, b, 3, 3, 1, d)

    dist = _toeplitz_dist()
    causal = dist >= 0
    gm64 = _group_mean_matrix((hd, hd))
    gm_proj = _group_mean_matrix((hd,) * (PROJ_CHUNK // hd), PROJ_CHUNK)
    per_chunk = PROJ_CHUNK // LANES
    chunked = lambda m: tuple(tuple(m[i:i + per_chunk]) for i in range(0, len(m), per_chunk))

    padc = FF_PAD - D_FF
    wa_all = jnp.pad(ffn_w_in[..., :D_FF], ((0, 0), (0, 0), (0, 0), (0, padc))).astype(BF16)
    wb_all = jnp.pad(ffn_w_in[..., D_FF:], ((0, 0), (0, 0), (0, 0), (0, padc))).astype(BF16)
    wo_all = jnp.pad(ffn_w_out, ((0, 0), (0, 0), (0, padc), (0, 0))).astype(BF16)

    nsa_tab = t5_bias[:, MLA_HEADS:MLA_HEADS + NSA_HEADS].reshape(NUM_BUCKETS, NSA_GROUPS, NSA_HPG)
    nsa_cols = jnp.transpose(nsa_tab, (0, 2, 1)).reshape(NUM_BUCKETS, NSA_HEADS)
    tab_sel = _pair(_bias_tiles(nsa_cols, causal))
    tab_win = _pair(_bias_tiles(nsa_cols, causal & (dist <= NSA_WINDOW - 1)))
    tab_mla = _toeplitz_tiles(jnp.where(jnp.asarray(causal), 0.0, NEG_INF).astype(F32)[None], 2)
    tab_mla = jnp.broadcast_to(tab_mla[None], (1, 2) + tab_mla.shape[1:])
    n_cmp_pad = s // NSA_CMP_STRIDE
    jj = np.arange(2 * n_cmp_pad)
    kk = np.where(jj < n_cmp_pad, jj, jj - 2 * n_cmp_pad)
    dc = NSA_CMP_STRIDE * (-kk)[None, :] + np.arange(NSA_CMP_STRIDE)[:, None] - (NSA_CMP_LEN - 1)
    ub = jnp.moveaxis(nsa_cols[_t5_bucket(dc)], -1, 0)
    mb = jnp.tile(ub, (1, 1, n_cmp_pad))[:, :, :n_cmp_pad * (2 * n_cmp_pad - 1)]
    mb = mb.reshape(NSA_HEADS, NSA_CMP_STRIDE, n_cmp_pad, 2 * n_cmp_pad - 1)[..., :n_cmp_pad]
    bias_c = jnp.transpose(mb, (0, 3, 2, 1)).reshape(NSA_HEADS, n_cmp_pad, s)
    n_cmp = (s - NSA_CMP_LEN) // NSA_CMP_STRIDE + 1
    cstart = np.arange(n_cmp) * NSA_CMP_STRIDE
    sstart = np.arange(s // NSA_SLC_BLOCK) * NSA_SLC_BLOCK
    overlap = np.clip(np.minimum(cstart[:, None] + NSA_CMP_LEN, sstart[None, :] + NSA_SLC_BLOCK)
                      - np.maximum(cstart[:, None], sstart[None, :]), 0, None).astype(np.float32) / NSA_CMP_LEN
    ovl = np.zeros((NSA_GROUPS, LANES, LANES), np.float32)
    for g in range(NSA_GROUPS):
        ovl[g, 32 * g:32 * g + 32, :n_cmp] = overlap.T
    ovl = jnp.asarray(ovl, BF16)
    eg = np.zeros((3, LANES, NSA_HEADS * hd), np.float32)
    for g in range(NSA_GROUPS):
        for p in range(NSA_HPG):
            for br in range(3):
                eg[br, (g * NSA_HPG + p) * 3 + br, p * LANES + g * hd:p * LANES + (g + 1) * hd] = 1.0
    eg = jnp.asarray(eg, BF16)
    gm_mla = jnp.asarray(np.kron(np.eye(2, dtype=np.float32), _group_mean_np((MLA_NOPE, MLA_ROPE))), BF16)
    rot = np.zeros((LANES, LANES), np.float32)
    half = MLA_ROPE // 2
    for i in range(half):
        rot[MLA_NOPE + half + i, MLA_NOPE + i] = -1.0
        rot[MLA_NOPE + i, MLA_NOPE + half + i] = 1.0
    rot = jnp.asarray(np.kron(np.eye(2, dtype=np.float32), rot), BF16)
    inv = ROPE_THETA ** (-jnp.arange(0, MLA_ROPE, 2, dtype=F32) / MLA_ROPE)
    ang = jnp.arange(s, dtype=F32)[:, None] * inv[None, :]
    ones = jnp.ones((s, MLA_NOPE), F32)
    tail = LANES - MLA_NOPE - MLA_ROPE
    cos_t = jnp.tile(jnp.concatenate([ones, jnp.cos(ang), jnp.cos(ang), jnp.ones((s, tail), F32)], axis=1), (1, 2))
    sin_t = jnp.tile(jnp.concatenate([0 * ones, jnp.sin(ang), jnp.sin(ang), jnp.zeros((s, tail), F32)], axis=1), (1, 2))

    ev_w = _even_w_in(ev_w_in).astype(BF16)
    ev_gain = jnp.ones((n_even, 14 * LANES), F32)
    ev_gain = ev_gain.at[:, 512:1024].set(jnp.tile(nsa_qk_g[:, 0], (1, 8)) * c64)
    ev_gain = ev_gain.at[:, 1280:1408].set(jnp.tile(nsa_qk_g[:, 1], (1, 2)))
    ev_gain = ev_gain.at[:, 1536:1664].set(jnp.tile(nsa_qk_g[:, 1], (1, 2)))
    gain_kc = jnp.tile(nsa_qk_g[:, 1], (1, 2))
    wuq = jnp.pad(mla_w_uq.reshape(n_even, MLA_Q_LORA, MLA_HEADS, MLA_NOPE + MLA_ROPE),
                  ((0, 0), (0, 0), (0, 0), (0, tail))).reshape(n_even, MLA_Q_LORA, MLA_HEADS * LANES).astype(BF16)
    ukv = mla_w_ukv.reshape(n_even, MLA_KV_LORA, MLA_HEADS, MLA_NOPE + MLA_V)
    wuk = jnp.pad(ukv[..., :MLA_NOPE], ((0, 0), (0, 0), (0, 0), (0, LANES - MLA_NOPE))
                  ).reshape(n_even, MLA_KV_LORA, MLA_HEADS * LANES).astype(BF16)
    wuv = ukv[..., MLA_NOPE:].reshape(n_even, MLA_KV_LORA, MLA_HEADS * MLA_V).astype(BF16)
    zt = jnp.zeros((n_even, tail), F32)
    gq = jnp.tile(jnp.concatenate([mla_qk_g[:, 0] * c96, zt], axis=1), (1, 2))
    gkn = jnp.tile(jnp.concatenate([mla_qk_g[:, 1, :MLA_NOPE], jnp.zeros((n_even, LANES - MLA_NOPE), F32)], axis=1), (1, 2))
    gkr = jnp.tile(jnp.concatenate([jnp.zeros((n_even, MLA_NOPE), F32), mla_qk_g[:, 1, MLA_NOPE:], zt], axis=1), (1, 2))
    pe2 = jnp.broadcast_to(nsa_cmp_pe.reshape(n_even, 2, 2, 16, 1, hd), (n_even, 2, 2, 16, NSA_GROUPS, hd)
                           ).reshape(n_even, 2, 2, 1, 16 * LANES)
    eye = jnp.eye(NSA_GROUPS, dtype=F32)
    w1 = nsa_cmp_w1.reshape(n_even, 2, 2, 16, hd, NSA_CMP_HID)
    w1x = jnp.einsum('ijaldc,gh->ijalgdhc', w1, eye).reshape(n_even, 2, 2, 16 * LANES, NSA_GROUPS * NSA_CMP_HID).astype(BF16)
    w2x = jnp.einsum('ijcd,gh->ijhdgc', nsa_cmp_w2, eye).reshape(n_even, 2, LANES, NSA_GROUPS * NSA_CMP_HID).astype(BF16)
    wa_o = ev_w_out[:, :MLA_HEADS * MLA_V].astype(BF16)
    wb_o = jnp.transpose(ev_w_out[:, MLA_HEADS * MLA_V:].reshape(n_even, NSA_GROUPS, NSA_HPG, hd, d),
                         (0, 2, 1, 3, 4)).reshape(n_even, NSA_HEADS * hd, d).astype(BF16)

    assert DIL_PAIRS == ((128, 1), (512, 4), (2048, 16))
    dil_cfg = ((1, 1), (4, 1), (8, 0))
    dil_ok = (causal & (dist <= 128), causal & (dist <= 128), causal & (dist % 2 == 0))
    tab_dil = [_pair(_bias_tiles(t5_bias[:, gi * DIL_HPG:(gi + 1) * DIL_HPG], dil_ok[gi], dist_scale=dil_cfg[gi][0]))
               for gi in range(len(DIL_PAIRS))]
    tab_moba = _pair(_bias_tiles(t5_bias[:, DIL_SLOTS:DIL_SLOTS + MOBA_HEADS], causal))
    avg = np.zeros((LANES, s), np.float32)
    for h in range(MOBA_HEADS):
        for m in range(s // MOBA_BLOCK):
            avg[8 * h + m, m * MOBA_BLOCK:(m + 1) * MOBA_BLOCK] = 1.0 / MOBA_BLOCK
    avg = jnp.asarray(avg, BF16)
    od_w = od_w_in.astype(BF16)
    od_gain = jnp.concatenate([jnp.tile(dil_qk_g[:, 0], (1, 12)) * c64, jnp.tile(dil_qk_g[:, 1], (1, 12)),
                               jnp.ones((n_odd, 768), F32), jnp.tile(moba_qk_g[:, 0], (1, 4)) * c64,
                               jnp.tile(moba_qk_g[:, 1], (1, 4)), jnp.ones((n_odd, 256), F32)], axis=1)
    wd_o = od_w_out[:, :DIL_HPG * hd].astype(BF16)
    wm_o = od_w_out[:, DIL_HPG * hd:].astype(BF16)

    sh3 = lambda a: a.reshape(b, s, a.shape[-1])
    tr3 = lambda a: jnp.swapaxes(sh3(a), 1, 2)
    xf = x.reshape(n, d)
    for i in range(DEPTH):
        j = i // 2
        g_i = norm_g[i].reshape(3, 1, d)
        xf = _ffn(xf, g_i[0], mod[i, :, 0, 0], mod[i, :, 0, 1], mod[i, :, 0, 2],
                  wa_all[i, 0], wb_all[i, 0], wo_all[i, 0], s)
        if i % 2 == 0:
            mla_in, nsa_q, kc, vc, ks, vs, kw, vw = _proj(
                xf, g_i[1], mod[i, :, 1, 0], mod[i, :, 1, 1], ev_w[j], ev_gain[j][None], gm_proj,
                chunked(EV_META), EV_OUTS, s)
            qf, kf, vf = _mla_prep(mla_in, mla_q_norm_g[j][None], mla_kv_norm_g[j][None], wuq[j], wuk[j], wuv[j],
                                   gm_mla, rot, gq[j][None], gkn[j][None], gkr[j][None], cos_t, sin_t, s)
            (o_a,) = _attn(sh3(qf), sh3(kf), tr3(vf), tab_mla, None, qc0=0, kc0=0, vb0=0, n_blk=MLA_HEADS // 2, qw=2,
                           tab_shared=True, out_dtype=BF16)
            kcmp, vcmpt = _nsa_cmp(kc.reshape(b, n_cmp_pad, 16 * LANES), vc.reshape(b, n_cmp_pad, 16 * LANES),
                                   pe2[j], w1x[j, :, 0], w1x[j, :, 1], w2x[j], gm64, gain_kc[j][None])
            o_c, sel = _nsa_sel(sh3(nsa_q), kcmp, vcmpt, bias_c, ovl)
            (o_s,) = _attn(sh3(nsa_q), sh3(ks), tr3(vs), tab_sel, sel, qc0=0, kc0=0, vb0=0, n_blk=NSA_HPG, qw=1,
                           kv_shared=True, sel_cfg=(TK // NSA_SLC_BLOCK, 32, 0))
            (o_w,) = _attn(sh3(nsa_q), sh3(kw), tr3(vw), tab_win, None, qc0=0, kc0=0, vb0=0, n_blk=NSA_HPG, qw=1,
                           kv_shared=True, backs=(2, 2, 2, 2))
            xf = _even_out(xf, o_a.reshape(n, -1), o_c.reshape(n, -1), o_s.reshape(n, -1), o_w.reshape(n, -1),
                           mla_in, eg, wa_o[j], wb_o[j], mod[i, :, 1, 2], s)
        else:
            (pr,) = _proj(xf, g_i[1], mod[i, :, 1, 0], mod[i, :, 1, 1], od_w[j], od_gain[j][None], gm_proj,
                          chunked(OD_META), [(24 * LANES, BF16)], s)
            pr3 = pr.reshape(b, s, 24 * LANES)
            vmt = jnp.swapaxes(pr3[:, :, 22 * LANES:24 * LANES], 1, 2)
            o_ds, lse_ds = [], []
            for gi, (r, bk) in enumerate(dil_cfg):
                cols = lambda c0: pr3[:, :, c0 * LANES:(c0 + 2) * LANES].reshape(b, s // r, r, 2 * LANES)
                vdt = jnp.transpose(cols(12 + 2 * gi), (0, 3, 2, 1)).reshape(b, 2 * LANES, s)
                if r == 1:
                    qk, qc0, kc0 = pr3, 0, 6
                else:
                    qk = jnp.concatenate([cols(2 * gi), cols(6 + 2 * gi)], axis=-1)
                    qk, qc0, kc0 = jnp.transpose(qk, (0, 2, 1, 3)).reshape(b, s, 4 * LANES), 0, 2
                o_g, lse_g = _attn(qk, qk, vdt, tab_dil[gi], None, qc0=qc0, kc0=kc0, vb0=0, n_blk=2,
                                   qw=1, backs=(bk,), want_lse=True, stride=r)
                o_ds.append(o_g.reshape(n, -1))
                lse_ds.append(lse_g.reshape(n, -1))
            selm = _moba_gate(pr3, avg, 9, 10)
            (o_m,) = _attn(pr3, pr3, vmt, tab_moba, selm, qc0=18, kc0=20, vb0=0, n_blk=2, qw=1,
                           sel_cfg=(1, 8, 2), out_dtype=BF16)
            xf = _odd_out(xf, o_ds, lse_ds, o_m.reshape(n, -1), wd_o[j], wm_o[j], mod[i, :, 1, 2], s)
        xf = _ffn(xf, g_i[2], mod[i, :, 2, 0], mod[i, :, 2, 1], mod[i, :, 2, 2],
                  wa_all[i, 1], wb_all[i, 1], wo_all[i, 1], s)
    return xf.reshape(b, s, d)
```

```python
import functools
import math

import numpy as np
import jax
import jax.numpy as jnp
from jax import lax
from jax.experimental import pallas as pl
from jax.experimental.pallas import tpu as pltpu

F32 = jnp.float32
BF16 = jnp.bfloat16

D_MODEL = 1024
DEPTH = 4
D_FF = 2752
HEAD_DIM = 64
NUM_BUCKETS = 32
T5_MAX_EXACT = 16
T5_MAX_DIST = 128
RMS_EPS = 1e-6
NEG_INF = -1e30
MLA_HEADS = 8
MLA_NOPE = 64
MLA_ROPE = 32
MLA_V = 64
MLA_Q_LORA = 256
MLA_KV_LORA = 128
ROPE_THETA = 10000.0
NSA_HEADS = 8
NSA_GROUPS = 2
NSA_HPG = 4
NSA_CMP_LEN = 32
NSA_CMP_STRIDE = 16
NSA_CMP_HID = 256
NSA_SLC_BLOCK = 64
NSA_SLC_TOP = 8
NSA_WINDOW = 512
NSA_FORCED = 1e6
DIL_PAIRS = ((128, 1), (512, 4), (2048, 16))
DIL_HPG = 4
DIL_SLOTS = len(DIL_PAIRS) * DIL_HPG
MOBA_HEADS = 4
MOBA_BLOCK = 256
MOBA_TOP = 3

LANES = 128
V7X_VMEM_BYTES = 64 * 1024 * 1024
VMEM_LIMIT = V7X_VMEM_BYTES * 7 // 8
TM = 512
TQ = 256
TK = 256
FF_CHUNK = 256
FF_PAD = -(-D_FF // FF_CHUNK) * FF_CHUNK
PROJ_CHUNK = 256
LOG2E = math.log2(math.e)
LN2 = math.log(2.0)


def _dot(a, b):
    return jnp.dot(a, b, preferred_element_type=F32)


def _dot_nt(a, b):
    return lax.dot_general(a, b, (((1,), (1,)), ((), ())), preferred_element_type=F32)


def _split(a):
    hi = a.astype(BF16)
    lo = (a - hi.astype(F32)).astype(BF16)
    return hi, lo


def _dot_hilo(a, b):
    hi, lo = _split(a)
    return _dot(hi, b) + _dot(lo, b)


def _sigmoid(x):
    return 1.0 / (1.0 + jnp.exp(-x))


def _modulated_norm(x, g, shift, scale):
    ms = jnp.mean(x * x, axis=-1, keepdims=True)
    y = x * lax.rsqrt(ms + RMS_EPS) * g
    return y * (1.0 + scale) + shift


def _params(*sem):
    return pltpu.CompilerParams(dimension_semantics=sem, vmem_limit_bytes=VMEM_LIMIT)


def _resident(shape):
    nd = len(shape)
    return pl.BlockSpec(shape, lambda *_: (0,) * nd, pipeline_mode=pl.Buffered(1))


def _ada_kernel(c_ref, w_ref, b_ref, o_ref):
    c = c_ref[...]
    ca = c * _sigmoid(c)
    o_ref[...] = jnp.dot(ca, w_ref[...], preferred_element_type=F32,
                         precision=lax.Precision.HIGHEST) + b_ref[...]


def _ada(c, ada_w, ada_b):
    depth, d, n = ada_w.shape
    b = c.shape[0]
    tn = 9 * LANES
    return pl.pallas_call(
        _ada_kernel,
        grid=(depth, n // tn),
        in_specs=[pl.BlockSpec((b, d), lambda l, j: (0, 0)),
                  pl.BlockSpec((None, d, tn), lambda l, j: (l, 0, j)),
                  pl.BlockSpec((None, 1, tn), lambda l, j: (l, 0, j))],
        out_specs=pl.BlockSpec((None, b, tn), lambda l, j: (l, 0, j)),
        out_shape=jax.ShapeDtypeStruct((depth, b, n), F32),
        compiler_params=_params("parallel", "parallel"),
        name="ada",
    )(c, ada_w, ada_b.reshape(depth, 1, n))


def _ffn_kernel(x_ref, g_ref, sh_ref, sc_ref, gt_ref, wab_ref, wo_ref, o_ref, y_ref, acc_ref):
    y_ref[...] = _modulated_norm(x_ref[...], g_ref[...], sh_ref[...], sc_ref[...]).astype(BF16)
    for c in range(FF_PAD // FF_CHUNK):
        sl = slice(c * FF_CHUNK, (c + 1) * FF_CHUNK)
        a = _dot(y_ref[...], wab_ref[:, sl])
        b = _dot(y_ref[...], wab_ref[:, FF_PAD + c * FF_CHUNK:FF_PAD + (c + 1) * FF_CHUNK])
        u = (a * _sigmoid(a) * b).astype(BF16)
        contrib = _dot(u, wo_ref[sl, :])
        if c == 0:
            acc_ref[...] = contrib
        else:
            acc_ref[...] += contrib
    o_ref[...] = x_ref[...] + 0.5 * gt_ref[...] * acc_ref[...]


def _mod_spec(tiles_per_batch, d):
    return pl.BlockSpec((None, 1, d), lambda i: (i // tiles_per_batch, 0, 0))


def _ffn(x, g, shift, scale, gate, wab, wo, s):
    n, d = x.shape
    tpb = s // TM
    return pl.pallas_call(
        _ffn_kernel,
        grid=(n // TM,),
        in_specs=[pl.BlockSpec((TM, d), lambda i: (i, 0)),
                  _resident((1, d)),
                  _mod_spec(tpb, d), _mod_spec(tpb, d), _mod_spec(tpb, d),
                  _resident(wab.shape), _resident(wo.shape)],
        out_specs=pl.BlockSpec((TM, d), lambda i: (i, 0)),
        out_shape=jax.ShapeDtypeStruct((n, d), F32),
        scratch_shapes=[pltpu.VMEM((TM, d), BF16), pltpu.VMEM((TM, d), F32)],
        compiler_params=_params("parallel"),
        name="ffn",
    )(x, g, shift, scale, gate, wab, wo)


def _proj_kernel(meta, n_out, x_ref, g_ref, sh_ref, sc_ref, w_ref, gain_ref, gm_ref, *rest):
    outs = rest[:n_out]
    y_ref = rest[n_out]
    y_ref[...] = _modulated_norm(x_ref[...], g_ref[...], sh_ref[...], sc_ref[...]).astype(BF16)
    for c, halves in enumerate(meta):
        sl = slice(c * PROJ_CHUNK, (c + 1) * PROJ_CHUNK)
        z = _dot(y_ref[...], w_ref[:, sl])
        if any(normed for _, _, normed in halves):
            msq = _dot((z * z).astype(BF16), gm_ref[...])
            zn = z * lax.rsqrt(msq + RMS_EPS) * gain_ref[:, sl]
        for hf, (oi, off, normed) in enumerate(halves):
            src = zn if normed else z
            outs[oi][:, off:off + LANES] = src[:, hf * LANES:(hf + 1) * LANES].astype(outs[oi].dtype)


def _proj(x, g, shift, scale, w, gain, gm, meta, out_defs, s):
    n, d = x.shape
    tpb = s // TM
    return pl.pallas_call(
        functools.partial(_proj_kernel, meta, len(out_defs)),
        grid=(n // TM,),
        in_specs=[pl.BlockSpec((TM, d), lambda i: (i, 0)),
                  _resident((1, d)),
                  _mod_spec(tpb, d), _mod_spec(tpb, d),
                  _resident(w.shape), _resident(gain.shape), _resident(gm.shape)],
        out_specs=[pl.BlockSpec((TM, wd), lambda i: (i, 0)) for wd, _ in out_defs],
        out_shape=[jax.ShapeDtypeStruct((n, wd), dt) for wd, dt in out_defs],
        scratch_shapes=[pltpu.VMEM((TM, d), BF16)],
        compiler_params=_params("parallel"),
        name="proj",
    )(x, g, shift, scale, w, gain, gm)


def _mla_prep_kernel(in_ref, qg_ref, kvg_ref, wuq_ref, wuk_ref, wuv_ref, gm_ref,
                     gq_ref, gkn_ref, gkr_ref, cos_ref, sin_ref, q_out, k_out, v_out):
    def rms(z, g):
        return z * lax.rsqrt(jnp.mean(z * z, axis=-1, keepdims=True) + RMS_EPS) * g

    cqn = rms(in_ref[:, 0:MLA_Q_LORA], qg_ref[...]).astype(BF16)
    ckvn = rms(in_ref[:, MLA_Q_LORA:MLA_Q_LORA + MLA_KV_LORA], kvg_ref[...]).astype(BF16)
    c3 = in_ref[:, 3 * LANES:4 * LANES]
    cos = cos_ref[...]
    sin = sin_ref[...]
    gm = gm_ref[...]
    half = MLA_ROPE // 2
    slot_lane = lax.broadcasted_iota(jnp.int32, (1, 2 * LANES), 1) & (LANES - 1)
    first_half = slot_lane < MLA_NOPE + half

    def norm_rope(z, gain):
        msq = _dot((z * z).astype(BF16), gm)
        z = z * lax.rsqrt(msq + RMS_EPS) * gain
        rot = jnp.where(first_half, -pltpu.roll(z, 2 * LANES - half, 1), pltpu.roll(z, half, 1))
        return z * cos + rot * sin

    kr = norm_rope(jnp.concatenate([c3, c3], axis=1), gkr_ref[...])
    for h in range(MLA_HEADS // 2):
        sl = slice(h * 2 * LANES, (h + 1) * 2 * LANES)
        q_out[:, sl] = norm_rope(_dot(cqn, wuq_ref[:, sl]), gq_ref[...]).astype(BF16)
        k_out[:, sl] = (norm_rope(_dot(ckvn, wuk_ref[:, sl]), gkn_ref[...]) + kr).astype(BF16)
    v_out[...] = _dot(ckvn, wuv_ref[...]).astype(BF16)


def _mla_prep(mla_in, qg, kvg, wuq, wuk, wuv, gm, gq, gkn, gkr, cos_t, sin_t, s):
    n = mla_in.shape[0]
    tpb = s // TM
    hw = MLA_HEADS * LANES
    vw = MLA_HEADS * MLA_V
    tab = pl.BlockSpec((TM, 2 * LANES), lambda i: (i % tpb, 0))
    consts = [qg, kvg, wuq, wuk, wuv, gm, gq, gkn, gkr]
    return pl.pallas_call(
        _mla_prep_kernel,
        grid=(n // TM,),
        in_specs=[pl.BlockSpec((TM, 4 * LANES), lambda i: (i, 0))] + [_resident(a.shape) for a in consts] + [tab, tab],
        out_specs=[pl.BlockSpec((TM, hw), lambda i: (i, 0)), pl.BlockSpec((TM, hw), lambda i: (i, 0)),
                   pl.BlockSpec((TM, vw), lambda i: (i, 0))],
        out_shape=[jax.ShapeDtypeStruct((n, hw), BF16), jax.ShapeDtypeStruct((n, hw), BF16),
                   jax.ShapeDtypeStruct((n, vw), BF16)],
        compiler_params=_params("parallel"),
        name="mla_prep",
    )(mla_in, *consts, cos_t, sin_t)


def _attn_kernel(cfg, q_ref, k_ref, vt_ref, tab_ref, *rest):
    n_tab, qw, backs, sel_cfg, want_lse, n_qt, stride = cfg
    n_qs = n_qt // stride
    if sel_cfg is not None:
        sel_ref, rest = rest[0], rest[1:]
        sel_bpt, sel_stride, sel_pair_mul = sel_cfg
    o_ref = rest[0]
    lse_ref = rest[1] if want_lse else None
    st_scr, p_scr = rest[-4:-2], rest[-2:]
    blk = pl.program_id(1)
    lane = lax.broadcasted_iota(jnp.int32, (1, LANES), 1)

    counts = tuple(sum(min(t % n_qs, bk) + 1 for t in range(n_qt)) for bk in backs)
    back = jnp.int32(backs[-1])
    n_tiles = jnp.int32(counts[-1])
    for bi in range(len(backs) - 2, -1, -1):
        back = jnp.where(blk == bi, jnp.int32(backs[bi]), back)
        n_tiles = jnp.where(blk == bi, jnp.int32(counts[bi]), n_tiles)

    def first_key_tile(qi):
        return qi - jnp.minimum(qi & (n_qs - 1), back)

    def rows(t, size):
        return pl.ds(pl.multiple_of(t * size, size), size)

    def out_rows(t, size):
        if stride == 1:
            return rows(t, size)
        return pl.ds((t >> int(math.log2(n_qs))) + stride * size * (t & (n_qs - 1)), size, stride=stride)

    def logits_to(slot, qi, j):
        if qw == 1:
            q = q_ref[rows(qi, TQ), :]
            zero = jnp.zeros_like(q)
            qs = [jnp.where(lane < HEAD_DIM, q, zero), jnp.where(lane >= HEAD_DIM, q, zero)]
        else:
            qs = [q_ref[rows(qi, TQ), s * LANES:(s + 1) * LANES] for s in range(2)]
        for s in range(2):
            kj = k_ref[rows(j, TK), :] if qw == 1 else k_ref[rows(j, TK), s * LANES:(s + 1) * LANES]
            st = _dot_nt(kj, qs[s])
            if sel_cfg is not None:
                off = sel_stride * (s + sel_pair_mul * blk) + sel_bpt * j
                kb = TK // sel_bpt
                qcol = pl.ds(pl.multiple_of(qi * TQ, TQ), TQ)
                st = jnp.concatenate(
                    [st[bk * kb:(bk + 1) * kb] + sel_ref[pl.ds(off + bk, 1), qcol]
                     for bk in range(sel_bpt)], axis=0)
            st_scr[slot][s] = st

    def accumulate(slot, j, alphas, accs):
        krow = pl.multiple_of(j * TK, TK)
        new = []
        for s in range(2):
            vt = vt_ref[s * HEAD_DIM:(s + 1) * HEAD_DIM, pl.ds(krow, TK)]
            new.append(alphas[s] * accs[s] + _dot(vt, p_scr[slot][s]))
        return tuple(new)

    def finalize(qi, stats, accs):
        out_t = jnp.concatenate([acc / l for acc, (_, l) in zip(accs, stats)], axis=0)
        o_ref[out_rows(qi, TQ), :] = out_t.T.astype(o_ref.dtype)
        if want_lse:
            lse_t = jnp.concatenate([jnp.broadcast_to((m + jnp.log2(l)) * LN2, (HEAD_DIM, TQ))
                                     for (m, l) in stats], axis=0)
            lse_ref[out_rows(qi, TQ), :] = lse_t.T

    def advance(q, j):
        last = j == q
        at_end = jnp.logical_and(last, q == n_qt - 1)
        starts = jnp.logical_and(last, jnp.logical_not(at_end))
        qn = jnp.where(starts, q + 1, q)
        jn = jnp.where(at_end, j, jnp.where(last, first_key_tile(q + 1), j + 1))
        return qn, jn, starts, at_end

    def softmax(slot, tile, stats):
        q, j, is_first, filler = tile
        keep = jnp.where(is_first, 0.0, 1.0)
        d = jnp.where(filler, n_tab, jnp.minimum(q - j, n_tab - 1))
        new_stats, alphas = [], []
        for s in range(2):
            m, l = stats[s]
            m = jnp.where(is_first, NEG_INF, m)
            st = st_scr[slot][s] + tab_ref[s, d]
            m_new = jnp.maximum(m, jnp.max(st, axis=0, keepdims=True))
            alpha = jnp.exp2(m - m_new) * keep
            p = jnp.exp2(st - m_new)
            new_stats.append((m_new, alpha * l + jnp.sum(p, axis=0, keepdims=True)))
            alphas.append(alpha)
            p_scr[slot][s] = p.astype(BF16)
        return tuple(new_stats), tuple(alphas)

    def body(u, carry):
        tile_a, (q2, j2), (q1, j1, first1), stats2, stats1, alphas2, alphas1, accs = carry
        qa, ja, first_a, _ = tile_a
        tile_b = advance(qa, ja)
        qb, jb, first_b, _ = tile_b
        tile_c = advance(qb, jb)
        logits_to(1, qb, jb)
        accs_x = accumulate(0, j2, alphas2, accs)
        accs_y = accumulate(1, j1, alphas1, accs_x)
        stats_a, alphas_a = softmax(0, tile_a, stats1)
        logits_to(0, tile_c[0], tile_c[1])
        stats_b, alphas_b = softmax(1, tile_b, stats_a)

        @pl.when(jnp.logical_and(first1, u > 0))
        def _():
            finalize(q2, stats2, accs_x)

        @pl.when(jnp.logical_and(first_a, u > 0))
        def _():
            finalize(q1, stats1, accs_y)

        return tile_c, (qa, ja), (qb, jb, first_b), stats_a, stats_b, alphas_a, alphas_b, accs_y

    zero_i = jnp.int32(0)
    logits_to(0, zero_i, zero_i)
    for p_slot in p_scr:
        p_slot[...] = jnp.zeros(p_slot.shape, BF16)
    stats0 = tuple((jnp.full((1, TQ), NEG_INF, F32), jnp.zeros((1, TQ), F32)) for _ in range(2))
    ones = tuple(jnp.ones((1, TQ), F32) for _ in range(2))
    init = ((zero_i, zero_i, zero_i == 0, zero_i != 0), (zero_i, zero_i), (zero_i, zero_i, zero_i != 0),
            stats0, stats0, ones, ones, tuple(jnp.zeros((HEAD_DIM, TQ), F32) for _ in range(2)))
    _, (q2, j2), (q1, j1, first1), stats2, stats1, alphas2, alphas1, accs = lax.fori_loop(
        0, (n_tiles + 1) // 2, body, init)
    accs_x = accumulate(0, j2, alphas2, accs)

    @pl.when(first1)
    def _():
        finalize(q2, stats2, accs_x)

    finalize(q1, stats1, accumulate(1, j1, alphas1, accs_x))


def _attn(q, k, vt, tab, sel, *, qc0, kc0, vb0, n_blk, qw, backs=None, kv_shared=False,
          tab_shared=False, sel_cfg=None, out_dtype=F32, want_lse=False, stride=1):
    b, s = q.shape[:2]
    n_tab = tab.shape[2] - 1
    n_qt = s // TQ
    if backs is None:
        backs = (n_qt,)
    cfg = (n_tab, qw, backs, sel_cfg, want_lse, n_qt, stride)
    kidx = (lambda bb, h: (bb, 0, kc0)) if kv_shared else (lambda bb, h: (bb, 0, kc0 + h))
    vidx = (lambda bb, h: (bb, vb0, 0)) if kv_shared else (lambda bb, h: (bb, vb0 + h, 0))
    tidx = (lambda bb, h: (0, 0, 0, 0, 0)) if tab_shared else (lambda bb, h: (h, 0, 0, 0, 0))
    in_specs = [pl.BlockSpec((None, s, qw * LANES), lambda bb, h: (bb, 0, qc0 + h)),
                pl.BlockSpec((None, s, qw * LANES), kidx),
                pl.BlockSpec((None, 2 * HEAD_DIM, s), vidx),
                pl.BlockSpec((None, 2, n_tab + 1, TK, TQ), tidx)]
    args = [q, k, vt, tab]
    if sel_cfg is not None:
        in_specs.append(pl.BlockSpec((None, LANES, s), lambda bb, h: (bb, 0, 0)))
        args.append(sel)
    ospec = pl.BlockSpec((None, s, LANES), lambda bb, h: (bb, 0, h))
    out_specs = [ospec]
    out_shape = [jax.ShapeDtypeStruct((b, s, n_blk * LANES), out_dtype)]
    if want_lse:
        out_specs.append(ospec)
        out_shape.append(jax.ShapeDtypeStruct((b, s, n_blk * LANES), F32))
    return pl.pallas_call(
        functools.partial(_attn_kernel, cfg),
        grid=(b, n_blk),
        in_specs=in_specs,
        out_specs=out_specs,
        out_shape=out_shape,
        scratch_shapes=[pltpu.VMEM((2, TK, TQ), F32)] * 2 + [pltpu.VMEM((2, TK, TQ), BF16)] * 2,
        compiler_params=_params("parallel", "parallel"),
        name="attn",
    )(*args)


def _nsa_cmp_kernel(kc_ref, vc_ref, pe_ref, wlo_ref, whi_ref, w2_ref, gm_ref, gain_ref, kcmp_ref, vcmpt_ref):
    def hidden(c, j):
        lo = _dot((c + pe_ref[j, 0]).astype(BF16), wlo_ref[j])
        hi = _dot((c + pe_ref[j, 1]).astype(BF16), whi_ref[j])
        h = lo + pltpu.roll(hi, hi.shape[0] - 1, 0)
        return (h * _sigmoid(h)).astype(BF16)

    kz = _dot_nt(hidden(kc_ref[...], 0), w2_ref[0])
    msq = _dot_hilo(kz * kz, gm_ref[...])
    kcmp_ref[...] = (kz * lax.rsqrt(msq + RMS_EPS) * gain_ref[...]).astype(BF16)
    vcmpt_ref[...] = _dot_nt(w2_ref[1], hidden(vc_ref[...], 1)).astype(BF16)


def _nsa_cmp(kc2, vc2, pe, wlo, whi, w2, gm, gain):
    b, nch, width = kc2.shape
    consts = [pe, wlo, whi, w2, gm, gain]
    blk = pl.BlockSpec((None, nch, width), lambda i: (i, 0, 0))
    oblk = pl.BlockSpec((None, nch, LANES), lambda i: (i, 0, 0))
    return pl.pallas_call(
        _nsa_cmp_kernel,
        grid=(b,),
        in_specs=[blk, blk] + [_resident(a.shape) for a in consts],
        out_specs=[oblk, oblk],
        out_shape=[jax.ShapeDtypeStruct((b, nch, LANES), BF16)] * 2,
        compiler_params=_params("parallel"),
        name="nsa_cmp",
    )(kc2, vc2, *consts)


def _rank_keep(score, ids, top):
    cnt = jnp.zeros(score.shape, jnp.int32)
    for mp in range(score.shape[0]):
        other = score[mp:mp + 1, :]
        tie = jnp.where(mp < ids, 1, 0)
        cnt = cnt + jnp.where(other > score, 1, jnp.where(other == score, tie, 0))
    return cnt < top


def _nsa_sel_kernel(q_ref, kcmp_ref, vcmpt_ref, bias_ref, ovl_ref, oc_ref, sel_ref):
    qi = pl.program_id(1)
    lane = lax.broadcasted_iota(jnp.int32, (1, LANES), 1)
    row = lax.broadcasted_iota(jnp.int32, (LANES, 1), 0)
    t = qi * TQ + lax.broadcasted_iota(jnp.int32, (1, TQ), 1)
    mask_c = (NSA_CMP_STRIDE * row + NSA_CMP_LEN - 1) <= t
    kcmp = kcmp_ref[...]
    heads = [(p, g) for p in range(NSA_HPG) for g in range(NSA_GROUPS)]
    raw = []
    for p, g in heads:
        qp = q_ref[:, p * LANES:(p + 1) * LANES]
        raw.append(_dot_nt(kcmp, jnp.where((lane >> 6) == g, qp, jnp.zeros_like(qp))))
    pcs = []
    for (p, g), r in zip(heads, raw):
        lg = jnp.where(mask_c, r * LN2 + bias_ref[p * NSA_GROUPS + g], NEG_INF)
        m = jnp.max(lg, axis=0, keepdims=True)
        e = jnp.where(mask_c, jnp.exp(lg - m), 0.0)
        den = jnp.maximum(jnp.sum(e, axis=0, keepdims=True), 1e-30)
        pcs.append(e / den)
    imp = jnp.zeros((LANES, TQ), F32)
    ocs = []
    for (p, g), pc in zip(heads, pcs):
        hi, lo = _split(pc)
        ocs.append(_dot(vcmpt_ref[g * HEAD_DIM:(g + 1) * HEAD_DIM, :], hi))
        imp = imp + _dot(ovl_ref[g], hi) + _dot(ovl_ref[g], lo)
    for p in range(NSA_HPG):
        oc_ref[:, p * LANES:(p + 1) * LANES] = jnp.concatenate([ocs[2 * p], ocs[2 * p + 1]], axis=0).T

    n_slc = 32
    ids = row[0:n_slc]
    cur = t >> 6
    forced = (ids == 0) | (ids == cur) | (ids == cur - 1)
    masks = []
    for g in range(NSA_GROUPS):
        score = jnp.where(forced, NSA_FORCED, jnp.where(ids <= cur, imp[g * n_slc:(g + 1) * n_slc], NEG_INF))
        keep = _rank_keep(score, ids, NSA_SLC_TOP) & (score > 0.5 * NEG_INF)
        masks.append(jnp.where(keep, 0.0, NEG_INF))
    masks.append(jnp.full((LANES - NSA_GROUPS * n_slc, TQ), NEG_INF, F32))
    sel_ref[...] = jnp.concatenate(masks, axis=0)


def _nsa_sel(q, kcmp, vcmpt, bias_c, ovl):
    b, s, w = q.shape
    return pl.pallas_call(
        _nsa_sel_kernel,
        grid=(b, s // TQ),
        in_specs=[pl.BlockSpec((None, TQ, w), lambda bb, i: (bb, i, 0)),
                  pl.BlockSpec((None, LANES, LANES), lambda bb, i: (bb, 0, 0)),
                  pl.BlockSpec((None, LANES, LANES), lambda bb, i: (bb, 0, 0)),
                  pl.BlockSpec((NSA_HEADS, LANES, TQ), lambda bb, i: (0, 0, i)),
                  _resident(ovl.shape)],
        out_specs=[pl.BlockSpec((None, TQ, w), lambda bb, i: (bb, i, 0)),
                   pl.BlockSpec((None, LANES, TQ), lambda bb, i: (bb, 0, i))],
        out_shape=[jax.ShapeDtypeStruct((b, s, w), F32), jax.ShapeDtypeStruct((b, LANES, s), F32)],
        compiler_params=_params("parallel", "parallel"),
        name="nsa_sel",
    )(q, kcmp, vcmpt, bias_c, ovl)


def _moba_gate_kernel(q_ref, k_ref, avg_ref, sel_ref):
    qi = pl.program_id(1)
    nb = 8
    kmean = _dot(avg_ref[...], k_ref[...])
    r2 = lax.broadcasted_iota(jnp.int32, kmean.shape, 0)
    c2 = lax.broadcasted_iota(jnp.int32, kmean.shape, 1)
    kmean = jnp.where((r2 >> 3) == (c2 >> 6), kmean, 0.0)
    kh, kl = _split(kmean)
    q = q_ref[...]
    gate = _dot_nt(kh, q) + _dot_nt(kl, q)
    ids = lax.broadcasted_iota(jnp.int32, (nb, 1), 0)
    past = ids < qi
    masks = []
    for h in range(MOBA_HEADS):
        score = jnp.where(past, gate[h * nb:(h + 1) * nb], NEG_INF)
        keep = (_rank_keep(score, ids, MOBA_TOP) & past) | (ids == qi)
        masks.append(jnp.where(keep, 0.0, NEG_INF))
    masks.append(jnp.full((LANES - MOBA_HEADS * nb, TQ), NEG_INF, F32))
    sel_ref[...] = jnp.concatenate(masks, axis=0)


def _moba_gate(p_arr, avg, qc, kc):
    b, s, _ = p_arr.shape
    w = MOBA_HEADS * HEAD_DIM
    return pl.pallas_call(
        _moba_gate_kernel,
        grid=(b, s // TQ),
        in_specs=[pl.BlockSpec((None, TQ, w), lambda bb, i: (bb, i, qc)),
                  pl.BlockSpec((None, s, w), lambda bb, i: (bb, 0, kc)),
                  _resident(avg.shape)],
        out_specs=pl.BlockSpec((None, LANES, TQ), lambda bb, i: (bb, 0, i)),
        out_shape=jax.ShapeDtypeStruct((b, LANES, s), F32),
        compiler_params=_params("parallel", "parallel"),
        name="moba_gate",
    )(p_arr, p_arr, avg)


def _even_out_kernel(x_ref, oa_ref, oc_ref, os_ref, ow_ref, gl_ref, eg_ref, wa_ref, wb_ref, gt_ref, o_ref):
    sg = _sigmoid(gl_ref[...])
    hi, lo = _split(sg)
    nsa = None
    for br, src in enumerate((oc_ref, os_ref, ow_ref)):
        gexp = _dot(hi, eg_ref[br]) + _dot(lo, eg_ref[br])
        term = gexp * src[...]
        nsa = term if nsa is None else nsa + term
    m = _dot(oa_ref[...], wa_ref[...]) + _dot(nsa.astype(BF16), wb_ref[...])
    o_ref[...] = x_ref[...] + gt_ref[...] * m


def _even_out(x, o_a, o_c, o_s, o_w, mla_in, eg, wa, wb, gate, s):
    n, d = x.shape
    tpb = s // TM
    row = lambda wd: pl.BlockSpec((TM, wd), lambda i: (i, 0))
    return pl.pallas_call(
        _even_out_kernel,
        grid=(n // TM,),
        in_specs=[row(d), row(o_a.shape[1]), row(o_c.shape[1]), row(o_s.shape[1]), row(o_w.shape[1]),
                  pl.BlockSpec((TM, LANES), lambda i: (i, 3)),
                  _resident(eg.shape), _resident(wa.shape), _resident(wb.shape), _mod_spec(tpb, d)],
        out_specs=row(d),
        out_shape=jax.ShapeDtypeStruct((n, d), F32),
        compiler_params=_params("parallel"),
        name="even_out",
    )(x, o_a, o_c, o_s, o_w, mla_in, eg, wa, wb, gate)


def _odd_out_kernel(x_ref, od0_ref, od1_ref, od2_ref, ls0_ref, ls1_ref, ls2_ref, om_ref, wd_ref, wm_ref, gt_ref, o_ref):
    ods = (od0_ref, od1_ref, od2_ref)
    ls = [r[...] for r in (ls0_ref, ls1_ref, ls2_ref)]
    mx = jnp.maximum(jnp.maximum(ls[0], ls[1]), ls[2])
    es = [jnp.exp(l - mx) for l in ls]
    den = es[0] + es[1] + es[2]
    merged = None
    for g in range(len(DIL_PAIRS)):
        term = (es[g] / den) * ods[g][...]
        merged = term if merged is None else merged + term
    m = _dot(merged.astype(BF16), wd_ref[...]) + _dot(om_ref[...], wm_ref[...])
    o_ref[...] = x_ref[...] + gt_ref[...] * m


def _odd_out(x, o_ds, lse_ds, o_m, wd, wm, gate, s):
    n, d = x.shape
    tpb = s // TM
    row = lambda wd_: pl.BlockSpec((TM, wd_), lambda i: (i, 0))
    return pl.pallas_call(
        _odd_out_kernel,
        grid=(n // TM,),
        in_specs=[row(d)] + [row(a.shape[1]) for a in (*o_ds, *lse_ds)] + [row(o_m.shape[1]),
                  _resident(wd.shape), _resident(wm.shape), _mod_spec(tpb, d)],
        out_specs=row(d),
        out_shape=jax.ShapeDtypeStruct((n, d), F32),
        compiler_params=_params("parallel"),
        name="odd_out",
    )(x, *o_ds, *lse_ds, o_m, wd, wm, gate)


def _t5_bucket(dist):
    n = jnp.maximum(jnp.asarray(dist, jnp.int32), 0)
    nf = jnp.maximum(n, 1).astype(F32)
    large = T5_MAX_EXACT + (jnp.log(nf / T5_MAX_EXACT) / math.log(T5_MAX_DIST / T5_MAX_EXACT)
                            * (NUM_BUCKETS - T5_MAX_EXACT)).astype(jnp.int32)
    return jnp.where(n < T5_MAX_EXACT, n, jnp.minimum(large, NUM_BUCKETS - 1))


TOEP_PERIOD = 4 * TQ


def _toeplitz_dist():
    j = np.arange(TOEP_PERIOD)
    return np.where(j < 3 * TQ, j, j - TOEP_PERIOD)


def _toeplitz_kernel(n_tab, u_ref, o_ref):
    x = jnp.broadcast_to(u_ref[...], (TK, TOEP_PERIOD))
    y = pltpu.roll(x, 0, 1, stride=1, stride_axis=0)
    for dlt in range(n_tab):
        o_ref[dlt] = y[:, dlt * TQ:(dlt + 1) * TQ]
    o_ref[n_tab] = jnp.full((TK, TQ), NEG_INF * LOG2E, F32)


def _toeplitz_tiles(u, n_tab):
    h = u.shape[0]
    return pl.pallas_call(
        functools.partial(_toeplitz_kernel, n_tab),
        grid=(h,),
        in_specs=[pl.BlockSpec((None, 1, TOEP_PERIOD), lambda i: (i, 0, 0))],
        out_specs=pl.BlockSpec((None, n_tab + 1, TK, TQ), lambda i: (i, 0, 0, 0)),
        out_shape=jax.ShapeDtypeStruct((h, n_tab + 1, TK, TQ), F32),
        compiler_params=_params("parallel"),
        name="toeplitz",
    )(u.reshape(h, 1, TOEP_PERIOD))


def _cmp_bias_kernel(u_ref, o_ref):
    x = jnp.broadcast_to(u_ref[...], o_ref.shape)
    o_ref[...] = pltpu.roll(x, 0, 1, stride=NSA_CMP_STRIDE, stride_axis=0)


def _cmp_bias(u, n_rows):
    h, s = u.shape
    return pl.pallas_call(
        _cmp_bias_kernel,
        grid=(h,),
        in_specs=[pl.BlockSpec((None, 1, s), lambda i: (i, 0, 0))],
        out_specs=pl.BlockSpec((None, n_rows, s), lambda i: (i, 0, 0)),
        out_shape=jax.ShapeDtypeStruct((h, n_rows, s), F32),
        compiler_params=_params("parallel"),
        name="cmp_bias",
    )(u.reshape(h, 1, s))


def _bias_tiles(t5_cols, ok, n_tab=3, dist_scale=1):
    dist = _toeplitz_dist()
    bias = jnp.transpose(t5_cols[_t5_bucket(dist * dist_scale)])
    u = jnp.where(jnp.asarray(ok)[None], bias, NEG_INF) * LOG2E
    return _toeplitz_tiles(u, n_tab)


def _pair(tiles):
    h = tiles.shape[0]
    return tiles.reshape(h // 2, 2, *tiles.shape[1:])


def _group_mean_np(sizes, width=LANES):
    gm = np.zeros((width, width), np.float32)
    o = 0
    for sz in sizes:
        gm[o:o + sz, o:o + sz] = 1.0 / sz
        o += sz
    return gm


def _group_mean_matrix(sizes, width=LANES):
    return jnp.asarray(_group_mean_np(sizes, width), BF16)


EV_META = ((0, 0, False), (0, 128, False), (0, 256, False), (0, 384, False),
           (1, 0, True), (1, 128, True), (1, 256, True), (1, 384, True),
           (2, 0, False), (3, 0, False), (4, 0, True), (5, 0, False), (6, 0, True), (7, 0, False))
EV_OUTS = [(512, F32), (512, BF16), (128, F32), (128, F32), (128, BF16), (128, BF16), (128, BF16), (128, BF16)]
OD_META = tuple((0, cidx * LANES, (cidx < 12) or (18 <= cidx < 22)) for cidx in range(24))


def _even_w_in(w):
    jn, d, _ = w.shape
    z = lambda n_: jnp.zeros((jn, d, n_), w.dtype)
    nq = w[:, :, 416:928].reshape(jn, d, NSA_GROUPS, NSA_HPG, HEAD_DIM)
    nq = jnp.transpose(nq, (0, 1, 3, 2, 4)).reshape(jn, d, NSA_HEADS * HEAD_DIM)
    chunk3 = jnp.concatenate([w[:, :, 1696:1720], z(HEAD_DIM - 24), w[:, :, 384:416], z(LANES - 96)], axis=-1)
    return jnp.concatenate([w[:, :, 0:384], chunk3, nq, w[:, :, 928:1696]], axis=-1)


def kernel(x, c, t5_bias, ada_w, ada_b, norm_g, ffn_w_in, ffn_w_out, ev_w_in, ev_w_out, mla_q_norm_g,
           mla_kv_norm_g, mla_w_uq, mla_w_ukv, mla_qk_g, nsa_cmp_pe, nsa_cmp_w1, nsa_cmp_w2, nsa_qk_g,
           od_w_in, od_w_out, dil_qk_g, moba_qk_g):
    b, s, d = x.shape
    assert (s, d) == (2048, D_MODEL) and s % TM == 0 and TQ == MOBA_BLOCK and TQ == TK
    n = b * s
    hd = HEAD_DIM
    n_even = ev_w_in.shape[0]
    n_odd = od_w_in.shape[0]
    c64 = hd ** -0.5 * LOG2E
    c96 = (MLA_NOPE + MLA_ROPE) ** -0.5 * LOG2E

    mod = _ada(c, ada_w, ada_b).reshape(DEPTH, b, 3, 3, 1, d)

    dist = _toeplitz_dist()
    causal = dist >= 0
    gm64 = _group_mean_matrix((hd, hd))
    gm_proj = _group_mean_matrix((hd,) * (PROJ_CHUNK // hd), PROJ_CHUNK)
    per_chunk = PROJ_CHUNK // LANES
    chunked = lambda m: tuple(tuple(m[i:i + per_chunk]) for i in range(0, len(m), per_chunk))

    padc = FF_PAD - D_FF
    wab_all = jnp.pad(ffn_w_in.reshape(DEPTH, 2, d, 2, D_FF), ((0, 0),) * 4 + ((0, padc),)
                      ).astype(BF16).reshape(DEPTH, 2, d, 2 * FF_PAD)
    wo_all = jnp.pad(ffn_w_out, ((0, 0), (0, 0), (0, padc), (0, 0))).astype(BF16)

    nsa_tab = t5_bias[:, MLA_HEADS:MLA_HEADS + NSA_HEADS].reshape(NUM_BUCKETS, NSA_GROUPS, NSA_HPG)
    nsa_cols = jnp.transpose(nsa_tab, (0, 2, 1)).reshape(NUM_BUCKETS, NSA_HEADS)
    tab_sel = _pair(_bias_tiles(nsa_cols, causal))
    tab_win = _pair(_bias_tiles(nsa_cols, causal & (dist <= NSA_WINDOW - 1)))
    tab_mla = _toeplitz_tiles(jnp.where(jnp.asarray(causal), 0.0, NEG_INF).astype(F32)[None], 2)
    tab_mla = jnp.broadcast_to(tab_mla[None], (1, 2) + tab_mla.shape[1:])
    n_cmp_pad = s // NSA_CMP_STRIDE
    bias_c = _cmp_bias(jnp.transpose(nsa_cols[_t5_bucket(np.arange(s) - (NSA_CMP_LEN - 1))]), n_cmp_pad)
    n_cmp = (s - NSA_CMP_LEN) // NSA_CMP_STRIDE + 1
    cstart = np.arange(n_cmp) * NSA_CMP_STRIDE
    sstart = np.arange(s // NSA_SLC_BLOCK) * NSA_SLC_BLOCK
    overlap = np.clip(np.minimum(cstart[:, None] + NSA_CMP_LEN, sstart[None, :] + NSA_SLC_BLOCK)
                      - np.maximum(cstart[:, None], sstart[None, :]), 0, None).astype(np.float32) / NSA_CMP_LEN
    ovl = np.zeros((NSA_GROUPS, LANES, LANES), np.float32)
    for g in range(NSA_GROUPS):
        ovl[g, 32 * g:32 * g + 32, :n_cmp] = overlap.T
    ovl = jnp.asarray(ovl, BF16)
    eg = np.zeros((3, LANES, NSA_HEADS * hd), np.float32)
    for g in range(NSA_GROUPS):
        for p in range(NSA_HPG):
            for br in range(3):
                eg[br, (g * NSA_HPG + p) * 3 + br, p * LANES + g * hd:p * LANES + (g + 1) * hd] = 1.0
    eg = jnp.asarray(eg, BF16)
    gm_mla = jnp.asarray(np.kron(np.eye(2, dtype=np.float32), _group_mean_np((MLA_NOPE, MLA_ROPE))), BF16)
    inv = ROPE_THETA ** (-jnp.arange(0, MLA_ROPE, 2, dtype=F32) / MLA_ROPE)
    ang = jnp.arange(s, dtype=F32)[:, None] * inv[None, :]
    ones = jnp.ones((s, MLA_NOPE), F32)
    tail = LANES - MLA_NOPE - MLA_ROPE
    cos_t = jnp.tile(jnp.concatenate([ones, jnp.cos(ang), jnp.cos(ang), jnp.ones((s, tail), F32)], axis=1), (1, 2))
    sin_t = jnp.tile(jnp.concatenate([0 * ones, jnp.sin(ang), jnp.sin(ang), jnp.zeros((s, tail), F32)], axis=1), (1, 2))

    ev_w = _even_w_in(ev_w_in).astype(BF16)
    ev_gain = jnp.ones((n_even, 14 * LANES), F32)
    ev_gain = ev_gain.at[:, 512:1024].set(jnp.tile(nsa_qk_g[:, 0], (1, 8)) * c64)
    ev_gain = ev_gain.at[:, 1280:1408].set(jnp.tile(nsa_qk_g[:, 1], (1, 2)))
    ev_gain = ev_gain.at[:, 1536:1664].set(jnp.tile(nsa_qk_g[:, 1], (1, 2)))
    gain_kc = jnp.tile(nsa_qk_g[:, 1], (1, 2))
    wuq = jnp.pad(mla_w_uq.reshape(n_even, MLA_Q_LORA, MLA_HEADS, MLA_NOPE + MLA_ROPE),
                  ((0, 0), (0, 0), (0, 0), (0, tail))).reshape(n_even, MLA_Q_LORA, MLA_HEADS * LANES).astype(BF16)
    ukv = mla_w_ukv.reshape(n_even, MLA_KV_LORA, MLA_HEADS, MLA_NOPE + MLA_V)
    wuk = jnp.pad(ukv[..., :MLA_NOPE], ((0, 0), (0, 0), (0, 0), (0, LANES - MLA_NOPE))
                  ).reshape(n_even, MLA_KV_LORA, MLA_HEADS * LANES).astype(BF16)
    wuv = ukv[..., MLA_NOPE:].reshape(n_even, MLA_KV_LORA, MLA_HEADS * MLA_V).astype(BF16)
    zt = jnp.zeros((n_even, tail), F32)
    gq = jnp.tile(jnp.concatenate([mla_qk_g[:, 0] * c96, zt], axis=1), (1, 2))
    gkn = jnp.tile(jnp.concatenate([mla_qk_g[:, 1, :MLA_NOPE], jnp.zeros((n_even, LANES - MLA_NOPE), F32)], axis=1), (1, 2))
    gkr = jnp.tile(jnp.concatenate([jnp.zeros((n_even, MLA_NOPE), F32), mla_qk_g[:, 1, MLA_NOPE:], zt], axis=1), (1, 2))
    pe2 = jnp.broadcast_to(nsa_cmp_pe.reshape(n_even, 2, 2, 16, 1, hd), (n_even, 2, 2, 16, NSA_GROUPS, hd)
                           ).reshape(n_even, 2, 2, 1, 16 * LANES)
    eye = jnp.eye(NSA_GROUPS, dtype=F32)
    w1 = nsa_cmp_w1.reshape(n_even, 2, 2, 16, hd, NSA_CMP_HID)
    w1x = jnp.einsum('ijaldc,gh->ijalgdhc', w1, eye).reshape(n_even, 2, 2, 16 * LANES, NSA_GROUPS * NSA_CMP_HID).astype(BF16)
    w2x = jnp.einsum('ijcd,gh->ijhdgc', nsa_cmp_w2, eye).reshape(n_even, 2, LANES, NSA_GROUPS * NSA_CMP_HID).astype(BF16)
    wa_o = ev_w_out[:, :MLA_HEADS * MLA_V].astype(BF16)
    wb_o = jnp.transpose(ev_w_out[:, MLA_HEADS * MLA_V:].reshape(n_even, NSA_GROUPS, NSA_HPG, hd, d),
                         (0, 2, 1, 3, 4)).reshape(n_even, NSA_HEADS * hd, d).astype(BF16)

    assert DIL_PAIRS == ((128, 1), (512, 4), (2048, 16))
    dil_cfg = ((1, 1), (4, 1), (8, 0))
    dil_ok = (causal & (dist <= 128), causal & (dist <= 128), causal & (dist % 2 == 0))
    tab_dil = [_pair(_bias_tiles(t5_bias[:, gi * DIL_HPG:(gi + 1) * DIL_HPG], dil_ok[gi], dist_scale=dil_cfg[gi][0]))
               for gi in range(len(DIL_PAIRS))]
    tab_moba = _pair(_bias_tiles(t5_bias[:, DIL_SLOTS:DIL_SLOTS + MOBA_HEADS], causal))
    avg = np.zeros((LANES, s), np.float32)
    for h in range(MOBA_HEADS):
        for m in range(s // MOBA_BLOCK):
            avg[8 * h + m, m * MOBA_BLOCK:(m + 1) * MOBA_BLOCK] = 1.0 / MOBA_BLOCK
    avg = jnp.asarray(avg, BF16)
    od_w = od_w_in.astype(BF16)
    od_gain = jnp.concatenate([jnp.tile(dil_qk_g[:, 0], (1, 12)) * c64, jnp.tile(dil_qk_g[:, 1], (1, 12)),
                               jnp.ones((n_odd, 768), F32), jnp.tile(moba_qk_g[:, 0], (1, 4)) * c64,
                               jnp.tile(moba_qk_g[:, 1], (1, 4)), jnp.ones((n_odd, 256), F32)], axis=1)
    wd_o = od_w_out[:, :DIL_HPG * hd].astype(BF16)
    wm_o = od_w_out[:, DIL_HPG * hd:].astype(BF16)

    sh3 = lambda a: a.reshape(b, s, a.shape[-1])
    tr3 = lambda a: jnp.swapaxes(sh3(a), 1, 2)
    xf = x.reshape(n, d)
    for i in range(DEPTH):
        j = i // 2
        g_i = norm_g[i].reshape(3, 1, d)
        xf = _ffn(xf, g_i[0], mod[i, :, 0, 0], mod[i, :, 0, 1], mod[i, :, 0, 2],
                  wab_all[i, 0], wo_all[i, 0], s)
        if i % 2 == 0:
            mla_in, nsa_q, kc, vc, ks, vs, kw, vw = _proj(
                xf, g_i[1], mod[i, :, 1, 0], mod[i, :, 1, 1], ev_w[j], ev_gain[j][None], gm_proj,
                chunked(EV_META), EV_OUTS, s)
            qf, kf, vf = _mla_prep(mla_in, mla_q_norm_g[j][None], mla_kv_norm_g[j][None], wuq[j], wuk[j], wuv[j],
                                   gm_mla, gq[j][None], gkn[j][None], gkr[j][None], cos_t, sin_t, s)
            (o_a,) = _attn(sh3(qf), sh3(kf), tr3(vf), tab_mla, None, qc0=0, kc0=0, vb0=0, n_blk=MLA_HEADS // 2, qw=2,
                           tab_shared=True, out_dtype=BF16)
            kcmp, vcmpt = _nsa_cmp(kc.reshape(b, n_cmp_pad, 16 * LANES), vc.reshape(b, n_cmp_pad, 16 * LANES),
                                   pe2[j], w1x[j, :, 0], w1x[j, :, 1], w2x[j], gm64, gain_kc[j][None])
            o_c, sel = _nsa_sel(sh3(nsa_q), kcmp, vcmpt, bias_c, ovl)
            (o_s,) = _attn(sh3(nsa_q), sh3(ks), tr3(vs), tab_sel, sel, qc0=0, kc0=0, vb0=0, n_blk=NSA_HPG, qw=1,
                           kv_shared=True, sel_cfg=(TK // NSA_SLC_BLOCK, 32, 0))
            (o_w,) = _attn(sh3(nsa_q), sh3(kw), tr3(vw), tab_win, None, qc0=0, kc0=0, vb0=0, n_blk=NSA_HPG, qw=1,
                           kv_shared=True, backs=(2, 2, 2, 2))
            xf = _even_out(xf, o_a.reshape(n, -1), o_c.reshape(n, -1), o_s.reshape(n, -1), o_w.reshape(n, -1),
                           mla_in, eg, wa_o[j], wb_o[j], mod[i, :, 1, 2], s)
        else:
            (pr,) = _proj(xf, g_i[1], mod[i, :, 1, 0], mod[i, :, 1, 1], od_w[j], od_gain[j][None], gm_proj,
                          chunked(OD_META), [(24 * LANES, BF16)], s)
            pr3 = pr.reshape(b, s, 24 * LANES)
            vmt = jnp.swapaxes(pr3[:, :, 22 * LANES:24 * LANES], 1, 2)
            o_ds, lse_ds = [], []
            for gi, (r, bk) in enumerate(dil_cfg):
                cols = lambda c0: pr3[:, :, c0 * LANES:(c0 + 2) * LANES].reshape(b, s // r, r, 2 * LANES)
                vdt = jnp.transpose(cols(12 + 2 * gi), (0, 3, 2, 1)).reshape(b, 2 * LANES, s)
                if r == 1:
                    qk, qc0, kc0 = pr3, 0, 6
                else:
                    qk = jnp.concatenate([cols(2 * gi), cols(6 + 2 * gi)], axis=-1)
                    qk, qc0, kc0 = jnp.transpose(qk, (0, 2, 1, 3)).reshape(b, s, 4 * LANES), 0, 2
                o_g, lse_g = _attn(qk, qk, vdt, tab_dil[gi], None, qc0=qc0, kc0=kc0, vb0=0, n_blk=2,
                                   qw=1, backs=(bk,), want_lse=True, stride=r)
                o_ds.append(o_g.reshape(n, -1))
                lse_ds.append(lse_g.reshape(n, -1))
            selm = _moba_gate(pr3, avg, 9, 10)
            (o_m,) = _attn(pr3, pr3, vmt, tab_moba, selm, qc0=18, kc0=20, vb0=0, n_blk=2, qw=1,
                           sel_cfg=(1, 8, 2), out_dtype=BF16)
            xf = _odd_out(xf, o_ds, lse_ds, o_m.reshape(n, -1), wd_o[j], wm_o[j], mod[i, :, 1, 2], s)
        xf = _ffn(xf, g_i[2], mod[i, :, 2, 0], mod[i, :, 2, 1], mod[i, :, 2, 2],
                  wab_all[i, 1], wo_all[i, 1], s)
    return xf.reshape(b, s, d)
```

```python
import functools
import math

import numpy as np
import jax
import jax.numpy as jnp
from jax import lax
from jax.experimental import pallas as pl
from jax.experimental.pallas import tpu as pltpu

F32 = jnp.float32
BF16 = jnp.bfloat16

D_MODEL = 1024
DEPTH = 4
D_FF = 2752
HEAD_DIM = 64
NUM_BUCKETS = 32
T5_MAX_EXACT = 16
T5_MAX_DIST = 128
RMS_EPS = 1e-6
NEG_INF = -1e30
MLA_HEADS = 8
MLA_NOPE = 64
MLA_ROPE = 32
MLA_V = 64
MLA_Q_LORA = 256
MLA_KV_LORA = 128
ROPE_THETA = 10000.0
NSA_HEADS = 8
NSA_GROUPS = 2
NSA_HPG = 4
NSA_CMP_LEN = 32
NSA_CMP_STRIDE = 16
NSA_CMP_HID = 256
NSA_SLC_BLOCK = 64
NSA_SLC_TOP = 8
NSA_WINDOW = 512
NSA_FORCED = 1e6
DIL_PAIRS = ((128, 1), (512, 4), (2048, 16))
DIL_HPG = 4
DIL_SLOTS = len(DIL_PAIRS) * DIL_HPG
MOBA_HEADS = 4
MOBA_BLOCK = 256
MOBA_TOP = 3

LANES = 128
V7X_VMEM_BYTES = 64 * 1024 * 1024
VMEM_LIMIT = V7X_VMEM_BYTES * 7 // 8
TM = 512
TQ = 256
TK = 256
FF_CHUNK = 256
FF_PAD = -(-D_FF // FF_CHUNK) * FF_CHUNK
PROJ_CHUNK = 256
LOG2E = math.log2(math.e)
LN2 = math.log(2.0)


def _dot(a, b):
    return jnp.dot(a, b, preferred_element_type=F32)


def _dot_nt(a, b):
    return lax.dot_general(a, b, (((1,), (1,)), ((), ())), preferred_element_type=F32)


def _split(a):
    hi = a.astype(BF16)
    lo = (a - hi.astype(F32)).astype(BF16)
    return hi, lo


def _dot_hilo(a, b):
    hi, lo = _split(a)
    return _dot(hi, b) + _dot(lo, b)


def _sigmoid(x):
    return 1.0 / (1.0 + jnp.exp(-x))


def _modulated_norm(x, g, shift, scale):
    ms = jnp.mean(x * x, axis=-1, keepdims=True)
    y = x * lax.rsqrt(ms + RMS_EPS) * g
    return y * (1.0 + scale) + shift


def _params(*sem):
    return pltpu.CompilerParams(dimension_semantics=sem, vmem_limit_bytes=VMEM_LIMIT)


def _resident(shape):
    nd = len(shape)
    return pl.BlockSpec(shape, lambda *_: (0,) * nd, pipeline_mode=pl.Buffered(1))


def _ada_kernel(c_ref, w_ref, b_ref, o_ref):
    c = c_ref[...]
    ca = c * _sigmoid(c)
    o_ref[...] = jnp.dot(ca, w_ref[...], preferred_element_type=F32,
                         precision=lax.Precision.HIGHEST) + b_ref[...]


def _ada(c, ada_w, ada_b):
    depth, d, n = ada_w.shape
    b = c.shape[0]
    tn = 9 * LANES
    return pl.pallas_call(
        _ada_kernel,
        grid=(depth, n // tn),
        in_specs=[pl.BlockSpec((b, d), lambda l, j: (0, 0)),
                  pl.BlockSpec((None, d, tn), lambda l, j: (l, 0, j)),
                  pl.BlockSpec((None, 1, tn), lambda l, j: (l, 0, j))],
        out_specs=pl.BlockSpec((None, b, tn), lambda l, j: (l, 0, j)),
        out_shape=jax.ShapeDtypeStruct((depth, b, n), F32),
        compiler_params=_params("parallel", "parallel"),
        name="ada",
    )(c, ada_w, ada_b.reshape(depth, 1, n))


def _ffn_kernel(x_ref, g_ref, sh_ref, sc_ref, gt_ref, wab_ref, wo_ref, o_ref, y_ref, acc_ref):
    y_ref[...] = _modulated_norm(x_ref[...], g_ref[...], sh_ref[...], sc_ref[...]).astype(BF16)
    for c in range(FF_PAD // FF_CHUNK):
        sl = slice(c * FF_CHUNK, (c + 1) * FF_CHUNK)
        a = _dot(y_ref[...], wab_ref[:, sl])
        b = _dot(y_ref[...], wab_ref[:, FF_PAD + c * FF_CHUNK:FF_PAD + (c + 1) * FF_CHUNK])
        u = (a * _sigmoid(a) * b).astype(BF16)
        contrib = _dot(u, wo_ref[sl, :])
        if c == 0:
            acc_ref[...] = contrib
        else:
            acc_ref[...] += contrib
    o_ref[...] = x_ref[...] + 0.5 * gt_ref[...] * acc_ref[...]


def _mod_spec(tiles_per_batch, d):
    return pl.BlockSpec((None, 1, d), lambda i: (i // tiles_per_batch, 0, 0))


def _ffn(x, g, shift, scale, gate, wab, wo, s):
    n, d = x.shape
    tpb = s // TM
    return pl.pallas_call(
        _ffn_kernel,
        grid=(n // TM,),
        in_specs=[pl.BlockSpec((TM, d), lambda i: (i, 0)),
                  _resident((1, d)),
                  _mod_spec(tpb, d), _mod_spec(tpb, d), _mod_spec(tpb, d),
                  _resident(wab.shape), _resident(wo.shape)],
        out_specs=pl.BlockSpec((TM, d), lambda i: (i, 0)),
        out_shape=jax.ShapeDtypeStruct((n, d), F32),
        scratch_shapes=[pltpu.VMEM((TM, d), BF16), pltpu.VMEM((TM, d), F32)],
        compiler_params=_params("parallel"),
        name="ffn",
    )(x, g, shift, scale, gate, wab, wo)


def _proj_kernel(meta, transposed, x_ref, g_ref, sh_ref, sc_ref, w_ref, gain_ref, gm_ref, *rest):
    n_out = len(transposed)
    outs = rest[:n_out]
    y_ref = rest[n_out]
    y_ref[...] = _modulated_norm(x_ref[...], g_ref[...], sh_ref[...], sc_ref[...]).astype(BF16)
    for c, halves in enumerate(meta):
        sl = slice(c * PROJ_CHUNK, (c + 1) * PROJ_CHUNK)
        z = _dot(y_ref[...], w_ref[:, sl])
        if any(normed for _, _, normed in halves):
            msq = _dot((z * z).astype(BF16), gm_ref[...])
            zn = z * lax.rsqrt(msq + RMS_EPS) * gain_ref[:, sl]
        for hf, (oi, off, normed) in enumerate(halves):
            src = (zn if normed else z)[:, hf * LANES:(hf + 1) * LANES]
            if transposed[oi]:
                src = src * gain_ref[:, c * PROJ_CHUNK + hf * LANES:c * PROJ_CHUNK + (hf + 1) * LANES]
                outs[oi][off:off + LANES, :] = src.T.astype(outs[oi].dtype)
            else:
                outs[oi][:, off:off + LANES] = src.astype(outs[oi].dtype)


def _proj(x, g, shift, scale, w, gain, gm, meta, out_defs, s):
    n, d = x.shape
    tpb = s // TM
    rows = lambda wd: pl.BlockSpec((TM, wd), lambda i: (i, 0))
    cols = lambda wd: pl.BlockSpec((None, wd, TM), lambda i: (i // tpb, 0, i % tpb))
    return pl.pallas_call(
        functools.partial(_proj_kernel, meta, tuple(t for _, _, t in out_defs)),
        grid=(n // TM,),
        in_specs=[pl.BlockSpec((TM, d), lambda i: (i, 0)),
                  _resident((1, d)),
                  _mod_spec(tpb, d), _mod_spec(tpb, d),
                  _resident(w.shape), _resident(gain.shape), _resident(gm.shape)],
        out_specs=[cols(wd) if t else rows(wd) for wd, _, t in out_defs],
        out_shape=[jax.ShapeDtypeStruct((n // s, wd, s) if t else (n, wd), dt) for wd, dt, t in out_defs],
        scratch_shapes=[pltpu.VMEM((TM, d), BF16)],
        compiler_params=_params("parallel"),
        name="proj",
    )(x, g, shift, scale, w, gain, gm)


def _mla_prep_kernel(in_ref, qg_ref, kvg_ref, wuq_ref, wuk_ref, wuv_ref, gm_ref,
                     gq_ref, gkn_ref, gkr_ref, cos_ref, sin_ref, q_out, k_out, v_out):
    def rms(z, g):
        return z * lax.rsqrt(jnp.mean(z * z, axis=-1, keepdims=True) + RMS_EPS) * g

    cqn = rms(in_ref[:, 0:MLA_Q_LORA], qg_ref[...]).astype(BF16)
    ckvn = rms(in_ref[:, MLA_Q_LORA:MLA_Q_LORA + MLA_KV_LORA], kvg_ref[...]).astype(BF16)
    c3 = in_ref[:, 3 * LANES:4 * LANES]
    cos = cos_ref[...]
    sin = sin_ref[...]
    gm = gm_ref[...]
    half = MLA_ROPE // 2
    slot_lane = lax.broadcasted_iota(jnp.int32, (1, 2 * LANES), 1) & (LANES - 1)
    first_half = slot_lane < MLA_NOPE + half

    def norm_rope(z, gain):
        msq = _dot((z * z).astype(BF16), gm)
        z = z * lax.rsqrt(msq + RMS_EPS) * gain
        rot = jnp.where(first_half, -pltpu.roll(z, 2 * LANES - half, 1), pltpu.roll(z, half, 1))
        return z * cos + rot * sin

    kr = norm_rope(jnp.concatenate([c3, c3], axis=1), gkr_ref[...])
    for h in range(MLA_HEADS // 2):
        sl = slice(h * 2 * LANES, (h + 1) * 2 * LANES)
        q_out[:, sl] = norm_rope(_dot(cqn, wuq_ref[:, sl]), gq_ref[...]).astype(BF16)
        k_out[:, sl] = (norm_rope(_dot(ckvn, wuk_ref[:, sl]), gkn_ref[...]) + kr).astype(BF16)
    v_out[...] = _dot_nt(wuv_ref[...], ckvn).astype(BF16)


def _mla_prep(mla_in, qg, kvg, wuq, wuk, wuv, gm, gq, gkn, gkr, cos_t, sin_t, s):
    n = mla_in.shape[0]
    tpb = s // TM
    hw = MLA_HEADS * LANES
    vw = MLA_HEADS * MLA_V
    tab = pl.BlockSpec((TM, 2 * LANES), lambda i: (i % tpb, 0))
    consts = [qg, kvg, wuq, wuk, wuv, gm, gq, gkn, gkr]
    return pl.pallas_call(
        _mla_prep_kernel,
        grid=(n // TM,),
        in_specs=[pl.BlockSpec((TM, 4 * LANES), lambda i: (i, 0))] + [_resident(a.shape) for a in consts] + [tab, tab],
        out_specs=[pl.BlockSpec((TM, hw), lambda i: (i, 0)), pl.BlockSpec((TM, hw), lambda i: (i, 0)),
                   pl.BlockSpec((None, vw, TM), lambda i: (i // tpb, 0, i % tpb))],
        out_shape=[jax.ShapeDtypeStruct((n, hw), BF16), jax.ShapeDtypeStruct((n, hw), BF16),
                   jax.ShapeDtypeStruct((n // s, vw, s), BF16)],
        compiler_params=_params("parallel"),
        name="mla_prep",
    )(mla_in, *consts, cos_t, sin_t)


def _attn_kernel(cfg, q_ref, k_ref, vt_ref, tab_ref, *rest):
    n_tab, qw, backs, sel_cfg, want_lse, n_qt, stride = cfg
    n_qs = n_qt // stride
    if sel_cfg is not None:
        sel_ref, rest = rest[0], rest[1:]
        sel_bpt, sel_stride, sel_pair_mul = sel_cfg
    o_ref = rest[0]
    lse_ref = rest[1] if want_lse else None
    st_scr, p_scr = rest[-4:-2], rest[-2:]
    blk = pl.program_id(1)
    lane = lax.broadcasted_iota(jnp.int32, (1, LANES), 1)

    counts = tuple(sum(min(t % n_qs, bk) + 1 for t in range(n_qt)) for bk in backs)
    back = jnp.int32(backs[-1])
    n_tiles = jnp.int32(counts[-1])
    for bi in range(len(backs) - 2, -1, -1):
        back = jnp.where(blk == bi, jnp.int32(backs[bi]), back)
        n_tiles = jnp.where(blk == bi, jnp.int32(counts[bi]), n_tiles)

    def first_key_tile(qi):
        return qi - jnp.minimum(qi & (n_qs - 1), back)

    def rows(t, size):
        return pl.ds(pl.multiple_of(t * size, size), size)

    def out_rows(t, size):
        if stride == 1:
            return rows(t, size)
        return pl.ds((t >> int(math.log2(n_qs))) + stride * size * (t & (n_qs - 1)), size, stride=stride)

    def logits_to(slot, qi, j):
        if qw == 1:
            q = q_ref[rows(qi, TQ), :]
            zero = jnp.zeros_like(q)
            qs = [jnp.where(lane < HEAD_DIM, q, zero), jnp.where(lane >= HEAD_DIM, q, zero)]
        else:
            qs = [q_ref[rows(qi, TQ), s * LANES:(s + 1) * LANES] for s in range(2)]
        for s in range(2):
            kj = k_ref[rows(j, TK), :] if qw == 1 else k_ref[rows(j, TK), s * LANES:(s + 1) * LANES]
            st = _dot_nt(kj, qs[s])
            if sel_cfg is not None:
                off = sel_stride * (s + sel_pair_mul * blk) + sel_bpt * j
                kb = TK // sel_bpt
                qcol = pl.ds(pl.multiple_of(qi * TQ, TQ), TQ)
                st = jnp.concatenate(
                    [st[bk * kb:(bk + 1) * kb] + sel_ref[pl.ds(off + bk, 1), qcol]
                     for bk in range(sel_bpt)], axis=0)
            st_scr[slot][s] = st

    def accumulate(slot, j, alphas, accs):
        krow = pl.multiple_of(j * TK, TK)
        new = []
        for s in range(2):
            vt = vt_ref[s * HEAD_DIM:(s + 1) * HEAD_DIM, pl.ds(krow, TK)]
            new.append(alphas[s] * accs[s] + _dot(vt, p_scr[slot][s]))
        return tuple(new)

    def finalize(qi, stats, accs):
        out_t = jnp.concatenate([acc / l for acc, (_, l) in zip(accs, stats)], axis=0)
        o_ref[out_rows(qi, TQ), :] = out_t.T.astype(o_ref.dtype)
        if want_lse:
            lse_t = jnp.concatenate([jnp.broadcast_to((m + jnp.log2(l)) * LN2, (HEAD_DIM, TQ))
                                     for (m, l) in stats], axis=0)
            lse_ref[out_rows(qi, TQ), :] = lse_t.T

    def advance(q, j):
        last = j == q
        at_end = jnp.logical_and(last, q == n_qt - 1)
        starts = jnp.logical_and(last, jnp.logical_not(at_end))
        qn = jnp.where(starts, q + 1, q)
        jn = jnp.where(at_end, j, jnp.where(last, first_key_tile(q + 1), j + 1))
        return qn, jn, starts, at_end

    def softmax(slot, tile, stats):
        q, j, is_first, filler = tile
        keep = jnp.where(is_first, 0.0, 1.0)
        d = jnp.where(filler, n_tab, jnp.minimum(q - j, n_tab - 1))
        new_stats, alphas = [], []
        for s in range(2):
            m, l = stats[s]
            m = jnp.where(is_first, NEG_INF, m)
            st = st_scr[slot][s] + tab_ref[s, d]
            m_new = jnp.maximum(m, jnp.max(st, axis=0, keepdims=True))
            alpha = jnp.exp2(m - m_new) * keep
            p = jnp.exp2(st - m_new)
            new_stats.append((m_new, alpha * l + jnp.sum(p, axis=0, keepdims=True)))
            alphas.append(alpha)
            p_scr[slot][s] = p.astype(BF16)
        return tuple(new_stats), tuple(alphas)

    def body(u, carry):
        tile_a, (q2, j2), (q1, j1, first1), stats2, stats1, alphas2, alphas1, accs = carry
        qa, ja, first_a, _ = tile_a
        tile_b = advance(qa, ja)
        qb, jb, first_b, _ = tile_b
        tile_c = advance(qb, jb)
        logits_to(1, qb, jb)
        accs_x = accumulate(0, j2, alphas2, accs)
        accs_y = accumulate(1, j1, alphas1, accs_x)
        stats_a, alphas_a = softmax(0, tile_a, stats1)
        logits_to(0, tile_c[0], tile_c[1])
        stats_b, alphas_b = softmax(1, tile_b, stats_a)

        @pl.when(jnp.logical_and(first1, u > 0))
        def _():
            finalize(q2, stats2, accs_x)

        @pl.when(jnp.logical_and(first_a, u > 0))
        def _():
            finalize(q1, stats1, accs_y)

        return tile_c, (qa, ja), (qb, jb, first_b), stats_a, stats_b, alphas_a, alphas_b, accs_y

    zero_i = jnp.int32(0)
    logits_to(0, zero_i, zero_i)
    for p_slot in p_scr:
        p_slot[...] = jnp.zeros(p_slot.shape, BF16)
    stats0 = tuple((jnp.full((1, TQ), NEG_INF, F32), jnp.zeros((1, TQ), F32)) for _ in range(2))
    ones = tuple(jnp.ones((1, TQ), F32) for _ in range(2))
    init = ((zero_i, zero_i, zero_i == 0, zero_i != 0), (zero_i, zero_i), (zero_i, zero_i, zero_i != 0),
            stats0, stats0, ones, ones, tuple(jnp.zeros((HEAD_DIM, TQ), F32) for _ in range(2)))
    _, (q2, j2), (q1, j1, first1), stats2, stats1, alphas2, alphas1, accs = lax.fori_loop(
        0, (n_tiles + 1) // 2, body, init)
    accs_x = accumulate(0, j2, alphas2, accs)

    @pl.when(first1)
    def _():
        finalize(q2, stats2, accs_x)

    finalize(q1, stats1, accumulate(1, j1, alphas1, accs_x))


def _attn(q, k, vt, tab, sel, *, qc0, kc0, vb0, n_blk, qw, backs=None, kv_shared=False,
          tab_shared=False, sel_cfg=None, out_dtype=F32, want_lse=False, stride=1):
    b, s = q.shape[:2]
    n_tab = tab.shape[2] - 1
    n_qt = s // TQ
    if backs is None:
        backs = (n_qt,)
    cfg = (n_tab, qw, backs, sel_cfg, want_lse, n_qt, stride)
    kidx = (lambda bb, h: (bb, 0, kc0)) if kv_shared else (lambda bb, h: (bb, 0, kc0 + h))
    vidx = (lambda bb, h: (bb, vb0, 0)) if kv_shared else (lambda bb, h: (bb, vb0 + h, 0))
    tidx = (lambda bb, h: (0, 0, 0, 0, 0)) if tab_shared else (lambda bb, h: (h, 0, 0, 0, 0))
    in_specs = [pl.BlockSpec((None, s, qw * LANES), lambda bb, h: (bb, 0, qc0 + h)),
                pl.BlockSpec((None, s, qw * LANES), kidx),
                pl.BlockSpec((None, 2 * HEAD_DIM, s), vidx),
                pl.BlockSpec((None, 2, n_tab + 1, TK, TQ), tidx)]
    args = [q, k, vt, tab]
    if sel_cfg is not None:
        in_specs.append(pl.BlockSpec((None, LANES, s), lambda bb, h: (bb, 0, 0)))
        args.append(sel)
    ospec = pl.BlockSpec((None, s, LANES), lambda bb, h: (bb, 0, h))
    out_specs = [ospec]
    out_shape = [jax.ShapeDtypeStruct((b, s, n_blk * LANES), out_dtype)]
    if want_lse:
        out_specs.append(ospec)
        out_shape.append(jax.ShapeDtypeStruct((b, s, n_blk * LANES), F32))
    return pl.pallas_call(
        functools.partial(_attn_kernel, cfg),
        grid=(b, n_blk),
        in_specs=in_specs,
        out_specs=out_specs,
        out_shape=out_shape,
        scratch_shapes=[pltpu.VMEM((2, TK, TQ), F32)] * 2 + [pltpu.VMEM((2, TK, TQ), BF16)] * 2,
        compiler_params=_params("parallel", "parallel"),
        name="attn",
    )(*args)


def _nsa_cmp_kernel(kc_ref, vc_ref, pe_ref, wlo_ref, whi_ref, w2_ref, gm_ref, gain_ref, kcmp_ref, vcmpt_ref):
    nch = kcmp_ref.shape[0]

    def hidden(c_ref, j):
        lo = hi = None
        for l in range(NSA_CMP_STRIDE):
            rows = c_ref[pl.ds(l, nch, stride=NSA_CMP_STRIDE), :]
            sl = slice(l * LANES, (l + 1) * LANES)
            t_lo = _dot((rows + pe_ref[j, 0][:, sl]).astype(BF16), wlo_ref[j][sl, :])
            t_hi = _dot((rows + pe_ref[j, 1][:, sl]).astype(BF16), whi_ref[j][sl, :])
            lo = t_lo if lo is None else lo + t_lo
            hi = t_hi if hi is None else hi + t_hi
        h = lo + pltpu.roll(hi, nch - 1, 0)
        return (h * _sigmoid(h)).astype(BF16)

    kz = _dot_nt(hidden(kc_ref, 0), w2_ref[0])
    msq = _dot_hilo(kz * kz, gm_ref[...])
    kcmp_ref[...] = (kz * lax.rsqrt(msq + RMS_EPS) * gain_ref[...]).astype(BF16)
    vcmpt_ref[...] = _dot_nt(w2_ref[1], hidden(vc_ref, 1)).astype(BF16)


def _nsa_cmp(kc3, vc3, pe, wlo, whi, w2, gm, gain):
    b, s, width = kc3.shape
    nch = s // NSA_CMP_STRIDE
    consts = [pe, wlo, whi, w2, gm, gain]
    blk = pl.BlockSpec((None, s, width), lambda i: (i, 0, 0))
    oblk = pl.BlockSpec((None, nch, LANES), lambda i: (i, 0, 0))
    return pl.pallas_call(
        _nsa_cmp_kernel,
        grid=(b,),
        in_specs=[blk, blk] + [_resident(a.shape) for a in consts],
        out_specs=[oblk, oblk],
        out_shape=[jax.ShapeDtypeStruct((b, nch, LANES), BF16)] * 2,
        compiler_params=_params("parallel"),
        name="nsa_cmp",
    )(kc3, vc3, *consts)


def _rank_keep(score, ids, top):
    cnt = jnp.zeros(score.shape, jnp.int32)
    for mp in range(score.shape[0]):
        other = score[mp:mp + 1, :]
        tie = jnp.where(mp < ids, 1, 0)
        cnt = cnt + jnp.where(other > score, 1, jnp.where(other == score, tie, 0))
    return cnt < top


def _nsa_sel_kernel(q_ref, kcmp_ref, vcmpt_ref, bias_ref, ovl_ref, oc_ref, sel_ref):
    qi = pl.program_id(1)
    lane = lax.broadcasted_iota(jnp.int32, (1, LANES), 1)
    row = lax.broadcasted_iota(jnp.int32, (LANES, 1), 0)
    t = qi * TQ + lax.broadcasted_iota(jnp.int32, (1, TQ), 1)
    mask_c = (NSA_CMP_STRIDE * row + NSA_CMP_LEN - 1) <= t
    kcmp = kcmp_ref[...]
    heads = [(p, g) for p in range(NSA_HPG) for g in range(NSA_GROUPS)]
    raw = []
    for p, g in heads:
        qp = q_ref[:, p * LANES:(p + 1) * LANES]
        raw.append(_dot_nt(kcmp, jnp.where((lane >> 6) == g, qp, jnp.zeros_like(qp))))
    pcs = []
    for (p, g), r in zip(heads, raw):
        lg = jnp.where(mask_c, r * LN2 + bias_ref[p * NSA_GROUPS + g], NEG_INF)
        m = jnp.max(lg, axis=0, keepdims=True)
        e = jnp.where(mask_c, jnp.exp(lg - m), 0.0)
        den = jnp.maximum(jnp.sum(e, axis=0, keepdims=True), 1e-30)
        pcs.append(e / den)
    imp = jnp.zeros((LANES, TQ), F32)
    ocs = []
    for (p, g), pc in zip(heads, pcs):
        hi, lo = _split(pc)
        ocs.append(_dot(vcmpt_ref[g * HEAD_DIM:(g + 1) * HEAD_DIM, :], hi))
        imp = imp + _dot(ovl_ref[g], hi) + _dot(ovl_ref[g], lo)
    for p in range(NSA_HPG):
        oc_ref[:, p * LANES:(p + 1) * LANES] = jnp.concatenate([ocs[2 * p], ocs[2 * p + 1]], axis=0).T

    n_slc = 32
    ids = row[0:n_slc]
    cur = t >> 6
    forced = (ids == 0) | (ids == cur) | (ids == cur - 1)
    masks = []
    for g in range(NSA_GROUPS):
        score = jnp.where(forced, NSA_FORCED, jnp.where(ids <= cur, imp[g * n_slc:(g + 1) * n_slc], NEG_INF))
        keep = _rank_keep(score, ids, NSA_SLC_TOP) & (score > 0.5 * NEG_INF)
        masks.append(jnp.where(keep, 0.0, NEG_INF))
    masks.append(jnp.full((LANES - NSA_GROUPS * n_slc, TQ), NEG_INF, F32))
    sel_ref[...] = jnp.concatenate(masks, axis=0)


def _nsa_sel(q, kcmp, vcmpt, bias_c, ovl):
    b, s, w = q.shape
    return pl.pallas_call(
        _nsa_sel_kernel,
        grid=(b, s // TQ),
        in_specs=[pl.BlockSpec((None, TQ, w), lambda bb, i: (bb, i, 0)),
                  pl.BlockSpec((None, LANES, LANES), lambda bb, i: (bb, 0, 0)),
                  pl.BlockSpec((None, LANES, LANES), lambda bb, i: (bb, 0, 0)),
                  pl.BlockSpec((NSA_HEADS, LANES, TQ), lambda bb, i: (0, 0, i)),
                  _resident(ovl.shape)],
        out_specs=[pl.BlockSpec((None, TQ, w), lambda bb, i: (bb, i, 0)),
                   pl.BlockSpec((None, LANES, TQ), lambda bb, i: (bb, 0, i))],
        out_shape=[jax.ShapeDtypeStruct((b, s, w), F32), jax.ShapeDtypeStruct((b, LANES, s), F32)],
        compiler_params=_params("parallel", "parallel"),
        name="nsa_sel",
    )(q, kcmp, vcmpt, bias_c, ovl)


def _moba_gate_kernel(q_ref, k_ref, avg_ref, sel_ref):
    qi = pl.program_id(1)
    nb = 8
    kmean = _dot(avg_ref[...], k_ref[...])
    r2 = lax.broadcasted_iota(jnp.int32, kmean.shape, 0)
    c2 = lax.broadcasted_iota(jnp.int32, kmean.shape, 1)
    kmean = jnp.where((r2 >> 3) == (c2 >> 6), kmean, 0.0)
    kh, kl = _split(kmean)
    q = q_ref[...]
    gate = _dot_nt(kh, q) + _dot_nt(kl, q)
    ids = lax.broadcasted_iota(jnp.int32, (nb, 1), 0)
    past = ids < qi
    masks = []
    for h in range(MOBA_HEADS):
        score = jnp.where(past, gate[h * nb:(h + 1) * nb], NEG_INF)
        keep = (_rank_keep(score, ids, MOBA_TOP) & past) | (ids == qi)
        masks.append(jnp.where(keep, 0.0, NEG_INF))
    masks.append(jnp.full((LANES - MOBA_HEADS * nb, TQ), NEG_INF, F32))
    sel_ref[...] = jnp.concatenate(masks, axis=0)


def _moba_gate(p_arr, avg, qc, kc):
    b, s, _ = p_arr.shape
    w = MOBA_HEADS * HEAD_DIM
    return pl.pallas_call(
        _moba_gate_kernel,
        grid=(b, s // TQ),
        in_specs=[pl.BlockSpec((None, TQ, w), lambda bb, i: (bb, i, qc)),
                  pl.BlockSpec((None, s, w), lambda bb, i: (bb, 0, kc)),
                  _resident(avg.shape)],
        out_specs=pl.BlockSpec((None, LANES, TQ), lambda bb, i: (bb, 0, i)),
        out_shape=jax.ShapeDtypeStruct((b, LANES, s), F32),
        compiler_params=_params("parallel", "parallel"),
        name="moba_gate",
    )(p_arr, p_arr, avg)


def _even_out_kernel(x_ref, oa_ref, oc_ref, os_ref, ow_ref, gl_ref, eg_ref, wa_ref, wb_ref, gt_ref, o_ref):
    sg = _sigmoid(gl_ref[...])
    hi, lo = _split(sg)
    nsa = None
    for br, src in enumerate((oc_ref, os_ref, ow_ref)):
        gexp = _dot(hi, eg_ref[br]) + _dot(lo, eg_ref[br])
        term = gexp * src[...]
        nsa = term if nsa is None else nsa + term
    m = _dot(oa_ref[...], wa_ref[...]) + _dot(nsa.astype(BF16), wb_ref[...])
    o_ref[...] = x_ref[...] + gt_ref[...] * m


def _even_out(x, o_a, o_c, o_s, o_w, mla_in, eg, wa, wb, gate, s):
    n, d = x.shape
    tpb = s // TM
    row = lambda wd: pl.BlockSpec((TM, wd), lambda i: (i, 0))
    return pl.pallas_call(
        _even_out_kernel,
        grid=(n // TM,),
        in_specs=[row(d), row(o_a.shape[1]), row(o_c.shape[1]), row(o_s.shape[1]), row(o_w.shape[1]),
                  pl.BlockSpec((TM, LANES), lambda i: (i, 3)),
                  _resident(eg.shape), _resident(wa.shape), _resident(wb.shape), _mod_spec(tpb, d)],
        out_specs=row(d),
        out_shape=jax.ShapeDtypeStruct((n, d), F32),
        compiler_params=_params("parallel"),
        name="even_out",
    )(x, o_a, o_c, o_s, o_w, mla_in, eg, wa, wb, gate)


def _odd_out_kernel(x_ref, od0_ref, od1_ref, od2_ref, ls0_ref, ls1_ref, ls2_ref, om_ref, wd_ref, wm_ref, gt_ref, o_ref):
    ods = (od0_ref, od1_ref, od2_ref)
    ls = [r[...] for r in (ls0_ref, ls1_ref, ls2_ref)]
    mx = jnp.maximum(jnp.maximum(ls[0], ls[1]), ls[2])
    es = [jnp.exp(l - mx) for l in ls]
    den = es[0] + es[1] + es[2]
    merged = None
    for g in range(len(DIL_PAIRS)):
        term = (es[g] / den) * ods[g][...]
        merged = term if merged is None else merged + term
    m = _dot(merged.astype(BF16), wd_ref[...]) + _dot(om_ref[...], wm_ref[...])
    o_ref[...] = x_ref[...] + gt_ref[...] * m


def _odd_out(x, o_ds, lse_ds, o_m, wd, wm, gate, s):
    n, d = x.shape
    tpb = s // TM
    row = lambda wd_: pl.BlockSpec((TM, wd_), lambda i: (i, 0))
    return pl.pallas_call(
        _odd_out_kernel,
        grid=(n // TM,),
        in_specs=[row(d)] + [row(a.shape[1]) for a in (*o_ds, *lse_ds)] + [row(o_m.shape[1]),
                  _resident(wd.shape), _resident(wm.shape), _mod_spec(tpb, d)],
        out_specs=row(d),
        out_shape=jax.ShapeDtypeStruct((n, d), F32),
        compiler_params=_params("parallel"),
        name="odd_out",
    )(x, *o_ds, *lse_ds, o_m, wd, wm, gate)


def _t5_bucket(dist):
    n = jnp.maximum(jnp.asarray(dist, jnp.int32), 0)
    nf = jnp.maximum(n, 1).astype(F32)
    large = T5_MAX_EXACT + (jnp.log(nf / T5_MAX_EXACT) / math.log(T5_MAX_DIST / T5_MAX_EXACT)
                            * (NUM_BUCKETS - T5_MAX_EXACT)).astype(jnp.int32)
    return jnp.where(n < T5_MAX_EXACT, n, jnp.minimum(large, NUM_BUCKETS - 1))


TOEP_PERIOD = 4 * TQ


def _toeplitz_dist():
    j = np.arange(TOEP_PERIOD)
    return np.where(j < 3 * TQ, j, j - TOEP_PERIOD)


def _toeplitz_kernel(n_tab, u_ref, o_ref):
    x = jnp.broadcast_to(u_ref[...], (TK, TOEP_PERIOD))
    y = pltpu.roll(x, 0, 1, stride=1, stride_axis=0)
    for dlt in range(n_tab):
        o_ref[dlt] = y[:, dlt * TQ:(dlt + 1) * TQ]
    o_ref[n_tab] = jnp.full((TK, TQ), NEG_INF * LOG2E, F32)


def _toeplitz_tiles(u, n_tab):
    h = u.shape[0]
    return pl.pallas_call(
        functools.partial(_toeplitz_kernel, n_tab),
        grid=(h,),
        in_specs=[pl.BlockSpec((None, 1, TOEP_PERIOD), lambda i: (i, 0, 0))],
        out_specs=pl.BlockSpec((None, n_tab + 1, TK, TQ), lambda i: (i, 0, 0, 0)),
        out_shape=jax.ShapeDtypeStruct((h, n_tab + 1, TK, TQ), F32),
        compiler_params=_params("parallel"),
        name="toeplitz",
    )(u.reshape(h, 1, TOEP_PERIOD))


def _cmp_bias_kernel(u_ref, o_ref):
    x = jnp.broadcast_to(u_ref[...], o_ref.shape)
    o_ref[...] = pltpu.roll(x, 0, 1, stride=NSA_CMP_STRIDE, stride_axis=0)


def _cmp_bias(u, n_rows):
    h, s = u.shape
    return pl.pallas_call(
        _cmp_bias_kernel,
        grid=(h,),
        in_specs=[pl.BlockSpec((None, 1, s), lambda i: (i, 0, 0))],
        out_specs=pl.BlockSpec((None, n_rows, s), lambda i: (i, 0, 0)),
        out_shape=jax.ShapeDtypeStruct((h, n_rows, s), F32),
        compiler_params=_params("parallel"),
        name="cmp_bias",
    )(u.reshape(h, 1, s))


def _bias_tiles(t5_cols, ok, n_tab=3, dist_scale=1):
    dist = _toeplitz_dist()
    bias = jnp.transpose(t5_cols[_t5_bucket(dist * dist_scale)])
    u = jnp.where(jnp.asarray(ok)[None], bias, NEG_INF) * LOG2E
    return _toeplitz_tiles(u, n_tab)


def _pair(tiles):
    h = tiles.shape[0]
    return tiles.reshape(h // 2, 2, *tiles.shape[1:])


def _group_mean_np(sizes, width=LANES):
    gm = np.zeros((width, width), np.float32)
    o = 0
    for sz in sizes:
        gm[o:o + sz, o:o + sz] = 1.0 / sz
        o += sz
    return gm


def _group_mean_matrix(sizes, width=LANES):
    return jnp.asarray(_group_mean_np(sizes, width), BF16)


EV_META = ((0, 0, False), (0, 128, False), (0, 256, False), (0, 384, False),
           (1, 0, True), (1, 128, True), (1, 256, True), (1, 384, True),
           (2, 0, False), (3, 0, False), (4, 0, True), (5, 0, False), (6, 0, True), (7, 0, False))
EV_OUTS = [(512, F32, False), (512, BF16, False), (128, F32, False), (128, F32, False), (128, BF16, False),
           (128, BF16, True), (128, BF16, False), (128, BF16, True)]
OD_META = (tuple((0, c * LANES, True) for c in range(12)) + ((1, 0, False), (1, LANES, False))
           + tuple((0, (12 + c) * LANES, False) for c in range(4)) + tuple((0, (16 + c) * LANES, True) for c in range(4))
           + ((2, 0, False), (2, LANES, False)))
OD_OUTS = [(20 * LANES, BF16, False), (2 * LANES, BF16, True), (2 * LANES, BF16, True)]


def _even_w_in(w):
    jn, d, _ = w.shape
    z = lambda n_: jnp.zeros((jn, d, n_), w.dtype)
    nq = w[:, :, 416:928].reshape(jn, d, NSA_GROUPS, NSA_HPG, HEAD_DIM)
    nq = jnp.transpose(nq, (0, 1, 3, 2, 4)).reshape(jn, d, NSA_HEADS * HEAD_DIM)
    chunk3 = jnp.concatenate([w[:, :, 1696:1720], z(HEAD_DIM - 24), w[:, :, 384:416], z(LANES - 96)], axis=-1)
    return jnp.concatenate([w[:, :, 0:384], chunk3, nq, w[:, :, 928:1696]], axis=-1)


def kernel(x, c, t5_bias, ada_w, ada_b, norm_g, ffn_w_in, ffn_w_out, ev_w_in, ev_w_out, mla_q_norm_g,
           mla_kv_norm_g, mla_w_uq, mla_w_ukv, mla_qk_g, nsa_cmp_pe, nsa_cmp_w1, nsa_cmp_w2, nsa_qk_g,
           od_w_in, od_w_out, dil_qk_g, moba_qk_g):
    b, s, d = x.shape
    assert (s, d) == (2048, D_MODEL) and s % TM == 0 and TQ == MOBA_BLOCK and TQ == TK
    n = b * s
    hd = HEAD_DIM
    n_even = ev_w_in.shape[0]
    n_odd = od_w_in.shape[0]
    c64 = hd ** -0.5 * LOG2E
    c96 = (MLA_NOPE + MLA_ROPE) ** -0.5 * LOG2E

    mod = _ada(c, ada_w, ada_b).reshape(DEPTH, b, 3, 3, 1, d)

    dist = _toeplitz_dist()
    causal = dist >= 0
    gm64 = _group_mean_matrix((hd, hd))
    gm_proj = _group_mean_matrix((hd,) * (PROJ_CHUNK // hd), PROJ_CHUNK)
    per_chunk = PROJ_CHUNK // LANES
    chunked = lambda m: tuple(tuple(m[i:i + per_chunk]) for i in range(0, len(m), per_chunk))

    padc = FF_PAD - D_FF
    zpad = jnp.zeros(ffn_w_in.shape[:-1] + (padc,), BF16)
    wab_all = jnp.concatenate([ffn_w_in[..., :D_FF].astype(BF16), zpad, ffn_w_in[..., D_FF:].astype(BF16), zpad],
                              axis=-1)
    wo_all = jnp.pad(ffn_w_out, ((0, 0), (0, 0), (0, padc), (0, 0))).astype(BF16)

    nsa_tab = t5_bias[:, MLA_HEADS:MLA_HEADS + NSA_HEADS].reshape(NUM_BUCKETS, NSA_GROUPS, NSA_HPG)
    nsa_cols = jnp.transpose(nsa_tab, (0, 2, 1)).reshape(NUM_BUCKETS, NSA_HEADS)
    tab_sel = _pair(_bias_tiles(nsa_cols, causal))
    tab_win = _pair(_bias_tiles(nsa_cols, causal & (dist <= NSA_WINDOW - 1)))
    tab_mla = _toeplitz_tiles(jnp.where(jnp.asarray(causal), 0.0, NEG_INF).astype(F32)[None], 2)
    tab_mla = jnp.broadcast_to(tab_mla[None], (1, 2) + tab_mla.shape[1:])
    n_cmp_pad = s // NSA_CMP_STRIDE
    bias_c = _cmp_bias(jnp.transpose(nsa_cols[_t5_bucket(np.arange(s) - (NSA_CMP_LEN - 1))]), n_cmp_pad)
    n_cmp = (s - NSA_CMP_LEN) // NSA_CMP_STRIDE + 1
    cstart = np.arange(n_cmp) * NSA_CMP_STRIDE
    sstart = np.arange(s // NSA_SLC_BLOCK) * NSA_SLC_BLOCK
    overlap = np.clip(np.minimum(cstart[:, None] + NSA_CMP_LEN, sstart[None, :] + NSA_SLC_BLOCK)
                      - np.maximum(cstart[:, None], sstart[None, :]), 0, None).astype(np.float32) / NSA_CMP_LEN
    ovl = np.zeros((NSA_GROUPS, LANES, LANES), np.float32)
    for g in range(NSA_GROUPS):
        ovl[g, 32 * g:32 * g + 32, :n_cmp] = overlap.T
    ovl = jnp.asarray(ovl, BF16)
    eg = np.zeros((3, LANES, NSA_HEADS * hd), np.float32)
    for g in range(NSA_GROUPS):
        for p in range(NSA_HPG):
            for br in range(3):
                eg[br, (g * NSA_HPG + p) * 3 + br, p * LANES + g * hd:p * LANES + (g + 1) * hd] = 1.0
    eg = jnp.asarray(eg, BF16)
    gm_mla = jnp.asarray(np.kron(np.eye(2, dtype=np.float32), _group_mean_np((MLA_NOPE, MLA_ROPE))), BF16)
    inv = ROPE_THETA ** (-jnp.arange(0, MLA_ROPE, 2, dtype=F32) / MLA_ROPE)
    ang = jnp.arange(s, dtype=F32)[:, None] * inv[None, :]
    ones = jnp.ones((s, MLA_NOPE), F32)
    tail = LANES - MLA_NOPE - MLA_ROPE
    cos_t = jnp.tile(jnp.concatenate([ones, jnp.cos(ang), jnp.cos(ang), jnp.ones((s, tail), F32)], axis=1), (1, 2))
    sin_t = jnp.tile(jnp.concatenate([0 * ones, jnp.sin(ang), jnp.sin(ang), jnp.zeros((s, tail), F32)], axis=1), (1, 2))

    ev_w = _even_w_in(ev_w_in).astype(BF16)
    ev_gain = jnp.ones((n_even, 14 * LANES), F32)
    ev_gain = ev_gain.at[:, 512:1024].set(jnp.tile(nsa_qk_g[:, 0], (1, 8)) * c64)
    ev_gain = ev_gain.at[:, 1280:1408].set(jnp.tile(nsa_qk_g[:, 1], (1, 2)))
    ev_gain = ev_gain.at[:, 1536:1664].set(jnp.tile(nsa_qk_g[:, 1], (1, 2)))
    gain_kc = jnp.tile(nsa_qk_g[:, 1], (1, 2))
    wuq = jnp.pad(mla_w_uq.reshape(n_even, MLA_Q_LORA, MLA_HEADS, MLA_NOPE + MLA_ROPE),
                  ((0, 0), (0, 0), (0, 0), (0, tail))).reshape(n_even, MLA_Q_LORA, MLA_HEADS * LANES).astype(BF16)
    ukv = mla_w_ukv.reshape(n_even, MLA_KV_LORA, MLA_HEADS, MLA_NOPE + MLA_V)
    wuk = jnp.pad(ukv[..., :MLA_NOPE], ((0, 0), (0, 0), (0, 0), (0, LANES - MLA_NOPE))
                  ).reshape(n_even, MLA_KV_LORA, MLA_HEADS * LANES).astype(BF16)
    wuv = jnp.swapaxes(ukv[..., MLA_NOPE:].reshape(n_even, MLA_KV_LORA, MLA_HEADS * MLA_V), 1, 2).astype(BF16)
    zt = jnp.zeros((n_even, tail), F32)
    gq = jnp.tile(jnp.concatenate([mla_qk_g[:, 0] * c96, zt], axis=1), (1, 2))
    gkn = jnp.tile(jnp.concatenate([mla_qk_g[:, 1, :MLA_NOPE], jnp.zeros((n_even, LANES - MLA_NOPE), F32)], axis=1), (1, 2))
    gkr = jnp.tile(jnp.concatenate([jnp.zeros((n_even, MLA_NOPE), F32), mla_qk_g[:, 1, MLA_NOPE:], zt], axis=1), (1, 2))
    pe2 = jnp.broadcast_to(nsa_cmp_pe.reshape(n_even, 2, 2, 16, 1, hd), (n_even, 2, 2, 16, NSA_GROUPS, hd)
                           ).reshape(n_even, 2, 2, 1, 16 * LANES)
    eye = jnp.eye(NSA_GROUPS, dtype=F32)
    w1 = nsa_cmp_w1.reshape(n_even, 2, 2, 16, hd, NSA_CMP_HID)
    w1x = jnp.einsum('ijaldc,gh->ijalgdhc', w1, eye).reshape(n_even, 2, 2, 16 * LANES, NSA_GROUPS * NSA_CMP_HID).astype(BF16)
    w2x = jnp.einsum('ijcd,gh->ijhdgc', nsa_cmp_w2, eye).reshape(n_even, 2, LANES, NSA_GROUPS * NSA_CMP_HID).astype(BF16)
    wa_o = ev_w_out[:, :MLA_HEADS * MLA_V].astype(BF16)
    wb_o = jnp.transpose(ev_w_out[:, MLA_HEADS * MLA_V:].reshape(n_even, NSA_GROUPS, NSA_HPG, hd, d),
                         (0, 2, 1, 3, 4)).reshape(n_even, NSA_HEADS * hd, d).astype(BF16)

    assert DIL_PAIRS == ((128, 1), (512, 4), (2048, 16))
    dil_cfg = ((1, 1), (4, 1), (8, 0))
    dil_ok = (causal & (dist <= 128), causal & (dist <= 128), causal & (dist % 2 == 0))
    tab_dil = [_pair(_bias_tiles(t5_bias[:, gi * DIL_HPG:(gi + 1) * DIL_HPG], dil_ok[gi], dist_scale=dil_cfg[gi][0]))
               for gi in range(len(DIL_PAIRS))]
    tab_moba = _pair(_bias_tiles(t5_bias[:, DIL_SLOTS:DIL_SLOTS + MOBA_HEADS], causal))
    avg = np.zeros((LANES, s), np.float32)
    for h in range(MOBA_HEADS):
        for m in range(s // MOBA_BLOCK):
            avg[8 * h + m, m * MOBA_BLOCK:(m + 1) * MOBA_BLOCK] = 1.0 / MOBA_BLOCK
    avg = jnp.asarray(avg, BF16)
    od_w = od_w_in.astype(BF16)
    od_gain = jnp.concatenate([jnp.tile(dil_qk_g[:, 0], (1, 12)) * c64, jnp.tile(dil_qk_g[:, 1], (1, 12)),
                               jnp.ones((n_odd, 768), F32), jnp.tile(moba_qk_g[:, 0], (1, 4)) * c64,
                               jnp.tile(moba_qk_g[:, 1], (1, 4)), jnp.ones((n_odd, 256), F32)], axis=1)
    wd_o = od_w_out[:, :DIL_HPG * hd].astype(BF16)
    wm_o = od_w_out[:, DIL_HPG * hd:].astype(BF16)

    sh3 = lambda a: a.reshape(b, s, a.shape[-1])
    tr3 = lambda a: jnp.swapaxes(sh3(a), 1, 2)
    xf = x.reshape(n, d)
    for i in range(DEPTH):
        j = i // 2
        g_i = norm_g[i].reshape(3, 1, d)
        xf = _ffn(xf, g_i[0], mod[i, :, 0, 0], mod[i, :, 0, 1], mod[i, :, 0, 2],
                  wab_all[i, 0], wo_all[i, 0], s)
        if i % 2 == 0:
            mla_in, nsa_q, kc, vc, ks, vs, kw, vw = _proj(
                xf, g_i[1], mod[i, :, 1, 0], mod[i, :, 1, 1], ev_w[j], ev_gain[j][None], gm_proj,
                chunked(EV_META), EV_OUTS, s)
            qf, kf, vf = _mla_prep(mla_in, mla_q_norm_g[j][None], mla_kv_norm_g[j][None], wuq[j], wuk[j], wuv[j],
                                   gm_mla, gq[j][None], gkn[j][None], gkr[j][None], cos_t, sin_t, s)
            (o_a,) = _attn(sh3(qf), sh3(kf), vf, tab_mla, None, qc0=0, kc0=0, vb0=0, n_blk=MLA_HEADS // 2, qw=2,
                           tab_shared=True, out_dtype=BF16)
            kcmp, vcmpt = _nsa_cmp(sh3(kc), sh3(vc), pe2[j], w1x[j, :, 0], w1x[j, :, 1], w2x[j], gm64, gain_kc[j][None])
            o_c, sel = _nsa_sel(sh3(nsa_q), kcmp, vcmpt, bias_c, ovl)
            (o_s,) = _attn(sh3(nsa_q), sh3(ks), vs, tab_sel, sel, qc0=0, kc0=0, vb0=0, n_blk=NSA_HPG, qw=1,
                           kv_shared=True, sel_cfg=(TK // NSA_SLC_BLOCK, 32, 0))
            (o_w,) = _attn(sh3(nsa_q), sh3(kw), vw, tab_win, None, qc0=0, kc0=0, vb0=0, n_blk=NSA_HPG, qw=1,
                           kv_shared=True, backs=(2, 2, 2, 2))
            xf = _even_out(xf, o_a.reshape(n, -1), o_c.reshape(n, -1), o_s.reshape(n, -1), o_w.reshape(n, -1),
                           mla_in, eg, wa_o[j], wb_o[j], mod[i, :, 1, 2], s)
        else:
            pr, vd0t, vmt = _proj(xf, g_i[1], mod[i, :, 1, 0], mod[i, :, 1, 1], od_w[j], od_gain[j][None], gm_proj,
                                  chunked(OD_META), OD_OUTS, s)
            pr3 = pr.reshape(b, s, pr.shape[-1])
            o_ds, lse_ds = [], []
            for gi, (r, bk) in enumerate(dil_cfg):
                cols = lambda c0: pr3[:, :, c0 * LANES:(c0 + 2) * LANES].reshape(b, s // r, r, 2 * LANES)
                if r == 1:
                    qk, qc0, kc0, vdt = pr3, 0, 6, vd0t
                else:
                    vdt = jnp.transpose(cols(10 + 2 * gi), (0, 3, 2, 1)).reshape(b, 2 * LANES, s)
                    qk = jnp.concatenate([cols(2 * gi), cols(6 + 2 * gi)], axis=-1)
                    qk, qc0, kc0 = jnp.transpose(qk, (0, 2, 1, 3)).reshape(b, s, 4 * LANES), 0, 2
                o_g, lse_g = _attn(qk, qk, vdt, tab_dil[gi], None, qc0=qc0, kc0=kc0, vb0=0, n_blk=2,
                                   qw=1, backs=(bk,), want_lse=True, stride=r)
                o_ds.append(o_g.reshape(n, -1))
                lse_ds.append(lse_g.reshape(n, -1))
            selm = _moba_gate(pr3, avg, 8, 9)
            (o_m,) = _attn(pr3, pr3, vmt, tab_moba, selm, qc0=16, kc0=18, vb0=0, n_blk=2, qw=1,
                           sel_cfg=(1, 8, 2), out_dtype=BF16)
            xf = _odd_out(xf, o_ds, lse_ds, o_m.reshape(n, -1), wd_o[j], wm_o[j], mod[i, :, 1, 2], s)
        xf = _ffn(xf, g_i[2], mod[i, :, 2, 0], mod[i, :, 2, 1], mod[i, :, 2, 2],
                  wab_all[i, 1], wo_all[i, 1], s)
    return xf.reshape(b, s, d)
```

```python
import functools
import math

import numpy as np
import jax
import jax.numpy as jnp
from jax import lax
from jax.experimental import pallas as pl
from jax.experimental.pallas import tpu as pltpu

F32 = jnp.float32
BF16 = jnp.bfloat16

D_MODEL = 1024
DEPTH = 4
D_FF = 2752
HEAD_DIM = 64
NUM_BUCKETS = 32
T5_MAX_EXACT = 16
T5_MAX_DIST = 128
RMS_EPS = 1e-6
NEG_INF = -1e30
MLA_HEADS = 8
MLA_NOPE = 64
MLA_ROPE = 32
MLA_V = 64
MLA_Q_LORA = 256
MLA_KV_LORA = 128
ROPE_THETA = 10000.0
NSA_HEADS = 8
NSA_GROUPS = 2
NSA_HPG = 4
NSA_CMP_LEN = 32
NSA_CMP_STRIDE = 16
NSA_CMP_HID = 256
NSA_SLC_BLOCK = 64
NSA_SLC_TOP = 8
NSA_WINDOW = 512
NSA_FORCED = 1e6
DIL_PAIRS = ((128, 1), (512, 4), (2048, 16))
DIL_HPG = 4
DIL_SLOTS = len(DIL_PAIRS) * DIL_HPG
MOBA_HEADS = 4
MOBA_BLOCK = 256
MOBA_TOP = 3

LANES = 128
V7X_VMEM_BYTES = 64 * 1024 * 1024
VMEM_LIMIT = V7X_VMEM_BYTES * 7 // 8
TM = 512
TQ = 256
TK = 256
FF_CHUNK = 256
FF_PAD = -(-D_FF // FF_CHUNK) * FF_CHUNK
PROJ_CHUNK = 256
LOG2E = math.log2(math.e)
LN2 = math.log(2.0)


def _dot(a, b):
    return jnp.dot(a, b, preferred_element_type=F32)


def _dot_nt(a, b):
    return lax.dot_general(a, b, (((1,), (1,)), ((), ())), preferred_element_type=F32)


def _split(a):
    hi = a.astype(BF16)
    lo = (a - hi.astype(F32)).astype(BF16)
    return hi, lo


def _dot_hilo(a, b):
    hi, lo = _split(a)
    return _dot(hi, b) + _dot(lo, b)


def _sigmoid(x):
    return 1.0 / (1.0 + jnp.exp(-x))


def _modulated_norm(x, g, shift, scale):
    ms = jnp.mean(x * x, axis=-1, keepdims=True)
    y = x * lax.rsqrt(ms + RMS_EPS) * g
    return y * (1.0 + scale) + shift


def _params(*sem):
    return pltpu.CompilerParams(dimension_semantics=sem, vmem_limit_bytes=VMEM_LIMIT)


def _resident(shape):
    nd = len(shape)
    return pl.BlockSpec(shape, lambda *_: (0,) * nd, pipeline_mode=pl.Buffered(1))


def _ada_kernel(c_ref, w_ref, b_ref, o_ref):
    c = c_ref[...]
    ca = c * _sigmoid(c)
    o_ref[...] = jnp.dot(ca, w_ref[...], preferred_element_type=F32,
                         precision=lax.Precision.HIGHEST) + b_ref[...]


def _ada(c, ada_w, ada_b):
    depth, d, n = ada_w.shape
    b = c.shape[0]
    tn = 9 * LANES
    return pl.pallas_call(
        _ada_kernel,
        grid=(depth, n // tn),
        in_specs=[pl.BlockSpec((b, d), lambda l, j: (0, 0)),
                  pl.BlockSpec((None, d, tn), lambda l, j: (l, 0, j)),
                  pl.BlockSpec((None, 1, tn), lambda l, j: (l, 0, j))],
        out_specs=pl.BlockSpec((None, b, tn), lambda l, j: (l, 0, j)),
        out_shape=jax.ShapeDtypeStruct((depth, b, n), F32),
        compiler_params=_params("parallel", "parallel"),
        name="ada",
    )(c, ada_w, ada_b.reshape(depth, 1, n))


def _ffn_kernel(x_ref, g_ref, sh_ref, sc_ref, gt_ref, wa_ref, wb_ref, wo_ref, o_ref, y_ref, acc_ref):
    y_ref[...] = _modulated_norm(x_ref[...], g_ref[...], sh_ref[...], sc_ref[...]).astype(BF16)
    for c in range(FF_PAD // FF_CHUNK):
        sl = slice(c * FF_CHUNK, (c + 1) * FF_CHUNK)
        a = _dot(y_ref[...], wa_ref[:, sl])
        b = _dot(y_ref[...], wb_ref[:, sl])
        u = (a * _sigmoid(a) * b).astype(BF16)
        contrib = _dot(u, wo_ref[sl, :])
        if c == 0:
            acc_ref[...] = contrib
        else:
            acc_ref[...] += contrib
    o_ref[...] = x_ref[...] + 0.5 * gt_ref[...] * acc_ref[...]


def _mod_spec(tiles_per_batch, d):
    return pl.BlockSpec((None, 1, d), lambda i: (i // tiles_per_batch, 0, 0))


def _ffn(x, g, shift, scale, gate, wa, wb, wo, s):
    n, d = x.shape
    tpb = s // TM
    return pl.pallas_call(
        _ffn_kernel,
        grid=(n // TM,),
        in_specs=[pl.BlockSpec((TM, d), lambda i: (i, 0)),
                  _resident((1, d)),
                  _mod_spec(tpb, d), _mod_spec(tpb, d), _mod_spec(tpb, d),
                  _resident(wa.shape), _resident(wb.shape), _resident(wo.shape)],
        out_specs=pl.BlockSpec((TM, d), lambda i: (i, 0)),
        out_shape=jax.ShapeDtypeStruct((n, d), F32),
        scratch_shapes=[pltpu.VMEM((TM, d), BF16), pltpu.VMEM((TM, d), F32)],
        compiler_params=_params("parallel"),
        name="ffn",
    )(x, g, shift, scale, gate, wa, wb, wo)


def _proj_kernel(meta, kinds, seq, x_ref, g_ref, sh_ref, sc_ref, w_ref, gain_ref, gm_ref, *rest):
    n_out = len(kinds)
    outs = rest[:n_out]
    y_ref, stage_ref = rest[n_out], rest[n_out + 1]
    tile_in_seq = pl.program_id(0) % (seq // TM)
    y_ref[...] = _modulated_norm(x_ref[...], g_ref[...], sh_ref[...], sc_ref[...]).astype(BF16)
    for c, halves in enumerate(meta):
        sl = slice(c * PROJ_CHUNK, (c + 1) * PROJ_CHUNK)
        z = _dot(y_ref[...], w_ref[:, sl])
        if any(normed for _, _, normed in halves):
            msq = _dot((z * z).astype(BF16), gm_ref[...])
            zn = z * lax.rsqrt(msq + RMS_EPS) * gain_ref[:, sl]
        for hf, (oi, off, normed) in enumerate(halves):
            src = (zn if normed else z)[:, hf * LANES:(hf + 1) * LANES]
            kind = kinds[oi]
            if kind == "cols":
                src = src * gain_ref[:, c * PROJ_CHUNK + hf * LANES:c * PROJ_CHUNK + (hf + 1) * LANES]
                outs[oi][off:off + LANES, :] = src.T.astype(outs[oi].dtype)
            elif kind == "rows":
                outs[oi][:, off:off + LANES] = src.astype(outs[oi].dtype)
            else:
                per = TM // kind
                stage_ref[...] = src
                for cs in range(kind):
                    dest = pl.multiple_of(cs * (seq // kind) + tile_in_seq * per, per)
                    outs[oi][pl.ds(dest, per), off:off + LANES] = (
                        stage_ref[pl.ds(cs, per, stride=kind), :].astype(outs[oi].dtype))


def _proj(x, g, shift, scale, w, gain, gm, meta, out_defs, s):
    n, d = x.shape
    tpb = s // TM
    specs = {"rows": lambda wd: pl.BlockSpec((TM, wd), lambda i: (i, 0)),
             "cols": lambda wd: pl.BlockSpec((None, wd, TM), lambda i: (i // tpb, 0, i % tpb))}
    stream = lambda wd: pl.BlockSpec((None, s, wd), lambda i: (i // tpb, 0, 0))
    shapes = {"rows": lambda wd: (n, wd), "cols": lambda wd: (n // s, wd, s)}
    kinds = tuple(k for _, _, k in out_defs)
    return pl.pallas_call(
        functools.partial(_proj_kernel, meta, kinds, s),
        grid=(n // TM,),
        in_specs=[pl.BlockSpec((TM, d), lambda i: (i, 0)),
                  _resident((1, d)),
                  _mod_spec(tpb, d), _mod_spec(tpb, d),
                  _resident(w.shape), _resident(gain.shape), _resident(gm.shape)],
        out_specs=[specs.get(k, stream)(wd) for wd, _, k in out_defs],
        out_shape=[jax.ShapeDtypeStruct(shapes.get(k, lambda wd: (n // s, s, wd))(wd), dt) for wd, dt, k in out_defs],
        scratch_shapes=[pltpu.VMEM((TM, d), BF16), pltpu.VMEM((TM, LANES), F32)],
        compiler_params=_params("arbitrary"),
        name="proj",
    )(x, g, shift, scale, w, gain, gm)


def _mla_prep_kernel(in_ref, qg_ref, kvg_ref, wuq_ref, wuk_ref, wuv_ref, gm_ref,
                     gq_ref, gkn_ref, gkr_ref, cos_ref, sin_ref, q_out, k_out, v_out):
    def rms(z, g):
        return z * lax.rsqrt(jnp.mean(z * z, axis=-1, keepdims=True) + RMS_EPS) * g

    cqn = rms(in_ref[:, 0:MLA_Q_LORA], qg_ref[...]).astype(BF16)
    ckvn = rms(in_ref[:, MLA_Q_LORA:MLA_Q_LORA + MLA_KV_LORA], kvg_ref[...]).astype(BF16)
    c3 = in_ref[:, 3 * LANES:4 * LANES]
    cos = cos_ref[...]
    sin = sin_ref[...]
    gm = gm_ref[...]
    half = MLA_ROPE // 2
    slot_lane = lax.broadcasted_iota(jnp.int32, (1, 2 * LANES), 1) & (LANES - 1)
    first_half = slot_lane < MLA_NOPE + half

    def norm_rope(z, gain):
        msq = _dot((z * z).astype(BF16), gm)
        z = z * lax.rsqrt(msq + RMS_EPS) * gain
        rot = jnp.where(first_half, -pltpu.roll(z, 2 * LANES - half, 1), pltpu.roll(z, half, 1))
        return z * cos + rot * sin

    kr = norm_rope(jnp.concatenate([c3, c3], axis=1), gkr_ref[...])
    for h in range(MLA_HEADS // 2):
        sl = slice(h * 2 * LANES, (h + 1) * 2 * LANES)
        q_out[:, sl] = norm_rope(_dot(cqn, wuq_ref[:, sl]), gq_ref[...]).astype(BF16)
        k_out[:, sl] = (norm_rope(_dot(ckvn, wuk_ref[:, sl]), gkn_ref[...]) + kr).astype(BF16)
    v_out[...] = _dot_nt(wuv_ref[...], ckvn).astype(BF16)


def _mla_prep(mla_in, qg, kvg, wuq, wuk, wuv, gm, gq, gkn, gkr, cos_t, sin_t, s):
    n = mla_in.shape[0]
    tpb = s // TM
    hw = MLA_HEADS * LANES
    vw = MLA_HEADS * MLA_V
    tab = pl.BlockSpec((TM, 2 * LANES), lambda i: (i % tpb, 0))
    consts = [qg, kvg, wuq, wuk, wuv, gm, gq, gkn, gkr]
    return pl.pallas_call(
        _mla_prep_kernel,
        grid=(n // TM,),
        in_specs=[pl.BlockSpec((TM, 4 * LANES), lambda i: (i, 0))] + [_resident(a.shape) for a in consts] + [tab, tab],
        out_specs=[pl.BlockSpec((TM, hw), lambda i: (i, 0)), pl.BlockSpec((TM, hw), lambda i: (i, 0)),
                   pl.BlockSpec((None, vw, TM), lambda i: (i // tpb, 0, i % tpb))],
        out_shape=[jax.ShapeDtypeStruct((n, hw), BF16), jax.ShapeDtypeStruct((n, hw), BF16),
                   jax.ShapeDtypeStruct((n // s, vw, s), BF16)],
        compiler_params=_params("parallel"),
        name="mla_prep",
    )(mla_in, *consts, cos_t, sin_t)


def _attn_kernel(cfg, q_ref, k_ref, vt_ref, tab_ref, *rest):
    n_tab, qw, backs, sel_cfg, want_lse, n_qt, stride = cfg
    n_qs = n_qt // stride
    if sel_cfg is not None:
        sel_ref, rest = rest[0], rest[1:]
        sel_bpt, sel_stride, sel_pair_mul = sel_cfg
    o_ref = rest[0]
    lse_ref = rest[1] if want_lse else None
    st_scr, p_scr = rest[-4:-2], rest[-2:]
    blk = pl.program_id(1)
    lane = lax.broadcasted_iota(jnp.int32, (1, LANES), 1)

    counts = tuple(sum(min(t % n_qs, bk) + 1 for t in range(n_qt)) for bk in backs)
    back = jnp.int32(backs[-1])
    n_tiles = jnp.int32(counts[-1])
    for bi in range(len(backs) - 2, -1, -1):
        back = jnp.where(blk == bi, jnp.int32(backs[bi]), back)
        n_tiles = jnp.where(blk == bi, jnp.int32(counts[bi]), n_tiles)

    def first_key_tile(qi):
        return qi - jnp.minimum(qi & (n_qs - 1), back)

    def rows(t, size):
        return pl.ds(pl.multiple_of(t * size, size), size)

    def out_rows(t, size):
        if stride == 1:
            return rows(t, size)
        return pl.ds((t >> int(math.log2(n_qs))) + stride * size * (t & (n_qs - 1)), size, stride=stride)

    def logits_to(slot, qi, j):
        if qw == 1:
            q = q_ref[rows(qi, TQ), :]
            zero = jnp.zeros_like(q)
            qs = [jnp.where(lane < HEAD_DIM, q, zero), jnp.where(lane >= HEAD_DIM, q, zero)]
        else:
            qs = [q_ref[rows(qi, TQ), s * LANES:(s + 1) * LANES] for s in range(2)]
        for s in range(2):
            kj = k_ref[rows(j, TK), :] if qw == 1 else k_ref[rows(j, TK), s * LANES:(s + 1) * LANES]
            st = _dot_nt(kj, qs[s])
            if sel_cfg is not None:
                off = sel_stride * (s + sel_pair_mul * blk) + sel_bpt * j
                kb = TK // sel_bpt
                qcol = pl.ds(pl.multiple_of(qi * TQ, TQ), TQ)
                st = jnp.concatenate(
                    [st[bk * kb:(bk + 1) * kb] + sel_ref[pl.ds(off + bk, 1), qcol]
                     for bk in range(sel_bpt)], axis=0)
            st_scr[slot][s] = st

    def accumulate(slot, j, alphas, accs):
        krow = pl.multiple_of(j * TK, TK)
        new = []
        for s in range(2):
            vt = vt_ref[s * HEAD_DIM:(s + 1) * HEAD_DIM, pl.ds(krow, TK)]
            new.append(alphas[s] * accs[s] + _dot(vt, p_scr[slot][s]))
        return tuple(new)

    def finalize(qi, stats, accs):
        out_t = jnp.concatenate([acc / l for acc, (_, l) in zip(accs, stats)], axis=0)
        o_ref[out_rows(qi, TQ), :] = out_t.T.astype(o_ref.dtype)
        if want_lse:
            lse_t = jnp.concatenate([jnp.broadcast_to((m + jnp.log2(l)) * LN2, (HEAD_DIM, TQ))
                                     for (m, l) in stats], axis=0)
            lse_ref[out_rows(qi, TQ), :] = lse_t.T

    def advance(q, j):
        last = j == q
        at_end = jnp.logical_and(last, q == n_qt - 1)
        starts = jnp.logical_and(last, jnp.logical_not(at_end))
        qn = jnp.where(starts, q + 1, q)
        jn = jnp.where(at_end, j, jnp.where(last, first_key_tile(q + 1), j + 1))
        return qn, jn, starts, at_end

    def softmax(slot, tile, stats):
        q, j, is_first, filler = tile
        keep = jnp.where(is_first, 0.0, 1.0)
        d = jnp.where(filler, n_tab, jnp.minimum(q - j, n_tab - 1))
        new_stats, alphas = [], []
        for s in range(2):
            m, l = stats[s]
            m = jnp.where(is_first, NEG_INF, m)
            st = st_scr[slot][s] + tab_ref[s, d]
            m_new = jnp.maximum(m, jnp.max(st, axis=0, keepdims=True))
            alpha = jnp.exp2(m - m_new) * keep
            p = jnp.exp2(st - m_new)
            new_stats.append((m_new, alpha * l + jnp.sum(p, axis=0, keepdims=True)))
            alphas.append(alpha)
            p_scr[slot][s] = p.astype(BF16)
        return tuple(new_stats), tuple(alphas)

    def body(u, carry):
        tile_a, (q2, j2), (q1, j1, first1), stats2, stats1, alphas2, alphas1, accs = carry
        qa, ja, first_a, _ = tile_a
        tile_b = advance(qa, ja)
        qb, jb, first_b, _ = tile_b
        tile_c = advance(qb, jb)
        logits_to(1, qb, jb)
        accs_x = accumulate(0, j2, alphas2, accs)
        accs_y = accumulate(1, j1, alphas1, accs_x)
        stats_a, alphas_a = softmax(0, tile_a, stats1)
        logits_to(0, tile_c[0], tile_c[1])
        stats_b, alphas_b = softmax(1, tile_b, stats_a)

        @pl.when(jnp.logical_and(first1, u > 0))
        def _():
            finalize(q2, stats2, accs_x)

        @pl.when(jnp.logical_and(first_a, u > 0))
        def _():
            finalize(q1, stats1, accs_y)

        return tile_c, (qa, ja), (qb, jb, first_b), stats_a, stats_b, alphas_a, alphas_b, accs_y

    zero_i = jnp.int32(0)
    logits_to(0, zero_i, zero_i)
    for p_slot in p_scr:
        p_slot[...] = jnp.zeros(p_slot.shape, BF16)
    stats0 = tuple((jnp.full((1, TQ), NEG_INF, F32), jnp.zeros((1, TQ), F32)) for _ in range(2))
    ones = tuple(jnp.ones((1, TQ), F32) for _ in range(2))
    init = ((zero_i, zero_i, zero_i == 0, zero_i != 0), (zero_i, zero_i), (zero_i, zero_i, zero_i != 0),
            stats0, stats0, ones, ones, tuple(jnp.zeros((HEAD_DIM, TQ), F32) for _ in range(2)))
    _, (q2, j2), (q1, j1, first1), stats2, stats1, alphas2, alphas1, accs = lax.fori_loop(
        0, (n_tiles + 1) // 2, body, init)
    accs_x = accumulate(0, j2, alphas2, accs)

    @pl.when(first1)
    def _():
        finalize(q2, stats2, accs_x)

    finalize(q1, stats1, accumulate(1, j1, alphas1, accs_x))


def _attn(q, k, vt, tab, sel, *, qc0, kc0, vb0, n_blk, qw, backs=None, kv_shared=False,
          tab_shared=False, sel_cfg=None, out_dtype=F32, want_lse=False, stride=1):
    b, s = q.shape[:2]
    n_tab = tab.shape[2] - 1
    n_qt = s // TQ
    if backs is None:
        backs = (n_qt,)
    cfg = (n_tab, qw, backs, sel_cfg, want_lse, n_qt, stride)
    kidx = (lambda bb, h: (bb, 0, kc0)) if kv_shared else (lambda bb, h: (bb, 0, kc0 + h))
    vidx = (lambda bb, h: (bb, vb0, 0)) if kv_shared else (lambda bb, h: (bb, vb0 + h, 0))
    tidx = (lambda bb, h: (0, 0, 0, 0, 0)) if tab_shared else (lambda bb, h: (h, 0, 0, 0, 0))
    in_specs = [pl.BlockSpec((None, s, qw * LANES), lambda bb, h: (bb, 0, qc0 + h)),
                pl.BlockSpec((None, s, qw * LANES), kidx),
                pl.BlockSpec((None, 2 * HEAD_DIM, s), vidx),
                pl.BlockSpec((None, 2, n_tab + 1, TK, TQ), tidx)]
    args = [q, k, vt, tab]
    if sel_cfg is not None:
        in_specs.append(pl.BlockSpec((None, LANES, s), lambda bb, h: (bb, 0, 0)))
        args.append(sel)
    ospec = pl.BlockSpec((None, s, LANES), lambda bb, h: (bb, 0, h))
    out_specs = [ospec]
    out_shape = [jax.ShapeDtypeStruct((b, s, n_blk * LANES), out_dtype)]
    if want_lse:
        out_specs.append(ospec)
        out_shape.append(jax.ShapeDtypeStruct((b, s, n_blk * LANES), F32))
    return pl.pallas_call(
        functools.partial(_attn_kernel, cfg),
        grid=(b, n_blk),
        in_specs=in_specs,
        out_specs=out_specs,
        out_shape=out_shape,
        scratch_shapes=[pltpu.VMEM((2, TK, TQ), F32)] * 2 + [pltpu.VMEM((2, TK, TQ), BF16)] * 2,
        compiler_params=_params("parallel", "parallel"),
        name="attn",
    )(*args)


def _nsa_cmp_kernel(kc_ref, vc_ref, pe_ref, wlo_ref, whi_ref, w2_ref, gm_ref, gain_ref, kcmp_ref, vcmpt_ref):
    nch = kcmp_ref.shape[0]

    def hidden(c_ref, j):
        lo = hi = None
        for l in range(NSA_CMP_STRIDE):
            rows = c_ref[pl.ds(l, nch, stride=NSA_CMP_STRIDE), :]
            sl = slice(l * LANES, (l + 1) * LANES)
            t_lo = _dot((rows + pe_ref[j, 0][:, sl]).astype(BF16), wlo_ref[j][sl, :])
            t_hi = _dot((rows + pe_ref[j, 1][:, sl]).astype(BF16), whi_ref[j][sl, :])
            lo = t_lo if lo is None else lo + t_lo
            hi = t_hi if hi is None else hi + t_hi
        h = lo + pltpu.roll(hi, nch - 1, 0)
        return (h * _sigmoid(h)).astype(BF16)

    kz = _dot_nt(hidden(kc_ref, 0), w2_ref[0])
    msq = _dot_hilo(kz * kz, gm_ref[...])
    kcmp_ref[...] = (kz * lax.rsqrt(msq + RMS_EPS) * gain_ref[...]).astype(BF16)
    vcmpt_ref[...] = _dot_nt(w2_ref[1], hidden(vc_ref, 1)).astype(BF16)


def _nsa_cmp(kc3, vc3, pe, wlo, whi, w2, gm, gain):
    b, s, width = kc3.shape
    nch = s // NSA_CMP_STRIDE
    consts = [pe, wlo, whi, w2, gm, gain]
    blk = pl.BlockSpec((None, s, width), lambda i: (i, 0, 0))
    oblk = pl.BlockSpec((None, nch, LANES), lambda i: (i, 0, 0))
    return pl.pallas_call(
        _nsa_cmp_kernel,
        grid=(b,),
        in_specs=[blk, blk] + [_resident(a.shape) for a in consts],
        out_specs=[oblk, oblk],
        out_shape=[jax.ShapeDtypeStruct((b, nch, LANES), BF16)] * 2,
        compiler_params=_params("parallel"),
        name="nsa_cmp",
    )(kc3, vc3, *consts)


def _rank_keep(score, ids, top):
    cnt = jnp.zeros(score.shape, jnp.int32)
    for mp in range(score.shape[0]):
        other = score[mp:mp + 1, :]
        tie = jnp.where(mp < ids, 1, 0)
        cnt = cnt + jnp.where(other > score, 1, jnp.where(other == score, tie, 0))
    return cnt < top


def _nsa_sel_kernel(q_ref, kcmp_ref, vcmpt_ref, bias_ref, ovl_ref, oc_ref, sel_ref):
    qi = pl.program_id(1)
    lane = lax.broadcasted_iota(jnp.int32, (1, LANES), 1)
    row = lax.broadcasted_iota(jnp.int32, (LANES, 1), 0)
    t = qi * TQ + lax.broadcasted_iota(jnp.int32, (1, TQ), 1)
    mask_c = (NSA_CMP_STRIDE * row + NSA_CMP_LEN - 1) <= t
    kcmp = kcmp_ref[...]
    heads = [(p, g) for p in range(NSA_HPG) for g in range(NSA_GROUPS)]
    raw = []
    for p, g in heads:
        qp = q_ref[:, p * LANES:(p + 1) * LANES]
        raw.append(_dot_nt(kcmp, jnp.where((lane >> 6) == g, qp, jnp.zeros_like(qp))))
    pcs = []
    for (p, g), r in zip(heads, raw):
        lg = jnp.where(mask_c, r * LN2 + bias_ref[p * NSA_GROUPS + g], NEG_INF)
        m = jnp.max(lg, axis=0, keepdims=True)
        e = jnp.where(mask_c, jnp.exp(lg - m), 0.0)
        den = jnp.maximum(jnp.sum(e, axis=0, keepdims=True), 1e-30)
        pcs.append(e / den)
    imp = jnp.zeros((LANES, TQ), F32)
    ocs = []
    for (p, g), pc in zip(heads, pcs):
        hi, lo = _split(pc)
        ocs.append(_dot(vcmpt_ref[g * HEAD_DIM:(g + 1) * HEAD_DIM, :], hi))
        imp = imp + _dot(ovl_ref[g], hi) + _dot(ovl_ref[g], lo)
    for p in range(NSA_HPG):
        oc_ref[:, p * LANES:(p + 1) * LANES] = jnp.concatenate([ocs[2 * p], ocs[2 * p + 1]], axis=0).T

    n_slc = 32
    ids = row[0:n_slc]
    cur = t >> 6
    forced = (ids == 0) | (ids == cur) | (ids == cur - 1)
    masks = []
    for g in range(NSA_GROUPS):
        score = jnp.where(forced, NSA_FORCED, jnp.where(ids <= cur, imp[g * n_slc:(g + 1) * n_slc], NEG_INF))
        keep = _rank_keep(score, ids, NSA_SLC_TOP) & (score > 0.5 * NEG_INF)
        masks.append(jnp.where(keep, 0.0, NEG_INF))
    masks.append(jnp.full((LANES - NSA_GROUPS * n_slc, TQ), NEG_INF, F32))
    sel_ref[...] = jnp.concatenate(masks, axis=0)


def _nsa_sel(q, kcmp, vcmpt, bias_c, ovl):
    b, s, w = q.shape
    return pl.pallas_call(
        _nsa_sel_kernel,
        grid=(b, s // TQ),
        in_specs=[pl.BlockSpec((None, TQ, w), lambda bb, i: (bb, i, 0)),
                  pl.BlockSpec((None, LANES, LANES), lambda bb, i: (bb, 0, 0)),
                  pl.BlockSpec((None, LANES, LANES), lambda bb, i: (bb, 0, 0)),
                  pl.BlockSpec((NSA_HEADS, LANES, TQ), lambda bb, i: (0, 0, i)),
                  _resident(ovl.shape)],
        out_specs=[pl.BlockSpec((None, TQ, w), lambda bb, i: (bb, i, 0)),
                   pl.BlockSpec((None, LANES, TQ), lambda bb, i: (bb, 0, i))],
        out_shape=[jax.ShapeDtypeStruct((b, s, w), F32), jax.ShapeDtypeStruct((b, LANES, s), F32)],
        compiler_params=_params("parallel", "parallel"),
        name="nsa_sel",
    )(q, kcmp, vcmpt, bias_c, ovl)


def _moba_gate_kernel(q_ref, k_ref, avg_ref, sel_ref):
    qi = pl.program_id(1)
    nb = 8
    kmean = _dot(avg_ref[...], k_ref[...])
    r2 = lax.broadcasted_iota(jnp.int32, kmean.shape, 0)
    c2 = lax.broadcasted_iota(jnp.int32, kmean.shape, 1)
    kmean = jnp.where((r2 >> 3) == (c2 >> 6), kmean, 0.0)
    kh, kl = _split(kmean)
    q = q_ref[...]
    gate = _dot_nt(kh, q) + _dot_nt(kl, q)
    ids = lax.broadcasted_iota(jnp.int32, (nb, 1), 0)
    past = ids < qi
    masks = []
    for h in range(MOBA_HEADS):
        score = jnp.where(past, gate[h * nb:(h + 1) * nb], NEG_INF)
        keep = (_rank_keep(score, ids, MOBA_TOP) & past) | (ids == qi)
        masks.append(jnp.where(keep, 0.0, NEG_INF))
    masks.append(jnp.full((LANES - MOBA_HEADS * nb, TQ), NEG_INF, F32))
    sel_ref[...] = jnp.concatenate(masks, axis=0)


def _moba_gate(p_arr, avg, qc, kc):
    b, s, _ = p_arr.shape
    w = MOBA_HEADS * HEAD_DIM
    return pl.pallas_call(
        _moba_gate_kernel,
        grid=(b, s // TQ),
        in_specs=[pl.BlockSpec((None, TQ, w), lambda bb, i: (bb, i, qc)),
                  pl.BlockSpec((None, s, w), lambda bb, i: (bb, 0, kc)),
                  _resident(avg.shape)],
        out_specs=pl.BlockSpec((None, LANES, TQ), lambda bb, i: (bb, 0, i)),
        out_shape=jax.ShapeDtypeStruct((b, LANES, s), F32),
        compiler_params=_params("parallel", "parallel"),
        name="moba_gate",
    )(p_arr, p_arr, avg)


def _even_out_kernel(x_ref, oa_ref, oc_ref, os_ref, ow_ref, gl_ref, eg_ref, wa_ref, wb_ref, gt_ref, o_ref):
    sg = _sigmoid(gl_ref[...])
    hi, lo = _split(sg)
    nsa = None
    for br, src in enumerate((oc_ref, os_ref, ow_ref)):
        gexp = _dot(hi, eg_ref[br]) + _dot(lo, eg_ref[br])
        term = gexp * src[...]
        nsa = term if nsa is None else nsa + term
    m = _dot(oa_ref[...], wa_ref[...]) + _dot(nsa.astype(BF16), wb_ref[...])
    o_ref[...] = x_ref[...] + gt_ref[...] * m


def _even_out(x, o_a, o_c, o_s, o_w, mla_in, eg, wa, wb, gate, s):
    n, d = x.shape
    tpb = s // TM
    row = lambda wd: pl.BlockSpec((TM, wd), lambda i: (i, 0))
    return pl.pallas_call(
        _even_out_kernel,
        grid=(n // TM,),
        in_specs=[row(d), row(o_a.shape[1]), row(o_c.shape[1]), row(o_s.shape[1]), row(o_w.shape[1]),
                  pl.BlockSpec((TM, LANES), lambda i: (i, 3)),
                  _resident(eg.shape), _resident(wa.shape), _resident(wb.shape), _mod_spec(tpb, d)],
        out_specs=row(d),
        out_shape=jax.ShapeDtypeStruct((n, d), F32),
        compiler_params=_params("parallel"),
        name="even_out",
    )(x, o_a, o_c, o_s, o_w, mla_in, eg, wa, wb, gate)


def _odd_out_kernel(x_ref, od0_ref, od1_ref, od2_ref, ls0_ref, ls1_ref, ls2_ref, om_ref, wd_ref, wm_ref, gt_ref, o_ref):
    ods = (od0_ref, od1_ref, od2_ref)
    ls = [r[...] for r in (ls0_ref, ls1_ref, ls2_ref)]
    mx = jnp.maximum(jnp.maximum(ls[0], ls[1]), ls[2])
    es = [jnp.exp(l - mx) for l in ls]
    den = es[0] + es[1] + es[2]
    merged = None
    for g in range(len(DIL_PAIRS)):
        term = (es[g] / den) * ods[g][...]
        merged = term if merged is None else merged + term
    m = _dot(merged.astype(BF16), wd_ref[...]) + _dot(om_ref[...], wm_ref[...])
    o_ref[...] = x_ref[...] + gt_ref[...] * m


def _odd_out(x, o_ds, lse_ds, o_m, wd, wm, gate, s):
    n, d = x.shape
    tpb = s // TM
    row = lambda wd_: pl.BlockSpec((TM, wd_), lambda i: (i, 0))
    return pl.pallas_call(
        _odd_out_kernel,
        grid=(n // TM,),
        in_specs=[row(d)] + [row(a.shape[1]) for a in (*o_ds, *lse_ds)] + [row(o_m.shape[1]),
                  _resident(wd.shape), _resident(wm.shape), _mod_spec(tpb, d)],
        out_specs=row(d),
        out_shape=jax.ShapeDtypeStruct((n, d), F32),
        compiler_params=_params("parallel"),
        name="odd_out",
    )(x, *o_ds, *lse_ds, o_m, wd, wm, gate)


def _t5_bucket(dist):
    n = jnp.maximum(jnp.asarray(dist, jnp.int32), 0)
    nf = jnp.maximum(n, 1).astype(F32)
    large = T5_MAX_EXACT + (jnp.log(nf / T5_MAX_EXACT) / math.log(T5_MAX_DIST / T5_MAX_EXACT)
                            * (NUM_BUCKETS - T5_MAX_EXACT)).astype(jnp.int32)
    return jnp.where(n < T5_MAX_EXACT, n, jnp.minimum(large, NUM_BUCKETS - 1))


TOEP_PERIOD = 4 * TQ


def _toeplitz_dist():
    j = np.arange(TOEP_PERIOD)
    return np.where(j < 3 * TQ, j, j - TOEP_PERIOD)


def _toeplitz_kernel(n_tab, u_ref, o_ref):
    x = jnp.broadcast_to(u_ref[...], (TK, TOEP_PERIOD))
    y = pltpu.roll(x, 0, 1, stride=1, stride_axis=0)
    for dlt in range(n_tab):
        o_ref[dlt] = y[:, dlt * TQ:(dlt + 1) * TQ]
    o_ref[n_tab] = jnp.full((TK, TQ), NEG_INF * LOG2E, F32)


def _toeplitz_tiles(u, n_tab):
    h = u.shape[0]
    return pl.pallas_call(
        functools.partial(_toeplitz_kernel, n_tab),
        grid=(h,),
        in_specs=[pl.BlockSpec((None, 1, TOEP_PERIOD), lambda i: (i, 0, 0))],
        out_specs=pl.BlockSpec((None, n_tab + 1, TK, TQ), lambda i: (i, 0, 0, 0)),
        out_shape=jax.ShapeDtypeStruct((h, n_tab + 1, TK, TQ), F32),
        compiler_params=_params("parallel"),
        name="toeplitz",
    )(u.reshape(h, 1, TOEP_PERIOD))


def _cmp_bias_kernel(u_ref, o_ref):
    x = jnp.broadcast_to(u_ref[...], o_ref.shape)
    o_ref[...] = pltpu.roll(x, 0, 1, stride=NSA_CMP_STRIDE, stride_axis=0)


def _cmp_bias(u, n_rows):
    h, s = u.shape
    return pl.pallas_call(
        _cmp_bias_kernel,
        grid=(h,),
        in_specs=[pl.BlockSpec((None, 1, s), lambda i: (i, 0, 0))],
        out_specs=pl.BlockSpec((None, n_rows, s), lambda i: (i, 0, 0)),
        out_shape=jax.ShapeDtypeStruct((h, n_rows, s), F32),
        compiler_params=_params("parallel"),
        name="cmp_bias",
    )(u.reshape(h, 1, s))


def _bias_tiles(t5_cols, ok, n_tab=3, dist_scale=1):
    dist = _toeplitz_dist()
    bias = jnp.transpose(t5_cols[_t5_bucket(dist * dist_scale)])
    u = jnp.where(jnp.asarray(ok)[None], bias, NEG_INF) * LOG2E
    return _toeplitz_tiles(u, n_tab)


def _pair(tiles):
    h = tiles.shape[0]
    return tiles.reshape(h // 2, 2, *tiles.shape[1:])


def _group_mean_np(sizes, width=LANES):
    gm = np.zeros((width, width), np.float32)
    o = 0
    for sz in sizes:
        gm[o:o + sz, o:o + sz] = 1.0 / sz
        o += sz
    return gm


def _group_mean_matrix(sizes, width=LANES):
    return jnp.asarray(_group_mean_np(sizes, width), BF16)


EV_META = ((0, 0, False), (0, 128, False), (0, 256, False), (0, 384, False),
           (1, 0, True), (1, 128, True), (1, 256, True), (1, 384, True),
           (2, 0, False), (3, 0, False), (4, 0, True), (5, 0, False), (6, 0, True), (7, 0, False))
EV_OUTS = [(512, F32, "rows"), (512, BF16, "rows"), (128, F32, "rows"), (128, F32, "rows"), (128, BF16, "rows"),
           (128, BF16, "cols"), (128, BF16, "rows"), (128, BF16, "cols")]
OD_DIL_STRIDES = (1, 4, 8)
OD_META = (((0, 0, True), (0, LANES, True)) + tuple((3 + g, h * LANES, True) for g in range(2) for h in range(2))
           + ((0, 2 * LANES, True), (0, 3 * LANES, True))
           + tuple((3 + g, (2 + h) * LANES, True) for g in range(2) for h in range(2))
           + ((1, 0, False), (1, LANES, False)) + tuple((0, (4 + c) * LANES, False) for c in range(4))
           + tuple((0, (8 + c) * LANES, True) for c in range(4)) + ((2, 0, False), (2, LANES, False)))
OD_OUTS = [(12 * LANES, BF16, "rows"), (2 * LANES, BF16, "cols"), (2 * LANES, BF16, "cols"),
           (4 * LANES, BF16, OD_DIL_STRIDES[1]), (4 * LANES, BF16, OD_DIL_STRIDES[2])]


def _even_w_in(w):
    jn, d, _ = w.shape
    z = lambda n_: jnp.zeros((jn, d, n_), w.dtype)
    nq = w[:, :, 416:928].reshape(jn, d, NSA_GROUPS, NSA_HPG, HEAD_DIM)
    nq = jnp.transpose(nq, (0, 1, 3, 2, 4)).reshape(jn, d, NSA_HEADS * HEAD_DIM)
    chunk3 = jnp.concatenate([w[:, :, 1696:1720], z(HEAD_DIM - 24), w[:, :, 384:416], z(LANES - 96)], axis=-1)
    return jnp.concatenate([w[:, :, 0:384], chunk3, nq, w[:, :, 928:1696]], axis=-1)


def kernel(x, c, t5_bias, ada_w, ada_b, norm_g, ffn_w_in, ffn_w_out, ev_w_in, ev_w_out, mla_q_norm_g,
           mla_kv_norm_g, mla_w_uq, mla_w_ukv, mla_qk_g, nsa_cmp_pe, nsa_cmp_w1, nsa_cmp_w2, nsa_qk_g,
           od_w_in, od_w_out, dil_qk_g, moba_qk_g):
    b, s, d = x.shape
    assert (s, d) == (2048, D_MODEL) and s % TM == 0 and TQ == MOBA_BLOCK and TQ == TK
    n = b * s
    hd = HEAD_DIM
    n_even = ev_w_in.shape[0]
    n_odd = od_w_in.shape[0]
    c64 = hd ** -0.5 * LOG2E
    c96 = (MLA_NOPE + MLA_ROPE) ** -0.5 * LOG2E

    mod = _ada(c, ada_w, ada_b).reshape(DEPTH, b, 3, 3, 1, d)

    dist = _toeplitz_dist()
    causal = dist >= 0
    gm64 = _group_mean_matrix((hd, hd))
    gm_proj = _group_mean_matrix((hd,) * (PROJ_CHUNK // hd), PROJ_CHUNK)
    per_chunk = PROJ_CHUNK // LANES
    chunked = lambda m: tuple(tuple(m[i:i + per_chunk]) for i in range(0, len(m), per_chunk))

    padc = FF_PAD - D_FF
    wa_all = jnp.pad(ffn_w_in[..., :D_FF], ((0, 0), (0, 0), (0, 0), (0, padc))).astype(BF16)
    wb_all = jnp.pad(ffn_w_in[..., D_FF:], ((0, 0), (0, 0), (0, 0), (0, padc))).astype(BF16)
    wo_all = jnp.pad(ffn_w_out, ((0, 0), (0, 0), (0, padc), (0, 0))).astype(BF16)

    nsa_tab = t5_bias[:, MLA_HEADS:MLA_HEADS + NSA_HEADS].reshape(NUM_BUCKETS, NSA_GROUPS, NSA_HPG)
    nsa_cols = jnp.transpose(nsa_tab, (0, 2, 1)).reshape(NUM_BUCKETS, NSA_HEADS)
    tab_sel = _pair(_bias_tiles(nsa_cols, causal))
    tab_win = _pair(_bias_tiles(nsa_cols, causal & (dist <= NSA_WINDOW - 1)))
    tab_mla = _toeplitz_tiles(jnp.where(jnp.asarray(causal), 0.0, NEG_INF).astype(F32)[None], 2)
    tab_mla = jnp.broadcast_to(tab_mla[None], (1, 2) + tab_mla.shape[1:])
    n_cmp_pad = s // NSA_CMP_STRIDE
    bias_c = _cmp_bias(jnp.transpose(nsa_cols[_t5_bucket(np.arange(s) - (NSA_CMP_LEN - 1))]), n_cmp_pad)
    n_cmp = (s - NSA_CMP_LEN) // NSA_CMP_STRIDE + 1
    cstart = np.arange(n_cmp) * NSA_CMP_STRIDE
    sstart = np.arange(s // NSA_SLC_BLOCK) * NSA_SLC_BLOCK
    overlap = np.clip(np.minimum(cstart[:, None] + NSA_CMP_LEN, sstart[None, :] + NSA_SLC_BLOCK)
                      - np.maximum(cstart[:, None], sstart[None, :]), 0, None).astype(np.float32) / NSA_CMP_LEN
    ovl = np.zeros((NSA_GROUPS, LANES, LANES), np.float32)
    for g in range(NSA_GROUPS):
        ovl[g, 32 * g:32 * g + 32, :n_cmp] = overlap.T
    ovl = jnp.asarray(ovl, BF16)
    eg = np.zeros((3, LANES, NSA_HEADS * hd), np.float32)
    for g in range(NSA_GROUPS):
        for p in range(NSA_HPG):
            for br in range(3):
                eg[br, (g * NSA_HPG + p) * 3 + br, p * LANES + g * hd:p * LANES + (g + 1) * hd] = 1.0
    eg = jnp.asarray(eg, BF16)
    gm_mla = jnp.asarray(np.kron(np.eye(2, dtype=np.float32), _group_mean_np((MLA_NOPE, MLA_ROPE))), BF16)
    inv = ROPE_THETA ** (-jnp.arange(0, MLA_ROPE, 2, dtype=F32) / MLA_ROPE)
    ang = jnp.arange(s, dtype=F32)[:, None] * inv[None, :]
    ones = jnp.ones((s, MLA_NOPE), F32)
    tail = LANES - MLA_NOPE - MLA_ROPE
    cos_t = jnp.tile(jnp.concatenate([ones, jnp.cos(ang), jnp.cos(ang), jnp.ones((s, tail), F32)], axis=1), (1, 2))
    sin_t = jnp.tile(jnp.concatenate([0 * ones, jnp.sin(ang), jnp.sin(ang), jnp.zeros((s, tail), F32)], axis=1), (1, 2))

    ev_w = _even_w_in(ev_w_in).astype(BF16)
    ev_gain = jnp.ones((n_even, 14 * LANES), F32)
    ev_gain = ev_gain.at[:, 512:1024].set(jnp.tile(nsa_qk_g[:, 0], (1, 8)) * c64)
    ev_gain = ev_gain.at[:, 1280:1408].set(jnp.tile(nsa_qk_g[:, 1], (1, 2)))
    ev_gain = ev_gain.at[:, 1536:1664].set(jnp.tile(nsa_qk_g[:, 1], (1, 2)))
    gain_kc = jnp.tile(nsa_qk_g[:, 1], (1, 2))
    wuq = jnp.pad(mla_w_uq.reshape(n_even, MLA_Q_LORA, MLA_HEADS, MLA_NOPE + MLA_ROPE),
                  ((0, 0), (0, 0), (0, 0), (0, tail))).reshape(n_even, MLA_Q_LORA, MLA_HEADS * LANES).astype(BF16)
    ukv = mla_w_ukv.reshape(n_even, MLA_KV_LORA, MLA_HEADS, MLA_NOPE + MLA_V)
    wuk = jnp.pad(ukv[..., :MLA_NOPE], ((0, 0), (0, 0), (0, 0), (0, LANES - MLA_NOPE))
                  ).reshape(n_even, MLA_KV_LORA, MLA_HEADS * LANES).astype(BF16)
    wuv = jnp.swapaxes(ukv[..., MLA_NOPE:].reshape(n_even, MLA_KV_LORA, MLA_HEADS * MLA_V), 1, 2).astype(BF16)
    zt = jnp.zeros((n_even, tail), F32)
    gq = jnp.tile(jnp.concatenate([mla_qk_g[:, 0] * c96, zt], axis=1), (1, 2))
    gkn = jnp.tile(jnp.concatenate([mla_qk_g[:, 1, :MLA_NOPE], jnp.zeros((n_even, LANES - MLA_NOPE), F32)], axis=1), (1, 2))
    gkr = jnp.tile(jnp.concatenate([jnp.zeros((n_even, MLA_NOPE), F32), mla_qk_g[:, 1, MLA_NOPE:], zt], axis=1), (1, 2))
    pe2 = jnp.broadcast_to(nsa_cmp_pe.reshape(n_even, 2, 2, 16, 1, hd), (n_even, 2, 2, 16, NSA_GROUPS, hd)
                           ).reshape(n_even, 2, 2, 1, 16 * LANES)
    eye = jnp.eye(NSA_GROUPS, dtype=F32)
    w1 = nsa_cmp_w1.reshape(n_even, 2, 2, 16, hd, NSA_CMP_HID)
    w1x = jnp.einsum('ijaldc,gh->ijalgdhc', w1, eye).reshape(n_even, 2, 2, 16 * LANES, NSA_GROUPS * NSA_CMP_HID).astype(BF16)
    w2x = jnp.einsum('ijcd,gh->ijhdgc', nsa_cmp_w2, eye).reshape(n_even, 2, LANES, NSA_GROUPS * NSA_CMP_HID).astype(BF16)
    wa_o = ev_w_out[:, :MLA_HEADS * MLA_V].astype(BF16)
    wb_o = jnp.transpose(ev_w_out[:, MLA_HEADS * MLA_V:].reshape(n_even, NSA_GROUPS, NSA_HPG, hd, d),
                         (0, 2, 1, 3, 4)).reshape(n_even, NSA_HEADS * hd, d).astype(BF16)

    assert DIL_PAIRS == ((128, 1), (512, 4), (2048, 16))
    dil_cfg = tuple(zip(OD_DIL_STRIDES, (1, 1, 0)))
    dil_ok = (causal & (dist <= 128), causal & (dist <= 128), causal & (dist % 2 == 0))
    tab_dil = [_pair(_bias_tiles(t5_bias[:, gi * DIL_HPG:(gi + 1) * DIL_HPG], dil_ok[gi], dist_scale=dil_cfg[gi][0]))
               for gi in range(len(DIL_PAIRS))]
    tab_moba = _pair(_bias_tiles(t5_bias[:, DIL_SLOTS:DIL_SLOTS + MOBA_HEADS], causal))
    avg = np.zeros((LANES, s), np.float32)
    for h in range(MOBA_HEADS):
        for m in range(s // MOBA_BLOCK):
            avg[8 * h + m, m * MOBA_BLOCK:(m + 1) * MOBA_BLOCK] = 1.0 / MOBA_BLOCK
    avg = jnp.asarray(avg, BF16)
    od_w = od_w_in.astype(BF16)
    od_gain = jnp.concatenate([jnp.tile(dil_qk_g[:, 0], (1, 12)) * c64, jnp.tile(dil_qk_g[:, 1], (1, 12)),
                               jnp.ones((n_odd, 768), F32), jnp.tile(moba_qk_g[:, 0], (1, 4)) * c64,
                               jnp.tile(moba_qk_g[:, 1], (1, 4)), jnp.ones((n_odd, 256), F32)], axis=1)
    wd_o = od_w_out[:, :DIL_HPG * hd].astype(BF16)
    wm_o = od_w_out[:, DIL_HPG * hd:].astype(BF16)

    sh3 = lambda a: a.reshape(b, s, a.shape[-1])
    tr3 = lambda a: jnp.swapaxes(sh3(a), 1, 2)
    xf = x.reshape(n, d)
    for i in range(DEPTH):
        j = i // 2
        g_i = norm_g[i].reshape(3, 1, d)
        xf = _ffn(xf, g_i[0], mod[i, :, 0, 0], mod[i, :, 0, 1], mod[i, :, 0, 2],
                  wa_all[i, 0], wb_all[i, 0], wo_all[i, 0], s)
        if i % 2 == 0:
            mla_in, nsa_q, kc, vc, ks, vs, kw, vw = _proj(
                xf, g_i[1], mod[i, :, 1, 0], mod[i, :, 1, 1], ev_w[j], ev_gain[j][None], gm_proj,
                chunked(EV_META), EV_OUTS, s)
            qf, kf, vf = _mla_prep(mla_in, mla_q_norm_g[j][None], mla_kv_norm_g[j][None], wuq[j], wuk[j], wuv[j],
                                   gm_mla, gq[j][None], gkn[j][None], gkr[j][None], cos_t, sin_t, s)
            (o_a,) = _attn(sh3(qf), sh3(kf), vf, tab_mla, None, qc0=0, kc0=0, vb0=0, n_blk=MLA_HEADS // 2, qw=2,
                           tab_shared=True, out_dtype=BF16)
            kcmp, vcmpt = _nsa_cmp(sh3(kc), sh3(vc), pe2[j], w1x[j, :, 0], w1x[j, :, 1], w2x[j], gm64, gain_kc[j][None])
            o_c, sel = _nsa_sel(sh3(nsa_q), kcmp, vcmpt, bias_c, ovl)
            (o_s,) = _attn(sh3(nsa_q), sh3(ks), vs, tab_sel, sel, qc0=0, kc0=0, vb0=0, n_blk=NSA_HPG, qw=1,
                           kv_shared=True, sel_cfg=(TK // NSA_SLC_BLOCK, 32, 0))
            (o_w,) = _attn(sh3(nsa_q), sh3(kw), vw, tab_win, None, qc0=0, kc0=0, vb0=0, n_blk=NSA_HPG, qw=1,
                           kv_shared=True, backs=(2, 2, 2, 2))
            xf = _even_out(xf, o_a.reshape(n, -1), o_c.reshape(n, -1), o_s.reshape(n, -1), o_w.reshape(n, -1),
                           mla_in, eg, wa_o[j], wb_o[j], mod[i, :, 1, 2], s)
        else:
            pr, vd0t, vmt, qk1, qk2 = _proj(xf, g_i[1], mod[i, :, 1, 0], mod[i, :, 1, 1], od_w[j], od_gain[j][None],
                                            gm_proj, chunked(OD_META), OD_OUTS, s)
            pr3 = pr.reshape(b, s, pr.shape[-1])
            o_ds, lse_ds = [], []
            for gi, (r, bk) in enumerate(dil_cfg):
                if r == 1:
                    qk, vdt = pr3, vd0t
                else:
                    v = pr3[:, :, (2 + 2 * gi) * LANES:(4 + 2 * gi) * LANES].reshape(b, s // r, r, 2 * LANES)
                    qk, vdt = (qk1, qk2)[gi - 1], jnp.transpose(v, (0, 3, 2, 1)).reshape(b, 2 * LANES, s)
                o_g, lse_g = _attn(qk, qk, vdt, tab_dil[gi], None, qc0=0, kc0=2, vb0=0, n_blk=2,
                                   qw=1, backs=(bk,), want_lse=True, stride=r)
                o_ds.append(o_g.reshape(n, -1))
                lse_ds.append(lse_g.reshape(n, -1))
            selm = _moba_gate(pr3, avg, 4, 5)
            (o_m,) = _attn(pr3, pr3, vmt, tab_moba, selm, qc0=8, kc0=10, vb0=0, n_blk=2, qw=1,
                           sel_cfg=(1, 8, 2), out_dtype=BF16)
            xf = _odd_out(xf, o_ds, lse_ds, o_m.reshape(n, -1), wd_o[j], wm_o[j], mod[i, :, 1, 2], s)
        xf = _ffn(xf, g_i[2], mod[i, :, 2, 0], mod[i, :, 2, 1], mod[i, :, 2, 2],
                  wa_all[i, 1], wb_all[i, 1], wo_all[i, 1], s)
    return xf.reshape(b, s, d)
```

```python
import functools
import math

import numpy as np
import jax
import jax.numpy as jnp
from jax import lax
from jax.experimental import pallas as pl
from jax.experimental.pallas import tpu as pltpu

F32 = jnp.float32
BF16 = jnp.bfloat16

D_MODEL = 1024
DEPTH = 4
D_FF = 2752
HEAD_DIM = 64
NUM_BUCKETS = 32
T5_MAX_EXACT = 16
T5_MAX_DIST = 128
RMS_EPS = 1e-6
NEG_INF = -1e30
MLA_HEADS = 8
MLA_NOPE = 64
MLA_ROPE = 32
MLA_V = 64
MLA_Q_LORA = 256
MLA_KV_LORA = 128
ROPE_THETA = 10000.0
NSA_HEADS = 8
NSA_GROUPS = 2
NSA_HPG = 4
NSA_CMP_LEN = 32
NSA_CMP_STRIDE = 16
NSA_CMP_HID = 256
NSA_SLC_BLOCK = 64
NSA_SLC_TOP = 8
NSA_WINDOW = 512
NSA_FORCED = 1e6
DIL_PAIRS = ((128, 1), (512, 4), (2048, 16))
DIL_HPG = 4
DIL_SLOTS = len(DIL_PAIRS) * DIL_HPG
MOBA_HEADS = 4
MOBA_BLOCK = 256
MOBA_TOP = 3

LANES = 128
V7X_VMEM_BYTES = 64 * 1024 * 1024
VMEM_LIMIT = V7X_VMEM_BYTES * 7 // 8
TM = 512
TQ = 256
TK = 256
FF_CHUNK = 256
FF_PAD = -(-D_FF // FF_CHUNK) * FF_CHUNK
PROJ_CHUNK = 256
DEN_ROWS = 16
LOG2E = math.log2(math.e)
LN2 = math.log(2.0)


def _dot(a, b):
    return jnp.dot(a, b, preferred_element_type=F32)


def _dot_nt(a, b):
    return lax.dot_general(a, b, (((1,), (1,)), ((), ())), preferred_element_type=F32)


def _split(a):
    hi = a.astype(BF16)
    lo = (a - hi.astype(F32)).astype(BF16)
    return hi, lo


def _dot_hilo(a, b):
    hi, lo = _split(a)
    return _dot(hi, b) + _dot(lo, b)


def _sigmoid(x):
    return 1.0 / (1.0 + jnp.exp(-x))


def _modulated_norm(x, g, shift, scale):
    ms = jnp.mean(x * x, axis=-1, keepdims=True)
    y = x * lax.rsqrt(ms + RMS_EPS) * g
    return y * (1.0 + scale) + shift


def _params(*sem):
    return pltpu.CompilerParams(dimension_semantics=sem, vmem_limit_bytes=VMEM_LIMIT)


def _resident(shape):
    nd = len(shape)
    return pl.BlockSpec(shape, lambda *_: (0,) * nd, pipeline_mode=pl.Buffered(1))


def _ada_kernel(c_ref, w_ref, b_ref, o_ref):
    c = c_ref[...]
    ca = c * _sigmoid(c)
    o_ref[...] = jnp.dot(ca, w_ref[...], preferred_element_type=F32,
                         precision=lax.Precision.HIGHEST) + b_ref[...]


def _ada(c, ada_w, ada_b):
    depth, d, n = ada_w.shape
    b = c.shape[0]
    tn = 9 * LANES
    return pl.pallas_call(
        _ada_kernel,
        grid=(depth, n // tn),
        in_specs=[pl.BlockSpec((b, d), lambda l, j: (0, 0)),
                  pl.BlockSpec((None, d, tn), lambda l, j: (l, 0, j)),
                  pl.BlockSpec((None, 1, tn), lambda l, j: (l, 0, j))],
        out_specs=pl.BlockSpec((None, b, tn), lambda l, j: (l, 0, j)),
        out_shape=jax.ShapeDtypeStruct((depth, b, n), F32),
        compiler_params=_params("parallel", "parallel"),
        name="ada",
    )(c, ada_w, ada_b.reshape(depth, 1, n))


def _ffn_kernel(x_ref, g_ref, sh_ref, sc_ref, gt_ref, wa_ref, wb_ref, wo_ref, o_ref, y_ref, acc_ref):
    y_ref[...] = _modulated_norm(x_ref[...], g_ref[...], sh_ref[...], sc_ref[...]).astype(BF16)
    for c in range(FF_PAD // FF_CHUNK):
        sl = slice(c * FF_CHUNK, (c + 1) * FF_CHUNK)
        a = _dot(y_ref[...], wa_ref[:, sl])
        b = _dot(y_ref[...], wb_ref[:, sl])
        u = (a * _sigmoid(a) * b).astype(BF16)
        contrib = _dot(u, wo_ref[sl, :])
        if c == 0:
            acc_ref[...] = contrib
        else:
            acc_ref[...] += contrib
    o_ref[...] = x_ref[...] + 0.5 * gt_ref[...] * acc_ref[...]


def _mod_spec(tiles_per_batch, d):
    return pl.BlockSpec((None, 1, d), lambda i: (i // tiles_per_batch, 0, 0))


def _ffn(x, g, shift, scale, gate, wa, wb, wo, s):
    n, d = x.shape
    tpb = s // TM
    return pl.pallas_call(
        _ffn_kernel,
        grid=(n // TM,),
        in_specs=[pl.BlockSpec((TM, d), lambda i: (i, 0)),
                  _resident((1, d)),
                  _mod_spec(tpb, d), _mod_spec(tpb, d), _mod_spec(tpb, d),
                  _resident(wa.shape), _resident(wb.shape), _resident(wo.shape)],
        out_specs=pl.BlockSpec((TM, d), lambda i: (i, 0)),
        out_shape=jax.ShapeDtypeStruct((n, d), F32),
        scratch_shapes=[pltpu.VMEM((TM, d), BF16), pltpu.VMEM((TM, d), F32)],
        compiler_params=_params("parallel"),
        name="ffn",
    )(x, g, shift, scale, gate, wa, wb, wo)


def _proj_kernel(meta, kinds, seq, x_ref, g_ref, sh_ref, sc_ref, w_ref, gain_ref, gm_ref, *rest):
    n_out = len(kinds)
    outs = rest[:n_out]
    y_ref, stage_ref = rest[n_out], rest[n_out + 1]
    tile_in_seq = pl.program_id(0) % (seq // TM)
    y_ref[...] = _modulated_norm(x_ref[...], g_ref[...], sh_ref[...], sc_ref[...]).astype(BF16)
    for c, halves in enumerate(meta):
        sl = slice(c * PROJ_CHUNK, (c + 1) * PROJ_CHUNK)
        z = _dot(y_ref[...], w_ref[:, sl])
        if any(normed for _, _, normed in halves):
            msq = _dot((z * z).astype(BF16), gm_ref[...])
            zn = z * lax.rsqrt(msq + RMS_EPS) * gain_ref[:, sl]
        for hf, (oi, off, normed) in enumerate(halves):
            src = (zn if normed else z)[:, hf * LANES:(hf + 1) * LANES]
            kind = kinds[oi]
            if kind == "cols":
                src = src * gain_ref[:, c * PROJ_CHUNK + hf * LANES:c * PROJ_CHUNK + (hf + 1) * LANES]
                outs[oi][off:off + LANES, :] = src.T.astype(outs[oi].dtype)
            elif kind == "rows":
                outs[oi][:, off:off + LANES] = src.astype(outs[oi].dtype)
            else:
                per = TM // kind
                stage_ref[...] = src
                for cs in range(kind):
                    dest = pl.multiple_of(cs * (seq // kind) + tile_in_seq * per, per)
                    outs[oi][pl.ds(dest, per), off:off + LANES] = (
                        stage_ref[pl.ds(cs, per, stride=kind), :].astype(outs[oi].dtype))


def _proj(x, g, shift, scale, w, gain, gm, meta, out_defs, s):
    n, d = x.shape
    tpb = s // TM
    specs = {"rows": lambda wd: pl.BlockSpec((TM, wd), lambda i: (i, 0)),
             "cols": lambda wd: pl.BlockSpec((None, wd, TM), lambda i: (i // tpb, 0, i % tpb))}
    stream = lambda wd: pl.BlockSpec((None, s, wd), lambda i: (i // tpb, 0, 0))
    shapes = {"rows": lambda wd: (n, wd), "cols": lambda wd: (n // s, wd, s)}
    kinds = tuple(k for _, _, k in out_defs)
    return pl.pallas_call(
        functools.partial(_proj_kernel, meta, kinds, s),
        grid=(n // TM,),
        in_specs=[pl.BlockSpec((TM, d), lambda i: (i, 0)),
                  _resident((1, d)),
                  _mod_spec(tpb, d), _mod_spec(tpb, d),
                  _resident(w.shape), _resident(gain.shape), _resident(gm.shape)],
        out_specs=[specs.get(k, stream)(wd) for wd, _, k in out_defs],
        out_shape=[jax.ShapeDtypeStruct(shapes.get(k, lambda wd: (n // s, s, wd))(wd), dt) for wd, dt, k in out_defs],
        scratch_shapes=[pltpu.VMEM((TM, d), BF16), pltpu.VMEM((TM, LANES), F32)],
        compiler_params=_params("arbitrary"),
        name="proj",
    )(x, g, shift, scale, w, gain, gm)


def _mla_prep_kernel(in_ref, qg_ref, kvg_ref, wuq_ref, wuk_ref, wuv_ref, gm_ref,
                     gq_ref, gkn_ref, gkr_ref, cos_ref, sin_ref, q_out, k_out, v_out):
    def rms(z, g):
        return z * lax.rsqrt(jnp.mean(z * z, axis=-1, keepdims=True) + RMS_EPS) * g

    cqn = rms(in_ref[:, 0:MLA_Q_LORA], qg_ref[...]).astype(BF16)
    ckvn = rms(in_ref[:, MLA_Q_LORA:MLA_Q_LORA + MLA_KV_LORA], kvg_ref[...]).astype(BF16)
    c3 = in_ref[:, 3 * LANES:4 * LANES]
    cos = cos_ref[...]
    sin = sin_ref[...]
    gm = gm_ref[...]
    half = MLA_ROPE // 2
    slot_lane = lax.broadcasted_iota(jnp.int32, (1, 2 * LANES), 1) & (LANES - 1)
    first_half = slot_lane < MLA_NOPE + half

    def norm_rope(z, gain):
        msq = _dot((z * z).astype(BF16), gm)
        z = z * lax.rsqrt(msq + RMS_EPS) * gain
        rot = jnp.where(first_half, -pltpu.roll(z, 2 * LANES - half, 1), pltpu.roll(z, half, 1))
        return z * cos + rot * sin

    kr = norm_rope(jnp.concatenate([c3, c3], axis=1), gkr_ref[...])
    for h in range(MLA_HEADS // 2):
        sl = slice(h * 2 * LANES, (h + 1) * 2 * LANES)
        q_out[:, sl] = norm_rope(_dot(cqn, wuq_ref[:, sl]), gq_ref[...]).astype(BF16)
        k_out[:, sl] = (norm_rope(_dot(ckvn, wuk_ref[:, sl]), gkn_ref[...]) + kr).astype(BF16)
    v_out[...] = _dot_nt(wuv_ref[...], ckvn).astype(BF16)


def _mla_prep(mla_in, qg, kvg, wuq, wuk, wuv, gm, gq, gkn, gkr, cos_t, sin_t, s):
    n = mla_in.shape[0]
    tpb = s // TM
    hw = MLA_HEADS * LANES
    vw = MLA_HEADS * MLA_V
    tab = pl.BlockSpec((TM, 2 * LANES), lambda i: (i % tpb, 0))
    consts = [qg, kvg, wuq, wuk, wuv, gm, gq, gkn, gkr]
    return pl.pallas_call(
        _mla_prep_kernel,
        grid=(n // TM,),
        in_specs=[pl.BlockSpec((TM, 4 * LANES), lambda i: (i, 0))] + [_resident(a.shape) for a in consts] + [tab, tab],
        out_specs=[pl.BlockSpec((TM, hw), lambda i: (i, 0)), pl.BlockSpec((TM, hw), lambda i: (i, 0)),
                   pl.BlockSpec((None, vw, TM), lambda i: (i // tpb, 0, i % tpb))],
        out_shape=[jax.ShapeDtypeStruct((n, hw), BF16), jax.ShapeDtypeStruct((n, hw), BF16),
                   jax.ShapeDtypeStruct((n // s, vw, s), BF16)],
        compiler_params=_params("parallel"),
        name="mla_prep",
    )(mla_in, *consts, cos_t, sin_t)


def _attn_kernel(cfg, q_ref, k_ref, vt_ref, tab_ref, *rest):
    n_tab, qw, backs, sel_cfg, want_lse, n_qt, stride = cfg
    n_qs = n_qt // stride
    if sel_cfg is not None:
        sel_ref, rest = rest[0], rest[1:]
        sel_bpt, sel_stride, sel_pair_mul = sel_cfg
    o_ref = rest[0]
    lse_ref = rest[1] if want_lse else None
    st_scr, p_scr = rest[-4:-2], rest[-2:]
    blk = pl.program_id(1)
    lane = lax.broadcasted_iota(jnp.int32, (1, LANES), 1)

    counts = tuple(sum(min(t % n_qs, bk) + 1 for t in range(n_qt)) for bk in backs)
    back = jnp.int32(backs[-1])
    n_tiles = jnp.int32(counts[-1])
    for bi in range(len(backs) - 2, -1, -1):
        back = jnp.where(blk == bi, jnp.int32(backs[bi]), back)
        n_tiles = jnp.where(blk == bi, jnp.int32(counts[bi]), n_tiles)

    ones_rows = jnp.ones((DEN_ROWS, TK), BF16)

    def first_key_tile(qi):
        return qi - jnp.minimum(qi & (n_qs - 1), back)

    def rows(t, size):
        return pl.ds(pl.multiple_of(t * size, size), size)

    def out_rows(t, size):
        if stride == 1:
            return rows(t, size)
        return pl.ds((t >> int(math.log2(n_qs))) + stride * size * (t & (n_qs - 1)), size, stride=stride)

    def logits_to(slot, qi, j):
        if qw == 1:
            q = q_ref[rows(qi, TQ), :]
            zero = jnp.zeros_like(q)
            qs = [jnp.where(lane < HEAD_DIM, q, zero), jnp.where(lane >= HEAD_DIM, q, zero)]
        else:
            qs = [q_ref[rows(qi, TQ), s * LANES:(s + 1) * LANES] for s in range(2)]
        for s in range(2):
            kj = k_ref[rows(j, TK), :] if qw == 1 else k_ref[rows(j, TK), s * LANES:(s + 1) * LANES]
            st = _dot_nt(kj, qs[s])
            if sel_cfg is not None:
                off = sel_stride * (s + sel_pair_mul * blk) + sel_bpt * j
                kb = TK // sel_bpt
                qcol = pl.ds(pl.multiple_of(qi * TQ, TQ), TQ)
                st = jnp.concatenate(
                    [st[bk * kb:(bk + 1) * kb] + sel_ref[pl.ds(off + bk, 1), qcol]
                     for bk in range(sel_bpt)], axis=0)
            st_scr[slot][s] = st

    def accumulate(slot, j, alphas, accs):
        krow = pl.multiple_of(j * TK, TK)
        new = []
        for s in range(2):
            vt = jnp.concatenate([vt_ref[s * HEAD_DIM:(s + 1) * HEAD_DIM, pl.ds(krow, TK)], ones_rows], axis=0)
            new.append(alphas[s] * accs[s] + _dot(vt, p_scr[slot][s]))
        return tuple(new)

    def finalize(qi, ms, accs):
        dens = [acc[HEAD_DIM:HEAD_DIM + 1] for acc in accs]
        out_t = jnp.concatenate([acc[:HEAD_DIM] / l for acc, l in zip(accs, dens)], axis=0)
        o_ref[out_rows(qi, TQ), :] = out_t.T.astype(o_ref.dtype)
        if want_lse:
            lse_t = jnp.concatenate([jnp.broadcast_to((m + jnp.log2(l)) * LN2, (HEAD_DIM, TQ))
                                     for m, l in zip(ms, dens)], axis=0)
            lse_ref[out_rows(qi, TQ), :] = lse_t.T

    def advance(q, j):
        last = j == q
        at_end = jnp.logical_and(last, q == n_qt - 1)
        starts = jnp.logical_and(last, jnp.logical_not(at_end))
        qn = jnp.where(starts, q + 1, q)
        jn = jnp.where(at_end, j, jnp.where(last, first_key_tile(q + 1), j + 1))
        return qn, jn, starts, at_end

    def softmax(slot, tile, stats):
        q, j, is_first, filler = tile
        keep = jnp.where(is_first, 0.0, 1.0)
        d = jnp.where(filler, n_tab, jnp.minimum(q - j, n_tab - 1))
        new_stats, alphas = [], []
        for s in range(2):
            m = jnp.where(is_first, NEG_INF, stats[s])
            st = st_scr[slot][s] + tab_ref[s, d]
            m_new = jnp.maximum(m, jnp.max(st, axis=0, keepdims=True))
            alphas.append(jnp.exp2(m - m_new) * keep)
            new_stats.append(m_new)
            p_scr[slot][s] = jnp.exp2(st - m_new).astype(BF16)
        return tuple(new_stats), tuple(alphas)

    def body(u, carry):
        tile_a, (q2, j2), (q1, j1, first1), stats2, stats1, alphas2, alphas1, accs = carry
        qa, ja, first_a, _ = tile_a
        tile_b = advance(qa, ja)
        qb, jb, first_b, _ = tile_b
        tile_c = advance(qb, jb)
        logits_to(1, qb, jb)
        accs_x = accumulate(0, j2, alphas2, accs)
        accs_y = accumulate(1, j1, alphas1, accs_x)
        stats_a, alphas_a = softmax(0, tile_a, stats1)
        logits_to(0, tile_c[0], tile_c[1])
        stats_b, alphas_b = softmax(1, tile_b, stats_a)

        @pl.when(jnp.logical_and(first1, u > 0))
        def _():
            finalize(q2, stats2, accs_x)

        @pl.when(jnp.logical_and(first_a, u > 0))
        def _():
            finalize(q1, stats1, accs_y)

        return tile_c, (qa, ja), (qb, jb, first_b), stats_a, stats_b, alphas_a, alphas_b, accs_y

    zero_i = jnp.int32(0)
    logits_to(0, zero_i, zero_i)
    for p_slot in p_scr:
        p_slot[...] = jnp.zeros(p_slot.shape, BF16)
    stats0 = tuple(jnp.full((1, TQ), NEG_INF, F32) for _ in range(2))
    ones = tuple(jnp.ones((1, TQ), F32) for _ in range(2))
    init = ((zero_i, zero_i, zero_i == 0, zero_i != 0), (zero_i, zero_i), (zero_i, zero_i, zero_i != 0),
            stats0, stats0, ones, ones, tuple(jnp.zeros((HEAD_DIM + DEN_ROWS, TQ), F32) for _ in range(2)))
    _, (q2, j2), (q1, j1, first1), stats2, stats1, alphas2, alphas1, accs = lax.fori_loop(
        0, (n_tiles + 1) // 2, body, init)
    accs_x = accumulate(0, j2, alphas2, accs)

    @pl.when(first1)
    def _():
        finalize(q2, stats2, accs_x)

    finalize(q1, stats1, accumulate(1, j1, alphas1, accs_x))


def _attn(q, k, vt, tab, sel, *, qc0, kc0, vb0, n_blk, qw, backs=None, kv_shared=False,
          tab_shared=False, sel_cfg=None, out_dtype=F32, want_lse=False, stride=1):
    b, s = q.shape[:2]
    n_tab = tab.shape[2] - 1
    n_qt = s // TQ
    if backs is None:
        backs = (n_qt,)
    cfg = (n_tab, qw, backs, sel_cfg, want_lse, n_qt, stride)
    kidx = (lambda bb, h: (bb, 0, kc0)) if kv_shared else (lambda bb, h: (bb, 0, kc0 + h))
    vidx = (lambda bb, h: (bb, vb0, 0)) if kv_shared else (lambda bb, h: (bb, vb0 + h, 0))
    tidx = (lambda bb, h: (0, 0, 0, 0, 0)) if tab_shared else (lambda bb, h: (h, 0, 0, 0, 0))
    in_specs = [pl.BlockSpec((None, s, qw * LANES), lambda bb, h: (bb, 0, qc0 + h)),
                pl.BlockSpec((None, s, qw * LANES), kidx),
                pl.BlockSpec((None, 2 * HEAD_DIM, s), vidx),
                pl.BlockSpec((None, 2, n_tab + 1, TK, TQ), tidx)]
    args = [q, k, vt, tab]
    if sel_cfg is not None:
        in_specs.append(pl.BlockSpec((None, LANES, s), lambda bb, h: (bb, 0, 0)))
        args.append(sel)
    ospec = pl.BlockSpec((None, s, LANES), lambda bb, h: (bb, 0, h))
    out_specs = [ospec]
    out_shape = [jax.ShapeDtypeStruct((b, s, n_blk * LANES), out_dtype)]
    if want_lse:
        out_specs.append(ospec)
        out_shape.append(jax.ShapeDtypeStruct((b, s, n_blk * LANES), F32))
    return pl.pallas_call(
        functools.partial(_attn_kernel, cfg),
        grid=(b, n_blk),
        in_specs=in_specs,
        out_specs=out_specs,
        out_shape=out_shape,
        scratch_shapes=[pltpu.VMEM((2, TK, TQ), F32)] * 2 + [pltpu.VMEM((2, TK, TQ), BF16)] * 2,
        compiler_params=_params("parallel", "parallel"),
        name="attn",
    )(*args)


def _nsa_cmp_kernel(kc_ref, vc_ref, pe_ref, wlo_ref, whi_ref, w2_ref, gm_ref, gain_ref, kcmp_ref, vcmpt_ref):
    nch = kcmp_ref.shape[0]

    def hidden(c_ref, j):
        lo = hi = None
        for l in range(NSA_CMP_STRIDE):
            rows = c_ref[pl.ds(l, nch, stride=NSA_CMP_STRIDE), :]
            sl = slice(l * LANES, (l + 1) * LANES)
            t_lo = _dot((rows + pe_ref[j, 0][:, sl]).astype(BF16), wlo_ref[j][sl, :])
            t_hi = _dot((rows + pe_ref[j, 1][:, sl]).astype(BF16), whi_ref[j][sl, :])
            lo = t_lo if lo is None else lo + t_lo
            hi = t_hi if hi is None else hi + t_hi
        h = lo + pltpu.roll(hi, nch - 1, 0)
        return (h * _sigmoid(h)).astype(BF16)

    kz = _dot_nt(hidden(kc_ref, 0), w2_ref[0])
    msq = _dot_hilo(kz * kz, gm_ref[...])
    kcmp_ref[...] = (kz * lax.rsqrt(msq + RMS_EPS) * gain_ref[...]).astype(BF16)
    vcmpt_ref[...] = _dot_nt(w2_ref[1], hidden(vc_ref, 1)).astype(BF16)


def _nsa_cmp(kc3, vc3, pe, wlo, whi, w2, gm, gain):
    b, s, width = kc3.shape
    nch = s // NSA_CMP_STRIDE
    consts = [pe, wlo, whi, w2, gm, gain]
    blk = pl.BlockSpec((None, s, width), lambda i: (i, 0, 0))
    oblk = pl.BlockSpec((None, nch, LANES), lambda i: (i, 0, 0))
    return pl.pallas_call(
        _nsa_cmp_kernel,
        grid=(b,),
        in_specs=[blk, blk] + [_resident(a.shape) for a in consts],
        out_specs=[oblk, oblk],
        out_shape=[jax.ShapeDtypeStruct((b, nch, LANES), BF16)] * 2,
        compiler_params=_params("parallel"),
        name="nsa_cmp",
    )(kc3, vc3, *consts)


def _rank_keep(score, ids, top):
    cnt = jnp.zeros(score.shape, jnp.int32)
    for mp in range(score.shape[0]):
        other = score[mp:mp + 1, :]
        tie = jnp.where(mp < ids, 1, 0)
        cnt = cnt + jnp.where(other > score, 1, jnp.where(other == score, tie, 0))
    return cnt < top


def _nsa_sel_kernel(q_ref, kcmp_ref, vcmpt_ref, bias_ref, ovl_ref, oc_ref, sel_ref):
    qi = pl.program_id(1)
    lane = lax.broadcasted_iota(jnp.int32, (1, LANES), 1)
    row = lax.broadcasted_iota(jnp.int32, (LANES, 1), 0)
    t = qi * TQ + lax.broadcasted_iota(jnp.int32, (1, TQ), 1)
    mask_c = (NSA_CMP_STRIDE * row + NSA_CMP_LEN - 1) <= t
    kcmp = kcmp_ref[...]
    heads = [(p, g) for p in range(NSA_HPG) for g in range(NSA_GROUPS)]
    raw = []
    for p, g in heads:
        qp = q_ref[:, p * LANES:(p + 1) * LANES]
        raw.append(_dot_nt(kcmp, jnp.where((lane >> 6) == g, qp, jnp.zeros_like(qp))))
    pcs = []
    for (p, g), r in zip(heads, raw):
        lg = jnp.where(mask_c, r * LN2 + bias_ref[p * NSA_GROUPS + g], NEG_INF)
        m = jnp.max(lg, axis=0, keepdims=True)
        e = jnp.where(mask_c, jnp.exp(lg - m), 0.0)
        den = jnp.maximum(jnp.sum(e, axis=0, keepdims=True), 1e-30)
        pcs.append(e / den)
    imp = jnp.zeros((LANES, TQ), F32)
    ocs = []
    for (p, g), pc in zip(heads, pcs):
        hi, lo = _split(pc)
        ocs.append(_dot(vcmpt_ref[g * HEAD_DIM:(g + 1) * HEAD_DIM, :], hi))
        imp = imp + _dot(ovl_ref[g], hi) + _dot(ovl_ref[g], lo)
    for p in range(NSA_HPG):
        oc_ref[:, p * LANES:(p + 1) * LANES] = jnp.concatenate([ocs[2 * p], ocs[2 * p + 1]], axis=0).T.astype(BF16)

    n_slc = 32
    ids = row[0:n_slc]
    cur = t >> 6
    forced = (ids == 0) | (ids == cur) | (ids == cur - 1)
    masks = []
    for g in range(NSA_GROUPS):
        score = jnp.where(forced, NSA_FORCED, jnp.where(ids <= cur, imp[g * n_slc:(g + 1) * n_slc], NEG_INF))
        keep = _rank_keep(score, ids, NSA_SLC_TOP) & (score > 0.5 * NEG_INF)
        masks.append(jnp.where(keep, 0.0, NEG_INF))
    masks.append(jnp.full((LANES - NSA_GROUPS * n_slc, TQ), NEG_INF, F32))
    sel_ref[...] = jnp.concatenate(masks, axis=0)


def _nsa_sel(q, kcmp, vcmpt, bias_c, ovl):
    b, s, w = q.shape
    return pl.pallas_call(
        _nsa_sel_kernel,
        grid=(b, s // TQ),
        in_specs=[pl.BlockSpec((None, TQ, w), lambda bb, i: (bb, i, 0)),
                  pl.BlockSpec((None, LANES, LANES), lambda bb, i: (bb, 0, 0)),
                  pl.BlockSpec((None, LANES, LANES), lambda bb, i: (bb, 0, 0)),
                  pl.BlockSpec((NSA_HEADS, LANES, TQ), lambda bb, i: (0, 0, i)),
                  _resident(ovl.shape)],
        out_specs=[pl.BlockSpec((None, TQ, w), lambda bb, i: (bb, i, 0)),
                   pl.BlockSpec((None, LANES, TQ), lambda bb, i: (bb, 0, i))],
        out_shape=[jax.ShapeDtypeStruct((b, s, w), BF16), jax.ShapeDtypeStruct((b, LANES, s), F32)],
        compiler_params=_params("parallel", "parallel"),
        name="nsa_sel",
    )(q, kcmp, vcmpt, bias_c, ovl)


def _moba_gate_kernel(q_ref, k_ref, avg_ref, sel_ref):
    qi = pl.program_id(1)
    nb = 8
    kmean = _dot(avg_ref[...], k_ref[...])
    r2 = lax.broadcasted_iota(jnp.int32, kmean.shape, 0)
    c2 = lax.broadcasted_iota(jnp.int32, kmean.shape, 1)
    kmean = jnp.where((r2 >> 3) == (c2 >> 6), kmean, 0.0)
    kh, kl = _split(kmean)
    q = q_ref[...]
    gate = _dot_nt(kh, q) + _dot_nt(kl, q)
    ids = lax.broadcasted_iota(jnp.int32, (nb, 1), 0)
    past = ids < qi
    masks = []
    for h in range(MOBA_HEADS):
        score = jnp.where(past, gate[h * nb:(h + 1) * nb], NEG_INF)
        keep = (_rank_keep(score, ids, MOBA_TOP) & past) | (ids == qi)
        masks.append(jnp.where(keep, 0.0, NEG_INF))
    masks.append(jnp.full((LANES - MOBA_HEADS * nb, TQ), NEG_INF, F32))
    sel_ref[...] = jnp.concatenate(masks, axis=0)


def _moba_gate(p_arr, avg, qc, kc):
    b, s, _ = p_arr.shape
    w = MOBA_HEADS * HEAD_DIM
    return pl.pallas_call(
        _moba_gate_kernel,
        grid=(b, s // TQ),
        in_specs=[pl.BlockSpec((None, TQ, w), lambda bb, i: (bb, i, qc)),
                  pl.BlockSpec((None, s, w), lambda bb, i: (bb, 0, kc)),
                  _resident(avg.shape)],
        out_specs=pl.BlockSpec((None, LANES, TQ), lambda bb, i: (bb, 0, i)),
        out_shape=jax.ShapeDtypeStruct((b, LANES, s), F32),
        compiler_params=_params("parallel", "parallel"),
        name="moba_gate",
    )(p_arr, p_arr, avg)


def _even_out_kernel(x_ref, oa_ref, oc_ref, os_ref, ow_ref, gl_ref, eg_ref, wa_ref, wb_ref, gt_ref, o_ref):
    sg = _sigmoid(gl_ref[...])
    hi, lo = _split(sg)
    nsa = None
    for br, src in enumerate((oc_ref, os_ref, ow_ref)):
        gexp = _dot(hi, eg_ref[br]) + _dot(lo, eg_ref[br])
        term = gexp * src[...]
        nsa = term if nsa is None else nsa + term
    m = _dot(oa_ref[...], wa_ref[...]) + _dot(nsa.astype(BF16), wb_ref[...])
    o_ref[...] = x_ref[...] + gt_ref[...] * m


def _even_out(x, o_a, o_c, o_s, o_w, mla_in, eg, wa, wb, gate, s):
    n, d = x.shape
    tpb = s // TM
    row = lambda wd: pl.BlockSpec((TM, wd), lambda i: (i, 0))
    return pl.pallas_call(
        _even_out_kernel,
        grid=(n // TM,),
        in_specs=[row(d), row(o_a.shape[1]), row(o_c.shape[1]), row(o_s.shape[1]), row(o_w.shape[1]),
                  pl.BlockSpec((TM, LANES), lambda i: (i, 3)),
                  _resident(eg.shape), _resident(wa.shape), _resident(wb.shape), _mod_spec(tpb, d)],
        out_specs=row(d),
        out_shape=jax.ShapeDtypeStruct((n, d), F32),
        compiler_params=_params("parallel"),
        name="even_out",
    )(x, o_a, o_c, o_s, o_w, mla_in, eg, wa, wb, gate)


def _odd_out_kernel(x_ref, od0_ref, od1_ref, od2_ref, ls0_ref, ls1_ref, ls2_ref, om_ref, wd_ref, wm_ref, gt_ref, o_ref):
    ods = (od0_ref, od1_ref, od2_ref)
    ls = [r[...] for r in (ls0_ref, ls1_ref, ls2_ref)]
    mx = jnp.maximum(jnp.maximum(ls[0], ls[1]), ls[2])
    es = [jnp.exp(l - mx) for l in ls]
    den = es[0] + es[1] + es[2]
    merged = None
    for g in range(len(DIL_PAIRS)):
        term = (es[g] / den) * ods[g][...]
        merged = term if merged is None else merged + term
    m = _dot(merged.astype(BF16), wd_ref[...]) + _dot(om_ref[...], wm_ref[...])
    o_ref[...] = x_ref[...] + gt_ref[...] * m


def _odd_out(x, o_ds, lse_ds, o_m, wd, wm, gate, s):
    n, d = x.shape
    tpb = s // TM
    row = lambda wd_: pl.BlockSpec((TM, wd_), lambda i: (i, 0))
    return pl.pallas_call(
        _odd_out_kernel,
        grid=(n // TM,),
        in_specs=[row(d)] + [row(a.shape[1]) for a in (*o_ds, *lse_ds)] + [row(o_m.shape[1]),
                  _resident(wd.shape), _resident(wm.shape), _mod_spec(tpb, d)],
        out_specs=row(d),
        out_shape=jax.ShapeDtypeStruct((n, d), F32),
        compiler_params=_params("parallel"),
        name="odd_out",
    )(x, *o_ds, *lse_ds, o_m, wd, wm, gate)


def _t5_bucket(dist):
    n = jnp.maximum(jnp.asarray(dist, jnp.int32), 0)
    nf = jnp.maximum(n, 1).astype(F32)
    large = T5_MAX_EXACT + (jnp.log(nf / T5_MAX_EXACT) / math.log(T5_MAX_DIST / T5_MAX_EXACT)
                            * (NUM_BUCKETS - T5_MAX_EXACT)).astype(jnp.int32)
    return jnp.where(n < T5_MAX_EXACT, n, jnp.minimum(large, NUM_BUCKETS - 1))


TOEP_PERIOD = 4 * TQ


def _toeplitz_dist():
    j = np.arange(TOEP_PERIOD)
    return np.where(j < 3 * TQ, j, j - TOEP_PERIOD)


def _toeplitz_kernel(n_tab, u_ref, o_ref):
    x = jnp.broadcast_to(u_ref[...], (TK, TOEP_PERIOD))
    y = pltpu.roll(x, 0, 1, stride=1, stride_axis=0)
    for dlt in range(n_tab):
        o_ref[dlt] = y[:, dlt * TQ:(dlt + 1) * TQ]
    o_ref[n_tab] = jnp.full((TK, TQ), NEG_INF * LOG2E, F32)


def _toeplitz_tiles(u, n_tab):
    h = u.shape[0]
    return pl.pallas_call(
        functools.partial(_toeplitz_kernel, n_tab),
        grid=(h,),
        in_specs=[pl.BlockSpec((None, 1, TOEP_PERIOD), lambda i: (i, 0, 0))],
        out_specs=pl.BlockSpec((None, n_tab + 1, TK, TQ), lambda i: (i, 0, 0, 0)),
        out_shape=jax.ShapeDtypeStruct((h, n_tab + 1, TK, TQ), F32),
        compiler_params=_params("parallel"),
        name="toeplitz",
    )(u.reshape(h, 1, TOEP_PERIOD))


def _cmp_bias_kernel(u_ref, o_ref):
    x = jnp.broadcast_to(u_ref[...], o_ref.shape)
    o_ref[...] = pltpu.roll(x, 0, 1, stride=NSA_CMP_STRIDE, stride_axis=0)


def _cmp_bias(u, n_rows):
    h, s = u.shape
    return pl.pallas_call(
        _cmp_bias_kernel,
        grid=(h,),
        in_specs=[pl.BlockSpec((None, 1, s), lambda i: (i, 0, 0))],
        out_specs=pl.BlockSpec((None, n_rows, s), lambda i: (i, 0, 0)),
        out_shape=jax.ShapeDtypeStruct((h, n_rows, s), F32),
        compiler_params=_params("parallel"),
        name="cmp_bias",
    )(u.reshape(h, 1, s))


def _bias_tiles(t5_cols, ok, n_tab=3, dist_scale=1):
    dist = _toeplitz_dist()
    bias = jnp.transpose(t5_cols[_t5_bucket(dist * dist_scale)])
    u = jnp.where(jnp.asarray(ok)[None], bias, NEG_INF) * LOG2E
    return _toeplitz_tiles(u, n_tab)


def _pair(tiles):
    h = tiles.shape[0]
    return tiles.reshape(h // 2, 2, *tiles.shape[1:])


def _group_mean_np(sizes, width=LANES):
    gm = np.zeros((width, width), np.float32)
    o = 0
    for sz in sizes:
        gm[o:o + sz, o:o + sz] = 1.0 / sz
        o += sz
    return gm


def _group_mean_matrix(sizes, width=LANES):
    return jnp.asarray(_group_mean_np(sizes, width), BF16)


EV_META = ((0, 0, False), (0, 128, False), (0, 256, False), (0, 384, False),
           (1, 0, True), (1, 128, True), (1, 256, True), (1, 384, True),
           (2, 0, False), (3, 0, False), (4, 0, True), (5, 0, False), (6, 0, True), (7, 0, False))
EV_OUTS = [(512, F32, "rows"), (512, BF16, "rows"), (128, F32, "rows"), (128, F32, "rows"), (128, BF16, "rows"),
           (128, BF16, "cols"), (128, BF16, "rows"), (128, BF16, "cols")]
OD_DIL_STRIDES = (1, 4, 8)
OD_META = (((0, 0, True), (0, LANES, True)) + tuple((3 + g, h * LANES, True) for g in range(2) for h in range(2))
           + ((0, 2 * LANES, True), (0, 3 * LANES, True))
           + tuple((3 + g, (2 + h) * LANES, True) for g in range(2) for h in range(2))
           + ((1, 0, False), (1, LANES, False)) + tuple((0, (4 + c) * LANES, False) for c in range(4))
           + tuple((0, (8 + c) * LANES, True) for c in range(4)) + ((2, 0, False), (2, LANES, False)))
OD_OUTS = [(12 * LANES, BF16, "rows"), (2 * LANES, BF16, "cols"), (2 * LANES, BF16, "cols"),
           (4 * LANES, BF16, OD_DIL_STRIDES[1]), (4 * LANES, BF16, OD_DIL_STRIDES[2])]


def _even_w_in(w):
    jn, d, _ = w.shape
    z = lambda n_: jnp.zeros((jn, d, n_), w.dtype)
    nq = w[:, :, 416:928].reshape(jn, d, NSA_GROUPS, NSA_HPG, HEAD_DIM)
    nq = jnp.transpose(nq, (0, 1, 3, 2, 4)).reshape(jn, d, NSA_HEADS * HEAD_DIM)
    chunk3 = jnp.concatenate([w[:, :, 1696:1720], z(HEAD_DIM - 24), w[:, :, 384:416], z(LANES - 96)], axis=-1)
    return jnp.concatenate([w[:, :, 0:384], chunk3, nq, w[:, :, 928:1696]], axis=-1)


def kernel(x, c, t5_bias, ada_w, ada_b, norm_g, ffn_w_in, ffn_w_out, ev_w_in, ev_w_out, mla_q_norm_g,
           mla_kv_norm_g, mla_w_uq, mla_w_ukv, mla_qk_g, nsa_cmp_pe, nsa_cmp_w1, nsa_cmp_w2, nsa_qk_g,
           od_w_in, od_w_out, dil_qk_g, moba_qk_g):
    b, s, d = x.shape
    assert (s, d) == (2048, D_MODEL) and s % TM == 0 and TQ == MOBA_BLOCK and TQ == TK
    n = b * s
    hd = HEAD_DIM
    n_even = ev_w_in.shape[0]
    n_odd = od_w_in.shape[0]
    c64 = hd ** -0.5 * LOG2E
    c96 = (MLA_NOPE + MLA_ROPE) ** -0.5 * LOG2E

    mod = _ada(c, ada_w, ada_b).reshape(DEPTH, b, 3, 3, 1, d)

    dist = _toeplitz_dist()
    causal = dist >= 0
    gm64 = _group_mean_matrix((hd, hd))
    gm_proj = _group_mean_matrix((hd,) * (PROJ_CHUNK // hd), PROJ_CHUNK)
    per_chunk = PROJ_CHUNK // LANES
    chunked = lambda m: tuple(tuple(m[i:i + per_chunk]) for i in range(0, len(m), per_chunk))

    padc = FF_PAD - D_FF
    wa_all = jnp.pad(ffn_w_in[..., :D_FF], ((0, 0), (0, 0), (0, 0), (0, padc))).astype(BF16)
    wb_all = jnp.pad(ffn_w_in[..., D_FF:], ((0, 0), (0, 0), (0, 0), (0, padc))).astype(BF16)
    wo_all = jnp.pad(ffn_w_out, ((0, 0), (0, 0), (0, padc), (0, 0))).astype(BF16)

    nsa_tab = t5_bias[:, MLA_HEADS:MLA_HEADS + NSA_HEADS].reshape(NUM_BUCKETS, NSA_GROUPS, NSA_HPG)
    nsa_cols = jnp.transpose(nsa_tab, (0, 2, 1)).reshape(NUM_BUCKETS, NSA_HEADS)
    tab_sel = _pair(_bias_tiles(nsa_cols, causal))
    tab_win = _pair(_bias_tiles(nsa_cols, causal & (dist <= NSA_WINDOW - 1)))
    tab_mla = _toeplitz_tiles(jnp.where(jnp.asarray(causal), 0.0, NEG_INF).astype(F32)[None], 2)
    tab_mla = jnp.broadcast_to(tab_mla[None], (1, 2) + tab_mla.shape[1:])
    n_cmp_pad = s // NSA_CMP_STRIDE
    bias_c = _cmp_bias(jnp.transpose(nsa_cols[_t5_bucket(np.arange(s) - (NSA_CMP_LEN - 1))]), n_cmp_pad)
    n_cmp = (s - NSA_CMP_LEN) // NSA_CMP_STRIDE + 1
    cstart = np.arange(n_cmp) * NSA_CMP_STRIDE
    sstart = np.arange(s // NSA_SLC_BLOCK) * NSA_SLC_BLOCK
    overlap = np.clip(np.minimum(cstart[:, None] + NSA_CMP_LEN, sstart[None, :] + NSA_SLC_BLOCK)
                      - np.maximum(cstart[:, None], sstart[None, :]), 0, None).astype(np.float32) / NSA_CMP_LEN
    ovl = np.zeros((NSA_GROUPS, LANES, LANES), np.float32)
    for g in range(NSA_GROUPS):
        ovl[g, 32 * g:32 * g + 32, :n_cmp] = overlap.T
    ovl = jnp.asarray(ovl, BF16)
    eg = np.zeros((3, LANES, NSA_HEADS * hd), np.float32)
    for g in range(NSA_GROUPS):
        for p in range(NSA_HPG):
            for br in range(3):
                eg[br, (g * NSA_HPG + p) * 3 + br, p * LANES + g * hd:p * LANES + (g + 1) * hd] = 1.0
    eg = jnp.asarray(eg, BF16)
    gm_mla = jnp.asarray(np.kron(np.eye(2, dtype=np.float32), _group_mean_np((MLA_NOPE, MLA_ROPE))), BF16)
    inv = ROPE_THETA ** (-jnp.arange(0, MLA_ROPE, 2, dtype=F32) / MLA_ROPE)
    ang = jnp.arange(s, dtype=F32)[:, None] * inv[None, :]
    ones = jnp.ones((s, MLA_NOPE), F32)
    tail = LANES - MLA_NOPE - MLA_ROPE
    cos_t = jnp.tile(jnp.concatenate([ones, jnp.cos(ang), jnp.cos(ang), jnp.ones((s, tail), F32)], axis=1), (1, 2))
    sin_t = jnp.tile(jnp.concatenate([0 * ones, jnp.sin(ang), jnp.sin(ang), jnp.zeros((s, tail), F32)], axis=1), (1, 2))

    ev_w = _even_w_in(ev_w_in).astype(BF16)
    ev_gain = jnp.ones((n_even, 14 * LANES), F32)
    ev_gain = ev_gain.at[:, 512:1024].set(jnp.tile(nsa_qk_g[:, 0], (1, 8)) * c64)
    ev_gain = ev_gain.at[:, 1280:1408].set(jnp.tile(nsa_qk_g[:, 1], (1, 2)))
    ev_gain = ev_gain.at[:, 1536:1664].set(jnp.tile(nsa_qk_g[:, 1], (1, 2)))
    gain_kc = jnp.tile(nsa_qk_g[:, 1], (1, 2))
    wuq = jnp.pad(mla_w_uq.reshape(n_even, MLA_Q_LORA, MLA_HEADS, MLA_NOPE + MLA_ROPE),
                  ((0, 0), (0, 0), (0, 0), (0, tail))).reshape(n_even, MLA_Q_LORA, MLA_HEADS * LANES).astype(BF16)
    ukv = mla_w_ukv.reshape(n_even, MLA_KV_LORA, MLA_HEADS, MLA_NOPE + MLA_V)
    wuk = jnp.pad(ukv[..., :MLA_NOPE], ((0, 0), (0, 0), (0, 0), (0, LANES - MLA_NOPE))
                  ).reshape(n_even, MLA_KV_LORA, MLA_HEADS * LANES).astype(BF16)
    wuv = jnp.swapaxes(ukv[..., MLA_NOPE:].reshape(n_even, MLA_KV_LORA, MLA_HEADS * MLA_V), 1, 2).astype(BF16)
    zt = jnp.zeros((n_even, tail), F32)
    gq = jnp.tile(jnp.concatenate([mla_qk_g[:, 0] * c96, zt], axis=1), (1, 2))
    gkn = jnp.tile(jnp.concatenate([mla_qk_g[:, 1, :MLA_NOPE], jnp.zeros((n_even, LANES - MLA_NOPE), F32)], axis=1), (1, 2))
    gkr = jnp.tile(jnp.concatenate([jnp.zeros((n_even, MLA_NOPE), F32), mla_qk_g[:, 1, MLA_NOPE:], zt], axis=1), (1, 2))
    pe2 = jnp.broadcast_to(nsa_cmp_pe.reshape(n_even, 2, 2, 16, 1, hd), (n_even, 2, 2, 16, NSA_GROUPS, hd)
                           ).reshape(n_even, 2, 2, 1, 16 * LANES)
    eye = jnp.eye(NSA_GROUPS, dtype=F32)
    w1 = nsa_cmp_w1.reshape(n_even, 2, 2, 16, hd, NSA_CMP_HID)
    w1x = jnp.einsum('ijaldc,gh->ijalgdhc', w1, eye).reshape(n_even, 2, 2, 16 * LANES, NSA_GROUPS * NSA_CMP_HID).astype(BF16)
    w2x = jnp.einsum('ijcd,gh->ijhdgc', nsa_cmp_w2, eye).reshape(n_even, 2, LANES, NSA_GROUPS * NSA_CMP_HID).astype(BF16)
    wa_o = ev_w_out[:, :MLA_HEADS * MLA_V].astype(BF16)
    wb_o = jnp.transpose(ev_w_out[:, MLA_HEADS * MLA_V:].reshape(n_even, NSA_GROUPS, NSA_HPG, hd, d),
                         (0, 2, 1, 3, 4)).reshape(n_even, NSA_HEADS * hd, d).astype(BF16)

    assert DIL_PAIRS == ((128, 1), (512, 4), (2048, 16))
    dil_cfg = tuple(zip(OD_DIL_STRIDES, (1, 1, 0)))
    dil_ok = (causal & (dist <= 128), causal & (dist <= 128), causal & (dist % 2 == 0))
    tab_dil = [_pair(_bias_tiles(t5_bias[:, gi * DIL_HPG:(gi + 1) * DIL_HPG], dil_ok[gi], dist_scale=dil_cfg[gi][0]))
               for gi in range(len(DIL_PAIRS))]
    tab_moba = _pair(_bias_tiles(t5_bias[:, DIL_SLOTS:DIL_SLOTS + MOBA_HEADS], causal))
    avg = np.zeros((LANES, s), np.float32)
    for h in range(MOBA_HEADS):
        for m in range(s // MOBA_BLOCK):
            avg[8 * h + m, m * MOBA_BLOCK:(m + 1) * MOBA_BLOCK] = 1.0 / MOBA_BLOCK
    avg = jnp.asarray(avg, BF16)
    od_w = od_w_in.astype(BF16)
    od_gain = jnp.concatenate([jnp.tile(dil_qk_g[:, 0], (1, 12)) * c64, jnp.tile(dil_qk_g[:, 1], (1, 12)),
                               jnp.ones((n_odd, 768), F32), jnp.tile(moba_qk_g[:, 0], (1, 4)) * c64,
                               jnp.tile(moba_qk_g[:, 1], (1, 4)), jnp.ones((n_odd, 256), F32)], axis=1)
    wd_o = od_w_out[:, :DIL_HPG * hd].astype(BF16)
    wm_o = od_w_out[:, DIL_HPG * hd:].astype(BF16)

    sh3 = lambda a: a.reshape(b, s, a.shape[-1])
    tr3 = lambda a: jnp.swapaxes(sh3(a), 1, 2)
    xf = x.reshape(n, d)
    for i in range(DEPTH):
        j = i // 2
        g_i = norm_g[i].reshape(3, 1, d)
        xf = _ffn(xf, g_i[0], mod[i, :, 0, 0], mod[i, :, 0, 1], mod[i, :, 0, 2],
                  wa_all[i, 0], wb_all[i, 0], wo_all[i, 0], s)
        if i % 2 == 0:
            mla_in, nsa_q, kc, vc, ks, vs, kw, vw = _proj(
                xf, g_i[1], mod[i, :, 1, 0], mod[i, :, 1, 1], ev_w[j], ev_gain[j][None], gm_proj,
                chunked(EV_META), EV_OUTS, s)
            qf, kf, vf = _mla_prep(mla_in, mla_q_norm_g[j][None], mla_kv_norm_g[j][None], wuq[j], wuk[j], wuv[j],
                                   gm_mla, gq[j][None], gkn[j][None], gkr[j][None], cos_t, sin_t, s)
            (o_a,) = _attn(sh3(qf), sh3(kf), vf, tab_mla, None, qc0=0, kc0=0, vb0=0, n_blk=MLA_HEADS // 2, qw=2,
                           tab_shared=True, out_dtype=BF16)
            kcmp, vcmpt = _nsa_cmp(sh3(kc), sh3(vc), pe2[j], w1x[j, :, 0], w1x[j, :, 1], w2x[j], gm64, gain_kc[j][None])
            o_c, sel = _nsa_sel(sh3(nsa_q), kcmp, vcmpt, bias_c, ovl)
            (o_s,) = _attn(sh3(nsa_q), sh3(ks), vs, tab_sel, sel, qc0=0, kc0=0, vb0=0, n_blk=NSA_HPG, qw=1,
                           kv_shared=True, sel_cfg=(TK // NSA_SLC_BLOCK, 32, 0), out_dtype=BF16)
            (o_w,) = _attn(sh3(nsa_q), sh3(kw), vw, tab_win, None, qc0=0, kc0=0, vb0=0, n_blk=NSA_HPG, qw=1,
                           kv_shared=True, backs=(2, 2, 2, 2), out_dtype=BF16)
            xf = _even_out(xf, o_a.reshape(n, -1), o_c.reshape(n, -1), o_s.reshape(n, -1), o_w.reshape(n, -1),
                           mla_in, eg, wa_o[j], wb_o[j], mod[i, :, 1, 2], s)
        else:
            pr, vd0t, vmt, qk1, qk2 = _proj(xf, g_i[1], mod[i, :, 1, 0], mod[i, :, 1, 1], od_w[j], od_gain[j][None],
                                            gm_proj, chunked(OD_META), OD_OUTS, s)
            pr3 = pr.reshape(b, s, pr.shape[-1])
            o_ds, lse_ds = [], []
            for gi, (r, bk) in enumerate(dil_cfg):
                if r == 1:
                    qk, vdt = pr3, vd0t
                else:
                    v = pr3[:, :, (2 + 2 * gi) * LANES:(4 + 2 * gi) * LANES].reshape(b, s // r, r, 2 * LANES)
                    qk, vdt = (qk1, qk2)[gi - 1], jnp.transpose(v, (0, 3, 2, 1)).reshape(b, 2 * LANES, s)
                o_g, lse_g = _attn(qk, qk, vdt, tab_dil[gi], None, qc0=0, kc0=2, vb0=0, n_blk=2,
                                   qw=1, backs=(bk,), want_lse=True, stride=r)
                o_ds.append(o_g.reshape(n, -1))
                lse_ds.append(lse_g.reshape(n, -1))
            selm = _moba_gate(pr3, avg, 4, 5)
            (o_m,) = _attn(pr3, pr3, vmt, tab_moba, selm, qc0=8, kc0=10, vb0=0, n_blk=2, qw=1,
                           sel_cfg=(1, 8, 2), out_dtype=BF16)
            xf = _odd_out(xf, o_ds, lse_ds, o_m.reshape(n, -1), wd_o[j], wm_o[j], mod[i, :, 1, 2], s)
        xf = _ffn(xf, g_i[2], mod[i, :, 2, 0], mod[i, :, 2, 1], mod[i, :, 2, 2],
                  wa_all[i, 1], wb_all[i, 1], wo_all[i, 1], s)
    return xf.reshape(b, s, d)
```

```python
import functools
import math

import numpy as np
import jax
import jax.numpy as jnp
from jax import lax
from jax.experimental import pallas as pl
from jax.experimental.pallas import tpu as pltpu

F32 = jnp.float32
BF16 = jnp.bfloat16

D_MODEL = 1024
DEPTH = 4
D_FF = 2752
HEAD_DIM = 64
NUM_BUCKETS = 32
T5_MAX_EXACT = 16
T5_MAX_DIST = 128
RMS_EPS = 1e-6
NEG_INF = -1e30
MLA_HEADS = 8
MLA_NOPE = 64
MLA_ROPE = 32
MLA_V = 64
MLA_Q_LORA = 256
MLA_KV_LORA = 128
ROPE_THETA = 10000.0
NSA_HEADS = 8
NSA_GROUPS = 2
NSA_HPG = 4
NSA_CMP_LEN = 32
NSA_CMP_STRIDE = 16
NSA_CMP_HID = 256
NSA_SLC_BLOCK = 64
NSA_SLC_TOP = 8
NSA_WINDOW = 512
NSA_FORCED = 1e6
DIL_PAIRS = ((128, 1), (512, 4), (2048, 16))
DIL_HPG = 4
DIL_SLOTS = len(DIL_PAIRS) * DIL_HPG
MOBA_HEADS = 4
MOBA_BLOCK = 256
MOBA_TOP = 3

LANES = 128
V7X_VMEM_BYTES = 64 * 1024 * 1024
VMEM_LIMIT = V7X_VMEM_BYTES * 7 // 8
TM = 512
TQ = 256
TK = 256
FF_CHUNK = 256
FF_PAD = -(-D_FF // FF_CHUNK) * FF_CHUNK
PROJ_CHUNK = 256
DEN_ROWS = 16
LOG2E = math.log2(math.e)
LN2 = math.log(2.0)


def _dot(a, b):
    return jnp.dot(a, b, preferred_element_type=F32)


def _dot_nt(a, b):
    return lax.dot_general(a, b, (((1,), (1,)), ((), ())), preferred_element_type=F32)


def _split(a):
    hi = a.astype(BF16)
    lo = (a - hi.astype(F32)).astype(BF16)
    return hi, lo


def _dot_hilo(a, b):
    hi, lo = _split(a)
    return _dot(hi, b) + _dot(lo, b)


def _sigmoid(x):
    return 1.0 / (1.0 + jnp.exp(-x))


def _modulated_norm(x, g, shift, scale):
    ms = jnp.mean(x * x, axis=-1, keepdims=True)
    y = x * lax.rsqrt(ms + RMS_EPS) * g
    return y * (1.0 + scale) + shift


def _params(*sem):
    return pltpu.CompilerParams(dimension_semantics=sem, vmem_limit_bytes=VMEM_LIMIT)


def _resident(shape):
    nd = len(shape)
    return pl.BlockSpec(shape, lambda *_: (0,) * nd, pipeline_mode=pl.Buffered(1))


def _ada_kernel(c_ref, w_ref, b_ref, o_ref):
    c = c_ref[...]
    ca = c * _sigmoid(c)
    o_ref[...] = jnp.dot(ca, w_ref[...], preferred_element_type=F32,
                         precision=lax.Precision.HIGHEST) + b_ref[...]


def _ada(c, ada_w, ada_b):
    depth, d, n = ada_w.shape
    b = c.shape[0]
    tn = 18 * LANES
    return pl.pallas_call(
        _ada_kernel,
        grid=(depth, n // tn),
        in_specs=[pl.BlockSpec((b, d), lambda l, j: (0, 0)),
                  pl.BlockSpec((None, d, tn), lambda l, j: (l, 0, j)),
                  pl.BlockSpec((None, 1, tn), lambda l, j: (l, 0, j))],
        out_specs=pl.BlockSpec((None, b, tn), lambda l, j: (l, 0, j)),
        out_shape=jax.ShapeDtypeStruct((depth, b, n), F32),
        compiler_params=_params("parallel", "parallel"),
        name="ada",
    )(c, ada_w, ada_b.reshape(depth, 1, n))


def _ffn_kernel(x_ref, g_ref, sh_ref, sc_ref, gt_ref, wa_ref, wb_ref, wo_ref, o_ref, y_ref, acc_ref):
    y_ref[...] = _modulated_norm(x_ref[...], g_ref[...], sh_ref[...], sc_ref[...]).astype(BF16)
    for c in range(FF_PAD // FF_CHUNK):
        sl = slice(c * FF_CHUNK, (c + 1) * FF_CHUNK)
        a = _dot(y_ref[...], wa_ref[:, sl])
        b = _dot(y_ref[...], wb_ref[:, sl])
        u = (a * _sigmoid(a) * b).astype(BF16)
        contrib = _dot(u, wo_ref[sl, :])
        if c == 0:
            acc_ref[...] = contrib
        else:
            acc_ref[...] += contrib
    o_ref[...] = x_ref[...] + 0.5 * gt_ref[...] * acc_ref[...]


def _mod_spec(tiles_per_batch, d):
    return pl.BlockSpec((None, 1, d), lambda i: (i // tiles_per_batch, 0, 0))


def _ffn(x, g, shift, scale, gate, wa, wb, wo, s):
    n, d = x.shape
    tpb = s // TM
    return pl.pallas_call(
        _ffn_kernel,
        grid=(n // TM,),
        in_specs=[pl.BlockSpec((TM, d), lambda i: (i, 0)),
                  _resident((1, d)),
                  _mod_spec(tpb, d), _mod_spec(tpb, d), _mod_spec(tpb, d),
                  _resident(wa.shape), _resident(wb.shape), _resident(wo.shape)],
        out_specs=pl.BlockSpec((TM, d), lambda i: (i, 0)),
        out_shape=jax.ShapeDtypeStruct((n, d), F32),
        scratch_shapes=[pltpu.VMEM((TM, d), BF16), pltpu.VMEM((TM, d), F32)],
        compiler_params=_params("parallel"),
        name="ffn",
    )(x, g, shift, scale, gate, wa, wb, wo)


def _proj_kernel(meta, kinds, seq, x_ref, g_ref, sh_ref, sc_ref, w_ref, gain_ref, gm_ref, *rest):
    n_out = len(kinds)
    outs = rest[:n_out]
    y_ref, stage_ref = rest[n_out], rest[n_out + 1]
    tile_in_seq = pl.program_id(0) % (seq // TM)
    y_ref[...] = _modulated_norm(x_ref[...], g_ref[...], sh_ref[...], sc_ref[...]).astype(BF16)
    for c, halves in enumerate(meta):
        sl = slice(c * PROJ_CHUNK, (c + 1) * PROJ_CHUNK)
        z = _dot(y_ref[...], w_ref[:, sl])
        if any(normed for _, _, normed in halves):
            msq = _dot((z * z).astype(BF16), gm_ref[...])
            zn = z * lax.rsqrt(msq + RMS_EPS) * gain_ref[:, sl]
        for hf, (oi, off, normed) in enumerate(halves):
            src = (zn if normed else z)[:, hf * LANES:(hf + 1) * LANES]
            kind = kinds[oi]
            if kind == "cols":
                src = src * gain_ref[:, c * PROJ_CHUNK + hf * LANES:c * PROJ_CHUNK + (hf + 1) * LANES]
                outs[oi][off:off + LANES, :] = src.T.astype(outs[oi].dtype)
            elif kind == "rows":
                outs[oi][:, off:off + LANES] = src.astype(outs[oi].dtype)
            else:
                per = TM // kind
                stage_ref[...] = src
                for cs in range(kind):
                    dest = pl.multiple_of(cs * (seq // kind) + tile_in_seq * per, per)
                    outs[oi][pl.ds(dest, per), off:off + LANES] = (
                        stage_ref[pl.ds(cs, per, stride=kind), :].astype(outs[oi].dtype))


def _proj(x, g, shift, scale, w, gain, gm, meta, out_defs, s):
    n, d = x.shape
    tpb = s // TM
    specs = {"rows": lambda wd: pl.BlockSpec((TM, wd), lambda i: (i, 0)),
             "cols": lambda wd: pl.BlockSpec((None, wd, TM), lambda i: (i // tpb, 0, i % tpb))}
    stream = lambda wd: pl.BlockSpec((None, s, wd), lambda i: (i // tpb, 0, 0))
    shapes = {"rows": lambda wd: (n, wd), "cols": lambda wd: (n // s, wd, s)}
    kinds = tuple(k for _, _, k in out_defs)
    return pl.pallas_call(
        functools.partial(_proj_kernel, meta, kinds, s),
        grid=(n // TM,),
        in_specs=[pl.BlockSpec((TM, d), lambda i: (i, 0)),
                  _resident((1, d)),
                  _mod_spec(tpb, d), _mod_spec(tpb, d),
                  _resident(w.shape), _resident(gain.shape), _resident(gm.shape)],
        out_specs=[specs.get(k, stream)(wd) for wd, _, k in out_defs],
        out_shape=[jax.ShapeDtypeStruct(shapes.get(k, lambda wd: (n // s, s, wd))(wd), dt) for wd, dt, k in out_defs],
        scratch_shapes=[pltpu.VMEM((TM, d), BF16), pltpu.VMEM((TM, LANES), F32)],
        compiler_params=_params("arbitrary"),
        name="proj",
    )(x, g, shift, scale, w, gain, gm)


def _mla_prep_kernel(in_ref, qg_ref, kvg_ref, wuq_ref, wuk_ref, wuv_ref, gm_ref,
                     gq_ref, gkn_ref, gkr_ref, cos_ref, sin_ref, q_out, k_out, v_out):
    def rms(z, g):
        return z * lax.rsqrt(jnp.mean(z * z, axis=-1, keepdims=True) + RMS_EPS) * g

    cqn = rms(in_ref[:, 0:MLA_Q_LORA], qg_ref[...]).astype(BF16)
    ckvn = rms(in_ref[:, MLA_Q_LORA:MLA_Q_LORA + MLA_KV_LORA], kvg_ref[...]).astype(BF16)
    c3 = in_ref[:, 3 * LANES:4 * LANES]
    cos = cos_ref[...]
    sin = sin_ref[...]
    gm = gm_ref[...]
    half = MLA_ROPE // 2
    slot_lane = lax.broadcasted_iota(jnp.int32, (1, 2 * LANES), 1) & (LANES - 1)
    first_half = slot_lane < MLA_NOPE + half

    def norm_rope(z, gain):
        msq = _dot((z * z).astype(BF16), gm)
        z = z * lax.rsqrt(msq + RMS_EPS) * gain
        rot = jnp.where(first_half, -pltpu.roll(z, 2 * LANES - half, 1), pltpu.roll(z, half, 1))
        return z * cos + rot * sin

    kr = norm_rope(jnp.concatenate([c3, c3], axis=1), gkr_ref[...])
    for h in range(MLA_HEADS // 2):
        sl = slice(h * 2 * LANES, (h + 1) * 2 * LANES)
        q_out[:, sl] = norm_rope(_dot(cqn, wuq_ref[:, sl]), gq_ref[...]).astype(BF16)
        k_out[:, sl] = (norm_rope(_dot(ckvn, wuk_ref[:, sl]), gkn_ref[...]) + kr).astype(BF16)
    v_out[...] = _dot_nt(wuv_ref[...], ckvn).astype(BF16)


def _mla_prep(mla_in, qg, kvg, wuq, wuk, wuv, gm, gq, gkn, gkr, cos_t, sin_t, s):
    n = mla_in.shape[0]
    tpb = s // TM
    hw = MLA_HEADS * LANES
    vw = MLA_HEADS * MLA_V
    tab = pl.BlockSpec((TM, 2 * LANES), lambda i: (i % tpb, 0))
    consts = [qg, kvg, wuq, wuk, wuv, gm, gq, gkn, gkr]
    return pl.pallas_call(
        _mla_prep_kernel,
        grid=(n // TM,),
        in_specs=[pl.BlockSpec((TM, 4 * LANES), lambda i: (i, 0))] + [_resident(a.shape) for a in consts] + [tab, tab],
        out_specs=[pl.BlockSpec((TM, hw), lambda i: (i, 0)), pl.BlockSpec((TM, hw), lambda i: (i, 0)),
                   pl.BlockSpec((None, vw, TM), lambda i: (i // tpb, 0, i % tpb))],
        out_shape=[jax.ShapeDtypeStruct((n, hw), BF16), jax.ShapeDtypeStruct((n, hw), BF16),
                   jax.ShapeDtypeStruct((n // s, vw, s), BF16)],
        compiler_params=_params("parallel"),
        name="mla_prep",
    )(mla_in, *consts, cos_t, sin_t)


def _attn_kernel(cfg, q_ref, k_ref, vt_ref, tab_ref, *rest):
    n_tab, qw, backs, sel_cfg, want_lse, n_qt, stride = cfg
    n_qs = n_qt // stride
    if sel_cfg is not None:
        sel_ref, rest = rest[0], rest[1:]
        sel_bpt, sel_stride, sel_pair_mul = sel_cfg
    o_ref = rest[0]
    lse_ref = rest[1] if want_lse else None
    st_scr, p_scr, acc_fin, m_fin = rest[-6:-4], rest[-4:-2], rest[-2], rest[-1]
    blk = pl.program_id(1)
    lane = lax.broadcasted_iota(jnp.int32, (1, LANES), 1)

    counts = tuple(sum(min(t % n_qs, bk) + 1 for t in range(n_qt)) for bk in backs)
    back = jnp.int32(backs[-1])
    n_tiles = jnp.int32(counts[-1])
    for bi in range(len(backs) - 2, -1, -1):
        back = jnp.where(blk == bi, jnp.int32(backs[bi]), back)
        n_tiles = jnp.where(blk == bi, jnp.int32(counts[bi]), n_tiles)

    ones_rows = jnp.ones((DEN_ROWS, TK), BF16)

    def first_key_tile(qi):
        return qi - jnp.minimum(qi & (n_qs - 1), back)

    def rows(t, size):
        return pl.ds(t * size if isinstance(t, int) else pl.multiple_of(t * size, size), size)

    def out_rows(t, size):
        if stride == 1:
            return rows(t, size)
        return pl.ds((t >> int(math.log2(n_qs))) + stride * size * (t & (n_qs - 1)), size, stride=stride)

    def logits_to(slot, qi, j):
        if qw == 1:
            q = q_ref[rows(qi, TQ), :]
            zero = jnp.zeros_like(q)
            qs = [jnp.where(lane < HEAD_DIM, q, zero), jnp.where(lane >= HEAD_DIM, q, zero)]
        else:
            qs = [q_ref[rows(qi, TQ), s * LANES:(s + 1) * LANES] for s in range(2)]
        for s in range(2):
            kj = k_ref[rows(j, TK), :] if qw == 1 else k_ref[rows(j, TK), s * LANES:(s + 1) * LANES]
            st = _dot_nt(kj, qs[s])
            if sel_cfg is not None:
                off = sel_stride * (s + sel_pair_mul * blk) + sel_bpt * j
                kb = TK // sel_bpt
                qcol = pl.ds(pl.multiple_of(qi * TQ, TQ), TQ)
                st = jnp.concatenate(
                    [st[bk * kb:(bk + 1) * kb] + sel_ref[pl.ds(off + bk, 1), qcol]
                     for bk in range(sel_bpt)], axis=0)
            st_scr[slot][s] = st

    def accumulate(slot, j, alphas, accs):
        krow = pl.multiple_of(j * TK, TK)
        new = []
        for s in range(2):
            vt = jnp.concatenate([vt_ref[s * HEAD_DIM:(s + 1) * HEAD_DIM, pl.ds(krow, TK)], ones_rows], axis=0)
            new.append(alphas[s] * accs[s] + _dot(vt, p_scr[slot][s]))
        return tuple(new)

    def finalize(qi, ms, accs):
        for s in range(2):
            acc_fin[qi, s] = accs[s]
            if want_lse:
                m_fin[qi, s] = jnp.broadcast_to(ms[s], m_fin.shape[2:])

    def write_out(t):
        accs = [acc_fin[t, s] for s in range(2)]
        dens = [acc[HEAD_DIM:HEAD_DIM + 1] for acc in accs]
        out_t = jnp.concatenate([acc[:HEAD_DIM] / l for acc, l in zip(accs, dens)], axis=0)
        o_ref[out_rows(t, TQ), :] = out_t.T.astype(o_ref.dtype)
        if want_lse:
            lse_t = jnp.concatenate([jnp.broadcast_to((m_fin[t, s][0:1] + jnp.log2(dens[s])) * LN2, (HEAD_DIM, TQ))
                                     for s in range(2)], axis=0)
            lse_ref[out_rows(t, TQ), :] = lse_t.T

    def advance(q, j):
        last = j == q
        at_end = jnp.logical_and(last, q == n_qt - 1)
        starts = jnp.logical_and(last, jnp.logical_not(at_end))
        qn = jnp.where(starts, q + 1, q)
        jn = jnp.where(at_end, j, jnp.where(last, first_key_tile(q + 1), j + 1))
        return qn, jn, starts, at_end

    def softmax(slot, tile, stats):
        q, j, is_first, filler = tile
        keep = jnp.where(is_first, 0.0, 1.0)
        d = jnp.where(filler, n_tab, jnp.minimum(q - j, n_tab - 1))
        new_stats, alphas = [], []
        for s in range(2):
            m = jnp.where(is_first, NEG_INF, stats[s])
            st = st_scr[slot][s] + tab_ref[s, d]
            m_new = jnp.maximum(m, jnp.max(st, axis=0, keepdims=True))
            alphas.append(jnp.exp2(m - m_new) * keep)
            new_stats.append(m_new)
            p_scr[slot][s] = jnp.exp2(st - m_new).astype(BF16)
        return tuple(new_stats), tuple(alphas)

    def body(u, carry):
        tile_a, (q2, j2), (q1, j1, first1), stats2, stats1, alphas2, alphas1, accs = carry
        qa, ja, first_a, _ = tile_a
        tile_b = advance(qa, ja)
        qb, jb, first_b, _ = tile_b
        tile_c = advance(qb, jb)
        logits_to(1, qb, jb)
        accs_x = accumulate(0, j2, alphas2, accs)
        accs_y = accumulate(1, j1, alphas1, accs_x)
        stats_a, alphas_a = softmax(0, tile_a, stats1)
        logits_to(0, tile_c[0], tile_c[1])
        stats_b, alphas_b = softmax(1, tile_b, stats_a)

        @pl.when(jnp.logical_and(first1, u > 0))
        def _():
            finalize(q2, stats2, accs_x)

        @pl.when(jnp.logical_and(first_a, u > 0))
        def _():
            finalize(q1, stats1, accs_y)

        return tile_c, (qa, ja), (qb, jb, first_b), stats_a, stats_b, alphas_a, alphas_b, accs_y

    zero_i = jnp.int32(0)
    logits_to(0, zero_i, zero_i)
    for p_slot in p_scr:
        p_slot[...] = jnp.zeros(p_slot.shape, BF16)
    stats0 = tuple(jnp.full((1, TQ), NEG_INF, F32) for _ in range(2))
    ones = tuple(jnp.ones((1, TQ), F32) for _ in range(2))
    init = ((zero_i, zero_i, zero_i == 0, zero_i != 0), (zero_i, zero_i), (zero_i, zero_i, zero_i != 0),
            stats0, stats0, ones, ones, tuple(jnp.zeros((HEAD_DIM + DEN_ROWS, TQ), F32) for _ in range(2)))
    _, (q2, j2), (q1, j1, first1), stats2, stats1, alphas2, alphas1, accs = lax.fori_loop(
        0, (n_tiles + 1) // 2, body, init)
    accs_x = accumulate(0, j2, alphas2, accs)

    @pl.when(first1)
    def _():
        finalize(q2, stats2, accs_x)

    finalize(q1, stats1, accumulate(1, j1, alphas1, accs_x))
    for t in range(n_qt):
        write_out(t)


def _attn(q, k, vt, tab, sel, *, qc0, kc0, vb0, n_blk, qw, backs=None, kv_shared=False,
          tab_shared=False, sel_cfg=None, out_dtype=F32, want_lse=False, stride=1):
    b, s = q.shape[:2]
    n_tab = tab.shape[2] - 1
    n_qt = s // TQ
    if backs is None:
        backs = (n_qt,)
    cfg = (n_tab, qw, backs, sel_cfg, want_lse, n_qt, stride)
    kidx = (lambda bb, h: (bb, 0, kc0)) if kv_shared else (lambda bb, h: (bb, 0, kc0 + h))
    vidx = (lambda bb, h: (bb, vb0, 0)) if kv_shared else (lambda bb, h: (bb, vb0 + h, 0))
    tidx = (lambda bb, h: (0, 0, 0, 0, 0)) if tab_shared else (lambda bb, h: (h, 0, 0, 0, 0))
    in_specs = [pl.BlockSpec((None, s, qw * LANES), lambda bb, h: (bb, 0, qc0 + h)),
                pl.BlockSpec((None, s, qw * LANES), kidx),
                pl.BlockSpec((None, 2 * HEAD_DIM, s), vidx),
                pl.BlockSpec((None, 2, n_tab + 1, TK, TQ), tidx)]
    args = [q, k, vt, tab]
    if sel_cfg is not None:
        in_specs.append(pl.BlockSpec((None, LANES, s), lambda bb, h: (bb, 0, 0)))
        args.append(sel)
    ospec = pl.BlockSpec((None, s, LANES), lambda bb, h: (bb, 0, h))
    out_specs = [ospec]
    out_shape = [jax.ShapeDtypeStruct((b, s, n_blk * LANES), out_dtype)]
    if want_lse:
        out_specs.append(ospec)
        out_shape.append(jax.ShapeDtypeStruct((b, s, n_blk * LANES), F32))
    return pl.pallas_call(
        functools.partial(_attn_kernel, cfg),
        grid=(b, n_blk),
        in_specs=in_specs,
        out_specs=out_specs,
        out_shape=out_shape,
        scratch_shapes=([pltpu.VMEM((2, TK, TQ), F32)] * 2 + [pltpu.VMEM((2, TK, TQ), BF16)] * 2
                        + [pltpu.VMEM((n_qt, 2, HEAD_DIM + DEN_ROWS, TQ), F32), pltpu.VMEM((n_qt, 2, 8, TQ), F32)]),
        compiler_params=_params("parallel", "parallel"),
        name="attn",
    )(*args)


def _nsa_cmp_kernel(kc_ref, vc_ref, pe_ref, wlo_ref, whi_ref, w2_ref, gm_ref, gain_ref, kcmp_ref, vcmpt_ref):
    nch = kcmp_ref.shape[0]

    def hidden(c_ref, j):
        lo = hi = None
        for l in range(NSA_CMP_STRIDE):
            rows = c_ref[pl.ds(l, nch, stride=NSA_CMP_STRIDE), :]
            sl = slice(l * LANES, (l + 1) * LANES)
            t_lo = _dot((rows + pe_ref[j, 0][:, sl]).astype(BF16), wlo_ref[j][sl, :])
            t_hi = _dot((rows + pe_ref[j, 1][:, sl]).astype(BF16), whi_ref[j][sl, :])
            lo = t_lo if lo is None else lo + t_lo
            hi = t_hi if hi is None else hi + t_hi
        h = lo + pltpu.roll(hi, nch - 1, 0)
        return (h * _sigmoid(h)).astype(BF16)

    kz = _dot_nt(hidden(kc_ref, 0), w2_ref[0])
    msq = _dot_hilo(kz * kz, gm_ref[...])
    kcmp_ref[...] = (kz * lax.rsqrt(msq + RMS_EPS) * gain_ref[...]).astype(BF16)
    vcmpt_ref[...] = _dot_nt(w2_ref[1], hidden(vc_ref, 1)).astype(BF16)


def _nsa_cmp(kc3, vc3, pe, wlo, whi, w2, gm, gain):
    b, s, width = kc3.shape
    nch = s // NSA_CMP_STRIDE
    consts = [pe, wlo, whi, w2, gm, gain]
    blk = pl.BlockSpec((None, s, width), lambda i: (i, 0, 0))
    oblk = pl.BlockSpec((None, nch, LANES), lambda i: (i, 0, 0))
    return pl.pallas_call(
        _nsa_cmp_kernel,
        grid=(b,),
        in_specs=[blk, blk] + [_resident(a.shape) for a in consts],
        out_specs=[oblk, oblk],
        out_shape=[jax.ShapeDtypeStruct((b, nch, LANES), BF16)] * 2,
        compiler_params=_params("parallel"),
        name="nsa_cmp",
    )(kc3, vc3, *consts)


def _rank_keep(score, ids, top):
    cnt = jnp.zeros(score.shape, jnp.int32)
    for mp in range(score.shape[0]):
        other = score[mp:mp + 1, :]
        tie = jnp.where(mp < ids, 1, 0)
        cnt = cnt + jnp.where(other > score, 1, jnp.where(other == score, tie, 0))
    return cnt < top


def _nsa_sel_kernel(q_ref, kcmp_ref, vcmpt_ref, bias_ref, ovl_ref, oc_ref, sel_ref):
    qi = pl.program_id(1)
    lane = lax.broadcasted_iota(jnp.int32, (1, LANES), 1)
    row = lax.broadcasted_iota(jnp.int32, (LANES, 1), 0)
    t = qi * TQ + lax.broadcasted_iota(jnp.int32, (1, TQ), 1)
    mask_c = (NSA_CMP_STRIDE * row + NSA_CMP_LEN - 1) <= t
    kcmp = kcmp_ref[...]
    heads = [(p, g) for p in range(NSA_HPG) for g in range(NSA_GROUPS)]
    raw = []
    for p, g in heads:
        qp = q_ref[:, p * LANES:(p + 1) * LANES]
        raw.append(_dot_nt(kcmp, jnp.where((lane >> 6) == g, qp, jnp.zeros_like(qp))))
    pcs = []
    for (p, g), r in zip(heads, raw):
        lg = jnp.where(mask_c, r * LN2 + bias_ref[p * NSA_GROUPS + g], NEG_INF)
        m = jnp.max(lg, axis=0, keepdims=True)
        e = jnp.where(mask_c, jnp.exp(lg - m), 0.0)
        den = jnp.maximum(jnp.sum(e, axis=0, keepdims=True), 1e-30)
        pcs.append(e / den)
    imp = jnp.zeros((LANES, TQ), F32)
    ocs = []
    for (p, g), pc in zip(heads, pcs):
        hi, lo = _split(pc)
        ocs.append(_dot(vcmpt_ref[g * HEAD_DIM:(g + 1) * HEAD_DIM, :], hi))
        imp = imp + _dot(ovl_ref[g], hi) + _dot(ovl_ref[g], lo)
    for p in range(NSA_HPG):
        oc_ref[:, p * LANES:(p + 1) * LANES] = jnp.concatenate([ocs[2 * p], ocs[2 * p + 1]], axis=0).T.astype(BF16)

    n_slc = 32
    ids = row[0:n_slc]
    cur = t >> 6
    forced = (ids == 0) | (ids == cur) | (ids == cur - 1)
    masks = []
    for g in range(NSA_GROUPS):
        score = jnp.where(forced, NSA_FORCED, jnp.where(ids <= cur, imp[g * n_slc:(g + 1) * n_slc], NEG_INF))
        keep = _rank_keep(score, ids, NSA_SLC_TOP) & (score > 0.5 * NEG_INF)
        masks.append(jnp.where(keep, 0.0, NEG_INF))
    masks.append(jnp.full((LANES - NSA_GROUPS * n_slc, TQ), NEG_INF, F32))
    sel_ref[...] = jnp.concatenate(masks, axis=0)


def _nsa_sel(q, kcmp, vcmpt, bias_c, ovl):
    b, s, w = q.shape
    return pl.pallas_call(
        _nsa_sel_kernel,
        grid=(b, s // TQ),
        in_specs=[pl.BlockSpec((None, TQ, w), lambda bb, i: (bb, i, 0)),
                  pl.BlockSpec((None, LANES, LANES), lambda bb, i: (bb, 0, 0)),
                  pl.BlockSpec((None, LANES, LANES), lambda bb, i: (bb, 0, 0)),
                  pl.BlockSpec((NSA_HEADS, LANES, TQ), lambda bb, i: (0, 0, i)),
                  _resident(ovl.shape)],
        out_specs=[pl.BlockSpec((None, TQ, w), lambda bb, i: (bb, i, 0)),
                   pl.BlockSpec((None, LANES, TQ), lambda bb, i: (bb, 0, i))],
        out_shape=[jax.ShapeDtypeStruct((b, s, w), BF16), jax.ShapeDtypeStruct((b, LANES, s), F32)],
        compiler_params=_params("parallel", "parallel"),
        name="nsa_sel",
    )(q, kcmp, vcmpt, bias_c, ovl)


def _moba_gate_kernel(q_ref, k_ref, avg_ref, sel_ref):
    qi = pl.program_id(1)
    nb = 8
    kmean = _dot(avg_ref[...], k_ref[...])
    r2 = lax.broadcasted_iota(jnp.int32, kmean.shape, 0)
    c2 = lax.broadcasted_iota(jnp.int32, kmean.shape, 1)
    kmean = jnp.where((r2 >> 3) == (c2 >> 6), kmean, 0.0)
    kh, kl = _split(kmean)
    q = q_ref[...]
    gate = _dot_nt(kh, q) + _dot_nt(kl, q)
    ids = lax.broadcasted_iota(jnp.int32, (nb, 1), 0)
    past = ids < qi
    masks = []
    for h in range(MOBA_HEADS):
        score = jnp.where(past, gate[h * nb:(h + 1) * nb], NEG_INF)
        keep = (_rank_keep(score, ids, MOBA_TOP) & past) | (ids == qi)
        masks.append(jnp.where(keep, 0.0, NEG_INF))
    masks.append(jnp.full((LANES - MOBA_HEADS * nb, TQ), NEG_INF, F32))
    sel_ref[...] = jnp.concatenate(masks, axis=0)


def _moba_gate(p_arr, avg, qc, kc):
    b, s, _ = p_arr.shape
    w = MOBA_HEADS * HEAD_DIM
    return pl.pallas_call(
        _moba_gate_kernel,
        grid=(b, s // TQ),
        in_specs=[pl.BlockSpec((None, TQ, w), lambda bb, i: (bb, i, qc)),
                  pl.BlockSpec((None, s, w), lambda bb, i: (bb, 0, kc)),
                  _resident(avg.shape)],
        out_specs=pl.BlockSpec((None, LANES, TQ), lambda bb, i: (bb, 0, i)),
        out_shape=jax.ShapeDtypeStruct((b, LANES, s), F32),
        compiler_params=_params("parallel", "parallel"),
        name="moba_gate",
    )(p_arr, p_arr, avg)


def _even_out_kernel(x_ref, oa_ref, oc_ref, os_ref, ow_ref, gl_ref, eg_ref, wa_ref, wb_ref, gt_ref, o_ref):
    sg = _sigmoid(gl_ref[...])
    hi, lo = _split(sg)
    nsa = None
    for br, src in enumerate((oc_ref, os_ref, ow_ref)):
        gexp = _dot(hi, eg_ref[br]) + _dot(lo, eg_ref[br])
        term = gexp * src[...]
        nsa = term if nsa is None else nsa + term
    m = _dot(oa_ref[...], wa_ref[...]) + _dot(nsa.astype(BF16), wb_ref[...])
    o_ref[...] = x_ref[...] + gt_ref[...] * m


def _even_out(x, o_a, o_c, o_s, o_w, mla_in, eg, wa, wb, gate, s):
    n, d = x.shape
    tpb = s // TM
    row = lambda wd: pl.BlockSpec((TM, wd), lambda i: (i, 0))
    return pl.pallas_call(
        _even_out_kernel,
        grid=(n // TM,),
        in_specs=[row(d), row(o_a.shape[1]), row(o_c.shape[1]), row(o_s.shape[1]), row(o_w.shape[1]),
                  pl.BlockSpec((TM, LANES), lambda i: (i, 3)),
                  _resident(eg.shape), _resident(wa.shape), _resident(wb.shape), _mod_spec(tpb, d)],
        out_specs=row(d),
        out_shape=jax.ShapeDtypeStruct((n, d), F32),
        compiler_params=_params("parallel"),
        name="even_out",
    )(x, o_a, o_c, o_s, o_w, mla_in, eg, wa, wb, gate)


def _odd_out_kernel(x_ref, od0_ref, od1_ref, od2_ref, ls0_ref, ls1_ref, ls2_ref, om_ref, wd_ref, wm_ref, gt_ref, o_ref):
    ods = (od0_ref, od1_ref, od2_ref)
    ls = [r[...] for r in (ls0_ref, ls1_ref, ls2_ref)]
    mx = jnp.maximum(jnp.maximum(ls[0], ls[1]), ls[2])
    es = [jnp.exp(l - mx) for l in ls]
    den = es[0] + es[1] + es[2]
    merged = None
    for g in range(len(DIL_PAIRS)):
        term = (es[g] / den) * ods[g][...]
        merged = term if merged is None else merged + term
    m = _dot(merged.astype(BF16), wd_ref[...]) + _dot(om_ref[...], wm_ref[...])
    o_ref[...] = x_ref[...] + gt_ref[...] * m


def _odd_out(x, o_ds, lse_ds, o_m, wd, wm, gate, s):
    n, d = x.shape
    tpb = s // TM
    row = lambda wd_: pl.BlockSpec((TM, wd_), lambda i: (i, 0))
    return pl.pallas_call(
        _odd_out_kernel,
        grid=(n // TM,),
        in_specs=[row(d)] + [row(a.shape[1]) for a in (*o_ds, *lse_ds)] + [row(o_m.shape[1]),
                  _resident(wd.shape), _resident(wm.shape), _mod_spec(tpb, d)],
        out_specs=row(d),
        out_shape=jax.ShapeDtypeStruct((n, d), F32),
        compiler_params=_params("parallel"),
        name="odd_out",
    )(x, *o_ds, *lse_ds, o_m, wd, wm, gate)


def _t5_bucket(dist):
    n = jnp.maximum(jnp.asarray(dist, jnp.int32), 0)
    nf = jnp.maximum(n, 1).astype(F32)
    large = T5_MAX_EXACT + (jnp.log(nf / T5_MAX_EXACT) / math.log(T5_MAX_DIST / T5_MAX_EXACT)
                            * (NUM_BUCKETS - T5_MAX_EXACT)).astype(jnp.int32)
    return jnp.where(n < T5_MAX_EXACT, n, jnp.minimum(large, NUM_BUCKETS - 1))


TOEP_PERIOD = 4 * TQ


def _toeplitz_dist():
    j = np.arange(TOEP_PERIOD)
    return np.where(j < 3 * TQ, j, j - TOEP_PERIOD)


def _toeplitz_kernel(n_tab, u_ref, o_ref):
    x = jnp.broadcast_to(u_ref[...], (TK, TOEP_PERIOD))
    y = pltpu.roll(x, 0, 1, stride=1, stride_axis=0)
    for dlt in range(n_tab):
        o_ref[dlt] = y[:, dlt * TQ:(dlt + 1) * TQ]
    o_ref[n_tab] = jnp.full((TK, TQ), NEG_INF * LOG2E, F32)


def _toeplitz_tiles(u, n_tab):
    h = u.shape[0]
    return pl.pallas_call(
        functools.partial(_toeplitz_kernel, n_tab),
        grid=(h,),
        in_specs=[pl.BlockSpec((None, 1, TOEP_PERIOD), lambda i: (i, 0, 0))],
        out_specs=pl.BlockSpec((None, n_tab + 1, TK, TQ), lambda i: (i, 0, 0, 0)),
        out_shape=jax.ShapeDtypeStruct((h, n_tab + 1, TK, TQ), F32),
        compiler_params=_params("parallel"),
        name="toeplitz",
    )(u.reshape(h, 1, TOEP_PERIOD))


def _cmp_bias_kernel(u_ref, o_ref):
    x = jnp.broadcast_to(u_ref[...], o_ref.shape)
    o_ref[...] = pltpu.roll(x, 0, 1, stride=NSA_CMP_STRIDE, stride_axis=0)


def _cmp_bias(u, n_rows):
    h, s = u.shape
    return pl.pallas_call(
        _cmp_bias_kernel,
        grid=(h,),
        in_specs=[pl.BlockSpec((None, 1, s), lambda i: (i, 0, 0))],
        out_specs=pl.BlockSpec((None, n_rows, s), lambda i: (i, 0, 0)),
        out_shape=jax.ShapeDtypeStruct((h, n_rows, s), F32),
        compiler_params=_params("parallel"),
        name="cmp_bias",
    )(u.reshape(h, 1, s))


def _bias_tiles(t5_cols, ok, n_tab=3, dist_scale=1):
    dist = _toeplitz_dist()
    bias = jnp.transpose(t5_cols[_t5_bucket(dist * dist_scale)])
    u = jnp.where(jnp.asarray(ok)[None], bias, NEG_INF) * LOG2E
    return _toeplitz_tiles(u, n_tab)


def _pair(tiles):
    h = tiles.shape[0]
    return tiles.reshape(h // 2, 2, *tiles.shape[1:])


def _group_mean_np(sizes, width=LANES):
    gm = np.zeros((width, width), np.float32)
    o = 0
    for sz in sizes:
        gm[o:o + sz, o:o + sz] = 1.0 / sz
        o += sz
    return gm


def _group_mean_matrix(sizes, width=LANES):
    return jnp.asarray(_group_mean_np(sizes, width), BF16)


EV_META = ((0, 0, False), (0, 128, False), (0, 256, False), (0, 384, False),
           (1, 0, True), (1, 128, True), (1, 256, True), (1, 384, True),
           (2, 0, False), (3, 0, False), (4, 0, True), (5, 0, False), (6, 0, True), (7, 0, False))
EV_OUTS = [(512, F32, "rows"), (512, BF16, "rows"), (128, F32, "rows"), (128, F32, "rows"), (128, BF16, "rows"),
           (128, BF16, "cols"), (128, BF16, "rows"), (128, BF16, "cols")]
OD_DIL_STRIDES = (1, 4, 8)
OD_META = (((0, 0, True), (0, LANES, True)) + tuple((3 + g, h * LANES, True) for g in range(2) for h in range(2))
           + ((0, 2 * LANES, True), (0, 3 * LANES, True))
           + tuple((3 + g, (2 + h) * LANES, True) for g in range(2) for h in range(2))
           + ((1, 0, False), (1, LANES, False)) + tuple((0, (4 + c) * LANES, False) for c in range(4))
           + tuple((0, (8 + c) * LANES, True) for c in range(4)) + ((2, 0, False), (2, LANES, False)))
OD_OUTS = [(12 * LANES, BF16, "rows"), (2 * LANES, BF16, "cols"), (2 * LANES, BF16, "cols"),
           (4 * LANES, BF16, OD_DIL_STRIDES[1]), (4 * LANES, BF16, OD_DIL_STRIDES[2])]


def _even_w_in(w):
    jn, d, _ = w.shape
    z = lambda n_: jnp.zeros((jn, d, n_), w.dtype)
    nq = w[:, :, 416:928].reshape(jn, d, NSA_GROUPS, NSA_HPG, HEAD_DIM)
    nq = jnp.transpose(nq, (0, 1, 3, 2, 4)).reshape(jn, d, NSA_HEADS * HEAD_DIM)
    chunk3 = jnp.concatenate([w[:, :, 1696:1720], z(HEAD_DIM - 24), w[:, :, 384:416], z(LANES - 96)], axis=-1)
    return jnp.concatenate([w[:, :, 0:384], chunk3, nq, w[:, :, 928:1696]], axis=-1)


def kernel(x, c, t5_bias, ada_w, ada_b, norm_g, ffn_w_in, ffn_w_out, ev_w_in, ev_w_out, mla_q_norm_g,
           mla_kv_norm_g, mla_w_uq, mla_w_ukv, mla_qk_g, nsa_cmp_pe, nsa_cmp_w1, nsa_cmp_w2, nsa_qk_g,
           od_w_in, od_w_out, dil_qk_g, moba_qk_g):
    b, s, d = x.shape
    assert (s, d) == (2048, D_MODEL) and s % TM == 0 and TQ == MOBA_BLOCK and TQ == TK
    n = b * s
    hd = HEAD_DIM
    n_even = ev_w_in.shape[0]
    n_odd = od_w_in.shape[0]
    c64 = hd ** -0.5 * LOG2E
    c96 = (MLA_NOPE + MLA_ROPE) ** -0.5 * LOG2E

    mod = _ada(c, ada_w, ada_b).reshape(DEPTH, b, 3, 3, 1, d)

    dist = _toeplitz_dist()
    causal = dist >= 0
    gm64 = _group_mean_matrix((hd, hd))
    gm_proj = _group_mean_matrix((hd,) * (PROJ_CHUNK // hd), PROJ_CHUNK)
    per_chunk = PROJ_CHUNK // LANES
    chunked = lambda m: tuple(tuple(m[i:i + per_chunk]) for i in range(0, len(m), per_chunk))

    padc = FF_PAD - D_FF
    wa_all = jnp.pad(ffn_w_in[..., :D_FF], ((0, 0), (0, 0), (0, 0), (0, padc))).astype(BF16)
    wb_all = jnp.pad(ffn_w_in[..., D_FF:], ((0, 0), (0, 0), (0, 0), (0, padc))).astype(BF16)
    wo_all = jnp.pad(ffn_w_out, ((0, 0), (0, 0), (0, padc), (0, 0))).astype(BF16)

    nsa_tab = t5_bias[:, MLA_HEADS:MLA_HEADS + NSA_HEADS].reshape(NUM_BUCKETS, NSA_GROUPS, NSA_HPG)
    nsa_cols = jnp.transpose(nsa_tab, (0, 2, 1)).reshape(NUM_BUCKETS, NSA_HEADS)
    tab_sel = _pair(_bias_tiles(nsa_cols, causal))
    tab_win = _pair(_bias_tiles(nsa_cols, causal & (dist <= NSA_WINDOW - 1)))
    tab_mla = _toeplitz_tiles(jnp.where(jnp.asarray(causal), 0.0, NEG_INF).astype(F32)[None], 2)
    tab_mla = jnp.broadcast_to(tab_mla[None], (1, 2) + tab_mla.shape[1:])
    n_cmp_pad = s // NSA_CMP_STRIDE
    bias_c = _cmp_bias(jnp.transpose(nsa_cols[_t5_bucket(np.arange(s) - (NSA_CMP_LEN - 1))]), n_cmp_pad)
    n_cmp = (s - NSA_CMP_LEN) // NSA_CMP_STRIDE + 1
    cstart = np.arange(n_cmp) * NSA_CMP_STRIDE
    sstart = np.arange(s // NSA_SLC_BLOCK) * NSA_SLC_BLOCK
    overlap = np.clip(np.minimum(cstart[:, None] + NSA_CMP_LEN, sstart[None, :] + NSA_SLC_BLOCK)
                      - np.maximum(cstart[:, None], sstart[None, :]), 0, None).astype(np.float32) / NSA_CMP_LEN
    ovl = np.zeros((NSA_GROUPS, LANES, LANES), np.float32)
    for g in range(NSA_GROUPS):
        ovl[g, 32 * g:32 * g + 32, :n_cmp] = overlap.T
    ovl = jnp.asarray(ovl, BF16)
    eg = np.zeros((3, LANES, NSA_HEADS * hd), np.float32)
    for g in range(NSA_GROUPS):
        for p in range(NSA_HPG):
            for br in range(3):
                eg[br, (g * NSA_HPG + p) * 3 + br, p * LANES + g * hd:p * LANES + (g + 1) * hd] = 1.0
    eg = jnp.asarray(eg, BF16)
    gm_mla = jnp.asarray(np.kron(np.eye(2, dtype=np.float32), _group_mean_np((MLA_NOPE, MLA_ROPE))), BF16)
    inv = ROPE_THETA ** (-jnp.arange(0, MLA_ROPE, 2, dtype=F32) / MLA_ROPE)
    ang = jnp.arange(s, dtype=F32)[:, None] * inv[None, :]
    ones = jnp.ones((s, MLA_NOPE), F32)
    tail = LANES - MLA_NOPE - MLA_ROPE
    cos_t = jnp.tile(jnp.concatenate([ones, jnp.cos(ang), jnp.cos(ang), jnp.ones((s, tail), F32)], axis=1), (1, 2))
    sin_t = jnp.tile(jnp.concatenate([0 * ones, jnp.sin(ang), jnp.sin(ang), jnp.zeros((s, tail), F32)], axis=1), (1, 2))

    ev_w = _even_w_in(ev_w_in).astype(BF16)
    ev_gain = jnp.ones((n_even, 14 * LANES), F32)
    ev_gain = ev_gain.at[:, 512:1024].set(jnp.tile(nsa_qk_g[:, 0], (1, 8)) * c64)
    ev_gain = ev_gain.at[:, 1280:1408].set(jnp.tile(nsa_qk_g[:, 1], (1, 2)))
    ev_gain = ev_gain.at[:, 1536:1664].set(jnp.tile(nsa_qk_g[:, 1], (1, 2)))
    gain_kc = jnp.tile(nsa_qk_g[:, 1], (1, 2))
    wuq = jnp.pad(mla_w_uq.reshape(n_even, MLA_Q_LORA, MLA_HEADS, MLA_NOPE + MLA_ROPE),
                  ((0, 0), (0, 0), (0, 0), (0, tail))).reshape(n_even, MLA_Q_LORA, MLA_HEADS * LANES).astype(BF16)
    ukv = mla_w_ukv.reshape(n_even, MLA_KV_LORA, MLA_HEADS, MLA_NOPE + MLA_V)
    wuk = jnp.pad(ukv[..., :MLA_NOPE], ((0, 0), (0, 0), (0, 0), (0, LANES - MLA_NOPE))
                  ).reshape(n_even, MLA_KV_LORA, MLA_HEADS * LANES).astype(BF16)
    wuv = jnp.swapaxes(ukv[..., MLA_NOPE:].reshape(n_even, MLA_KV_LORA, MLA_HEADS * MLA_V), 1, 2).astype(BF16)
    zt = jnp.zeros((n_even, tail), F32)
    gq = jnp.tile(jnp.concatenate([mla_qk_g[:, 0] * c96, zt], axis=1), (1, 2))
    gkn = jnp.tile(jnp.concatenate([mla_qk_g[:, 1, :MLA_NOPE], jnp.zeros((n_even, LANES - MLA_NOPE), F32)], axis=1), (1, 2))
    gkr = jnp.tile(jnp.concatenate([jnp.zeros((n_even, MLA_NOPE), F32), mla_qk_g[:, 1, MLA_NOPE:], zt], axis=1), (1, 2))
    pe2 = jnp.broadcast_to(nsa_cmp_pe.reshape(n_even, 2, 2, 16, 1, hd), (n_even, 2, 2, 16, NSA_GROUPS, hd)
                           ).reshape(n_even, 2, 2, 1, 16 * LANES)
    eye = jnp.eye(NSA_GROUPS, dtype=F32)
    w1 = nsa_cmp_w1.reshape(n_even, 2, 2, 16, hd, NSA_CMP_HID)
    w1x = jnp.einsum('ijaldc,gh->ijalgdhc', w1, eye).reshape(n_even, 2, 2, 16 * LANES, NSA_GROUPS * NSA_CMP_HID).astype(BF16)
    w2x = jnp.einsum('ijcd,gh->ijhdgc', nsa_cmp_w2, eye).reshape(n_even, 2, LANES, NSA_GROUPS * NSA_CMP_HID).astype(BF16)
    wa_o = ev_w_out[:, :MLA_HEADS * MLA_V].astype(BF16)
    wb_o = jnp.transpose(ev_w_out[:, MLA_HEADS * MLA_V:].reshape(n_even, NSA_GROUPS, NSA_HPG, hd, d),
                         (0, 2, 1, 3, 4)).reshape(n_even, NSA_HEADS * hd, d).astype(BF16)

    assert DIL_PAIRS == ((128, 1), (512, 4), (2048, 16))
    dil_cfg = tuple(zip(OD_DIL_STRIDES, (1, 1, 0)))
    dil_ok = (causal & (dist <= 128), causal & (dist <= 128), causal & (dist % 2 == 0))
    tab_dil = [_pair(_bias_tiles(t5_bias[:, gi * DIL_HPG:(gi + 1) * DIL_HPG], dil_ok[gi], dist_scale=dil_cfg[gi][0]))
               for gi in range(len(DIL_PAIRS))]
    tab_moba = _pair(_bias_tiles(t5_bias[:, DIL_SLOTS:DIL_SLOTS + MOBA_HEADS], causal))
    avg = np.zeros((LANES, s), np.float32)
    for h in range(MOBA_HEADS):
        for m in range(s // MOBA_BLOCK):
            avg[8 * h + m, m * MOBA_BLOCK:(m + 1) * MOBA_BLOCK] = 1.0 / MOBA_BLOCK
    avg = jnp.asarray(avg, BF16)
    od_w = od_w_in.astype(BF16)
    od_gain = jnp.concatenate([jnp.tile(dil_qk_g[:, 0], (1, 12)) * c64, jnp.tile(dil_qk_g[:, 1], (1, 12)),
                               jnp.ones((n_odd, 768), F32), jnp.tile(moba_qk_g[:, 0], (1, 4)) * c64,
                               jnp.tile(moba_qk_g[:, 1], (1, 4)), jnp.ones((n_odd, 256), F32)], axis=1)
    wd_o = od_w_out[:, :DIL_HPG * hd].astype(BF16)
    wm_o = od_w_out[:, DIL_HPG * hd:].astype(BF16)

    sh3 = lambda a: a.reshape(b, s, a.shape[-1])
    tr3 = lambda a: jnp.swapaxes(sh3(a), 1, 2)
    xf = x.reshape(n, d)
    for i in range(DEPTH):
        j = i // 2
        g_i = norm_g[i].reshape(3, 1, d)
        xf = _ffn(xf, g_i[0], mod[i, :, 0, 0], mod[i, :, 0, 1], mod[i, :, 0, 2],
                  wa_all[i, 0], wb_all[i, 0], wo_all[i, 0], s)
        if i % 2 == 0:
            mla_in, nsa_q, kc, vc, ks, vs, kw, vw = _proj(
                xf, g_i[1], mod[i, :, 1, 0], mod[i, :, 1, 1], ev_w[j], ev_gain[j][None], gm_proj,
                chunked(EV_META), EV_OUTS, s)
            qf, kf, vf = _mla_prep(mla_in, mla_q_norm_g[j][None], mla_kv_norm_g[j][None], wuq[j], wuk[j], wuv[j],
                                   gm_mla, gq[j][None], gkn[j][None], gkr[j][None], cos_t, sin_t, s)
            (o_a,) = _attn(sh3(qf), sh3(kf), vf, tab_mla, None, qc0=0, kc0=0, vb0=0, n_blk=MLA_HEADS // 2, qw=2,
                           tab_shared=True, out_dtype=BF16)
            kcmp, vcmpt = _nsa_cmp(sh3(kc), sh3(vc), pe2[j], w1x[j, :, 0], w1x[j, :, 1], w2x[j], gm64, gain_kc[j][None])
            o_c, sel = _nsa_sel(sh3(nsa_q), kcmp, vcmpt, bias_c, ovl)
            (o_s,) = _attn(sh3(nsa_q), sh3(ks), vs, tab_sel, sel, qc0=0, kc0=0, vb0=0, n_blk=NSA_HPG, qw=1,
                           kv_shared=True, sel_cfg=(TK // NSA_SLC_BLOCK, 32, 0), out_dtype=BF16)
            (o_w,) = _attn(sh3(nsa_q), sh3(kw), vw, tab_win, None, qc0=0, kc0=0, vb0=0, n_blk=NSA_HPG, qw=1,
                           kv_shared=True, backs=(2, 2, 2, 2), out_dtype=BF16)
            xf = _even_out(xf, o_a.reshape(n, -1), o_c.reshape(n, -1), o_s.reshape(n, -1), o_w.reshape(n, -1),
                           mla_in, eg, wa_o[j], wb_o[j], mod[i, :, 1, 2], s)
        else:
            pr, vd0t, vmt, qk1, qk2 = _proj(xf, g_i[1], mod[i, :, 1, 0], mod[i, :, 1, 1], od_w[j], od_gain[j][None],
                                            gm_proj, chunked(OD_META), OD_OUTS, s)
            pr3 = pr.reshape(b, s, pr.shape[-1])
            o_ds, lse_ds = [], []
            for gi, (r, bk) in enumerate(dil_cfg):
                if r == 1:
                    qk, vdt = pr3, vd0t
                else:
                    v = pr3[:, :, (2 + 2 * gi) * LANES:(4 + 2 * gi) * LANES].reshape(b, s // r, r, 2 * LANES)
                    qk, vdt = (qk1, qk2)[gi - 1], jnp.transpose(v, (0, 3, 2, 1)).reshape(b, 2 * LANES, s)
                o_g, lse_g = _attn(qk, qk, vdt, tab_dil[gi], None, qc0=0, kc0=2, vb0=0, n_blk=2,
                                   qw=1, backs=(bk,), want_lse=True, stride=r)
                o_ds.append(o_g.reshape(n, -1))
                lse_ds.append(lse_g.reshape(n, -1))
            selm = _moba_gate(pr3, avg, 4, 5)
            (o_m,) = _attn(pr3, pr3, vmt, tab_moba, selm, qc0=8, kc0=10, vb0=0, n_blk=2, qw=1,
                           sel_cfg=(1, 8, 2), out_dtype=BF16)
            xf = _odd_out(xf, o_ds, lse_ds, o_m.reshape(n, -1), wd_o[j], wm_o[j], mod[i, :, 1, 2], s)
        xf = _ffn(xf, g_i[2], mod[i, :, 2, 0], mod[i, :, 2, 1], mod[i, :, 2, 2],
                  wa_all[i, 1], wb_all[i, 1], wo_all[i, 1], s)
    return xf.reshape(b, s, d)
```

```python
import functools
import math

import numpy as np
import jax
import jax.numpy as jnp
from jax import lax
from jax.experimental import pallas as pl
from jax.experimental.pallas import tpu as pltpu

F32 = jnp.float32
BF16 = jnp.bfloat16

D_MODEL = 1024
DEPTH = 4
D_FF = 2752
HEAD_DIM = 64
NUM_BUCKETS = 32
T5_MAX_EXACT = 16
T5_MAX_DIST = 128
RMS_EPS = 1e-6
NEG_INF = -1e30
MLA_HEADS = 8
MLA_NOPE = 64
MLA_ROPE = 32
MLA_V = 64
MLA_Q_LORA = 256
MLA_KV_LORA = 128
ROPE_THETA = 10000.0
NSA_HEADS = 8
NSA_GROUPS = 2
NSA_HPG = 4
NSA_CMP_LEN = 32
NSA_CMP_STRIDE = 16
NSA_CMP_HID = 256
NSA_SLC_BLOCK = 64
NSA_SLC_TOP = 8
NSA_WINDOW = 512
NSA_FORCED = 1e6
DIL_PAIRS = ((128, 1), (512, 4), (2048, 16))
DIL_HPG = 4
DIL_SLOTS = len(DIL_PAIRS) * DIL_HPG
MOBA_HEADS = 4
MOBA_BLOCK = 256
MOBA_TOP = 3

LANES = 128
V7X_VMEM_BYTES = 64 * 1024 * 1024
VMEM_LIMIT = V7X_VMEM_BYTES * 7 // 8
TM = 512
TQ = 256
TK = 256
FF_CHUNK = 256
FF_PAD = -(-D_FF // FF_CHUNK) * FF_CHUNK
PROJ_CHUNK = 256
DEN_ROWS = 16
LOG2E = math.log2(math.e)
LN2 = math.log(2.0)


def _dot(a, b):
    return jnp.dot(a, b, preferred_element_type=F32)


def _dot_nt(a, b):
    return lax.dot_general(a, b, (((1,), (1,)), ((), ())), preferred_element_type=F32)


def _split(a):
    hi = a.astype(BF16)
    lo = (a - hi.astype(F32)).astype(BF16)
    return hi, lo


def _dot_hilo(a, b):
    hi, lo = _split(a)
    return _dot(hi, b) + _dot(lo, b)


def _sigmoid(x):
    return 1.0 / (1.0 + jnp.exp(-x))


def _modulated_norm(x, g, shift, scale):
    ms = jnp.mean(x * x, axis=-1, keepdims=True)
    y = x * lax.rsqrt(ms + RMS_EPS) * g
    return y * (1.0 + scale) + shift


def _params(*sem):
    return pltpu.CompilerParams(dimension_semantics=sem, vmem_limit_bytes=VMEM_LIMIT)


def _resident(shape):
    nd = len(shape)
    return pl.BlockSpec(shape, lambda *_: (0,) * nd, pipeline_mode=pl.Buffered(1))


def _ada_kernel(c_ref, w_ref, b_ref, o_ref):
    c = c_ref[...]
    ca = c * _sigmoid(c)
    o_ref[...] = jnp.dot(ca, w_ref[...], preferred_element_type=F32,
                         precision=lax.Precision.HIGHEST) + b_ref[...]


def _ada(c, ada_w, ada_b):
    depth, d, n = ada_w.shape
    b = c.shape[0]
    tn = 18 * LANES
    return pl.pallas_call(
        _ada_kernel,
        grid=(depth, n // tn),
        in_specs=[pl.BlockSpec((b, d), lambda l, j: (0, 0)),
                  pl.BlockSpec((None, d, tn), lambda l, j: (l, 0, j)),
                  pl.BlockSpec((None, 1, tn), lambda l, j: (l, 0, j))],
        out_specs=pl.BlockSpec((None, b, tn), lambda l, j: (l, 0, j)),
        out_shape=jax.ShapeDtypeStruct((depth, b, n), F32),
        compiler_params=_params("parallel", "parallel"),
        name="ada",
    )(c, ada_w, ada_b.reshape(depth, 1, n))


def _ffn_kernel(x_ref, g_ref, sh_ref, sc_ref, gt_ref, wa_ref, wb_ref, wo_ref, o_ref, y_ref, acc_ref):
    y_ref[...] = _modulated_norm(x_ref[...], g_ref[...], sh_ref[...], sc_ref[...]).astype(BF16)
    for c in range(FF_PAD // FF_CHUNK):
        sl = slice(c * FF_CHUNK, (c + 1) * FF_CHUNK)
        a = _dot(y_ref[...], wa_ref[:, sl])
        b = _dot(y_ref[...], wb_ref[:, sl])
        u = (a * _sigmoid(a) * b).astype(BF16)
        contrib = _dot(u, wo_ref[sl, :])
        if c == 0:
            acc_ref[...] = contrib
        else:
            acc_ref[...] += contrib
    o_ref[...] = x_ref[...] + 0.5 * gt_ref[...] * acc_ref[...]


def _mod_spec(tiles_per_batch, d):
    return pl.BlockSpec((None, 1, d), lambda i: (i // tiles_per_batch, 0, 0))


def _ffn(x, g, shift, scale, gate, wa, wb, wo, s):
    n, d = x.shape
    tpb = s // TM
    return pl.pallas_call(
        _ffn_kernel,
        grid=(n // TM,),
        in_specs=[pl.BlockSpec((TM, d), lambda i: (i, 0)),
                  _resident((1, d)),
                  _mod_spec(tpb, d), _mod_spec(tpb, d), _mod_spec(tpb, d),
                  _resident(wa.shape), _resident(wb.shape), _resident(wo.shape)],
        out_specs=pl.BlockSpec((TM, d), lambda i: (i, 0)),
        out_shape=jax.ShapeDtypeStruct((n, d), F32),
        scratch_shapes=[pltpu.VMEM((TM, d), BF16), pltpu.VMEM((TM, d), F32)],
        compiler_params=_params("parallel"),
        name="ffn",
    )(x, g, shift, scale, gate, wa, wb, wo)


def _proj_kernel(meta, kinds, seq, x_ref, g_ref, sh_ref, sc_ref, w_ref, gain_ref, gm_ref, *rest):
    n_out = len(kinds)
    outs = rest[:n_out]
    y_ref, stage_ref = rest[n_out], rest[n_out + 1]
    tile_in_seq = pl.program_id(0) % (seq // TM)
    y_ref[...] = _modulated_norm(x_ref[...], g_ref[...], sh_ref[...], sc_ref[...]).astype(BF16)
    for c, halves in enumerate(meta):
        sl = slice(c * PROJ_CHUNK, (c + 1) * PROJ_CHUNK)
        z = _dot(y_ref[...], w_ref[:, sl])
        if any(normed for _, _, normed in halves):
            msq = _dot((z * z).astype(BF16), gm_ref[...])
            zn = z * lax.rsqrt(msq + RMS_EPS) * gain_ref[:, sl]
        for hf, (oi, off, normed) in enumerate(halves):
            src = (zn if normed else z)[:, hf * LANES:(hf + 1) * LANES]
            kind = kinds[oi]
            if kind == "cols":
                src = src * gain_ref[:, c * PROJ_CHUNK + hf * LANES:c * PROJ_CHUNK + (hf + 1) * LANES]
                outs[oi][off:off + LANES, :] = src.T.astype(outs[oi].dtype)
            elif kind == "rows":
                outs[oi][:, off:off + LANES] = src.astype(outs[oi].dtype)
            else:
                per = TM // kind
                stage_ref[...] = src
                for cs in range(kind):
                    dest = pl.multiple_of(cs * (seq // kind) + tile_in_seq * per, per)
                    outs[oi][pl.ds(dest, per), off:off + LANES] = (
                        stage_ref[pl.ds(cs, per, stride=kind), :].astype(outs[oi].dtype))


def _proj(x, g, shift, scale, w, gain, gm, meta, out_defs, s):
    n, d = x.shape
    tpb = s // TM
    specs = {"rows": lambda wd: pl.BlockSpec((TM, wd), lambda i: (i, 0)),
             "cols": lambda wd: pl.BlockSpec((None, wd, TM), lambda i: (i // tpb, 0, i % tpb))}
    stream = lambda wd: pl.BlockSpec((None, s, wd), lambda i: (i // tpb, 0, 0))
    shapes = {"rows": lambda wd: (n, wd), "cols": lambda wd: (n // s, wd, s)}
    kinds = tuple(k for _, _, k in out_defs)
    return pl.pallas_call(
        functools.partial(_proj_kernel, meta, kinds, s),
        grid=(n // TM,),
        in_specs=[pl.BlockSpec((TM, d), lambda i: (i, 0)),
                  _resident((1, d)),
                  _mod_spec(tpb, d), _mod_spec(tpb, d),
                  _resident(w.shape), _resident(gain.shape), _resident(gm.shape)],
        out_specs=[specs.get(k, stream)(wd) for wd, _, k in out_defs],
        out_shape=[jax.ShapeDtypeStruct(shapes.get(k, lambda wd: (n // s, s, wd))(wd), dt) for wd, dt, k in out_defs],
        scratch_shapes=[pltpu.VMEM((TM, d), BF16), pltpu.VMEM((TM, LANES), F32)],
        compiler_params=_params("arbitrary"),
        name="proj",
    )(x, g, shift, scale, w, gain, gm)


def _mla_prep_kernel(in_ref, qg_ref, kvg_ref, wuq_ref, wuk_ref, wuv_ref, gm_ref,
                     gq_ref, gkn_ref, gkr_ref, cos_ref, sin_ref, q_out, k_out, v_out):
    def rms(z, g):
        return z * lax.rsqrt(jnp.mean(z * z, axis=-1, keepdims=True) + RMS_EPS) * g

    cqn = rms(in_ref[:, 0:MLA_Q_LORA], qg_ref[...]).astype(BF16)
    ckvn = rms(in_ref[:, MLA_Q_LORA:MLA_Q_LORA + MLA_KV_LORA], kvg_ref[...]).astype(BF16)
    c3 = in_ref[:, 3 * LANES:4 * LANES]
    cos = cos_ref[...]
    sin = sin_ref[...]
    gm = gm_ref[...]
    half = MLA_ROPE // 2
    slot_lane = lax.broadcasted_iota(jnp.int32, (1, 2 * LANES), 1) & (LANES - 1)
    first_half = slot_lane < MLA_NOPE + half

    def norm_rope(z, gain):
        msq = _dot((z * z).astype(BF16), gm)
        z = z * lax.rsqrt(msq + RMS_EPS) * gain
        rot = jnp.where(first_half, -pltpu.roll(z, 2 * LANES - half, 1), pltpu.roll(z, half, 1))
        return z * cos + rot * sin

    kr = norm_rope(jnp.concatenate([c3, c3], axis=1), gkr_ref[...])
    for h in range(MLA_HEADS // 2):
        sl = slice(h * 2 * LANES, (h + 1) * 2 * LANES)
        q_out[:, sl] = norm_rope(_dot(cqn, wuq_ref[:, sl]), gq_ref[...]).astype(BF16)
        k_out[:, sl] = (norm_rope(_dot(ckvn, wuk_ref[:, sl]), gkn_ref[...]) + kr).astype(BF16)
    v_out[...] = _dot_nt(wuv_ref[...], ckvn).astype(BF16)


def _mla_prep(mla_in, qg, kvg, wuq, wuk, wuv, gm, gq, gkn, gkr, cos_t, sin_t, s):
    n = mla_in.shape[0]
    tpb = s // TM
    hw = MLA_HEADS * LANES
    vw = MLA_HEADS * MLA_V
    tab = pl.BlockSpec((TM, 2 * LANES), lambda i: (i % tpb, 0))
    consts = [qg, kvg, wuq, wuk, wuv, gm, gq, gkn, gkr]
    return pl.pallas_call(
        _mla_prep_kernel,
        grid=(n // TM,),
        in_specs=[pl.BlockSpec((TM, 4 * LANES), lambda i: (i, 0))] + [_resident(a.shape) for a in consts] + [tab, tab],
        out_specs=[pl.BlockSpec((TM, hw), lambda i: (i, 0)), pl.BlockSpec((TM, hw), lambda i: (i, 0)),
                   pl.BlockSpec((None, vw, TM), lambda i: (i // tpb, 0, i % tpb))],
        out_shape=[jax.ShapeDtypeStruct((n, hw), BF16), jax.ShapeDtypeStruct((n, hw), BF16),
                   jax.ShapeDtypeStruct((n // s, vw, s), BF16)],
        compiler_params=_params("parallel"),
        name="mla_prep",
    )(mla_in, *consts, cos_t, sin_t)


def _attn_kernel(cfg, q_ref, k_ref, vt_ref, tab_ref, *rest):
    n_tab, qw, backs, sel_cfg, want_lse, n_qt, stride = cfg
    n_qs = n_qt // stride
    if sel_cfg is not None:
        sel_ref, rest = rest[0], rest[1:]
        sel_bpt, sel_stride, sel_pair_mul = sel_cfg
    o_ref = rest[0]
    lse_ref = rest[1] if want_lse else None
    st_scr, p_scr, acc_fin, m_fin = rest[-6:-4], rest[-4:-2], rest[-2], rest[-1]
    blk = pl.program_id(1)
    lane = lax.broadcasted_iota(jnp.int32, (1, LANES), 1)

    counts = tuple(sum(min(t % n_qs, bk) + 1 for t in range(n_qt)) for bk in backs)
    back = jnp.int32(backs[-1])
    n_tiles = jnp.int32(counts[-1])
    for bi in range(len(backs) - 2, -1, -1):
        back = jnp.where(blk == bi, jnp.int32(backs[bi]), back)
        n_tiles = jnp.where(blk == bi, jnp.int32(counts[bi]), n_tiles)

    ones_rows = jnp.ones((DEN_ROWS, TK), BF16)

    def first_key_tile(qi):
        return qi - jnp.minimum(qi & (n_qs - 1), back)

    def rows(t, size):
        return pl.ds(t * size if isinstance(t, int) else pl.multiple_of(t * size, size), size)

    def out_rows(t, size):
        if stride == 1:
            return rows(t, size)
        return pl.ds((t >> int(math.log2(n_qs))) + stride * size * (t & (n_qs - 1)), size, stride=stride)

    def logits_to(slot, qi, j, filler):
        d = jnp.where(filler, n_tab, jnp.minimum(qi - j, n_tab - 1))
        if qw == 1:
            q = q_ref[rows(qi, TQ), :]
            zero = jnp.zeros_like(q)
            qs = [jnp.where(lane < HEAD_DIM, q, zero), jnp.where(lane >= HEAD_DIM, q, zero)]
        else:
            qs = [q_ref[rows(qi, TQ), s * LANES:(s + 1) * LANES] for s in range(2)]
        for s in range(2):
            kj = k_ref[rows(j, TK), :] if qw == 1 else k_ref[rows(j, TK), s * LANES:(s + 1) * LANES]
            st = _dot_nt(kj, qs[s])
            if sel_cfg is not None:
                off = sel_stride * (s + sel_pair_mul * blk) + sel_bpt * j
                kb = TK // sel_bpt
                qcol = pl.ds(pl.multiple_of(qi * TQ, TQ), TQ)
                st = jnp.concatenate(
                    [st[bk * kb:(bk + 1) * kb] + sel_ref[pl.ds(off + bk, 1), qcol]
                     for bk in range(sel_bpt)], axis=0)
            st_scr[slot][s] = st + tab_ref[s, d]

    def accumulate(slot, j, alphas, accs):
        krow = pl.multiple_of(j * TK, TK)
        new = []
        for s in range(2):
            vt = jnp.concatenate([vt_ref[s * HEAD_DIM:(s + 1) * HEAD_DIM, pl.ds(krow, TK)], ones_rows], axis=0)
            new.append(alphas[s] * accs[s] + _dot(vt, p_scr[slot][s]))
        return tuple(new)

    def finalize(qi, ms, accs):
        for s in range(2):
            acc_fin[qi, s] = accs[s]
            if want_lse:
                m_fin[qi, s] = jnp.broadcast_to(ms[s], m_fin.shape[2:])

    def write_out(t):
        accs = [acc_fin[t, s] for s in range(2)]
        dens = [acc[HEAD_DIM:HEAD_DIM + 1] for acc in accs]
        out_t = jnp.concatenate([acc[:HEAD_DIM] / l for acc, l in zip(accs, dens)], axis=0)
        o_ref[out_rows(t, TQ), :] = out_t.T.astype(o_ref.dtype)
        if want_lse:
            lse_t = jnp.concatenate([jnp.broadcast_to((m_fin[t, s][0:1] + jnp.log2(dens[s])) * LN2, (HEAD_DIM, TQ))
                                     for s in range(2)], axis=0)
            lse_ref[out_rows(t, TQ), :] = lse_t.T

    def advance(q, j):
        last = j == q
        at_end = jnp.logical_and(last, q == n_qt - 1)
        starts = jnp.logical_and(last, jnp.logical_not(at_end))
        qn = jnp.where(starts, q + 1, q)
        jn = jnp.where(at_end, j, jnp.where(last, first_key_tile(q + 1), j + 1))
        return qn, jn, starts, at_end

    def softmax(slot, tile, stats):
        is_first = tile[2]
        keep = jnp.where(is_first, 0.0, 1.0)
        new_stats, alphas = [], []
        for s in range(2):
            m = jnp.where(is_first, NEG_INF, stats[s])
            st = st_scr[slot][s]
            m_new = jnp.maximum(m, jnp.max(st, axis=0, keepdims=True))
            alphas.append(jnp.exp2(m - m_new) * keep)
            new_stats.append(m_new)
            p_scr[slot][s] = jnp.exp2(st - m_new).astype(BF16)
        return tuple(new_stats), tuple(alphas)

    def body(u, carry):
        tile_a, (q2, j2), (q1, j1, first1), stats2, stats1, alphas2, alphas1, accs = carry
        qa, ja, first_a, _ = tile_a
        tile_b = advance(qa, ja)
        qb, jb, first_b, _ = tile_b
        tile_c = advance(qb, jb)
        logits_to(1, qb, jb, tile_b[3])
        accs_x = accumulate(0, j2, alphas2, accs)
        accs_y = accumulate(1, j1, alphas1, accs_x)
        stats_a, alphas_a = softmax(0, tile_a, stats1)
        logits_to(0, tile_c[0], tile_c[1], tile_c[3])
        stats_b, alphas_b = softmax(1, tile_b, stats_a)

        @pl.when(jnp.logical_and(first1, u > 0))
        def _():
            finalize(q2, stats2, accs_x)

        @pl.when(jnp.logical_and(first_a, u > 0))
        def _():
            finalize(q1, stats1, accs_y)

        return tile_c, (qa, ja), (qb, jb, first_b), stats_a, stats_b, alphas_a, alphas_b, accs_y

    zero_i = jnp.int32(0)
    logits_to(0, zero_i, zero_i, zero_i != 0)
    for p_slot in p_scr:
        p_slot[...] = jnp.zeros(p_slot.shape, BF16)
    stats0 = tuple(jnp.full((1, TQ), NEG_INF, F32) for _ in range(2))
    ones = tuple(jnp.ones((1, TQ), F32) for _ in range(2))
    init = ((zero_i, zero_i, zero_i == 0, zero_i != 0), (zero_i, zero_i), (zero_i, zero_i, zero_i != 0),
            stats0, stats0, ones, ones, tuple(jnp.zeros((HEAD_DIM + DEN_ROWS, TQ), F32) for _ in range(2)))
    _, (q2, j2), (q1, j1, first1), stats2, stats1, alphas2, alphas1, accs = lax.fori_loop(
        0, (n_tiles + 1) // 2, body, init)
    accs_x = accumulate(0, j2, alphas2, accs)

    @pl.when(first1)
    def _():
        finalize(q2, stats2, accs_x)

    finalize(q1, stats1, accumulate(1, j1, alphas1, accs_x))
    for t in range(n_qt):
        write_out(t)


def _attn(q, k, vt, tab, sel, *, qc0, kc0, vb0, n_blk, qw, backs=None, kv_shared=False,
          tab_shared=False, sel_cfg=None, out_dtype=F32, want_lse=False, stride=1):
    b, s = q.shape[:2]
    n_tab = tab.shape[2] - 1
    n_qt = s // TQ
    if backs is None:
        backs = (n_qt,)
    cfg = (n_tab, qw, backs, sel_cfg, want_lse, n_qt, stride)
    kidx = (lambda bb, h: (bb, 0, kc0)) if kv_shared else (lambda bb, h: (bb, 0, kc0 + h))
    vidx = (lambda bb, h: (bb, vb0, 0)) if kv_shared else (lambda bb, h: (bb, vb0 + h, 0))
    tidx = (lambda bb, h: (0, 0, 0, 0, 0)) if tab_shared else (lambda bb, h: (h, 0, 0, 0, 0))
    in_specs = [pl.BlockSpec((None, s, qw * LANES), lambda bb, h: (bb, 0, qc0 + h)),
                pl.BlockSpec((None, s, qw * LANES), kidx),
                pl.BlockSpec((None, 2 * HEAD_DIM, s), vidx),
                pl.BlockSpec((None, 2, n_tab + 1, TK, TQ), tidx)]
    args = [q, k, vt, tab]
    if sel_cfg is not None:
        in_specs.append(pl.BlockSpec((None, LANES, s), lambda bb, h: (bb, 0, 0)))
        args.append(sel)
    ospec = pl.BlockSpec((None, s, LANES), lambda bb, h: (bb, 0, h))
    out_specs = [ospec]
    out_shape = [jax.ShapeDtypeStruct((b, s, n_blk * LANES), out_dtype)]
    if want_lse:
        out_specs.append(ospec)
        out_shape.append(jax.ShapeDtypeStruct((b, s, n_blk * LANES), F32))
    return pl.pallas_call(
        functools.partial(_attn_kernel, cfg),
        grid=(b, n_blk),
        in_specs=in_specs,
        out_specs=out_specs,
        out_shape=out_shape,
        scratch_shapes=([pltpu.VMEM((2, TK, TQ), F32)] * 2 + [pltpu.VMEM((2, TK, TQ), BF16)] * 2
                        + [pltpu.VMEM((n_qt, 2, HEAD_DIM + DEN_ROWS, TQ), F32), pltpu.VMEM((n_qt, 2, 8, TQ), F32)]),
        compiler_params=_params("parallel", "parallel"),
        name="attn",
    )(*args)


def _nsa_cmp_kernel(kc_ref, vc_ref, pe_ref, wlo_ref, whi_ref, w2_ref, gm_ref, gain_ref, kcmp_ref, vcmpt_ref):
    nch = kcmp_ref.shape[0]

    def hidden(c_ref, j):
        lo = hi = None
        for l in range(NSA_CMP_STRIDE):
            rows = c_ref[pl.ds(l, nch, stride=NSA_CMP_STRIDE), :]
            sl = slice(l * LANES, (l + 1) * LANES)
            t_lo = _dot((rows + pe_ref[j, 0][:, sl]).astype(BF16), wlo_ref[j][sl, :])
            t_hi = _dot((rows + pe_ref[j, 1][:, sl]).astype(BF16), whi_ref[j][sl, :])
            lo = t_lo if lo is None else lo + t_lo
            hi = t_hi if hi is None else hi + t_hi
        h = lo + pltpu.roll(hi, nch - 1, 0)
        return (h * _sigmoid(h)).astype(BF16)

    kz = _dot_nt(hidden(kc_ref, 0), w2_ref[0])
    msq = _dot_hilo(kz * kz, gm_ref[...])
    kcmp_ref[...] = (kz * lax.rsqrt(msq + RMS_EPS) * gain_ref[...]).astype(BF16)
    vcmpt_ref[...] = _dot_nt(w2_ref[1], hidden(vc_ref, 1)).astype(BF16)


def _nsa_cmp(kc3, vc3, pe, wlo, whi, w2, gm, gain):
    b, s, width = kc3.shape
    nch = s // NSA_CMP_STRIDE
    consts = [pe, wlo, whi, w2, gm, gain]
    blk = pl.BlockSpec((None, s, width), lambda i: (i, 0, 0))
    oblk = pl.BlockSpec((None, nch, LANES), lambda i: (i, 0, 0))
    return pl.pallas_call(
        _nsa_cmp_kernel,
        grid=(b,),
        in_specs=[blk, blk] + [_resident(a.shape) for a in consts],
        out_specs=[oblk, oblk],
        out_shape=[jax.ShapeDtypeStruct((b, nch, LANES), BF16)] * 2,
        compiler_params=_params("parallel"),
        name="nsa_cmp",
    )(kc3, vc3, *consts)


def _rank_keep(score, ids, top):
    cnt = jnp.zeros(score.shape, jnp.int32)
    for mp in range(score.shape[0]):
        other = score[mp:mp + 1, :]
        tie = jnp.where(mp < ids, 1, 0)
        cnt = cnt + jnp.where(other > score, 1, jnp.where(other == score, tie, 0))
    return cnt < top


def _nsa_sel_kernel(q_ref, kcmp_ref, vcmpt_ref, bias_ref, ovl_ref, oc_ref, sel_ref):
    qi = pl.program_id(1)
    lane = lax.broadcasted_iota(jnp.int32, (1, LANES), 1)
    row = lax.broadcasted_iota(jnp.int32, (LANES, 1), 0)
    t = qi * TQ + lax.broadcasted_iota(jnp.int32, (1, TQ), 1)
    mask_c = (NSA_CMP_STRIDE * row + NSA_CMP_LEN - 1) <= t
    kcmp = kcmp_ref[...]
    heads = [(p, g) for p in range(NSA_HPG) for g in range(NSA_GROUPS)]
    raw = []
    for p, g in heads:
        qp = q_ref[:, p * LANES:(p + 1) * LANES]
        raw.append(_dot_nt(kcmp, jnp.where((lane >> 6) == g, qp, jnp.zeros_like(qp))))
    pcs = []
    for (p, g), r in zip(heads, raw):
        lg = jnp.where(mask_c, r * LN2 + bias_ref[p * NSA_GROUPS + g], NEG_INF)
        m = jnp.max(lg, axis=0, keepdims=True)
        e = jnp.where(mask_c, jnp.exp(lg - m), 0.0)
        den = jnp.maximum(jnp.sum(e, axis=0, keepdims=True), 1e-30)
        pcs.append(e / den)
    imp = jnp.zeros((LANES, TQ), F32)
    ocs = []
    for (p, g), pc in zip(heads, pcs):
        hi, lo = _split(pc)
        ocs.append(_dot(vcmpt_ref[g * HEAD_DIM:(g + 1) * HEAD_DIM, :], hi))
        imp = imp + _dot(ovl_ref[g], hi) + _dot(ovl_ref[g], lo)
    for p in range(NSA_HPG):
        oc_ref[:, p * LANES:(p + 1) * LANES] = jnp.concatenate([ocs[2 * p], ocs[2 * p + 1]], axis=0).T.astype(BF16)

    n_slc = 32
    ids = row[0:n_slc]
    cur = t >> 6
    forced = (ids == 0) | (ids == cur) | (ids == cur - 1)
    masks = []
    for g in range(NSA_GROUPS):
        score = jnp.where(forced, NSA_FORCED, jnp.where(ids <= cur, imp[g * n_slc:(g + 1) * n_slc], NEG_INF))
        keep = _rank_keep(score, ids, NSA_SLC_TOP) & (score > 0.5 * NEG_INF)
        masks.append(jnp.where(keep, 0.0, NEG_INF))
    masks.append(jnp.full((LANES - NSA_GROUPS * n_slc, TQ), NEG_INF, F32))
    sel_ref[...] = jnp.concatenate(masks, axis=0)


def _nsa_sel(q, kcmp, vcmpt, bias_c, ovl):
    b, s, w = q.shape
    return pl.pallas_call(
        _nsa_sel_kernel,
        grid=(b, s // TQ),
        in_specs=[pl.BlockSpec((None, TQ, w), lambda bb, i: (bb, i, 0)),
                  pl.BlockSpec((None, LANES, LANES), lambda bb, i: (bb, 0, 0)),
                  pl.BlockSpec((None, LANES, LANES), lambda bb, i: (bb, 0, 0)),
                  pl.BlockSpec((NSA_HEADS, LANES, TQ), lambda bb, i: (0, 0, i)),
                  _resident(ovl.shape)],
        out_specs=[pl.BlockSpec((None, TQ, w), lambda bb, i: (bb, i, 0)),
                   pl.BlockSpec((None, LANES, TQ), lambda bb, i: (bb, 0, i))],
        out_shape=[jax.ShapeDtypeStruct((b, s, w), BF16), jax.ShapeDtypeStruct((b, LANES, s), F32)],
        compiler_params=_params("parallel", "parallel"),
        name="nsa_sel",
    )(q, kcmp, vcmpt, bias_c, ovl)


def _moba_gate_kernel(q_ref, k_ref, avg_ref, sel_ref, kh_ref, kl_ref):
    qi = pl.program_id(1)
    nb = 8

    @pl.when(qi == 0)
    def _():
        kmean = _dot(avg_ref[...], k_ref[...])
        r2 = lax.broadcasted_iota(jnp.int32, kmean.shape, 0)
        c2 = lax.broadcasted_iota(jnp.int32, kmean.shape, 1)
        kh_ref[...], kl_ref[...] = _split(jnp.where((r2 >> 3) == (c2 >> 6), kmean, 0.0))

    q = q_ref[...]
    gate = _dot_nt(kh_ref[...], q) + _dot_nt(kl_ref[...], q)
    ids = lax.broadcasted_iota(jnp.int32, (nb, 1), 0)
    past = ids < qi
    masks = []
    for h in range(MOBA_HEADS):
        score = jnp.where(past, gate[h * nb:(h + 1) * nb], NEG_INF)
        keep = (_rank_keep(score, ids, MOBA_TOP) & past) | (ids == qi)
        masks.append(jnp.where(keep, 0.0, NEG_INF))
    masks.append(jnp.full((LANES - MOBA_HEADS * nb, TQ), NEG_INF, F32))
    sel_ref[...] = jnp.concatenate(masks, axis=0)


def _moba_gate(p_arr, avg, qc, kc):
    b, s, _ = p_arr.shape
    w = MOBA_HEADS * HEAD_DIM
    return pl.pallas_call(
        _moba_gate_kernel,
        grid=(b, s // TQ),
        in_specs=[pl.BlockSpec((None, TQ, w), lambda bb, i: (bb, i, qc)),
                  pl.BlockSpec((None, s, w), lambda bb, i: (bb, 0, kc)),
                  _resident(avg.shape)],
        out_specs=pl.BlockSpec((None, LANES, TQ), lambda bb, i: (bb, 0, i)),
        out_shape=jax.ShapeDtypeStruct((b, LANES, s), F32),
        scratch_shapes=[pltpu.VMEM((LANES, w), BF16)] * 2,
        compiler_params=_params("parallel", "arbitrary"),
        name="moba_gate",
    )(p_arr, p_arr, avg)


def _even_out_kernel(x_ref, oa_ref, oc_ref, os_ref, ow_ref, gl_ref, eg_ref, wa_ref, wb_ref, gt_ref, o_ref):
    sg = _sigmoid(gl_ref[...])
    hi, lo = _split(sg)
    nsa = None
    for br, src in enumerate((oc_ref, os_ref, ow_ref)):
        gexp = _dot(hi, eg_ref[br]) + _dot(lo, eg_ref[br])
        term = gexp * src[...]
        nsa = term if nsa is None else nsa + term
    m = _dot(oa_ref[...], wa_ref[...]) + _dot(nsa.astype(BF16), wb_ref[...])
    o_ref[...] = x_ref[...] + gt_ref[...] * m


def _even_out(x, o_a, o_c, o_s, o_w, mla_in, eg, wa, wb, gate, s):
    n, d = x.shape
    tpb = s // TM
    row = lambda wd: pl.BlockSpec((TM, wd), lambda i: (i, 0))
    return pl.pallas_call(
        _even_out_kernel,
        grid=(n // TM,),
        in_specs=[row(d), row(o_a.shape[1]), row(o_c.shape[1]), row(o_s.shape[1]), row(o_w.shape[1]),
                  pl.BlockSpec((TM, LANES), lambda i: (i, 3)),
                  _resident(eg.shape), _resident(wa.shape), _resident(wb.shape), _mod_spec(tpb, d)],
        out_specs=row(d),
        out_shape=jax.ShapeDtypeStruct((n, d), F32),
        compiler_params=_params("parallel"),
        name="even_out",
    )(x, o_a, o_c, o_s, o_w, mla_in, eg, wa, wb, gate)


def _odd_out_kernel(x_ref, od0_ref, od1_ref, od2_ref, ls0_ref, ls1_ref, ls2_ref, om_ref, wd_ref, wm_ref, gt_ref, o_ref):
    ods = (od0_ref, od1_ref, od2_ref)
    ls = [r[...] for r in (ls0_ref, ls1_ref, ls2_ref)]
    mx = jnp.maximum(jnp.maximum(ls[0], ls[1]), ls[2])
    es = [jnp.exp(l - mx) for l in ls]
    den = es[0] + es[1] + es[2]
    merged = None
    for g in range(len(DIL_PAIRS)):
        term = (es[g] / den) * ods[g][...]
        merged = term if merged is None else merged + term
    m = _dot(merged.astype(BF16), wd_ref[...]) + _dot(om_ref[...], wm_ref[...])
    o_ref[...] = x_ref[...] + gt_ref[...] * m


def _odd_out(x, o_ds, lse_ds, o_m, wd, wm, gate, s):
    n, d = x.shape
    tpb = s // TM
    row = lambda wd_: pl.BlockSpec((TM, wd_), lambda i: (i, 0))
    return pl.pallas_call(
        _odd_out_kernel,
        grid=(n // TM,),
        in_specs=[row(d)] + [row(a.shape[1]) for a in (*o_ds, *lse_ds)] + [row(o_m.shape[1]),
                  _resident(wd.shape), _resident(wm.shape), _mod_spec(tpb, d)],
        out_specs=row(d),
        out_shape=jax.ShapeDtypeStruct((n, d), F32),
        compiler_params=_params("parallel"),
        name="odd_out",
    )(x, *o_ds, *lse_ds, o_m, wd, wm, gate)


def _t5_bucket(dist):
    n = jnp.maximum(jnp.asarray(dist, jnp.int32), 0)
    nf = jnp.maximum(n, 1).astype(F32)
    large = T5_MAX_EXACT + (jnp.log(nf / T5_MAX_EXACT) / math.log(T5_MAX_DIST / T5_MAX_EXACT)
                            * (NUM_BUCKETS - T5_MAX_EXACT)).astype(jnp.int32)
    return jnp.where(n < T5_MAX_EXACT, n, jnp.minimum(large, NUM_BUCKETS - 1))


TOEP_PERIOD = 4 * TQ


def _toeplitz_dist():
    j = np.arange(TOEP_PERIOD)
    return np.where(j < 3 * TQ, j, j - TOEP_PERIOD)


def _toeplitz_kernel(n_tab, u_ref, o_ref):
    x = jnp.broadcast_to(u_ref[...], (TK, TOEP_PERIOD))
    y = pltpu.roll(x, 0, 1, stride=1, stride_axis=0)
    for dlt in range(n_tab):
        o_ref[dlt] = y[:, dlt * TQ:(dlt + 1) * TQ]
    o_ref[n_tab] = jnp.full((TK, TQ), NEG_INF * LOG2E, F32)


def _toeplitz_tiles(u, n_tab):
    h = u.shape[0]
    return pl.pallas_call(
        functools.partial(_toeplitz_kernel, n_tab),
        grid=(h,),
        in_specs=[pl.BlockSpec((None, 1, TOEP_PERIOD), lambda i: (i, 0, 0))],
        out_specs=pl.BlockSpec((None, n_tab + 1, TK, TQ), lambda i: (i, 0, 0, 0)),
        out_shape=jax.ShapeDtypeStruct((h, n_tab + 1, TK, TQ), F32),
        compiler_params=_params("parallel"),
        name="toeplitz",
    )(u.reshape(h, 1, TOEP_PERIOD))


def _cmp_bias_kernel(u_ref, o_ref):
    x = jnp.broadcast_to(u_ref[...], o_ref.shape)
    o_ref[...] = pltpu.roll(x, 0, 1, stride=NSA_CMP_STRIDE, stride_axis=0)


def _cmp_bias(u, n_rows):
    h, s = u.shape
    return pl.pallas_call(
        _cmp_bias_kernel,
        grid=(h,),
        in_specs=[pl.BlockSpec((None, 1, s), lambda i: (i, 0, 0))],
        out_specs=pl.BlockSpec((None, n_rows, s), lambda i: (i, 0, 0)),
        out_shape=jax.ShapeDtypeStruct((h, n_rows, s), F32),
        compiler_params=_params("parallel"),
        name="cmp_bias",
    )(u.reshape(h, 1, s))


def _bias_tiles(t5_cols, ok, n_tab=3, dist_scale=1):
    dist = _toeplitz_dist()
    bias = jnp.transpose(t5_cols[_t5_bucket(dist * dist_scale)])
    u = jnp.where(jnp.asarray(ok)[None], bias, NEG_INF) * LOG2E
    return _toeplitz_tiles(u, n_tab)


def _pair(tiles):
    h = tiles.shape[0]
    return tiles.reshape(h // 2, 2, *tiles.shape[1:])


def _group_mean_np(sizes, width=LANES):
    gm = np.zeros((width, width), np.float32)
    o = 0
    for sz in sizes:
        gm[o:o + sz, o:o + sz] = 1.0 / sz
        o += sz
    return gm


def _group_mean_matrix(sizes, width=LANES):
    return jnp.asarray(_group_mean_np(sizes, width), BF16)


EV_META = ((0, 0, False), (0, 128, False), (0, 256, False), (0, 384, False),
           (1, 0, True), (1, 128, True), (1, 256, True), (1, 384, True),
           (2, 0, False), (3, 0, False), (4, 0, True), (5, 0, False), (6, 0, True), (7, 0, False))
EV_OUTS = [(512, F32, "rows"), (512, BF16, "rows"), (128, F32, "rows"), (128, F32, "rows"), (128, BF16, "rows"),
           (128, BF16, "cols"), (128, BF16, "rows"), (128, BF16, "cols")]
OD_DIL_STRIDES = (1, 4, 8)
OD_META = (((0, 0, True), (0, LANES, True)) + tuple((3 + g, h * LANES, True) for g in range(2) for h in range(2))
           + ((0, 2 * LANES, True), (0, 3 * LANES, True))
           + tuple((3 + g, (2 + h) * LANES, True) for g in range(2) for h in range(2))
           + ((1, 0, False), (1, LANES, False)) + tuple((0, (4 + c) * LANES, False) for c in range(4))
           + tuple((0, (8 + c) * LANES, True) for c in range(4)) + ((2, 0, False), (2, LANES, False)))
OD_OUTS = [(12 * LANES, BF16, "rows"), (2 * LANES, BF16, "cols"), (2 * LANES, BF16, "cols"),
           (4 * LANES, BF16, OD_DIL_STRIDES[1]), (4 * LANES, BF16, OD_DIL_STRIDES[2])]


def _even_w_in(w):
    jn, d, _ = w.shape
    z = lambda n_: jnp.zeros((jn, d, n_), w.dtype)
    nq = w[:, :, 416:928].reshape(jn, d, NSA_GROUPS, NSA_HPG, HEAD_DIM)
    nq = jnp.transpose(nq, (0, 1, 3, 2, 4)).reshape(jn, d, NSA_HEADS * HEAD_DIM)
    chunk3 = jnp.concatenate([w[:, :, 1696:1720], z(HEAD_DIM - 24), w[:, :, 384:416], z(LANES - 96)], axis=-1)
    return jnp.concatenate([w[:, :, 0:384], chunk3, nq, w[:, :, 928:1696]], axis=-1)


def kernel(x, c, t5_bias, ada_w, ada_b, norm_g, ffn_w_in, ffn_w_out, ev_w_in, ev_w_out, mla_q_norm_g,
           mla_kv_norm_g, mla_w_uq, mla_w_ukv, mla_qk_g, nsa_cmp_pe, nsa_cmp_w1, nsa_cmp_w2, nsa_qk_g,
           od_w_in, od_w_out, dil_qk_g, moba_qk_g):
    b, s, d = x.shape
    assert (s, d) == (2048, D_MODEL) and s % TM == 0 and TQ == MOBA_BLOCK and TQ == TK
    n = b * s
    hd = HEAD_DIM
    n_even = ev_w_in.shape[0]
    n_odd = od_w_in.shape[0]
    c64 = hd ** -0.5 * LOG2E
    c96 = (MLA_NOPE + MLA_ROPE) ** -0.5 * LOG2E

    mod = _ada(c, ada_w, ada_b).reshape(DEPTH, b, 3, 3, 1, d)

    dist = _toeplitz_dist()
    causal = dist >= 0
    gm64 = _group_mean_matrix((hd, hd))
    gm_proj = _group_mean_matrix((hd,) * (PROJ_CHUNK // hd), PROJ_CHUNK)
    per_chunk = PROJ_CHUNK // LANES
    chunked = lambda m: tuple(tuple(m[i:i + per_chunk]) for i in range(0, len(m), per_chunk))

    padc = FF_PAD - D_FF
    wa_all = jnp.pad(ffn_w_in[..., :D_FF], ((0, 0), (0, 0), (0, 0), (0, padc))).astype(BF16)
    wb_all = jnp.pad(ffn_w_in[..., D_FF:], ((0, 0), (0, 0), (0, 0), (0, padc))).astype(BF16)
    wo_all = jnp.pad(ffn_w_out, ((0, 0), (0, 0), (0, padc), (0, 0))).astype(BF16)

    nsa_tab = t5_bias[:, MLA_HEADS:MLA_HEADS + NSA_HEADS].reshape(NUM_BUCKETS, NSA_GROUPS, NSA_HPG)
    nsa_cols = jnp.transpose(nsa_tab, (0, 2, 1)).reshape(NUM_BUCKETS, NSA_HEADS)
    tab_sel = _pair(_bias_tiles(nsa_cols, causal))
    tab_win = _pair(_bias_tiles(nsa_cols, causal & (dist <= NSA_WINDOW - 1)))
    tab_mla = _toeplitz_tiles(jnp.where(jnp.asarray(causal), 0.0, NEG_INF).astype(F32)[None], 2)
    tab_mla = jnp.broadcast_to(tab_mla[None], (1, 2) + tab_mla.shape[1:])
    n_cmp_pad = s // NSA_CMP_STRIDE
    bias_c = _cmp_bias(jnp.transpose(nsa_cols[_t5_bucket(np.arange(s) - (NSA_CMP_LEN - 1))]), n_cmp_pad)
    n_cmp = (s - NSA_CMP_LEN) // NSA_CMP_STRIDE + 1
    cstart = np.arange(n_cmp) * NSA_CMP_STRIDE
    sstart = np.arange(s // NSA_SLC_BLOCK) * NSA_SLC_BLOCK
    overlap = np.clip(np.minimum(cstart[:, None] + NSA_CMP_LEN, sstart[None, :] + NSA_SLC_BLOCK)
                      - np.maximum(cstart[:, None], sstart[None, :]), 0, None).astype(np.float32) / NSA_CMP_LEN
    ovl = np.zeros((NSA_GROUPS, LANES, LANES), np.float32)
    for g in range(NSA_GROUPS):
        ovl[g, 32 * g:32 * g + 32, :n_cmp] = overlap.T
    ovl = jnp.asarray(ovl, BF16)
    eg = np.zeros((3, LANES, NSA_HEADS * hd), np.float32)
    for g in range(NSA_GROUPS):
        for p in range(NSA_HPG):
            for br in range(3):
                eg[br, (g * NSA_HPG + p) * 3 + br, p * LANES + g * hd:p * LANES + (g + 1) * hd] = 1.0
    eg = jnp.asarray(eg, BF16)
    gm_mla = jnp.asarray(np.kron(np.eye(2, dtype=np.float32), _group_mean_np((MLA_NOPE, MLA_ROPE))), BF16)
    inv = ROPE_THETA ** (-jnp.arange(0, MLA_ROPE, 2, dtype=F32) / MLA_ROPE)
    ang = jnp.arange(s, dtype=F32)[:, None] * inv[None, :]
    ones = jnp.ones((s, MLA_NOPE), F32)
    tail = LANES - MLA_NOPE - MLA_ROPE
    cos_t = jnp.tile(jnp.concatenate([ones, jnp.cos(ang), jnp.cos(ang), jnp.ones((s, tail), F32)], axis=1), (1, 2))
    sin_t = jnp.tile(jnp.concatenate([0 * ones, jnp.sin(ang), jnp.sin(ang), jnp.zeros((s, tail), F32)], axis=1), (1, 2))

    ev_w = _even_w_in(ev_w_in).astype(BF16)
    ev_gain = jnp.ones((n_even, 14 * LANES), F32)
    ev_gain = ev_gain.at[:, 512:1024].set(jnp.tile(nsa_qk_g[:, 0], (1, 8)) * c64)
    ev_gain = ev_gain.at[:, 1280:1408].set(jnp.tile(nsa_qk_g[:, 1], (1, 2)))
    ev_gain = ev_gain.at[:, 1536:1664].set(jnp.tile(nsa_qk_g[:, 1], (1, 2)))
    gain_kc = jnp.tile(nsa_qk_g[:, 1], (1, 2))
    wuq = jnp.pad(mla_w_uq.reshape(n_even, MLA_Q_LORA, MLA_HEADS, MLA_NOPE + MLA_ROPE),
                  ((0, 0), (0, 0), (0, 0), (0, tail))).reshape(n_even, MLA_Q_LORA, MLA_HEADS * LANES).astype(BF16)
    ukv = mla_w_ukv.reshape(n_even, MLA_KV_LORA, MLA_HEADS, MLA_NOPE + MLA_V)
    wuk = jnp.pad(ukv[..., :MLA_NOPE], ((0, 0), (0, 0), (0, 0), (0, LANES - MLA_NOPE))
                  ).reshape(n_even, MLA_KV_LORA, MLA_HEADS * LANES).astype(BF16)
    wuv = jnp.swapaxes(ukv[..., MLA_NOPE:].reshape(n_even, MLA_KV_LORA, MLA_HEADS * MLA_V), 1, 2).astype(BF16)
    zt = jnp.zeros((n_even, tail), F32)
    gq = jnp.tile(jnp.concatenate([mla_qk_g[:, 0] * c96, zt], axis=1), (1, 2))
    gkn = jnp.tile(jnp.concatenate([mla_qk_g[:, 1, :MLA_NOPE], jnp.zeros((n_even, LANES - MLA_NOPE), F32)], axis=1), (1, 2))
    gkr = jnp.tile(jnp.concatenate([jnp.zeros((n_even, MLA_NOPE), F32), mla_qk_g[:, 1, MLA_NOPE:], zt], axis=1), (1, 2))
    pe2 = jnp.broadcast_to(nsa_cmp_pe.reshape(n_even, 2, 2, 16, 1, hd), (n_even, 2, 2, 16, NSA_GROUPS, hd)
                           ).reshape(n_even, 2, 2, 1, 16 * LANES)
    eye = jnp.eye(NSA_GROUPS, dtype=F32)
    w1 = nsa_cmp_w1.reshape(n_even, 2, 2, 16, hd, NSA_CMP_HID)
    w1x = jnp.einsum('ijaldc,gh->ijalgdhc', w1, eye).reshape(n_even, 2, 2, 16 * LANES, NSA_GROUPS * NSA_CMP_HID).astype(BF16)
    w2x = jnp.einsum('ijcd,gh->ijhdgc', nsa_cmp_w2, eye).reshape(n_even, 2, LANES, NSA_GROUPS * NSA_CMP_HID).astype(BF16)
    wa_o = ev_w_out[:, :MLA_HEADS * MLA_V].astype(BF16)
    wb_o = jnp.transpose(ev_w_out[:, MLA_HEADS * MLA_V:].reshape(n_even, NSA_GROUPS, NSA_HPG, hd, d),
                         (0, 2, 1, 3, 4)).reshape(n_even, NSA_HEADS * hd, d).astype(BF16)

    assert DIL_PAIRS == ((128, 1), (512, 4), (2048, 16))
    dil_cfg = tuple(zip(OD_DIL_STRIDES, (1, 1, 0)))
    dil_ok = (causal & (dist <= 128), causal & (dist <= 128), causal & (dist % 2 == 0))
    tab_dil = [_pair(_bias_tiles(t5_bias[:, gi * DIL_HPG:(gi + 1) * DIL_HPG], dil_ok[gi], dist_scale=dil_cfg[gi][0]))
               for gi in range(len(DIL_PAIRS))]
    tab_moba = _pair(_bias_tiles(t5_bias[:, DIL_SLOTS:DIL_SLOTS + MOBA_HEADS], causal))
    avg = np.zeros((LANES, s), np.float32)
    for h in range(MOBA_HEADS):
        for m in range(s // MOBA_BLOCK):
            avg[8 * h + m, m * MOBA_BLOCK:(m + 1) * MOBA_BLOCK] = 1.0 / MOBA_BLOCK
    avg = jnp.asarray(avg, BF16)
    od_w = od_w_in.astype(BF16)
    od_gain = jnp.concatenate([jnp.tile(dil_qk_g[:, 0], (1, 12)) * c64, jnp.tile(dil_qk_g[:, 1], (1, 12)),
                               jnp.ones((n_odd, 768), F32), jnp.tile(moba_qk_g[:, 0], (1, 4)) * c64,
                               jnp.tile(moba_qk_g[:, 1], (1, 4)), jnp.ones((n_odd, 256), F32)], axis=1)
    wd_o = od_w_out[:, :DIL_HPG * hd].astype(BF16)
    wm_o = od_w_out[:, DIL_HPG * hd:].astype(BF16)

    sh3 = lambda a: a.reshape(b, s, a.shape[-1])
    tr3 = lambda a: jnp.swapaxes(sh3(a), 1, 2)
    xf = x.reshape(n, d)
    for i in range(DEPTH):
        j = i // 2
        g_i = norm_g[i].reshape(3, 1, d)
        xf = _ffn(xf, g_i[0], mod[i, :, 0, 0], mod[i, :, 0, 1], mod[i, :, 0, 2],
                  wa_all[i, 0], wb_all[i, 0], wo_all[i, 0], s)
        if i % 2 == 0:
            mla_in, nsa_q, kc, vc, ks, vs, kw, vw = _proj(
                xf, g_i[1], mod[i, :, 1, 0], mod[i, :, 1, 1], ev_w[j], ev_gain[j][None], gm_proj,
                chunked(EV_META), EV_OUTS, s)
            qf, kf, vf = _mla_prep(mla_in, mla_q_norm_g[j][None], mla_kv_norm_g[j][None], wuq[j], wuk[j], wuv[j],
                                   gm_mla, gq[j][None], gkn[j][None], gkr[j][None], cos_t, sin_t, s)
            (o_a,) = _attn(sh3(qf), sh3(kf), vf, tab_mla, None, qc0=0, kc0=0, vb0=0, n_blk=MLA_HEADS // 2, qw=2,
                           tab_shared=True, out_dtype=BF16)
            kcmp, vcmpt = _nsa_cmp(sh3(kc), sh3(vc), pe2[j], w1x[j, :, 0], w1x[j, :, 1], w2x[j], gm64, gain_kc[j][None])
            o_c, sel = _nsa_sel(sh3(nsa_q), kcmp, vcmpt, bias_c, ovl)
            (o_s,) = _attn(sh3(nsa_q), sh3(ks), vs, tab_sel, sel, qc0=0, kc0=0, vb0=0, n_blk=NSA_HPG, qw=1,
                           kv_shared=True, sel_cfg=(TK // NSA_SLC_BLOCK, 32, 0), out_dtype=BF16)
            (o_w,) = _attn(sh3(nsa_q), sh3(kw), vw, tab_win, None, qc0=0, kc0=0, vb0=0, n_blk=NSA_HPG, qw=1,
                           kv_shared=True, backs=(2, 2, 2, 2), out_dtype=BF16)
            xf = _even_out(xf, o_a.reshape(n, -1), o_c.reshape(n, -1), o_s.reshape(n, -1), o_w.reshape(n, -1),
                           mla_in, eg, wa_o[j], wb_o[j], mod[i, :, 1, 2], s)
        else:
            pr, vd0t, vmt, qk1, qk2 = _proj(xf, g_i[1], mod[i, :, 1, 0], mod[i, :, 1, 1], od_w[j], od_gain[j][None],
                                            gm_proj, chunked(OD_META), OD_OUTS, s)
            pr3 = pr.reshape(b, s, pr.shape[-1])
            o_ds, lse_ds = [], []
            for gi, (r, bk) in enumerate(dil_cfg):
                if r == 1:
                    qk, vdt = pr3, vd0t
                else:
                    v = pr3[:, :, (2 + 2 * gi) * LANES:(4 + 2 * gi) * LANES].reshape(b, s // r, r, 2 * LANES)
                    qk, vdt = (qk1, qk2)[gi - 1], jnp.transpose(v, (0, 3, 2, 1)).reshape(b, 2 * LANES, s)
                o_g, lse_g = _attn(qk, qk, vdt, tab_dil[gi], None, qc0=0, kc0=2, vb0=0, n_blk=2,
                                   qw=1, backs=(bk,), want_lse=True, stride=r)
                o_ds.append(o_g.reshape(n, -1))
                lse_ds.append(lse_g.reshape(n, -1))
            selm = _moba_gate(pr3, avg, 4, 5)
            (o_m,) = _attn(pr3, pr3, vmt, tab_moba, selm, qc0=8, kc0=10, vb0=0, n_blk=2, qw=1,
                           sel_cfg=(1, 8, 2), out_dtype=BF16)
            xf = _odd_out(xf, o_ds, lse_ds, o_m.reshape(n, -1), wd_o[j], wm_o[j], mod[i, :, 1, 2], s)
        xf = _ffn(xf, g_i[2], mod[i, :, 2, 0], mod[i, :, 2, 1], mod[i, :, 2, 2],
                  wa_all[i, 1], wb_all[i, 1], wo_all[i, 1], s)
    return xf.reshape(b, s, d)
```

```python
import functools
import math

import numpy as np
import jax
import jax.numpy as jnp
from jax import lax
from jax.experimental import pallas as pl
from jax.experimental.pallas import tpu as pltpu

F32 = jnp.float32
BF16 = jnp.bfloat16

D_MODEL = 1024
DEPTH = 4
D_FF = 2752
HEAD_DIM = 64
NUM_BUCKETS = 32
T5_MAX_EXACT = 16
T5_MAX_DIST = 128
RMS_EPS = 1e-6
NEG_INF = -1e30
MLA_HEADS = 8
MLA_NOPE = 64
MLA_ROPE = 32
MLA_V = 64
MLA_Q_LORA = 256
MLA_KV_LORA = 128
ROPE_THETA = 10000.0
NSA_HEADS = 8
NSA_GROUPS = 2
NSA_HPG = 4
NSA_CMP_LEN = 32
NSA_CMP_STRIDE = 16
NSA_CMP_HID = 256
NSA_SLC_BLOCK = 64
NSA_SLC_TOP = 8
NSA_WINDOW = 512
NSA_FORCED = 1e6
DIL_PAIRS = ((128, 1), (512, 4), (2048, 16))
DIL_HPG = 4
DIL_SLOTS = len(DIL_PAIRS) * DIL_HPG
MOBA_HEADS = 4
MOBA_BLOCK = 256
MOBA_TOP = 3

LANES = 128
V7X_VMEM_BYTES = 64 * 1024 * 1024
VMEM_LIMIT = V7X_VMEM_BYTES * 7 // 8
TM = 512
TQ = 256
TK = 256
FF_CHUNK = 256
FF_PAD = -(-D_FF // FF_CHUNK) * FF_CHUNK
PROJ_CHUNK = 256
DEN_ROWS = 16
LOG2E = math.log2(math.e)
LN2 = math.log(2.0)


def _dot(a, b):
    return jnp.dot(a, b, preferred_element_type=F32)


def _dot_nt(a, b):
    return lax.dot_general(a, b, (((1,), (1,)), ((), ())), preferred_element_type=F32)


def _split(a):
    hi = a.astype(BF16)
    lo = (a - hi.astype(F32)).astype(BF16)
    return hi, lo


def _dot_hilo(a, b):
    hi, lo = _split(a)
    return _dot(hi, b) + _dot(lo, b)


def _sigmoid(x):
    return 1.0 / (1.0 + jnp.exp(-x))


def _modulated_norm(x, g, shift, scale):
    ms = jnp.mean(x * x, axis=-1, keepdims=True)
    y = x * lax.rsqrt(ms + RMS_EPS) * g
    return y * (1.0 + scale) + shift


def _params(*sem):
    return pltpu.CompilerParams(dimension_semantics=sem, vmem_limit_bytes=VMEM_LIMIT)


def _resident(shape):
    nd = len(shape)
    return pl.BlockSpec(shape, lambda *_: (0,) * nd, pipeline_mode=pl.Buffered(1))


def _ada_kernel(c_ref, w_ref, b_ref, o_ref):
    c = c_ref[...]
    ca = c * _sigmoid(c)
    o_ref[...] = jnp.dot(ca, w_ref[...], preferred_element_type=F32,
                         precision=lax.Precision.HIGHEST) + b_ref[...]


def _ada(c, ada_w, ada_b):
    depth, d, n = ada_w.shape
    b = c.shape[0]
    tn = 18 * LANES
    return pl.pallas_call(
        _ada_kernel,
        grid=(depth, n // tn),
        in_specs=[pl.BlockSpec((b, d), lambda l, j: (0, 0)),
                  pl.BlockSpec((None, d, tn), lambda l, j: (l, 0, j)),
                  pl.BlockSpec((None, 1, tn), lambda l, j: (l, 0, j))],
        out_specs=pl.BlockSpec((None, b, tn), lambda l, j: (l, 0, j)),
        out_shape=jax.ShapeDtypeStruct((depth, b, n), F32),
        compiler_params=_params("parallel", "parallel"),
        name="ada",
    )(c, ada_w, ada_b.reshape(depth, 1, n))


def _ffn_kernel(x_ref, g_ref, sh_ref, sc_ref, gt_ref, wa_ref, wb_ref, wo_ref, o_ref, y_ref, acc_ref):
    y_ref[...] = _modulated_norm(x_ref[...], g_ref[...], sh_ref[...], sc_ref[...]).astype(BF16)
    for c in range(FF_PAD // FF_CHUNK):
        sl = slice(c * FF_CHUNK, (c + 1) * FF_CHUNK)
        a = _dot(y_ref[...], wa_ref[:, sl])
        b = _dot(y_ref[...], wb_ref[:, sl])
        u = (a * _sigmoid(a) * b).astype(BF16)
        contrib = _dot(u, wo_ref[sl, :])
        if c == 0:
            acc_ref[...] = contrib
        else:
            acc_ref[...] += contrib
    o_ref[...] = x_ref[...] + 0.5 * gt_ref[...] * acc_ref[...]


def _mod_spec(tiles_per_batch, d):
    return pl.BlockSpec((None, 1, d), lambda i: (i // tiles_per_batch, 0, 0))


def _ffn(x, g, shift, scale, gate, wa, wb, wo, s):
    n, d = x.shape
    tpb = s // TM
    return pl.pallas_call(
        _ffn_kernel,
        grid=(n // TM,),
        in_specs=[pl.BlockSpec((TM, d), lambda i: (i, 0)),
                  _resident((1, d)),
                  _mod_spec(tpb, d), _mod_spec(tpb, d), _mod_spec(tpb, d),
                  _resident(wa.shape), _resident(wb.shape), _resident(wo.shape)],
        out_specs=pl.BlockSpec((TM, d), lambda i: (i, 0)),
        out_shape=jax.ShapeDtypeStruct((n, d), F32),
        scratch_shapes=[pltpu.VMEM((TM, d), BF16), pltpu.VMEM((TM, d), F32)],
        compiler_params=_params("parallel"),
        name="ffn",
    )(x, g, shift, scale, gate, wa, wb, wo)


def _proj_kernel(meta, kinds, seq, x_ref, g_ref, sh_ref, sc_ref, w_ref, gain_ref, gm_ref, *rest):
    n_out = len(kinds)
    outs = rest[:n_out]
    y_ref, stage_ref = rest[n_out], rest[n_out + 1]
    tile_in_seq = pl.program_id(0) % (seq // TM)
    y_ref[...] = _modulated_norm(x_ref[...], g_ref[...], sh_ref[...], sc_ref[...]).astype(BF16)
    for c, halves in enumerate(meta):
        sl = slice(c * PROJ_CHUNK, (c + 1) * PROJ_CHUNK)
        z = _dot(y_ref[...], w_ref[:, sl])
        if any(normed for _, _, normed in halves):
            msq = _dot((z * z).astype(BF16), gm_ref[...])
            zn = z * lax.rsqrt(msq + RMS_EPS) * gain_ref[:, sl]
        for hf, (oi, off, normed) in enumerate(halves):
            src = (zn if normed else z)[:, hf * LANES:(hf + 1) * LANES]
            kind = kinds[oi]
            if kind == "cols":
                src = src * gain_ref[:, c * PROJ_CHUNK + hf * LANES:c * PROJ_CHUNK + (hf + 1) * LANES]
                outs[oi][off:off + LANES, :] = src.T.astype(outs[oi].dtype)
            elif kind == "rows":
                outs[oi][:, off:off + LANES] = src.astype(outs[oi].dtype)
            else:
                per = TM // kind
                stage_ref[...] = src
                for cs in range(kind):
                    dest = pl.multiple_of(cs * (seq // kind) + tile_in_seq * per, per)
                    outs[oi][pl.ds(dest, per), off:off + LANES] = (
                        stage_ref[pl.ds(cs, per, stride=kind), :].astype(outs[oi].dtype))


def _proj(x, g, shift, scale, w, gain, gm, meta, out_defs, s):
    n, d = x.shape
    tpb = s // TM
    specs = {"rows": lambda wd: pl.BlockSpec((TM, wd), lambda i: (i, 0)),
             "cols": lambda wd: pl.BlockSpec((None, wd, TM), lambda i: (i // tpb, 0, i % tpb))}
    stream = lambda wd: pl.BlockSpec((None, s, wd), lambda i: (i // tpb, 0, 0))
    shapes = {"rows": lambda wd: (n, wd), "cols": lambda wd: (n // s, wd, s)}
    kinds = tuple(k for _, _, k in out_defs)
    return pl.pallas_call(
        functools.partial(_proj_kernel, meta, kinds, s),
        grid=(n // TM,),
        in_specs=[pl.BlockSpec((TM, d), lambda i: (i, 0)),
                  _resident((1, d)),
                  _mod_spec(tpb, d), _mod_spec(tpb, d),
                  _resident(w.shape), _resident(gain.shape), _resident(gm.shape)],
        out_specs=[specs.get(k, stream)(wd) for wd, _, k in out_defs],
        out_shape=[jax.ShapeDtypeStruct(shapes.get(k, lambda wd: (n // s, s, wd))(wd), dt) for wd, dt, k in out_defs],
        scratch_shapes=[pltpu.VMEM((TM, d), BF16), pltpu.VMEM((TM, LANES), F32)],
        compiler_params=_params("arbitrary"),
        name="proj",
    )(x, g, shift, scale, w, gain, gm)


def _mla_prep_kernel(in_ref, qg_ref, kvg_ref, wuq_ref, wuk_ref, wuv_ref, gm_ref,
                     gq_ref, gkn_ref, gkr_ref, cos_ref, sin_ref, q_out, k_out, v_out):
    def rms(z, g):
        return z * lax.rsqrt(jnp.mean(z * z, axis=-1, keepdims=True) + RMS_EPS) * g

    cqn = rms(in_ref[:, 0:MLA_Q_LORA], qg_ref[...]).astype(BF16)
    ckvn = rms(in_ref[:, MLA_Q_LORA:MLA_Q_LORA + MLA_KV_LORA], kvg_ref[...]).astype(BF16)
    c3 = in_ref[:, 3 * LANES:4 * LANES]
    cos = cos_ref[...]
    sin = sin_ref[...]
    gm = gm_ref[...]
    half = MLA_ROPE // 2
    slot_lane = lax.broadcasted_iota(jnp.int32, (1, 2 * LANES), 1) & (LANES - 1)
    first_half = slot_lane < MLA_NOPE + half

    def norm_rope(z, gain):
        msq = _dot((z * z).astype(BF16), gm)
        z = z * lax.rsqrt(msq + RMS_EPS) * gain
        rot = jnp.where(first_half, -pltpu.roll(z, 2 * LANES - half, 1), pltpu.roll(z, half, 1))
        return z * cos + rot * sin

    kr = norm_rope(jnp.concatenate([c3, c3], axis=1), gkr_ref[...])
    for h in range(MLA_HEADS // 2):
        sl = slice(h * 2 * LANES, (h + 1) * 2 * LANES)
        q_out[:, sl] = norm_rope(_dot(cqn, wuq_ref[:, sl]), gq_ref[...]).astype(BF16)
        k_out[:, sl] = (norm_rope(_dot(ckvn, wuk_ref[:, sl]), gkn_ref[...]) + kr).astype(BF16)
    v_out[...] = _dot_nt(wuv_ref[...], ckvn).astype(BF16)


def _mla_prep(mla_in, qg, kvg, wuq, wuk, wuv, gm, gq, gkn, gkr, cos_t, sin_t, s):
    n = mla_in.shape[0]
    tpb = s // TM
    hw = MLA_HEADS * LANES
    vw = MLA_HEADS * MLA_V
    tab = pl.BlockSpec((TM, 2 * LANES), lambda i: (i % tpb, 0))
    consts = [qg, kvg, wuq, wuk, wuv, gm, gq, gkn, gkr]
    return pl.pallas_call(
        _mla_prep_kernel,
        grid=(n // TM,),
        in_specs=[pl.BlockSpec((TM, 4 * LANES), lambda i: (i, 0))] + [_resident(a.shape) for a in consts] + [tab, tab],
        out_specs=[pl.BlockSpec((TM, hw), lambda i: (i, 0)), pl.BlockSpec((TM, hw), lambda i: (i, 0)),
                   pl.BlockSpec((None, vw, TM), lambda i: (i // tpb, 0, i % tpb))],
        out_shape=[jax.ShapeDtypeStruct((n, hw), BF16), jax.ShapeDtypeStruct((n, hw), BF16),
                   jax.ShapeDtypeStruct((n // s, vw, s), BF16)],
        compiler_params=_params("parallel"),
        name="mla_prep",
    )(mla_in, *consts, cos_t, sin_t)


def _attn_kernel(cfg, q_ref, k_ref, vt_ref, tab_ref, *rest):
    n_tab, qw, backs, sel_cfg, want_lse, n_qt, stride = cfg
    n_qs = n_qt // stride
    if sel_cfg is not None:
        sel_ref, rest = rest[0], rest[1:]
        sel_bpt, sel_stride, sel_pair_mul = sel_cfg
    o_ref = rest[0]
    lse_ref = rest[1] if want_lse else None
    st_scr, p_scr, acc_fin, m_fin = rest[-6:-4], rest[-4:-2], rest[-2], rest[-1]
    blk = pl.program_id(1)
    lane = lax.broadcasted_iota(jnp.int32, (1, LANES), 1)

    counts = tuple(sum(min(t % n_qs, bk) + 1 for t in range(n_qt)) for bk in backs)
    back = jnp.int32(backs[-1])
    n_tiles = jnp.int32(counts[-1])
    for bi in range(len(backs) - 2, -1, -1):
        back = jnp.where(blk == bi, jnp.int32(backs[bi]), back)
        n_tiles = jnp.where(blk == bi, jnp.int32(counts[bi]), n_tiles)

    ones_rows = jnp.ones((DEN_ROWS, TK), BF16)

    def first_key_tile(qi):
        return qi - jnp.minimum(qi & (n_qs - 1), back)

    def rows(t, size):
        return pl.ds(t * size if isinstance(t, int) else pl.multiple_of(t * size, size), size)

    def out_rows(t, size):
        if stride == 1:
            return rows(t, size)
        return pl.ds((t >> int(math.log2(n_qs))) + stride * size * (t & (n_qs - 1)), size, stride=stride)

    def logits_to(slot, qi, j, filler):
        d = jnp.where(filler, n_tab, jnp.minimum(qi - j, n_tab - 1))
        if qw == 1:
            q = q_ref[rows(qi, TQ), :]
            zero = jnp.zeros_like(q)
            qs = [jnp.where(lane < HEAD_DIM, q, zero), jnp.where(lane >= HEAD_DIM, q, zero)]
        else:
            qs = [q_ref[rows(qi, TQ), s * LANES:(s + 1) * LANES] for s in range(2)]
        for s in range(2):
            kj = k_ref[rows(j, TK), :] if qw == 1 else k_ref[rows(j, TK), s * LANES:(s + 1) * LANES]
            st = _dot_nt(kj, qs[s])
            if sel_cfg is not None:
                off = sel_stride * (s + sel_pair_mul * blk) + sel_bpt * j
                kb = TK // sel_bpt
                qcol = pl.ds(pl.multiple_of(qi * TQ, TQ), TQ)
                st = jnp.concatenate(
                    [st[bk * kb:(bk + 1) * kb] + sel_ref[pl.ds(off + bk, 1), qcol]
                     for bk in range(sel_bpt)], axis=0)
            st_scr[slot][s] = st + tab_ref[s, d]

    def accumulate(slot, j, alphas, accs):
        krow = pl.multiple_of(j * TK, TK)
        new = []
        for s in range(2):
            vt = jnp.concatenate([vt_ref[s * HEAD_DIM:(s + 1) * HEAD_DIM, pl.ds(krow, TK)], ones_rows], axis=0)
            new.append(alphas[s] * accs[s] + _dot(vt, p_scr[slot][s]))
        return tuple(new)

    def finalize(qi, ms, accs):
        for s in range(2):
            acc_fin[qi, s] = accs[s]
            if want_lse:
                m_fin[qi, s] = jnp.broadcast_to(ms[s], m_fin.shape[2:])

    def write_out(t):
        accs = [acc_fin[t, s] for s in range(2)]
        dens = [acc[HEAD_DIM:HEAD_DIM + 1] for acc in accs]
        out_t = jnp.concatenate([acc[:HEAD_DIM] / l for acc, l in zip(accs, dens)], axis=0)
        o_ref[out_rows(t, TQ), :] = out_t.T.astype(o_ref.dtype)
        if want_lse:
            lse_t = jnp.concatenate([jnp.broadcast_to((m_fin[t, s][0:1] + jnp.log2(dens[s])) * LN2, (HEAD_DIM, TQ))
                                     for s in range(2)], axis=0)
            lse_ref[out_rows(t, TQ), :] = lse_t.T

    def advance(q, j):
        last = j == q
        at_end = jnp.logical_and(last, q == n_qt - 1)
        starts = jnp.logical_and(last, jnp.logical_not(at_end))
        qn = jnp.where(starts, q + 1, q)
        jn = jnp.where(at_end, j, jnp.where(last, first_key_tile(q + 1), j + 1))
        return qn, jn, starts, at_end

    def softmax(slot, tile, stats):
        is_first = tile[2]
        keep = jnp.where(is_first, 0.0, 1.0)
        new_stats, alphas = [], []
        for s in range(2):
            m = jnp.where(is_first, NEG_INF, stats[s])
            st = st_scr[slot][s]
            m_new = jnp.maximum(m, jnp.max(st, axis=0, keepdims=True))
            alphas.append(jnp.exp2(m - m_new) * keep)
            new_stats.append(m_new)
            p_scr[slot][s] = jnp.exp2(st - m_new).astype(BF16)
        return tuple(new_stats), tuple(alphas)

    def body(u, carry):
        tile_a, (q2, j2), (q1, j1, first1), stats2, stats1, alphas2, alphas1, accs = carry
        qa, ja, first_a, _ = tile_a
        tile_b = advance(qa, ja)
        qb, jb, first_b, _ = tile_b
        tile_c = advance(qb, jb)
        logits_to(1, qb, jb, tile_b[3])
        accs_x = accumulate(0, j2, alphas2, accs)
        accs_y = accumulate(1, j1, alphas1, accs_x)
        stats_a, alphas_a = softmax(0, tile_a, stats1)
        logits_to(0, tile_c[0], tile_c[1], tile_c[3])
        stats_b, alphas_b = softmax(1, tile_b, stats_a)

        finalize(q2, stats2, accs_x)
        finalize(q1, stats1, accs_y)
        return tile_c, (qa, ja), (qb, jb, first_b), stats_a, stats_b, alphas_a, alphas_b, accs_y

    zero_i = jnp.int32(0)
    logits_to(0, zero_i, zero_i, zero_i != 0)
    for p_slot in p_scr:
        p_slot[...] = jnp.zeros(p_slot.shape, BF16)
    stats0 = tuple(jnp.full((1, TQ), NEG_INF, F32) for _ in range(2))
    ones = tuple(jnp.ones((1, TQ), F32) for _ in range(2))
    init = ((zero_i, zero_i, zero_i == 0, zero_i != 0), (zero_i, zero_i), (zero_i, zero_i, zero_i != 0),
            stats0, stats0, ones, ones, tuple(jnp.zeros((HEAD_DIM + DEN_ROWS, TQ), F32) for _ in range(2)))
    _, (q2, j2), (q1, j1, first1), stats2, stats1, alphas2, alphas1, accs = lax.fori_loop(
        0, (n_tiles + 1) // 2, body, init)
    accs_x = accumulate(0, j2, alphas2, accs)
    finalize(q2, stats2, accs_x)
    finalize(q1, stats1, accumulate(1, j1, alphas1, accs_x))
    for t in range(n_qt):
        write_out(t)


def _attn(q, k, vt, tab, sel, *, qc0, kc0, vb0, n_blk, qw, backs=None, kv_shared=False,
          tab_shared=False, sel_cfg=None, out_dtype=F32, want_lse=False, stride=1):
    b, s = q.shape[:2]
    n_tab = tab.shape[2] - 1
    n_qt = s // TQ
    if backs is None:
        backs = (n_qt,)
    cfg = (n_tab, qw, backs, sel_cfg, want_lse, n_qt, stride)
    kidx = (lambda bb, h: (bb, 0, kc0)) if kv_shared else (lambda bb, h: (bb, 0, kc0 + h))
    vidx = (lambda bb, h: (bb, vb0, 0)) if kv_shared else (lambda bb, h: (bb, vb0 + h, 0))
    tidx = (lambda bb, h: (0, 0, 0, 0, 0)) if tab_shared else (lambda bb, h: (h, 0, 0, 0, 0))
    in_specs = [pl.BlockSpec((None, s, qw * LANES), lambda bb, h: (bb, 0, qc0 + h)),
                pl.BlockSpec((None, s, qw * LANES), kidx),
                pl.BlockSpec((None, 2 * HEAD_DIM, s), vidx),
                pl.BlockSpec((None, 2, n_tab + 1, TK, TQ), tidx)]
    args = [q, k, vt, tab]
    if sel_cfg is not None:
        in_specs.append(pl.BlockSpec((None, LANES, s), lambda bb, h: (bb, 0, 0)))
        args.append(sel)
    ospec = pl.BlockSpec((None, s, LANES), lambda bb, h: (bb, 0, h))
    out_specs = [ospec]
    out_shape = [jax.ShapeDtypeStruct((b, s, n_blk * LANES), out_dtype)]
    if want_lse:
        out_specs.append(ospec)
        out_shape.append(jax.ShapeDtypeStruct((b, s, n_blk * LANES), F32))
    return pl.pallas_call(
        functools.partial(_attn_kernel, cfg),
        grid=(b, n_blk),
        in_specs=in_specs,
        out_specs=out_specs,
        out_shape=out_shape,
        scratch_shapes=([pltpu.VMEM((2, TK, TQ), F32)] * 2 + [pltpu.VMEM((2, TK, TQ), BF16)] * 2
                        + [pltpu.VMEM((n_qt, 2, HEAD_DIM + DEN_ROWS, TQ), F32), pltpu.VMEM((n_qt, 2, 8, TQ), F32)]),
        compiler_params=_params("parallel", "parallel"),
        name="attn",
    )(*args)


def _nsa_cmp_kernel(kc_ref, vc_ref, pe_ref, wlo_ref, whi_ref, w2_ref, gm_ref, gain_ref, kcmp_ref, vcmpt_ref):
    nch = kcmp_ref.shape[0]

    def hidden(c_ref, j):
        lo = hi = None
        for l in range(NSA_CMP_STRIDE):
            rows = c_ref[pl.ds(l, nch, stride=NSA_CMP_STRIDE), :]
            sl = slice(l * LANES, (l + 1) * LANES)
            t_lo = _dot((rows + pe_ref[j, 0][:, sl]).astype(BF16), wlo_ref[j][sl, :])
            t_hi = _dot((rows + pe_ref[j, 1][:, sl]).astype(BF16), whi_ref[j][sl, :])
            lo = t_lo if lo is None else lo + t_lo
            hi = t_hi if hi is None else hi + t_hi
        h = lo + pltpu.roll(hi, nch - 1, 0)
        return (h * _sigmoid(h)).astype(BF16)

    kz = _dot_nt(hidden(kc_ref, 0), w2_ref[0])
    msq = _dot_hilo(kz * kz, gm_ref[...])
    kcmp_ref[...] = (kz * lax.rsqrt(msq + RMS_EPS) * gain_ref[...]).astype(BF16)
    vcmpt_ref[...] = _dot_nt(w2_ref[1], hidden(vc_ref, 1)).astype(BF16)


def _nsa_cmp(kc3, vc3, pe, wlo, whi, w2, gm, gain):
    b, s, width = kc3.shape
    nch = s // NSA_CMP_STRIDE
    consts = [pe, wlo, whi, w2, gm, gain]
    blk = pl.BlockSpec((None, s, width), lambda i: (i, 0, 0))
    oblk = pl.BlockSpec((None, nch, LANES), lambda i: (i, 0, 0))
    return pl.pallas_call(
        _nsa_cmp_kernel,
        grid=(b,),
        in_specs=[blk, blk] + [_resident(a.shape) for a in consts],
        out_specs=[oblk, oblk],
        out_shape=[jax.ShapeDtypeStruct((b, nch, LANES), BF16)] * 2,
        compiler_params=_params("parallel"),
        name="nsa_cmp",
    )(kc3, vc3, *consts)


def _rank_keep(score, ids, top):
    cnt = jnp.zeros(score.shape, jnp.int32)
    for mp in range(score.shape[0]):
        other = score[mp:mp + 1, :]
        tie = jnp.where(mp < ids, 1, 0)
        cnt = cnt + jnp.where(other > score, 1, jnp.where(other == score, tie, 0))
    return cnt < top


def _nsa_sel_kernel(q_ref, kcmp_ref, vcmpt_ref, bias_ref, ovl_ref, oc_ref, sel_ref):
    qi = pl.program_id(1)
    lane = lax.broadcasted_iota(jnp.int32, (1, LANES), 1)
    row = lax.broadcasted_iota(jnp.int32, (LANES, 1), 0)
    t = qi * TQ + lax.broadcasted_iota(jnp.int32, (1, TQ), 1)
    mask_c = (NSA_CMP_STRIDE * row + NSA_CMP_LEN - 1) <= t
    kcmp = kcmp_ref[...]
    heads = [(p, g) for p in range(NSA_HPG) for g in range(NSA_GROUPS)]
    raw = []
    for p, g in heads:
        qp = q_ref[:, p * LANES:(p + 1) * LANES]
        raw.append(_dot_nt(kcmp, jnp.where((lane >> 6) == g, qp, jnp.zeros_like(qp))))
    pcs = []
    for (p, g), r in zip(heads, raw):
        lg = jnp.where(mask_c, r * LN2 + bias_ref[p * NSA_GROUPS + g], NEG_INF)
        m = jnp.max(lg, axis=0, keepdims=True)
        e = jnp.where(mask_c, jnp.exp(lg - m), 0.0)
        den = jnp.maximum(jnp.sum(e, axis=0, keepdims=True), 1e-30)
        pcs.append(e / den)
    imp = jnp.zeros((LANES, TQ), F32)
    ocs = []
    for (p, g), pc in zip(heads, pcs):
        hi, lo = _split(pc)
        ocs.append(_dot(vcmpt_ref[g * HEAD_DIM:(g + 1) * HEAD_DIM, :], hi))
        imp = imp + _dot(ovl_ref[g], hi) + _dot(ovl_ref[g], lo)
    for p in range(NSA_HPG):
        oc_ref[:, p * LANES:(p + 1) * LANES] = jnp.concatenate([ocs[2 * p], ocs[2 * p + 1]], axis=0).T.astype(BF16)

    n_slc = 32
    ids = row[0:n_slc]
    cur = t >> 6
    forced = (ids == 0) | (ids == cur) | (ids == cur - 1)
    masks = []
    for g in range(NSA_GROUPS):
        score = jnp.where(forced, NSA_FORCED, jnp.where(ids <= cur, imp[g * n_slc:(g + 1) * n_slc], NEG_INF))
        keep = _rank_keep(score, ids, NSA_SLC_TOP) & (score > 0.5 * NEG_INF)
        masks.append(jnp.where(keep, 0.0, NEG_INF))
    masks.append(jnp.full((LANES - NSA_GROUPS * n_slc, TQ), NEG_INF, F32))
    sel_ref[...] = jnp.concatenate(masks, axis=0)


def _nsa_sel(q, kcmp, vcmpt, bias_c, ovl):
    b, s, w = q.shape
    return pl.pallas_call(
        _nsa_sel_kernel,
        grid=(b, s // TQ),
        in_specs=[pl.BlockSpec((None, TQ, w), lambda bb, i: (bb, i, 0)),
                  pl.BlockSpec((None, LANES, LANES), lambda bb, i: (bb, 0, 0)),
                  pl.BlockSpec((None, LANES, LANES), lambda bb, i: (bb, 0, 0)),
                  pl.BlockSpec((NSA_HEADS, LANES, TQ), lambda bb, i: (0, 0, i)),
                  _resident(ovl.shape)],
        out_specs=[pl.BlockSpec((None, TQ, w), lambda bb, i: (bb, i, 0)),
                   pl.BlockSpec((None, LANES, TQ), lambda bb, i: (bb, 0, i))],
        out_shape=[jax.ShapeDtypeStruct((b, s, w), BF16), jax.ShapeDtypeStruct((b, LANES, s), F32)],
        compiler_params=_params("parallel", "parallel"),
        name="nsa_sel",
    )(q, kcmp, vcmpt, bias_c, ovl)


def _moba_gate_kernel(q_ref, k_ref, avg_ref, sel_ref, kh_ref, kl_ref):
    qi = pl.program_id(1)
    nb = 8

    @pl.when(qi == 0)
    def _():
        kmean = _dot(avg_ref[...], k_ref[...])
        r2 = lax.broadcasted_iota(jnp.int32, kmean.shape, 0)
        c2 = lax.broadcasted_iota(jnp.int32, kmean.shape, 1)
        kh_ref[...], kl_ref[...] = _split(jnp.where((r2 >> 3) == (c2 >> 6), kmean, 0.0))

    q = q_ref[...]
    gate = _dot_nt(kh_ref[...], q) + _dot_nt(kl_ref[...], q)
    ids = lax.broadcasted_iota(jnp.int32, (nb, 1), 0)
    past = ids < qi
    masks = []
    for h in range(MOBA_HEADS):
        score = jnp.where(past, gate[h * nb:(h + 1) * nb], NEG_INF)
        keep = (_rank_keep(score, ids, MOBA_TOP) & past) | (ids == qi)
        masks.append(jnp.where(keep, 0.0, NEG_INF))
    masks.append(jnp.full((LANES - MOBA_HEADS * nb, TQ), NEG_INF, F32))
    sel_ref[...] = jnp.concatenate(masks, axis=0)


def _moba_gate(p_arr, avg, qc, kc):
    b, s, _ = p_arr.shape
    w = MOBA_HEADS * HEAD_DIM
    return pl.pallas_call(
        _moba_gate_kernel,
        grid=(b, s // TQ),
        in_specs=[pl.BlockSpec((None, TQ, w), lambda bb, i: (bb, i, qc)),
                  pl.BlockSpec((None, s, w), lambda bb, i: (bb, 0, kc)),
                  _resident(avg.shape)],
        out_specs=pl.BlockSpec((None, LANES, TQ), lambda bb, i: (bb, 0, i)),
        out_shape=jax.ShapeDtypeStruct((b, LANES, s), F32),
        scratch_shapes=[pltpu.VMEM((LANES, w), BF16)] * 2,
        compiler_params=_params("parallel", "arbitrary"),
        name="moba_gate",
    )(p_arr, p_arr, avg)


def _even_out_kernel(x_ref, oa_ref, oc_ref, os_ref, ow_ref, gl_ref, eg_ref, wa_ref, wb_ref, gt_ref, o_ref):
    sg = _sigmoid(gl_ref[...])
    hi, lo = _split(sg)
    nsa = None
    for br, src in enumerate((oc_ref, os_ref, ow_ref)):
        gexp = _dot(hi, eg_ref[br]) + _dot(lo, eg_ref[br])
        term = gexp * src[...]
        nsa = term if nsa is None else nsa + term
    m = _dot(oa_ref[...], wa_ref[...]) + _dot(nsa.astype(BF16), wb_ref[...])
    o_ref[...] = x_ref[...] + gt_ref[...] * m


def _even_out(x, o_a, o_c, o_s, o_w, mla_in, eg, wa, wb, gate, s):
    n, d = x.shape
    tpb = s // TM
    row = lambda wd: pl.BlockSpec((TM, wd), lambda i: (i, 0))
    return pl.pallas_call(
        _even_out_kernel,
        grid=(n // TM,),
        in_specs=[row(d), row(o_a.shape[1]), row(o_c.shape[1]), row(o_s.shape[1]), row(o_w.shape[1]),
                  pl.BlockSpec((TM, LANES), lambda i: (i, 3)),
                  _resident(eg.shape), _resident(wa.shape), _resident(wb.shape), _mod_spec(tpb, d)],
        out_specs=row(d),
        out_shape=jax.ShapeDtypeStruct((n, d), F32),
        compiler_params=_params("parallel"),
        name="even_out",
    )(x, o_a, o_c, o_s, o_w, mla_in, eg, wa, wb, gate)


def _odd_out_kernel(x_ref, od0_ref, od1_ref, od2_ref, ls0_ref, ls1_ref, ls2_ref, om_ref, wd_ref, wm_ref, gt_ref, o_ref):
    ods = (od0_ref, od1_ref, od2_ref)
    ls = [r[...] for r in (ls0_ref, ls1_ref, ls2_ref)]
    mx = jnp.maximum(jnp.maximum(ls[0], ls[1]), ls[2])
    es = [jnp.exp(l - mx) for l in ls]
    den = es[0] + es[1] + es[2]
    merged = None
    for g in range(len(DIL_PAIRS)):
        term = (es[g] / den) * ods[g][...]
        merged = term if merged is None else merged + term
    m = _dot(merged.astype(BF16), wd_ref[...]) + _dot(om_ref[...], wm_ref[...])
    o_ref[...] = x_ref[...] + gt_ref[...] * m


def _odd_out(x, o_ds, lse_ds, o_m, wd, wm, gate, s):
    n, d = x.shape
    tpb = s // TM
    row = lambda wd_: pl.BlockSpec((TM, wd_), lambda i: (i, 0))
    return pl.pallas_call(
        _odd_out_kernel,
        grid=(n // TM,),
        in_specs=[row(d)] + [row(a.shape[1]) for a in (*o_ds, *lse_ds)] + [row(o_m.shape[1]),
                  _resident(wd.shape), _resident(wm.shape), _mod_spec(tpb, d)],
        out_specs=row(d),
        out_shape=jax.ShapeDtypeStruct((n, d), F32),
        compiler_params=_params("parallel"),
        name="odd_out",
    )(x, *o_ds, *lse_ds, o_m, wd, wm, gate)


def _t5_bucket(dist):
    n = jnp.maximum(jnp.asarray(dist, jnp.int32), 0)
    nf = jnp.maximum(n, 1).astype(F32)
    large = T5_MAX_EXACT + (jnp.log(nf / T5_MAX_EXACT) / math.log(T5_MAX_DIST / T5_MAX_EXACT)
                            * (NUM_BUCKETS - T5_MAX_EXACT)).astype(jnp.int32)
    return jnp.where(n < T5_MAX_EXACT, n, jnp.minimum(large, NUM_BUCKETS - 1))


TOEP_PERIOD = 4 * TQ


def _toeplitz_dist():
    j = np.arange(TOEP_PERIOD)
    return np.where(j < 3 * TQ, j, j - TOEP_PERIOD)


def _toeplitz_kernel(n_tab, u_ref, o_ref):
    x = jnp.broadcast_to(u_ref[...], (TK, TOEP_PERIOD))
    y = pltpu.roll(x, 0, 1, stride=1, stride_axis=0)
    for dlt in range(n_tab):
        o_ref[dlt] = y[:, dlt * TQ:(dlt + 1) * TQ]
    o_ref[n_tab] = jnp.full((TK, TQ), NEG_INF * LOG2E, F32)


def _toeplitz_tiles(u, n_tab):
    h = u.shape[0]
    return pl.pallas_call(
        functools.partial(_toeplitz_kernel, n_tab),
        grid=(h,),
        in_specs=[pl.BlockSpec((None, 1, TOEP_PERIOD), lambda i: (i, 0, 0))],
        out_specs=pl.BlockSpec((None, n_tab + 1, TK, TQ), lambda i: (i, 0, 0, 0)),
        out_shape=jax.ShapeDtypeStruct((h, n_tab + 1, TK, TQ), F32),
        compiler_params=_params("parallel"),
        name="toeplitz",
    )(u.reshape(h, 1, TOEP_PERIOD))


def _cmp_bias_kernel(u_ref, o_ref):
    x = jnp.broadcast_to(u_ref[...], o_ref.shape)
    o_ref[...] = pltpu.roll(x, 0, 1, stride=NSA_CMP_STRIDE, stride_axis=0)


def _cmp_bias(u, n_rows):
    h, s = u.shape
    return pl.pallas_call(
        _cmp_bias_kernel,
        grid=(h,),
        in_specs=[pl.BlockSpec((None, 1, s), lambda i: (i, 0, 0))],
        out_specs=pl.BlockSpec((None, n_rows, s), lambda i: (i, 0, 0)),
        out_shape=jax.ShapeDtypeStruct((h, n_rows, s), F32),
        compiler_params=_params("parallel"),
        name="cmp_bias",
    )(u.reshape(h, 1, s))


def _bias_tiles(t5_cols, ok, n_tab=3, dist_scale=1):
    dist = _toeplitz_dist()
    bias = jnp.transpose(t5_cols[_t5_bucket(dist * dist_scale)])
    u = jnp.where(jnp.asarray(ok)[None], bias, NEG_INF) * LOG2E
    return _toeplitz_tiles(u, n_tab)


def _pair(tiles):
    h = tiles.shape[0]
    return tiles.reshape(h // 2, 2, *tiles.shape[1:])


def _group_mean_np(sizes, width=LANES):
    gm = np.zeros((width, width), np.float32)
    o = 0
    for sz in sizes:
        gm[o:o + sz, o:o + sz] = 1.0 / sz
        o += sz
    return gm


def _group_mean_matrix(sizes, width=LANES):
    return jnp.asarray(_group_mean_np(sizes, width), BF16)


EV_META = ((0, 0, False), (0, 128, False), (0, 256, False), (0, 384, False),
           (1, 0, True), (1, 128, True), (1, 256, True), (1, 384, True),
           (2, 0, False), (3, 0, False), (4, 0, True), (5, 0, False), (6, 0, True), (7, 0, False))
EV_OUTS = [(512, F32, "rows"), (512, BF16, "rows"), (128, F32, "rows"), (128, F32, "rows"), (128, BF16, "rows"),
           (128, BF16, "cols"), (128, BF16, "rows"), (128, BF16, "cols")]
OD_DIL_STRIDES = (1, 4, 8)
OD_META = (((0, 0, True), (0, LANES, True)) + tuple((3 + g, h * LANES, True) for g in range(2) for h in range(2))
           + ((0, 2 * LANES, True), (0, 3 * LANES, True))
           + tuple((3 + g, (2 + h) * LANES, True) for g in range(2) for h in range(2))
           + ((1, 0, False), (1, LANES, False)) + tuple((0, (4 + c) * LANES, False) for c in range(4))
           + tuple((0, (8 + c) * LANES, True) for c in range(4)) + ((2, 0, False), (2, LANES, False)))
OD_OUTS = [(12 * LANES, BF16, "rows"), (2 * LANES, BF16, "cols"), (2 * LANES, BF16, "cols"),
           (4 * LANES, BF16, OD_DIL_STRIDES[1]), (4 * LANES, BF16, OD_DIL_STRIDES[2])]


def _even_w_in(w):
    jn, d, _ = w.shape
    z = lambda n_: jnp.zeros((jn, d, n_), w.dtype)
    nq = w[:, :, 416:928].reshape(jn, d, NSA_GROUPS, NSA_HPG, HEAD_DIM)
    nq = jnp.transpose(nq, (0, 1, 3, 2, 4)).reshape(jn, d, NSA_HEADS * HEAD_DIM)
    chunk3 = jnp.concatenate([w[:, :, 1696:1720], z(HEAD_DIM - 24), w[:, :, 384:416], z(LANES - 96)], axis=-1)
    return jnp.concatenate([w[:, :, 0:384], chunk3, nq, w[:, :, 928:1696]], axis=-1)


def kernel(x, c, t5_bias, ada_w, ada_b, norm_g, ffn_w_in, ffn_w_out, ev_w_in, ev_w_out, mla_q_norm_g,
           mla_kv_norm_g, mla_w_uq, mla_w_ukv, mla_qk_g, nsa_cmp_pe, nsa_cmp_w1, nsa_cmp_w2, nsa_qk_g,
           od_w_in, od_w_out, dil_qk_g, moba_qk_g):
    b, s, d = x.shape
    assert (s, d) == (2048, D_MODEL) and s % TM == 0 and TQ == MOBA_BLOCK and TQ == TK
    n = b * s
    hd = HEAD_DIM
    n_even = ev_w_in.shape[0]
    n_odd = od_w_in.shape[0]
    c64 = hd ** -0.5 * LOG2E
    c96 = (MLA_NOPE + MLA_ROPE) ** -0.5 * LOG2E

    mod = _ada(c, ada_w, ada_b).reshape(DEPTH, b, 3, 3, 1, d)

    dist = _toeplitz_dist()
    causal = dist >= 0
    gm64 = _group_mean_matrix((hd, hd))
    gm_proj = _group_mean_matrix((hd,) * (PROJ_CHUNK // hd), PROJ_CHUNK)
    per_chunk = PROJ_CHUNK // LANES
    chunked = lambda m: tuple(tuple(m[i:i + per_chunk]) for i in range(0, len(m), per_chunk))

    padc = FF_PAD - D_FF
    wa_all = jnp.pad(ffn_w_in[..., :D_FF], ((0, 0), (0, 0), (0, 0), (0, padc))).astype(BF16)
    wb_all = jnp.pad(ffn_w_in[..., D_FF:], ((0, 0), (0, 0), (0, 0), (0, padc))).astype(BF16)
    wo_all = jnp.pad(ffn_w_out, ((0, 0), (0, 0), (0, padc), (0, 0))).astype(BF16)

    nsa_tab = t5_bias[:, MLA_HEADS:MLA_HEADS + NSA_HEADS].reshape(NUM_BUCKETS, NSA_GROUPS, NSA_HPG)
    nsa_cols = jnp.transpose(nsa_tab, (0, 2, 1)).reshape(NUM_BUCKETS, NSA_HEADS)
    tab_sel = _pair(_bias_tiles(nsa_cols, causal))
    tab_win = _pair(_bias_tiles(nsa_cols, causal & (dist <= NSA_WINDOW - 1)))
    tab_mla = _toeplitz_tiles(jnp.where(jnp.asarray(causal), 0.0, NEG_INF).astype(F32)[None], 2)
    tab_mla = jnp.broadcast_to(tab_mla[None], (1, 2) + tab_mla.shape[1:])
    n_cmp_pad = s // NSA_CMP_STRIDE
    bias_c = _cmp_bias(jnp.transpose(nsa_cols[_t5_bucket(np.arange(s) - (NSA_CMP_LEN - 1))]), n_cmp_pad)
    n_cmp = (s - NSA_CMP_LEN) // NSA_CMP_STRIDE + 1
    cstart = np.arange(n_cmp) * NSA_CMP_STRIDE
    sstart = np.arange(s // NSA_SLC_BLOCK) * NSA_SLC_BLOCK
    overlap = np.clip(np.minimum(cstart[:, None] + NSA_CMP_LEN, sstart[None, :] + NSA_SLC_BLOCK)
                      - np.maximum(cstart[:, None], sstart[None, :]), 0, None).astype(np.float32) / NSA_CMP_LEN
    ovl = np.zeros((NSA_GROUPS, LANES, LANES), np.float32)
    for g in range(NSA_GROUPS):
        ovl[g, 32 * g:32 * g + 32, :n_cmp] = overlap.T
    ovl = jnp.asarray(ovl, BF16)
    eg = np.zeros((3, LANES, NSA_HEADS * hd), np.float32)
    for g in range(NSA_GROUPS):
        for p in range(NSA_HPG):
            for br in range(3):
                eg[br, (g * NSA_HPG + p) * 3 + br, p * LANES + g * hd:p * LANES + (g + 1) * hd] = 1.0
    eg = jnp.asarray(eg, BF16)
    gm_mla = jnp.asarray(np.kron(np.eye(2, dtype=np.float32), _group_mean_np((MLA_NOPE, MLA_ROPE))), BF16)
    inv = ROPE_THETA ** (-jnp.arange(0, MLA_ROPE, 2, dtype=F32) / MLA_ROPE)
    ang = jnp.arange(s, dtype=F32)[:, None] * inv[None, :]
    ones = jnp.ones((s, MLA_NOPE), F32)
    tail = LANES - MLA_NOPE - MLA_ROPE
    cos_t = jnp.tile(jnp.concatenate([ones, jnp.cos(ang), jnp.cos(ang), jnp.ones((s, tail), F32)], axis=1), (1, 2))
    sin_t = jnp.tile(jnp.concatenate([0 * ones, jnp.sin(ang), jnp.sin(ang), jnp.zeros((s, tail), F32)], axis=1), (1, 2))

    ev_w = _even_w_in(ev_w_in).astype(BF16)
    ev_gain = jnp.ones((n_even, 14 * LANES), F32)
    ev_gain = ev_gain.at[:, 512:1024].set(jnp.tile(nsa_qk_g[:, 0], (1, 8)) * c64)
    ev_gain = ev_gain.at[:, 1280:1408].set(jnp.tile(nsa_qk_g[:, 1], (1, 2)))
    ev_gain = ev_gain.at[:, 1536:1664].set(jnp.tile(nsa_qk_g[:, 1], (1, 2)))
    gain_kc = jnp.tile(nsa_qk_g[:, 1], (1, 2))
    wuq = jnp.pad(mla_w_uq.reshape(n_even, MLA_Q_LORA, MLA_HEADS, MLA_NOPE + MLA_ROPE),
                  ((0, 0), (0, 0), (0, 0), (0, tail))).reshape(n_even, MLA_Q_LORA, MLA_HEADS * LANES).astype(BF16)
    ukv = mla_w_ukv.reshape(n_even, MLA_KV_LORA, MLA_HEADS, MLA_NOPE + MLA_V)
    wuk = jnp.pad(ukv[..., :MLA_NOPE], ((0, 0), (0, 0), (0, 0), (0, LANES - MLA_NOPE))
                  ).reshape(n_even, MLA_KV_LORA, MLA_HEADS * LANES).astype(BF16)
    wuv = jnp.swapaxes(ukv[..., MLA_NOPE:].reshape(n_even, MLA_KV_LORA, MLA_HEADS * MLA_V), 1, 2).astype(BF16)
    zt = jnp.zeros((n_even, tail), F32)
    gq = jnp.tile(jnp.concatenate([mla_qk_g[:, 0] * c96, zt], axis=1), (1, 2))
    gkn = jnp.tile(jnp.concatenate([mla_qk_g[:, 1, :MLA_NOPE], jnp.zeros((n_even, LANES - MLA_NOPE), F32)], axis=1), (1, 2))
    gkr = jnp.tile(jnp.concatenate([jnp.zeros((n_even, MLA_NOPE), F32), mla_qk_g[:, 1, MLA_NOPE:], zt], axis=1), (1, 2))
    pe2 = jnp.broadcast_to(nsa_cmp_pe.reshape(n_even, 2, 2, 16, 1, hd), (n_even, 2, 2, 16, NSA_GROUPS, hd)
                           ).reshape(n_even, 2, 2, 1, 16 * LANES)
    eye = jnp.eye(NSA_GROUPS, dtype=F32)
    w1 = nsa_cmp_w1.reshape(n_even, 2, 2, 16, hd, NSA_CMP_HID)
    w1x = jnp.einsum('ijaldc,gh->ijalgdhc', w1, eye).reshape(n_even, 2, 2, 16 * LANES, NSA_GROUPS * NSA_CMP_HID).astype(BF16)
    w2x = jnp.einsum('ijcd,gh->ijhdgc', nsa_cmp_w2, eye).reshape(n_even, 2, LANES, NSA_GROUPS * NSA_CMP_HID).astype(BF16)
    wa_o = ev_w_out[:, :MLA_HEADS * MLA_V].astype(BF16)
    wb_o = jnp.transpose(ev_w_out[:, MLA_HEADS * MLA_V:].reshape(n_even, NSA_GROUPS, NSA_HPG, hd, d),
                         (0, 2, 1, 3, 4)).reshape(n_even, NSA_HEADS * hd, d).astype(BF16)

    assert DIL_PAIRS == ((128, 1), (512, 4), (2048, 16))
    dil_cfg = tuple(zip(OD_DIL_STRIDES, (1, 1, 0)))
    dil_ok = (causal & (dist <= 128), causal & (dist <= 128), causal & (dist % 2 == 0))
    tab_dil = [_pair(_bias_tiles(t5_bias[:, gi * DIL_HPG:(gi + 1) * DIL_HPG], dil_ok[gi], dist_scale=dil_cfg[gi][0]))
               for gi in range(len(DIL_PAIRS))]
    tab_moba = _pair(_bias_tiles(t5_bias[:, DIL_SLOTS:DIL_SLOTS + MOBA_HEADS], causal))
    avg = np.zeros((LANES, s), np.float32)
    for h in range(MOBA_HEADS):
        for m in range(s // MOBA_BLOCK):
            avg[8 * h + m, m * MOBA_BLOCK:(m + 1) * MOBA_BLOCK] = 1.0 / MOBA_BLOCK
    avg = jnp.asarray(avg, BF16)
    od_w = od_w_in.astype(BF16)
    od_gain = jnp.concatenate([jnp.tile(dil_qk_g[:, 0], (1, 12)) * c64, jnp.tile(dil_qk_g[:, 1], (1, 12)),
                               jnp.ones((n_odd, 768), F32), jnp.tile(moba_qk_g[:, 0], (1, 4)) * c64,
                               jnp.tile(moba_qk_g[:, 1], (1, 4)), jnp.ones((n_odd, 256), F32)], axis=1)
    wd_o = od_w_out[:, :DIL_HPG * hd].astype(BF16)
    wm_o = od_w_out[:, DIL_HPG * hd:].astype(BF16)

    sh3 = lambda a: a.reshape(b, s, a.shape[-1])
    tr3 = lambda a: jnp.swapaxes(sh3(a), 1, 2)
    xf = x.reshape(n, d)
    for i in range(DEPTH):
        j = i // 2
        g_i = norm_g[i].reshape(3, 1, d)
        xf = _ffn(xf, g_i[0], mod[i, :, 0, 0], mod[i, :, 0, 1], mod[i, :, 0, 2],
                  wa_all[i, 0], wb_all[i, 0], wo_all[i, 0], s)
        if i % 2 == 0:
            mla_in, nsa_q, kc, vc, ks, vs, kw, vw = _proj(
                xf, g_i[1], mod[i, :, 1, 0], mod[i, :, 1, 1], ev_w[j], ev_gain[j][None], gm_proj,
                chunked(EV_META), EV_OUTS, s)
            qf, kf, vf = _mla_prep(mla_in, mla_q_norm_g[j][None], mla_kv_norm_g[j][None], wuq[j], wuk[j], wuv[j],
                                   gm_mla, gq[j][None], gkn[j][None], gkr[j][None], cos_t, sin_t, s)
            (o_a,) = _attn(sh3(qf), sh3(kf), vf, tab_mla, None, qc0=0, kc0=0, vb0=0, n_blk=MLA_HEADS // 2, qw=2,
                           tab_shared=True, out_dtype=BF16)
            kcmp, vcmpt = _nsa_cmp(sh3(kc), sh3(vc), pe2[j], w1x[j, :, 0], w1x[j, :, 1], w2x[j], gm64, gain_kc[j][None])
            o_c, sel = _nsa_sel(sh3(nsa_q), kcmp, vcmpt, bias_c, ovl)
            (o_s,) = _attn(sh3(nsa_q), sh3(ks), vs, tab_sel, sel, qc0=0, kc0=0, vb0=0, n_blk=NSA_HPG, qw=1,
                           kv_shared=True, sel_cfg=(TK // NSA_SLC_BLOCK, 32, 0), out_dtype=BF16)
            (o_w,) = _attn(sh3(nsa_q), sh3(kw), vw, tab_win, None, qc0=0, kc0=0, vb0=0, n_blk=NSA_HPG, qw=1,
                           kv_shared=True, backs=(2, 2, 2, 2), out_dtype=BF16)
            xf = _even_out(xf, o_a.reshape(n, -1), o_c.reshape(n, -1), o_s.reshape(n, -1), o_w.reshape(n, -1),
                           mla_in, eg, wa_o[j], wb_o[j], mod[i, :, 1, 2], s)
        else:
            pr, vd0t, vmt, qk1, qk2 = _proj(xf, g_i[1], mod[i, :, 1, 0], mod[i, :, 1, 1], od_w[j], od_gain[j][None],
                                            gm_proj, chunked(OD_META), OD_OUTS, s)
            pr3 = pr.reshape(b, s, pr.shape[-1])
            o_ds, lse_ds = [], []
            for gi, (r, bk) in enumerate(dil_cfg):
                if r == 1:
                    qk, vdt = pr3, vd0t
                else:
                    v = pr3[:, :, (2 + 2 * gi) * LANES:(4 + 2 * gi) * LANES].reshape(b, s // r, r, 2 * LANES)
                    qk, vdt = (qk1, qk2)[gi - 1], jnp.transpose(v, (0, 3, 2, 1)).reshape(b, 2 * LANES, s)
                o_g, lse_g = _attn(qk, qk, vdt, tab_dil[gi], None, qc0=0, kc0=2, vb0=0, n_blk=2,
                                   qw=1, backs=(bk,), want_lse=True, stride=r)
                o_ds.append(o_g.reshape(n, -1))
                lse_ds.append(lse_g.reshape(n, -1))
            selm = _moba_gate(pr3, avg, 4, 5)
            (o_m,) = _attn(pr3, pr3, vmt, tab_moba, selm, qc0=8, kc0=10, vb0=0, n_blk=2, qw=1,
                           sel_cfg=(1, 8, 2), out_dtype=BF16)
            xf = _odd_out(xf, o_ds, lse_ds, o_m.reshape(n, -1), wd_o[j], wm_o[j], mod[i, :, 1, 2], s)
        xf = _ffn(xf, g_i[2], mod[i, :, 2, 0], mod[i, :, 2, 1], mod[i, :, 2, 2],
                  wa_all[i, 1], wb_all[i, 1], wo_all[i, 1], s)
    return xf.reshape(b, s, d)
```

```python
import functools
import math

import numpy as np
import jax
import jax.numpy as jnp
from jax import lax
from jax.experimental import pallas as pl
from jax.experimental.pallas import tpu as pltpu

F32 = jnp.float32
BF16 = jnp.bfloat16

D_MODEL = 1024
DEPTH = 4
D_FF = 2752
HEAD_DIM = 64
NUM_BUCKETS = 32
T5_MAX_EXACT = 16
T5_MAX_DIST = 128
RMS_EPS = 1e-6
NEG_INF = -1e30
MLA_HEADS = 8
MLA_NOPE = 64
MLA_ROPE = 32
MLA_V = 64
MLA_Q_LORA = 256
MLA_KV_LORA = 128
ROPE_THETA = 10000.0
NSA_HEADS = 8
NSA_GROUPS = 2
NSA_HPG = 4
NSA_CMP_LEN = 32
NSA_CMP_STRIDE = 16
NSA_CMP_HID = 256
NSA_SLC_BLOCK = 64
NSA_SLC_TOP = 8
NSA_WINDOW = 512
NSA_FORCED = 1e6
DIL_PAIRS = ((128, 1), (512, 4), (2048, 16))
DIL_HPG = 4
DIL_SLOTS = len(DIL_PAIRS) * DIL_HPG
MOBA_HEADS = 4
MOBA_BLOCK = 256
MOBA_TOP = 3

LANES = 128
V7X_VMEM_BYTES = 64 * 1024 * 1024
VMEM_LIMIT = V7X_VMEM_BYTES * 7 // 8
TM = 512
TQ = 256
TK = 256
FF_CHUNK = 256
FF_PAD = -(-D_FF // FF_CHUNK) * FF_CHUNK
PROJ_CHUNK = 256
DEN_ROWS = 16
LOG2E = math.log2(math.e)
LN2 = math.log(2.0)


def _dot(a, b):
    return jnp.dot(a, b, preferred_element_type=F32)


def _dot_nt(a, b):
    return lax.dot_general(a, b, (((1,), (1,)), ((), ())), preferred_element_type=F32)


def _split(a):
    hi = a.astype(BF16)
    lo = (a - hi.astype(F32)).astype(BF16)
    return hi, lo


def _dot_hilo(a, b):
    hi, lo = _split(a)
    return _dot(hi, b) + _dot(lo, b)


def _sigmoid(x):
    return 1.0 / (1.0 + jnp.exp(-x))


def _modulated_norm(x, g, shift, scale):
    ms = jnp.mean(x * x, axis=-1, keepdims=True)
    y = x * lax.rsqrt(ms + RMS_EPS) * g
    return y * (1.0 + scale) + shift


def _params(*sem):
    return pltpu.CompilerParams(dimension_semantics=sem, vmem_limit_bytes=VMEM_LIMIT)


def _resident(shape):
    nd = len(shape)
    return pl.BlockSpec(shape, lambda *_: (0,) * nd, pipeline_mode=pl.Buffered(1))


def _ada_kernel(c_ref, w_ref, b_ref, o_ref):
    c = c_ref[...]
    ca = c * _sigmoid(c)
    o_ref[...] = jnp.dot(ca, w_ref[...], preferred_element_type=F32,
                         precision=lax.Precision.HIGHEST) + b_ref[...]


def _ada(c, ada_w, ada_b):
    depth, d, n = ada_w.shape
    b = c.shape[0]
    tn = 18 * LANES
    return pl.pallas_call(
        _ada_kernel,
        grid=(depth, n // tn),
        in_specs=[pl.BlockSpec((b, d), lambda l, j: (0, 0)),
                  pl.BlockSpec((None, d, tn), lambda l, j: (l, 0, j)),
                  pl.BlockSpec((None, 1, tn), lambda l, j: (l, 0, j))],
        out_specs=pl.BlockSpec((None, b, tn), lambda l, j: (l, 0, j)),
        out_shape=jax.ShapeDtypeStruct((depth, b, n), F32),
        compiler_params=_params("parallel", "parallel"),
        name="ada",
    )(c, ada_w, ada_b.reshape(depth, 1, n))


def _ffn_kernel(x_ref, g_ref, sh_ref, sc_ref, gt_ref, wa_ref, wb_ref, wo_ref, o_ref, y_ref, acc_ref):
    y_ref[...] = _modulated_norm(x_ref[...], g_ref[...], sh_ref[...], sc_ref[...]).astype(BF16)
    for c in range(FF_PAD // FF_CHUNK):
        sl = slice(c * FF_CHUNK, (c + 1) * FF_CHUNK)
        a = _dot(y_ref[...], wa_ref[:, sl])
        b = _dot(y_ref[...], wb_ref[:, sl])
        u = (a * _sigmoid(a) * b).astype(BF16)
        contrib = _dot(u, wo_ref[sl, :])
        if c == 0:
            acc_ref[...] = contrib
        else:
            acc_ref[...] += contrib
    o_ref[...] = x_ref[...] + 0.5 * gt_ref[...] * acc_ref[...]


def _mod_spec(tiles_per_batch, d):
    return pl.BlockSpec((None, 1, d), lambda i: (i // tiles_per_batch, 0, 0))


def _ffn(x, g, shift, scale, gate, wa, wb, wo, s):
    n, d = x.shape
    tpb = s // TM
    return pl.pallas_call(
        _ffn_kernel,
        grid=(n // TM,),
        in_specs=[pl.BlockSpec((TM, d), lambda i: (i, 0)),
                  _resident((1, d)),
                  _mod_spec(tpb, d), _mod_spec(tpb, d), _mod_spec(tpb, d),
                  _resident(wa.shape), _resident(wb.shape), _resident(wo.shape)],
        out_specs=pl.BlockSpec((TM, d), lambda i: (i, 0)),
        out_shape=jax.ShapeDtypeStruct((n, d), F32),
        scratch_shapes=[pltpu.VMEM((TM, d), BF16), pltpu.VMEM((TM, d), F32)],
        compiler_params=_params("parallel"),
        name="ffn",
    )(x, g, shift, scale, gate, wa, wb, wo)


def _proj_kernel(meta, kinds, seq, x_ref, g_ref, sh_ref, sc_ref, w_ref, gain_ref, gm_ref, *rest):
    n_out = len(kinds)
    outs = rest[:n_out]
    y_ref, stage_ref = rest[n_out], rest[n_out + 1]
    tile_in_seq = pl.program_id(0) % (seq // TM)
    y_ref[...] = _modulated_norm(x_ref[...], g_ref[...], sh_ref[...], sc_ref[...]).astype(BF16)
    for c, halves in enumerate(meta):
        sl = slice(c * PROJ_CHUNK, (c + 1) * PROJ_CHUNK)
        z = _dot(y_ref[...], w_ref[:, sl])
        if any(normed for _, _, normed in halves):
            msq = _dot((z * z).astype(BF16), gm_ref[...])
            zn = z * lax.rsqrt(msq + RMS_EPS) * gain_ref[:, sl]
        for hf, (oi, off, normed) in enumerate(halves):
            src = (zn if normed else z)[:, hf * LANES:(hf + 1) * LANES]
            kind = kinds[oi]
            if kind == "cols":
                src = src * gain_ref[:, c * PROJ_CHUNK + hf * LANES:c * PROJ_CHUNK + (hf + 1) * LANES]
                outs[oi][off:off + LANES, :] = src.T.astype(outs[oi].dtype)
            elif kind == "rows":
                outs[oi][:, off:off + LANES] = src.astype(outs[oi].dtype)
            else:
                per = TM // kind
                stage_ref[...] = src
                for cs in range(kind):
                    dest = pl.multiple_of(cs * (seq // kind) + tile_in_seq * per, per)
                    outs[oi][pl.ds(dest, per), off:off + LANES] = (
                        stage_ref[pl.ds(cs, per, stride=kind), :].astype(outs[oi].dtype))


def _proj(x, g, shift, scale, w, gain, gm, meta, out_defs, s):
    n, d = x.shape
    tpb = s // TM
    specs = {"rows": lambda wd: pl.BlockSpec((TM, wd), lambda i: (i, 0)),
             "cols": lambda wd: pl.BlockSpec((None, wd, TM), lambda i: (i // tpb, 0, i % tpb))}
    stream = lambda wd: pl.BlockSpec((None, s, wd), lambda i: (i // tpb, 0, 0))
    shapes = {"rows": lambda wd: (n, wd), "cols": lambda wd: (n // s, wd, s)}
    kinds = tuple(k for _, _, k in out_defs)
    return pl.pallas_call(
        functools.partial(_proj_kernel, meta, kinds, s),
        grid=(n // TM,),
        in_specs=[pl.BlockSpec((TM, d), lambda i: (i, 0)),
                  _resident((1, d)),
                  _mod_spec(tpb, d), _mod_spec(tpb, d),
                  _resident(w.shape), _resident(gain.shape), _resident(gm.shape)],
        out_specs=[specs.get(k, stream)(wd) for wd, _, k in out_defs],
        out_shape=[jax.ShapeDtypeStruct(shapes.get(k, lambda wd: (n // s, s, wd))(wd), dt) for wd, dt, k in out_defs],
        scratch_shapes=[pltpu.VMEM((TM, d), BF16), pltpu.VMEM((TM, LANES), F32)],
        compiler_params=_params("arbitrary"),
        name="proj",
    )(x, g, shift, scale, w, gain, gm)


def _mla_prep_kernel(in_ref, qg_ref, kvg_ref, wuq_ref, wuk_ref, wuv_ref, gm_ref,
                     gq_ref, gkn_ref, gkr_ref, cos_ref, sin_ref, q_out, k_out, v_out):
    def rms(z, g):
        return z * lax.rsqrt(jnp.mean(z * z, axis=-1, keepdims=True) + RMS_EPS) * g

    cqn = rms(in_ref[:, 0:MLA_Q_LORA], qg_ref[...]).astype(BF16)
    ckvn = rms(in_ref[:, MLA_Q_LORA:MLA_Q_LORA + MLA_KV_LORA], kvg_ref[...]).astype(BF16)
    c3 = in_ref[:, 3 * LANES:4 * LANES]
    cos = cos_ref[...]
    sin = sin_ref[...]
    gm = gm_ref[...]
    half = MLA_ROPE // 2
    slot_lane = lax.broadcasted_iota(jnp.int32, (1, 2 * LANES), 1) & (LANES - 1)
    first_half = slot_lane < MLA_NOPE + half

    def norm_rope(z, gain):
        msq = _dot((z * z).astype(BF16), gm)
        z = z * lax.rsqrt(msq + RMS_EPS) * gain
        rot = jnp.where(first_half, -pltpu.roll(z, 2 * LANES - half, 1), pltpu.roll(z, half, 1))
        return z * cos + rot * sin

    kr = norm_rope(jnp.concatenate([c3, c3], axis=1), gkr_ref[...])
    for h in range(MLA_HEADS // 2):
        sl = slice(h * 2 * LANES, (h + 1) * 2 * LANES)
        q_out[:, sl] = norm_rope(_dot(cqn, wuq_ref[:, sl]), gq_ref[...]).astype(BF16)
        k_out[:, sl] = (norm_rope(_dot(ckvn, wuk_ref[:, sl]), gkn_ref[...]) + kr).astype(BF16)
    v_out[...] = _dot_nt(wuv_ref[...], ckvn).astype(BF16)


def _mla_prep(mla_in, qg, kvg, wuq, wuk, wuv, gm, gq, gkn, gkr, cos_t, sin_t, s):
    n = mla_in.shape[0]
    tpb = s // TM
    hw = MLA_HEADS * LANES
    vw = MLA_HEADS * MLA_V
    tab = pl.BlockSpec((TM, 2 * LANES), lambda i: (i % tpb, 0))
    consts = [qg, kvg, wuq, wuk, wuv, gm, gq, gkn, gkr]
    return pl.pallas_call(
        _mla_prep_kernel,
        grid=(n // TM,),
        in_specs=[pl.BlockSpec((TM, 4 * LANES), lambda i: (i, 0))] + [_resident(a.shape) for a in consts] + [tab, tab],
        out_specs=[pl.BlockSpec((TM, hw), lambda i: (i, 0)), pl.BlockSpec((TM, hw), lambda i: (i, 0)),
                   pl.BlockSpec((None, vw, TM), lambda i: (i // tpb, 0, i % tpb))],
        out_shape=[jax.ShapeDtypeStruct((n, hw), BF16), jax.ShapeDtypeStruct((n, hw), BF16),
                   jax.ShapeDtypeStruct((n // s, vw, s), BF16)],
        compiler_params=_params("parallel"),
        name="mla_prep",
    )(mla_in, *consts, cos_t, sin_t)


def _attn_kernel(cfg, q_ref, k_ref, vt_ref, tab_ref, *rest):
    n_tab, qw, backs, sel_cfg, want_lse, n_qt, stride = cfg
    n_qs = n_qt // stride
    if sel_cfg is not None:
        sel_ref, rest = rest[0], rest[1:]
        sel_bpt, sel_stride, sel_pair_mul = sel_cfg
    o_ref = rest[0]
    lse_ref = rest[1] if want_lse else None
    st_scr, p_scr, acc_fin, m_fin = rest[-6:-4], rest[-4:-2], rest[-2], rest[-1]
    blk = pl.program_id(1)
    lane = lax.broadcasted_iota(jnp.int32, (1, LANES), 1)

    counts = tuple(sum(min(t % n_qs, bk) + 1 for t in range(n_qt)) for bk in backs)
    back = jnp.int32(backs[-1])
    n_tiles = jnp.int32(counts[-1])
    for bi in range(len(backs) - 2, -1, -1):
        back = jnp.where(blk == bi, jnp.int32(backs[bi]), back)
        n_tiles = jnp.where(blk == bi, jnp.int32(counts[bi]), n_tiles)

    ones_rows = jnp.ones((DEN_ROWS, TK), BF16)

    def first_key_tile(qi):
        return qi - jnp.minimum(qi & (n_qs - 1), back)

    def rows(t, size):
        return pl.ds(t * size if isinstance(t, int) else pl.multiple_of(t * size, size), size)

    def out_rows(t, size):
        if stride == 1:
            return rows(t, size)
        return pl.ds((t >> int(math.log2(n_qs))) + stride * size * (t & (n_qs - 1)), size, stride=stride)

    def logits_to(slot, qi, j, filler):
        d = jnp.where(filler, n_tab, jnp.minimum(qi - j, n_tab - 1))
        if qw == 1:
            q = q_ref[rows(qi, TQ), :]
            zero = jnp.zeros_like(q)
            qs = [jnp.where(lane < HEAD_DIM, q, zero), jnp.where(lane >= HEAD_DIM, q, zero)]
        else:
            qs = [q_ref[rows(qi, TQ), s * LANES:(s + 1) * LANES] for s in range(2)]
        for s in range(2):
            kj = k_ref[rows(j, TK), :] if qw == 1 else k_ref[rows(j, TK), s * LANES:(s + 1) * LANES]
            st = _dot_nt(kj, qs[s])
            if sel_cfg is not None:
                off = sel_stride * (s + sel_pair_mul * blk) + sel_bpt * j
                kb = TK // sel_bpt
                qcol = pl.ds(pl.multiple_of(qi * TQ, TQ), TQ)
                st = jnp.concatenate(
                    [st[bk * kb:(bk + 1) * kb] + sel_ref[pl.ds(off + bk, 1), qcol]
                     for bk in range(sel_bpt)], axis=0)
            st_scr[slot][s] = st + tab_ref[s, d]

    def accumulate(slot, j, alphas, accs):
        krow = pl.multiple_of(j * TK, TK)
        new = []
        for s in range(2):
            vt = jnp.concatenate([vt_ref[s * HEAD_DIM:(s + 1) * HEAD_DIM, pl.ds(krow, TK)], ones_rows], axis=0)
            new.append(alphas[s] * accs[s] + _dot(vt, p_scr[slot][s]))
        return tuple(new)

    def finalize(qi, ms, accs):
        for s in range(2):
            acc_fin[qi, s] = accs[s]
            if want_lse:
                m_fin[qi, s] = jnp.broadcast_to(ms[s], m_fin.shape[2:])

    def write_out(t):
        accs = [acc_fin[t, s] for s in range(2)]
        dens = [acc[HEAD_DIM:HEAD_DIM + 1] for acc in accs]
        out_t = jnp.concatenate([acc[:HEAD_DIM] / l for acc, l in zip(accs, dens)], axis=0)
        o_ref[out_rows(t, TQ), :] = out_t.T.astype(o_ref.dtype)
        if want_lse:
            lse_t = jnp.concatenate([jnp.broadcast_to((m_fin[t, s][0:1] + jnp.log2(dens[s])) * LN2, (HEAD_DIM, TQ))
                                     for s in range(2)], axis=0)
            lse_ref[out_rows(t, TQ), :] = lse_t.T

    def advance(q, j):
        last = j == q
        at_end = jnp.logical_and(last, q == n_qt - 1)
        starts = jnp.logical_and(last, jnp.logical_not(at_end))
        qn = jnp.where(starts, q + 1, q)
        jn = jnp.where(at_end, j, jnp.where(last, first_key_tile(q + 1), j + 1))
        return qn, jn, starts, at_end

    def softmax(slot, tile, stats):
        is_first = tile[2]
        keep = jnp.where(is_first, 0.0, 1.0)
        new_stats, alphas = [], []
        for s in range(2):
            m = jnp.where(is_first, NEG_INF, stats[s])
            st = st_scr[slot][s]
            m_new = jnp.maximum(m, jnp.max(st, axis=0, keepdims=True))
            alphas.append(jnp.exp2(m - m_new) * keep)
            new_stats.append(m_new)
            p_scr[slot][s] = jnp.exp2(st - m_new).astype(BF16)
        return tuple(new_stats), tuple(alphas)

    def body(u, carry):
        tile_a, (q2, j2), (q1, j1, first1), stats2, stats1, alphas2, alphas1, accs = carry
        qa, ja, first_a, _ = tile_a
        tile_b = advance(qa, ja)
        qb, jb, first_b, _ = tile_b
        tile_c = advance(qb, jb)
        logits_to(1, qb, jb, tile_b[3])
        accs_x = accumulate(0, j2, alphas2, accs)
        accs_y = accumulate(1, j1, alphas1, accs_x)
        stats_a, alphas_a = softmax(0, tile_a, stats1)
        logits_to(0, tile_c[0], tile_c[1], tile_c[3])
        stats_b, alphas_b = softmax(1, tile_b, stats_a)

        finalize(q2, stats2, accs_x)
        finalize(q1, stats1, accs_y)
        return tile_c, (qa, ja), (qb, jb, first_b), stats_a, stats_b, alphas_a, alphas_b, accs_y

    zero_i = jnp.int32(0)
    logits_to(0, zero_i, zero_i, zero_i != 0)
    for p_slot in p_scr:
        p_slot[...] = jnp.zeros(p_slot.shape, BF16)
    stats0 = tuple(jnp.full((1, TQ), NEG_INF, F32) for _ in range(2))
    ones = tuple(jnp.ones((1, TQ), F32) for _ in range(2))
    init = ((zero_i, zero_i, zero_i == 0, zero_i != 0), (zero_i, zero_i), (zero_i, zero_i, zero_i != 0),
            stats0, stats0, ones, ones, tuple(jnp.zeros((HEAD_DIM + DEN_ROWS, TQ), F32) for _ in range(2)))
    _, (q2, j2), (q1, j1, first1), stats2, stats1, alphas2, alphas1, accs = lax.fori_loop(
        0, (n_tiles + 1) // 2, body, init)
    accs_x = accumulate(0, j2, alphas2, accs)
    finalize(q2, stats2, accs_x)
    finalize(q1, stats1, accumulate(1, j1, alphas1, accs_x))
    for t in range(n_qt):
        write_out(t)


def _attn(q, k, vt, tab, sel, *, qc0, kc0, vb0, n_blk, qw, backs=None, kv_shared=False,
          tab_shared=False, sel_cfg=None, out_dtype=F32, want_lse=False, stride=1):
    b, s = q.shape[:2]
    n_tab = tab.shape[2] - 1
    n_qt = s // TQ
    if backs is None:
        backs = (n_qt,)
    cfg = (n_tab, qw, backs, sel_cfg, want_lse, n_qt, stride)
    kidx = (lambda bb, h: (bb, 0, kc0)) if kv_shared else (lambda bb, h: (bb, 0, kc0 + h))
    vidx = (lambda bb, h: (bb, vb0, 0)) if kv_shared else (lambda bb, h: (bb, vb0 + h, 0))
    tidx = (lambda bb, h: (0, 0, 0, 0, 0)) if tab_shared else (lambda bb, h: (h, 0, 0, 0, 0))
    in_specs = [pl.BlockSpec((None, s, qw * LANES), lambda bb, h: (bb, 0, qc0 + h)),
                pl.BlockSpec((None, s, qw * LANES), kidx),
                pl.BlockSpec((None, 2 * HEAD_DIM, s), vidx),
                pl.BlockSpec((None, 2, n_tab + 1, TK, TQ), tidx)]
    args = [q, k, vt, tab]
    if sel_cfg is not None:
        in_specs.append(pl.BlockSpec((None, LANES, s), lambda bb, h: (bb, 0, 0)))
        args.append(sel)
    ospec = pl.BlockSpec((None, s, LANES), lambda bb, h: (bb, 0, h))
    out_specs = [ospec]
    out_shape = [jax.ShapeDtypeStruct((b, s, n_blk * LANES), out_dtype)]
    if want_lse:
        out_specs.append(ospec)
        out_shape.append(jax.ShapeDtypeStruct((b, s, n_blk * LANES), F32))
    return pl.pallas_call(
        functools.partial(_attn_kernel, cfg),
        grid=(b, n_blk),
        in_specs=in_specs,
        out_specs=out_specs,
        out_shape=out_shape,
        scratch_shapes=([pltpu.VMEM((2, TK, TQ), F32)] * 2 + [pltpu.VMEM((2, TK, TQ), BF16)] * 2
                        + [pltpu.VMEM((n_qt, 2, HEAD_DIM + DEN_ROWS, TQ), F32), pltpu.VMEM((n_qt, 2, 8, TQ), F32)]),
        compiler_params=_params("parallel", "parallel"),
        name="attn",
    )(*args)


def _nsa_cmp_kernel(kc_ref, vc_ref, pe_ref, wlo_ref, whi_ref, w2_ref, gm_ref, gain_ref, kcmp_ref, vcmpt_ref):
    nch = kcmp_ref.shape[0]

    def hidden(c_ref, j):
        lo = hi = None
        for l in range(NSA_CMP_STRIDE):
            rows = c_ref[pl.ds(l, nch, stride=NSA_CMP_STRIDE), :]
            sl = slice(l * LANES, (l + 1) * LANES)
            t_lo = _dot((rows + pe_ref[j, 0][:, sl]).astype(BF16), wlo_ref[j][sl, :])
            t_hi = _dot((rows + pe_ref[j, 1][:, sl]).astype(BF16), whi_ref[j][sl, :])
            lo = t_lo if lo is None else lo + t_lo
            hi = t_hi if hi is None else hi + t_hi
        h = lo + pltpu.roll(hi, nch - 1, 0)
        return (h * _sigmoid(h)).astype(BF16)

    kz = _dot_nt(hidden(kc_ref, 0), w2_ref[0])
    msq = _dot_hilo(kz * kz, gm_ref[...])
    kcmp_ref[...] = (kz * lax.rsqrt(msq + RMS_EPS) * gain_ref[...]).astype(BF16)
    vcmpt_ref[...] = _dot_nt(w2_ref[1], hidden(vc_ref, 1)).astype(BF16)


def _nsa_cmp(kc3, vc3, pe, wlo, whi, w2, gm, gain):
    b, s, width = kc3.shape
    nch = s // NSA_CMP_STRIDE
    consts = [pe, wlo, whi, w2, gm, gain]
    blk = pl.BlockSpec((None, s, width), lambda i: (i, 0, 0))
    oblk = pl.BlockSpec((None, nch, LANES), lambda i: (i, 0, 0))
    return pl.pallas_call(
        _nsa_cmp_kernel,
        grid=(b,),
        in_specs=[blk, blk] + [_resident(a.shape) for a in consts],
        out_specs=[oblk, oblk],
        out_shape=[jax.ShapeDtypeStruct((b, nch, LANES), BF16)] * 2,
        compiler_params=_params("parallel"),
        name="nsa_cmp",
    )(kc3, vc3, *consts)


def _rank_keep(score, ids, top):
    cnt = jnp.zeros(score.shape, jnp.int32)
    for mp in range(score.shape[0]):
        other = score[mp:mp + 1, :]
        tie = jnp.where(mp < ids, 1, 0)
        cnt = cnt + jnp.where(other > score, 1, jnp.where(other == score, tie, 0))
    return cnt < top


def _nsa_sel_kernel(q_ref, kcmp_ref, vcmpt_ref, bias_ref, ovl_ref, oc_ref, sel_ref):
    qi = pl.program_id(1)
    lane = lax.broadcasted_iota(jnp.int32, (1, LANES), 1)
    row = lax.broadcasted_iota(jnp.int32, (LANES, 1), 0)
    t = qi * TQ + lax.broadcasted_iota(jnp.int32, (1, TQ), 1)
    mask_c = (NSA_CMP_STRIDE * row + NSA_CMP_LEN - 1) <= t
    kcmp = kcmp_ref[...]
    heads = [(p, g) for p in range(NSA_HPG) for g in range(NSA_GROUPS)]
    raw = []
    for p, g in heads:
        qp = q_ref[:, p * LANES:(p + 1) * LANES]
        raw.append(_dot_nt(kcmp, jnp.where((lane >> 6) == g, qp, jnp.zeros_like(qp))))
    pcs = []
    for (p, g), r in zip(heads, raw):
        lg = jnp.where(mask_c, r + bias_ref[p * NSA_GROUPS + g], NEG_INF)
        m = jnp.max(lg, axis=0, keepdims=True)
        e = jnp.where(mask_c, jnp.exp2(lg - m), 0.0)
        den = jnp.maximum(jnp.sum(e, axis=0, keepdims=True), 1e-30)
        pcs.append(e / den)
    imp = jnp.zeros((LANES, TQ), F32)
    ocs = []
    for (p, g), pc in zip(heads, pcs):
        hi, lo = _split(pc)
        ocs.append(_dot(vcmpt_ref[g * HEAD_DIM:(g + 1) * HEAD_DIM, :], hi))
        imp = imp + _dot(ovl_ref[g], hi) + _dot(ovl_ref[g], lo)
    for p in range(NSA_HPG):
        oc_ref[:, p * LANES:(p + 1) * LANES] = jnp.concatenate([ocs[2 * p], ocs[2 * p + 1]], axis=0).T.astype(BF16)

    n_slc = 32
    ids = row[0:n_slc]
    cur = t >> 6
    forced = (ids == 0) | (ids == cur) | (ids == cur - 1)
    masks = []
    for g in range(NSA_GROUPS):
        score = jnp.where(forced, NSA_FORCED, jnp.where(ids <= cur, imp[g * n_slc:(g + 1) * n_slc], NEG_INF))
        keep = _rank_keep(score, ids, NSA_SLC_TOP) & (score > 0.5 * NEG_INF)
        masks.append(jnp.where(keep, 0.0, NEG_INF))
    masks.append(jnp.full((LANES - NSA_GROUPS * n_slc, TQ), NEG_INF, F32))
    sel_ref[...] = jnp.concatenate(masks, axis=0)


def _nsa_sel(q, kcmp, vcmpt, bias_c, ovl):
    b, s, w = q.shape
    return pl.pallas_call(
        _nsa_sel_kernel,
        grid=(b, s // TQ),
        in_specs=[pl.BlockSpec((None, TQ, w), lambda bb, i: (bb, i, 0)),
                  pl.BlockSpec((None, LANES, LANES), lambda bb, i: (bb, 0, 0)),
                  pl.BlockSpec((None, LANES, LANES), lambda bb, i: (bb, 0, 0)),
                  pl.BlockSpec((NSA_HEADS, LANES, TQ), lambda bb, i: (0, 0, i)),
                  _resident(ovl.shape)],
        out_specs=[pl.BlockSpec((None, TQ, w), lambda bb, i: (bb, i, 0)),
                   pl.BlockSpec((None, LANES, TQ), lambda bb, i: (bb, 0, i))],
        out_shape=[jax.ShapeDtypeStruct((b, s, w), BF16), jax.ShapeDtypeStruct((b, LANES, s), F32)],
        compiler_params=_params("parallel", "parallel"),
        name="nsa_sel",
    )(q, kcmp, vcmpt, bias_c, ovl)


def _moba_gate_kernel(q_ref, k_ref, avg_ref, sel_ref, kh_ref, kl_ref):
    qi = pl.program_id(1)
    nb = 8

    @pl.when(qi == 0)
    def _():
        kmean = _dot(avg_ref[...], k_ref[...])
        r2 = lax.broadcasted_iota(jnp.int32, kmean.shape, 0)
        c2 = lax.broadcasted_iota(jnp.int32, kmean.shape, 1)
        kh_ref[...], kl_ref[...] = _split(jnp.where((r2 >> 3) == (c2 >> 6), kmean, 0.0))

    q = q_ref[...]
    gate = _dot_nt(kh_ref[...], q) + _dot_nt(kl_ref[...], q)
    ids = lax.broadcasted_iota(jnp.int32, (nb, 1), 0)
    past = ids < qi
    masks = []
    for h in range(MOBA_HEADS):
        score = jnp.where(past, gate[h * nb:(h + 1) * nb], NEG_INF)
        keep = (_rank_keep(score, ids, MOBA_TOP) & past) | (ids == qi)
        masks.append(jnp.where(keep, 0.0, NEG_INF))
    masks.append(jnp.full((LANES - MOBA_HEADS * nb, TQ), NEG_INF, F32))
    sel_ref[...] = jnp.concatenate(masks, axis=0)


def _moba_gate(p_arr, avg, qc, kc):
    b, s, _ = p_arr.shape
    w = MOBA_HEADS * HEAD_DIM
    return pl.pallas_call(
        _moba_gate_kernel,
        grid=(b, s // TQ),
        in_specs=[pl.BlockSpec((None, TQ, w), lambda bb, i: (bb, i, qc)),
                  pl.BlockSpec((None, s, w), lambda bb, i: (bb, 0, kc)),
                  _resident(avg.shape)],
        out_specs=pl.BlockSpec((None, LANES, TQ), lambda bb, i: (bb, 0, i)),
        out_shape=jax.ShapeDtypeStruct((b, LANES, s), F32),
        scratch_shapes=[pltpu.VMEM((LANES, w), BF16)] * 2,
        compiler_params=_params("parallel", "arbitrary"),
        name="moba_gate",
    )(p_arr, p_arr, avg)


def _even_out_kernel(x_ref, oa_ref, oc_ref, os_ref, ow_ref, gl_ref, eg_ref, wa_ref, wb_ref, gt_ref, o_ref):
    sg = _sigmoid(gl_ref[...])
    hi, lo = _split(sg)
    nsa = None
    for br, src in enumerate((oc_ref, os_ref, ow_ref)):
        gexp = _dot(hi, eg_ref[br]) + _dot(lo, eg_ref[br])
        term = gexp * src[...]
        nsa = term if nsa is None else nsa + term
    m = _dot(oa_ref[...], wa_ref[...]) + _dot(nsa.astype(BF16), wb_ref[...])
    o_ref[...] = x_ref[...] + gt_ref[...] * m


def _even_out(x, o_a, o_c, o_s, o_w, mla_in, eg, wa, wb, gate, s):
    n, d = x.shape
    tpb = s // TM
    row = lambda wd: pl.BlockSpec((TM, wd), lambda i: (i, 0))
    return pl.pallas_call(
        _even_out_kernel,
        grid=(n // TM,),
        in_specs=[row(d), row(o_a.shape[1]), row(o_c.shape[1]), row(o_s.shape[1]), row(o_w.shape[1]),
                  pl.BlockSpec((TM, LANES), lambda i: (i, 3)),
                  _resident(eg.shape), _resident(wa.shape), _resident(wb.shape), _mod_spec(tpb, d)],
        out_specs=row(d),
        out_shape=jax.ShapeDtypeStruct((n, d), F32),
        compiler_params=_params("parallel"),
        name="even_out",
    )(x, o_a, o_c, o_s, o_w, mla_in, eg, wa, wb, gate)


def _odd_out_kernel(x_ref, od0_ref, od1_ref, od2_ref, ls0_ref, ls1_ref, ls2_ref, om_ref, wd_ref, wm_ref, gt_ref, o_ref):
    ods = (od0_ref, od1_ref, od2_ref)
    ls = [r[...] for r in (ls0_ref, ls1_ref, ls2_ref)]
    mx = jnp.maximum(jnp.maximum(ls[0], ls[1]), ls[2])
    es = [jnp.exp(l - mx) for l in ls]
    den = es[0] + es[1] + es[2]
    merged = None
    for g in range(len(DIL_PAIRS)):
        term = (es[g] / den) * ods[g][...]
        merged = term if merged is None else merged + term
    m = _dot(merged.astype(BF16), wd_ref[...]) + _dot(om_ref[...], wm_ref[...])
    o_ref[...] = x_ref[...] + gt_ref[...] * m


def _odd_out(x, o_ds, lse_ds, o_m, wd, wm, gate, s):
    n, d = x.shape
    tpb = s // TM
    row = lambda wd_: pl.BlockSpec((TM, wd_), lambda i: (i, 0))
    return pl.pallas_call(
        _odd_out_kernel,
        grid=(n // TM,),
        in_specs=[row(d)] + [row(a.shape[1]) for a in (*o_ds, *lse_ds)] + [row(o_m.shape[1]),
                  _resident(wd.shape), _resident(wm.shape), _mod_spec(tpb, d)],
        out_specs=row(d),
        out_shape=jax.ShapeDtypeStruct((n, d), F32),
        compiler_params=_params("parallel"),
        name="odd_out",
    )(x, *o_ds, *lse_ds, o_m, wd, wm, gate)


def _t5_bucket(dist):
    n = jnp.maximum(jnp.asarray(dist, jnp.int32), 0)
    nf = jnp.maximum(n, 1).astype(F32)
    large = T5_MAX_EXACT + (jnp.log(nf / T5_MAX_EXACT) / math.log(T5_MAX_DIST / T5_MAX_EXACT)
                            * (NUM_BUCKETS - T5_MAX_EXACT)).astype(jnp.int32)
    return jnp.where(n < T5_MAX_EXACT, n, jnp.minimum(large, NUM_BUCKETS - 1))


TOEP_PERIOD = 4 * TQ


def _toeplitz_dist():
    j = np.arange(TOEP_PERIOD)
    return np.where(j < 3 * TQ, j, j - TOEP_PERIOD)


def _toeplitz_kernel(n_tab, u_ref, o_ref):
    x = jnp.broadcast_to(u_ref[...], (TK, TOEP_PERIOD))
    y = pltpu.roll(x, 0, 1, stride=1, stride_axis=0)
    for dlt in range(n_tab):
        o_ref[dlt] = y[:, dlt * TQ:(dlt + 1) * TQ]
    o_ref[n_tab] = jnp.full((TK, TQ), NEG_INF * LOG2E, F32)


def _toeplitz_tiles(u, n_tab):
    h = u.shape[0]
    return pl.pallas_call(
        functools.partial(_toeplitz_kernel, n_tab),
        grid=(h,),
        in_specs=[pl.BlockSpec((None, 1, TOEP_PERIOD), lambda i: (i, 0, 0))],
        out_specs=pl.BlockSpec((None, n_tab + 1, TK, TQ), lambda i: (i, 0, 0, 0)),
        out_shape=jax.ShapeDtypeStruct((h, n_tab + 1, TK, TQ), F32),
        compiler_params=_params("parallel"),
        name="toeplitz",
    )(u.reshape(h, 1, TOEP_PERIOD))


def _cmp_bias_kernel(u_ref, o_ref):
    x = jnp.broadcast_to(u_ref[...], o_ref.shape)
    o_ref[...] = pltpu.roll(x, 0, 1, stride=NSA_CMP_STRIDE, stride_axis=0)


def _cmp_bias(u, n_rows):
    h, s = u.shape
    return pl.pallas_call(
        _cmp_bias_kernel,
        grid=(h,),
        in_specs=[pl.BlockSpec((None, 1, s), lambda i: (i, 0, 0))],
        out_specs=pl.BlockSpec((None, n_rows, s), lambda i: (i, 0, 0)),
        out_shape=jax.ShapeDtypeStruct((h, n_rows, s), F32),
        compiler_params=_params("parallel"),
        name="cmp_bias",
    )(u.reshape(h, 1, s))


def _bias_tiles(t5_cols, ok, n_tab=3, dist_scale=1):
    dist = _toeplitz_dist()
    bias = jnp.transpose(t5_cols[_t5_bucket(dist * dist_scale)])
    u = jnp.where(jnp.asarray(ok)[None], bias, NEG_INF) * LOG2E
    return _toeplitz_tiles(u, n_tab)


def _pair(tiles):
    h = tiles.shape[0]
    return tiles.reshape(h // 2, 2, *tiles.shape[1:])


def _group_mean_np(sizes, width=LANES):
    gm = np.zeros((width, width), np.float32)
    o = 0
    for sz in sizes:
        gm[o:o + sz, o:o + sz] = 1.0 / sz
        o += sz
    return gm


def _group_mean_matrix(sizes, width=LANES):
    return jnp.asarray(_group_mean_np(sizes, width), BF16)


EV_META = ((0, 0, False), (0, 128, False), (0, 256, False), (0, 384, False),
           (1, 0, True), (1, 128, True), (1, 256, True), (1, 384, True),
           (2, 0, False), (3, 0, False), (4, 0, True), (5, 0, False), (6, 0, True), (7, 0, False))
EV_OUTS = [(512, F32, "rows"), (512, BF16, "rows"), (128, F32, "rows"), (128, F32, "rows"), (128, BF16, "rows"),
           (128, BF16, "cols"), (128, BF16, "rows"), (128, BF16, "cols")]
OD_DIL_STRIDES = (1, 4, 8)
OD_META = (((0, 0, True), (0, LANES, True)) + tuple((3 + g, h * LANES, True) for g in range(2) for h in range(2))
           + ((0, 2 * LANES, True), (0, 3 * LANES, True))
           + tuple((3 + g, (2 + h) * LANES, True) for g in range(2) for h in range(2))
           + ((1, 0, False), (1, LANES, False)) + tuple((0, (4 + c) * LANES, False) for c in range(4))
           + tuple((0, (8 + c) * LANES, True) for c in range(4)) + ((2, 0, False), (2, LANES, False)))
OD_OUTS = [(12 * LANES, BF16, "rows"), (2 * LANES, BF16, "cols"), (2 * LANES, BF16, "cols"),
           (4 * LANES, BF16, OD_DIL_STRIDES[1]), (4 * LANES, BF16, OD_DIL_STRIDES[2])]


def _even_w_in(w):
    jn, d, _ = w.shape
    z = lambda n_: jnp.zeros((jn, d, n_), w.dtype)
    nq = w[:, :, 416:928].reshape(jn, d, NSA_GROUPS, NSA_HPG, HEAD_DIM)
    nq = jnp.transpose(nq, (0, 1, 3, 2, 4)).reshape(jn, d, NSA_HEADS * HEAD_DIM)
    chunk3 = jnp.concatenate([w[:, :, 1696:1720], z(HEAD_DIM - 24), w[:, :, 384:416], z(LANES - 96)], axis=-1)
    return jnp.concatenate([w[:, :, 0:384], chunk3, nq, w[:, :, 928:1696]], axis=-1)


def kernel(x, c, t5_bias, ada_w, ada_b, norm_g, ffn_w_in, ffn_w_out, ev_w_in, ev_w_out, mla_q_norm_g,
           mla_kv_norm_g, mla_w_uq, mla_w_ukv, mla_qk_g, nsa_cmp_pe, nsa_cmp_w1, nsa_cmp_w2, nsa_qk_g,
           od_w_in, od_w_out, dil_qk_g, moba_qk_g):
    b, s, d = x.shape
    assert (s, d) == (2048, D_MODEL) and s % TM == 0 and TQ == MOBA_BLOCK and TQ == TK
    n = b * s
    hd = HEAD_DIM
    n_even = ev_w_in.shape[0]
    n_odd = od_w_in.shape[0]
    c64 = hd ** -0.5 * LOG2E
    c96 = (MLA_NOPE + MLA_ROPE) ** -0.5 * LOG2E

    mod = _ada(c, ada_w, ada_b).reshape(DEPTH, b, 3, 3, 1, d)

    dist = _toeplitz_dist()
    causal = dist >= 0
    gm64 = _group_mean_matrix((hd, hd))
    gm_proj = _group_mean_matrix((hd,) * (PROJ_CHUNK // hd), PROJ_CHUNK)
    per_chunk = PROJ_CHUNK // LANES
    chunked = lambda m: tuple(tuple(m[i:i + per_chunk]) for i in range(0, len(m), per_chunk))

    padc = FF_PAD - D_FF
    wa_all = jnp.pad(ffn_w_in[..., :D_FF], ((0, 0), (0, 0), (0, 0), (0, padc))).astype(BF16)
    wb_all = jnp.pad(ffn_w_in[..., D_FF:], ((0, 0), (0, 0), (0, 0), (0, padc))).astype(BF16)
    wo_all = jnp.pad(ffn_w_out, ((0, 0), (0, 0), (0, padc), (0, 0))).astype(BF16)

    nsa_tab = t5_bias[:, MLA_HEADS:MLA_HEADS + NSA_HEADS].reshape(NUM_BUCKETS, NSA_GROUPS, NSA_HPG)
    nsa_cols = jnp.transpose(nsa_tab, (0, 2, 1)).reshape(NUM_BUCKETS, NSA_HEADS)
    tab_sel = _pair(_bias_tiles(nsa_cols, causal))
    tab_win = _pair(_bias_tiles(nsa_cols, causal & (dist <= NSA_WINDOW - 1)))
    tab_mla = _toeplitz_tiles(jnp.where(jnp.asarray(causal), 0.0, NEG_INF).astype(F32)[None], 2)
    tab_mla = jnp.broadcast_to(tab_mla[None], (1, 2) + tab_mla.shape[1:])
    n_cmp_pad = s // NSA_CMP_STRIDE
    bias_c = _cmp_bias(jnp.transpose(nsa_cols[_t5_bucket(np.arange(s) - (NSA_CMP_LEN - 1))]) * LOG2E, n_cmp_pad)
    n_cmp = (s - NSA_CMP_LEN) // NSA_CMP_STRIDE + 1
    cstart = np.arange(n_cmp) * NSA_CMP_STRIDE
    sstart = np.arange(s // NSA_SLC_BLOCK) * NSA_SLC_BLOCK
    overlap = np.clip(np.minimum(cstart[:, None] + NSA_CMP_LEN, sstart[None, :] + NSA_SLC_BLOCK)
                      - np.maximum(cstart[:, None], sstart[None, :]), 0, None).astype(np.float32) / NSA_CMP_LEN
    ovl = np.zeros((NSA_GROUPS, LANES, LANES), np.float32)
    for g in range(NSA_GROUPS):
        ovl[g, 32 * g:32 * g + 32, :n_cmp] = overlap.T
    ovl = jnp.asarray(ovl, BF16)
    eg = np.zeros((3, LANES, NSA_HEADS * hd), np.float32)
    for g in range(NSA_GROUPS):
        for p in range(NSA_HPG):
            for br in range(3):
                eg[br, (g * NSA_HPG + p) * 3 + br, p * LANES + g * hd:p * LANES + (g + 1) * hd] = 1.0
    eg = jnp.asarray(eg, BF16)
    gm_mla = jnp.asarray(np.kron(np.eye(2, dtype=np.float32), _group_mean_np((MLA_NOPE, MLA_ROPE))), BF16)
    inv = ROPE_THETA ** (-jnp.arange(0, MLA_ROPE, 2, dtype=F32) / MLA_ROPE)
    ang = jnp.arange(s, dtype=F32)[:, None] * inv[None, :]
    ones = jnp.ones((s, MLA_NOPE), F32)
    tail = LANES - MLA_NOPE - MLA_ROPE
    cos_t = jnp.tile(jnp.concatenate([ones, jnp.cos(ang), jnp.cos(ang), jnp.ones((s, tail), F32)], axis=1), (1, 2))
    sin_t = jnp.tile(jnp.concatenate([0 * ones, jnp.sin(ang), jnp.sin(ang), jnp.zeros((s, tail), F32)], axis=1), (1, 2))

    ev_w = _even_w_in(ev_w_in).astype(BF16)
    ev_gain = jnp.ones((n_even, 14 * LANES), F32)
    ev_gain = ev_gain.at[:, 512:1024].set(jnp.tile(nsa_qk_g[:, 0], (1, 8)) * c64)
    ev_gain = ev_gain.at[:, 1280:1408].set(jnp.tile(nsa_qk_g[:, 1], (1, 2)))
    ev_gain = ev_gain.at[:, 1536:1664].set(jnp.tile(nsa_qk_g[:, 1], (1, 2)))
    gain_kc = jnp.tile(nsa_qk_g[:, 1], (1, 2))
    wuq = jnp.pad(mla_w_uq.reshape(n_even, MLA_Q_LORA, MLA_HEADS, MLA_NOPE + MLA_ROPE),
                  ((0, 0), (0, 0), (0, 0), (0, tail))).reshape(n_even, MLA_Q_LORA, MLA_HEADS * LANES).astype(BF16)
    ukv = mla_w_ukv.reshape(n_even, MLA_KV_LORA, MLA_HEADS, MLA_NOPE + MLA_V)
    wuk = jnp.pad(ukv[..., :MLA_NOPE], ((0, 0), (0, 0), (0, 0), (0, LANES - MLA_NOPE))
                  ).reshape(n_even, MLA_KV_LORA, MLA_HEADS * LANES).astype(BF16)
    wuv = jnp.swapaxes(ukv[..., MLA_NOPE:].reshape(n_even, MLA_KV_LORA, MLA_HEADS * MLA_V), 1, 2).astype(BF16)
    zt = jnp.zeros((n_even, tail), F32)
    gq = jnp.tile(jnp.concatenate([mla_qk_g[:, 0] * c96, zt], axis=1), (1, 2))
    gkn = jnp.tile(jnp.concatenate([mla_qk_g[:, 1, :MLA_NOPE], jnp.zeros((n_even, LANES - MLA_NOPE), F32)], axis=1), (1, 2))
    gkr = jnp.tile(jnp.concatenate([jnp.zeros((n_even, MLA_NOPE), F32), mla_qk_g[:, 1, MLA_NOPE:], zt], axis=1), (1, 2))
    pe2 = jnp.broadcast_to(nsa_cmp_pe.reshape(n_even, 2, 2, 16, 1, hd), (n_even, 2, 2, 16, NSA_GROUPS, hd)
                           ).reshape(n_even, 2, 2, 1, 16 * LANES)
    eye = jnp.eye(NSA_GROUPS, dtype=F32)
    w1 = nsa_cmp_w1.reshape(n_even, 2, 2, 16, hd, NSA_CMP_HID)
    w1x = jnp.einsum('ijaldc,gh->ijalgdhc', w1, eye).reshape(n_even, 2, 2, 16 * LANES, NSA_GROUPS * NSA_CMP_HID).astype(BF16)
    w2x = jnp.einsum('ijcd,gh->ijhdgc', nsa_cmp_w2, eye).reshape(n_even, 2, LANES, NSA_GROUPS * NSA_CMP_HID).astype(BF16)
    wa_o = ev_w_out[:, :MLA_HEADS * MLA_V].astype(BF16)
    wb_o = jnp.transpose(ev_w_out[:, MLA_HEADS * MLA_V:].reshape(n_even, NSA_GROUPS, NSA_HPG, hd, d),
                         (0, 2, 1, 3, 4)).reshape(n_even, NSA_HEADS * hd, d).astype(BF16)

    assert DIL_PAIRS == ((128, 1), (512, 4), (2048, 16))
    dil_cfg = tuple(zip(OD_DIL_STRIDES, (1, 1, 0)))
    dil_ok = (causal & (dist <= 128), causal & (dist <= 128), causal & (dist % 2 == 0))
    tab_dil = [_pair(_bias_tiles(t5_bias[:, gi * DIL_HPG:(gi + 1) * DIL_HPG], dil_ok[gi], dist_scale=dil_cfg[gi][0]))
               for gi in range(len(DIL_PAIRS))]
    tab_moba = _pair(_bias_tiles(t5_bias[:, DIL_SLOTS:DIL_SLOTS + MOBA_HEADS], causal))
    avg = np.zeros((LANES, s), np.float32)
    for h in range(MOBA_HEADS):
        for m in range(s // MOBA_BLOCK):
            avg[8 * h + m, m * MOBA_BLOCK:(m + 1) * MOBA_BLOCK] = 1.0 / MOBA_BLOCK
    avg = jnp.asarray(avg, BF16)
    od_w = od_w_in.astype(BF16)
    od_gain = jnp.concatenate([jnp.tile(dil_qk_g[:, 0], (1, 12)) * c64, jnp.tile(dil_qk_g[:, 1], (1, 12)),
                               jnp.ones((n_odd, 768), F32), jnp.tile(moba_qk_g[:, 0], (1, 4)) * c64,
                               jnp.tile(moba_qk_g[:, 1], (1, 4)), jnp.ones((n_odd, 256), F32)], axis=1)
    wd_o = od_w_out[:, :DIL_HPG * hd].astype(BF16)
    wm_o = od_w_out[:, DIL_HPG * hd:].astype(BF16)

    sh3 = lambda a: a.reshape(b, s, a.shape[-1])
    tr3 = lambda a: jnp.swapaxes(sh3(a), 1, 2)
    xf = x.reshape(n, d)
    for i in range(DEPTH):
        j = i // 2
        g_i = norm_g[i].reshape(3, 1, d)
        xf = _ffn(xf, g_i[0], mod[i, :, 0, 0], mod[i, :, 0, 1], mod[i, :, 0, 2],
                  wa_all[i, 0], wb_all[i, 0], wo_all[i, 0], s)
        if i % 2 == 0:
            mla_in, nsa_q, kc, vc, ks, vs, kw, vw = _proj(
                xf, g_i[1], mod[i, :, 1, 0], mod[i, :, 1, 1], ev_w[j], ev_gain[j][None], gm_proj,
                chunked(EV_META), EV_OUTS, s)
            qf, kf, vf = _mla_prep(mla_in, mla_q_norm_g[j][None], mla_kv_norm_g[j][None], wuq[j], wuk[j], wuv[j],
                                   gm_mla, gq[j][None], gkn[j][None], gkr[j][None], cos_t, sin_t, s)
            (o_a,) = _attn(sh3(qf), sh3(kf), vf, tab_mla, None, qc0=0, kc0=0, vb0=0, n_blk=MLA_HEADS // 2, qw=2,
                           tab_shared=True, out_dtype=BF16)
            kcmp, vcmpt = _nsa_cmp(sh3(kc), sh3(vc), pe2[j], w1x[j, :, 0], w1x[j, :, 1], w2x[j], gm64, gain_kc[j][None])
            o_c, sel = _nsa_sel(sh3(nsa_q), kcmp, vcmpt, bias_c, ovl)
            (o_s,) = _attn(sh3(nsa_q), sh3(ks), vs, tab_sel, sel, qc0=0, kc0=0, vb0=0, n_blk=NSA_HPG, qw=1,
                           kv_shared=True, sel_cfg=(TK // NSA_SLC_BLOCK, 32, 0), out_dtype=BF16)
            (o_w,) = _attn(sh3(nsa_q), sh3(kw), vw, tab_win, None, qc0=0, kc0=0, vb0=0, n_blk=NSA_HPG, qw=1,
                           kv_shared=True, backs=(2, 2, 2, 2), out_dtype=BF16)
            xf = _even_out(xf, o_a.reshape(n, -1), o_c.reshape(n, -1), o_s.reshape(n, -1), o_w.reshape(n, -1),
                           mla_in, eg, wa_o[j], wb_o[j], mod[i, :, 1, 2], s)
        else:
            pr, vd0t, vmt, qk1, qk2 = _proj(xf, g_i[1], mod[i, :, 1, 0], mod[i, :, 1, 1], od_w[j], od_gain[j][None],
                                            gm_proj, chunked(OD_META), OD_OUTS, s)
            pr3 = pr.reshape(b, s, pr.shape[-1])
            o_ds, lse_ds = [], []
            for gi, (r, bk) in enumerate(dil_cfg):
                if r == 1:
                    qk, vdt = pr3, vd0t
                else:
                    v = pr3[:, :, (2 + 2 * gi) * LANES:(4 + 2 * gi) * LANES].reshape(b, s // r, r, 2 * LANES)
                    qk, vdt = (qk1, qk2)[gi - 1], jnp.transpose(v, (0, 3, 2, 1)).reshape(b, 2 * LANES, s)
                o_g, lse_g = _attn(qk, qk, vdt, tab_dil[gi], None, qc0=0, kc0=2, vb0=0, n_blk=2,
                                   qw=1, backs=(bk,), want_lse=True, stride=r)
                o_ds.append(o_g.reshape(n, -1))
                lse_ds.append(lse_g.reshape(n, -1))
            selm = _moba_gate(pr3, avg, 4, 5)
            (o_m,) = _attn(pr3, pr3, vmt, tab_moba, selm, qc0=8, kc0=10, vb0=0, n_blk=2, qw=1,
                           sel_cfg=(1, 8, 2), out_dtype=BF16)
            xf = _odd_out(xf, o_ds, lse_ds, o_m.reshape(n, -1), wd_o[j], wm_o[j], mod[i, :, 1, 2], s)
        xf = _ffn(xf, g_i[2], mod[i, :, 2, 0], mod[i, :, 2, 1], mod[i, :, 2, 2],
                  wa_all[i, 1], wb_all[i, 1], wo_all[i, 1], s)
    return xf.reshape(b, s, d)
```

```python
import functools
import math

import numpy as np
import jax
import jax.numpy as jnp
from jax import lax
from jax.experimental import pallas as pl
from jax.experimental.pallas import tpu as pltpu

F32 = jnp.float32
BF16 = jnp.bfloat16

D_MODEL = 1024
DEPTH = 4
D_FF = 2752
HEAD_DIM = 64
NUM_BUCKETS = 32
T5_MAX_EXACT = 16
T5_MAX_DIST = 128
RMS_EPS = 1e-6
NEG_INF = -1e30
MLA_HEADS = 8
MLA_NOPE = 64
MLA_ROPE = 32
MLA_V = 64
MLA_Q_LORA = 256
MLA_KV_LORA = 128
ROPE_THETA = 10000.0
NSA_HEADS = 8
NSA_GROUPS = 2
NSA_HPG = 4
NSA_CMP_LEN = 32
NSA_CMP_STRIDE = 16
NSA_CMP_HID = 256
NSA_SLC_BLOCK = 64
NSA_SLC_TOP = 8
NSA_WINDOW = 512
NSA_FORCED = 1e6
DIL_PAIRS = ((128, 1), (512, 4), (2048, 16))
DIL_HPG = 4
DIL_SLOTS = len(DIL_PAIRS) * DIL_HPG
MOBA_HEADS = 4
MOBA_BLOCK = 256
MOBA_TOP = 3

LANES = 128
V7X_VMEM_BYTES = 64 * 1024 * 1024
VMEM_LIMIT = V7X_VMEM_BYTES * 7 // 8
TM = 512
TQ = 256
TK = 256
FF_CHUNK = 256
FF_PAD = -(-D_FF // FF_CHUNK) * FF_CHUNK
PROJ_CHUNK = 256
DEN_ROWS = 16
LOG2E = math.log2(math.e)
LN2 = math.log(2.0)


def _dot(a, b):
    return jnp.dot(a, b, preferred_element_type=F32)


def _dot_nt(a, b):
    return lax.dot_general(a, b, (((1,), (1,)), ((), ())), preferred_element_type=F32)


def _split(a):
    hi = a.astype(BF16)
    lo = (a - hi.astype(F32)).astype(BF16)
    return hi, lo


def _dot_hilo(a, b):
    hi, lo = _split(a)
    return _dot(hi, b) + _dot(lo, b)


def _sigmoid(x):
    return 1.0 / (1.0 + jnp.exp(-x))


def _modulated_norm(x, g, shift, scale):
    ms = jnp.mean(x * x, axis=-1, keepdims=True)
    y = x * lax.rsqrt(ms + RMS_EPS) * g
    return y * (1.0 + scale) + shift


def _params(*sem):
    return pltpu.CompilerParams(dimension_semantics=sem, vmem_limit_bytes=VMEM_LIMIT)


def _resident(shape):
    nd = len(shape)
    return pl.BlockSpec(shape, lambda *_: (0,) * nd, pipeline_mode=pl.Buffered(1))


def _ada_kernel(c_ref, w_ref, b_ref, o_ref):
    c = c_ref[...]
    ca = c * _sigmoid(c)
    o_ref[...] = jnp.dot(ca, w_ref[...], preferred_element_type=F32,
                         precision=lax.Precision.HIGHEST) + b_ref[...]


def _ada(c, ada_w, ada_b):
    depth, d, n = ada_w.shape
    b = c.shape[0]
    tn = 18 * LANES
    return pl.pallas_call(
        _ada_kernel,
        grid=(depth, n // tn),
        in_specs=[pl.BlockSpec((b, d), lambda l, j: (0, 0)),
                  pl.BlockSpec((None, d, tn), lambda l, j: (l, 0, j)),
                  pl.BlockSpec((None, 1, tn), lambda l, j: (l, 0, j))],
        out_specs=pl.BlockSpec((None, b, tn), lambda l, j: (l, 0, j)),
        out_shape=jax.ShapeDtypeStruct((depth, b, n), F32),
        compiler_params=_params("parallel", "parallel"),
        name="ada",
    )(c, ada_w, ada_b.reshape(depth, 1, n))


def _ffn_kernel(x_ref, g_ref, sh_ref, sc_ref, gt_ref, wa_ref, wb_ref, wo_ref, o_ref, y_ref, acc_ref):
    y_ref[...] = _modulated_norm(x_ref[...], g_ref[...], sh_ref[...], sc_ref[...]).astype(BF16)
    for c in range(FF_PAD // FF_CHUNK):
        sl = slice(c * FF_CHUNK, (c + 1) * FF_CHUNK)
        a = _dot(y_ref[...], wa_ref[:, sl])
        b = _dot(y_ref[...], wb_ref[:, sl])
        u = (a * _sigmoid(a) * b).astype(BF16)
        contrib = _dot(u, wo_ref[sl, :])
        if c == 0:
            acc_ref[...] = contrib
        else:
            acc_ref[...] += contrib
    o_ref[...] = x_ref[...] + 0.5 * gt_ref[...] * acc_ref[...]


def _mod_spec(tiles_per_batch, d):
    return pl.BlockSpec((None, 1, d), lambda i: (i // tiles_per_batch, 0, 0))


def _ffn(x, g, shift, scale, gate, wa, wb, wo, s):
    n, d = x.shape
    tpb = s // TM
    return pl.pallas_call(
        _ffn_kernel,
        grid=(n // TM,),
        in_specs=[pl.BlockSpec((TM, d), lambda i: (i, 0)),
                  _resident((1, d)),
                  _mod_spec(tpb, d), _mod_spec(tpb, d), _mod_spec(tpb, d),
                  _resident(wa.shape), _resident(wb.shape), _resident(wo.shape)],
        out_specs=pl.BlockSpec((TM, d), lambda i: (i, 0)),
        out_shape=jax.ShapeDtypeStruct((n, d), F32),
        scratch_shapes=[pltpu.VMEM((TM, d), BF16), pltpu.VMEM((TM, d), F32)],
        compiler_params=_params("parallel"),
        name="ffn",
    )(x, g, shift, scale, gate, wa, wb, wo)


def _proj_kernel(meta, kinds, seq, x_ref, g_ref, sh_ref, sc_ref, w_ref, gain_ref, gm_ref, *rest):
    n_out = len(kinds)
    outs = rest[:n_out]
    y_ref, z_ref, stage_ref = rest[n_out:n_out + 3]
    tile_in_seq = pl.program_id(0) % (seq // TM)
    y_ref[...] = _modulated_norm(x_ref[...], g_ref[...], sh_ref[...], sc_ref[...]).astype(BF16)
    for c in range(len(meta)):
        sl = slice(c * PROJ_CHUNK, (c + 1) * PROJ_CHUNK)
        z_ref[:, sl] = _dot(y_ref[...], w_ref[:, sl])
    for c, halves in enumerate(meta):
        sl = slice(c * PROJ_CHUNK, (c + 1) * PROJ_CHUNK)
        if any(normed for _, _, normed in halves):
            z = z_ref[:, sl]
            msq = _dot((z * z).astype(BF16), gm_ref[...])
            zn = z * lax.rsqrt(msq + RMS_EPS) * gain_ref[:, sl]
        for hf, (oi, off, normed) in enumerate(halves):
            hsl = slice(c * PROJ_CHUNK + hf * LANES, c * PROJ_CHUNK + (hf + 1) * LANES)
            kind = kinds[oi]
            if kind == "cols":
                src = (zn[:, hf * LANES:(hf + 1) * LANES] if normed else z_ref[:, hsl]) * gain_ref[:, hsl]
                outs[oi][off:off + LANES, :] = src.T.astype(outs[oi].dtype)
            elif kind == "rows":
                src = zn[:, hf * LANES:(hf + 1) * LANES] if normed else z_ref[:, hsl]
                outs[oi][:, off:off + LANES] = src.astype(outs[oi].dtype)
            else:
                stage_ref[...] = zn[:, hf * LANES:(hf + 1) * LANES] if normed else z_ref[:, hsl]
                per = TM // kind
                for cs in range(kind):
                    dest = pl.multiple_of(cs * (seq // kind) + tile_in_seq * per, per)
                    outs[oi][pl.ds(dest, per), off:off + LANES] = (
                        stage_ref[pl.ds(cs, per, stride=kind), :].astype(outs[oi].dtype))


def _proj(x, g, shift, scale, w, gain, gm, meta, out_defs, s):
    n, d = x.shape
    tpb = s // TM
    specs = {"rows": lambda wd: pl.BlockSpec((TM, wd), lambda i: (i, 0)),
             "cols": lambda wd: pl.BlockSpec((None, wd, TM), lambda i: (i // tpb, 0, i % tpb))}
    stream = lambda wd: pl.BlockSpec((None, s, wd), lambda i: (i // tpb, 0, 0))
    shapes = {"rows": lambda wd: (n, wd), "cols": lambda wd: (n // s, wd, s)}
    kinds = tuple(k for _, _, k in out_defs)
    return pl.pallas_call(
        functools.partial(_proj_kernel, meta, kinds, s),
        grid=(n // TM,),
        in_specs=[pl.BlockSpec((TM, d), lambda i: (i, 0)),
                  _resident((1, d)),
                  _mod_spec(tpb, d), _mod_spec(tpb, d),
                  _resident(w.shape), _resident(gain.shape), _resident(gm.shape)],
        out_specs=[specs.get(k, stream)(wd) for wd, _, k in out_defs],
        out_shape=[jax.ShapeDtypeStruct(shapes.get(k, lambda wd: (n // s, s, wd))(wd), dt) for wd, dt, k in out_defs],
        scratch_shapes=[pltpu.VMEM((TM, d), BF16), pltpu.VMEM((TM, w.shape[1]), F32), pltpu.VMEM((TM, LANES), F32)],
        compiler_params=_params("arbitrary"),
        name="proj",
    )(x, g, shift, scale, w, gain, gm)


def _mla_prep_kernel(in_ref, qg_ref, kvg_ref, wuq_ref, wuk_ref, wuv_ref, gm_ref,
                     gq_ref, gkn_ref, gkr_ref, cos_ref, sin_ref, q_out, k_out, v_out):
    def rms(z, g):
        return z * lax.rsqrt(jnp.mean(z * z, axis=-1, keepdims=True) + RMS_EPS) * g

    cqn = rms(in_ref[:, 0:MLA_Q_LORA], qg_ref[...]).astype(BF16)
    ckvn = rms(in_ref[:, MLA_Q_LORA:MLA_Q_LORA + MLA_KV_LORA], kvg_ref[...]).astype(BF16)
    c3 = in_ref[:, 3 * LANES:4 * LANES]
    cos = cos_ref[...]
    sin = sin_ref[...]
    gm = gm_ref[...]
    half = MLA_ROPE // 2
    slot_lane = lax.broadcasted_iota(jnp.int32, (1, 2 * LANES), 1) & (LANES - 1)
    first_half = slot_lane < MLA_NOPE + half

    def norm_rope(z, gain):
        msq = _dot((z * z).astype(BF16), gm)
        z = z * lax.rsqrt(msq + RMS_EPS) * gain
        rot = jnp.where(first_half, -pltpu.roll(z, 2 * LANES - half, 1), pltpu.roll(z, half, 1))
        return z * cos + rot * sin

    kr = norm_rope(jnp.concatenate([c3, c3], axis=1), gkr_ref[...])
    for h in range(MLA_HEADS // 2):
        sl = slice(h * 2 * LANES, (h + 1) * 2 * LANES)
        q_out[:, sl] = norm_rope(_dot(cqn, wuq_ref[:, sl]), gq_ref[...]).astype(BF16)
        k_out[:, sl] = (norm_rope(_dot(ckvn, wuk_ref[:, sl]), gkn_ref[...]) + kr).astype(BF16)
    v_out[...] = _dot_nt(wuv_ref[...], ckvn).astype(BF16)


def _mla_prep(mla_in, qg, kvg, wuq, wuk, wuv, gm, gq, gkn, gkr, cos_t, sin_t, s):
    n = mla_in.shape[0]
    tpb = s // TM
    hw = MLA_HEADS * LANES
    vw = MLA_HEADS * MLA_V
    tab = pl.BlockSpec((TM, 2 * LANES), lambda i: (i % tpb, 0))
    consts = [qg, kvg, wuq, wuk, wuv, gm, gq, gkn, gkr]
    return pl.pallas_call(
        _mla_prep_kernel,
        grid=(n // TM,),
        in_specs=[pl.BlockSpec((TM, 4 * LANES), lambda i: (i, 0))] + [_resident(a.shape) for a in consts] + [tab, tab],
        out_specs=[pl.BlockSpec((TM, hw), lambda i: (i, 0)), pl.BlockSpec((TM, hw), lambda i: (i, 0)),
                   pl.BlockSpec((None, vw, TM), lambda i: (i // tpb, 0, i % tpb))],
        out_shape=[jax.ShapeDtypeStruct((n, hw), BF16), jax.ShapeDtypeStruct((n, hw), BF16),
                   jax.ShapeDtypeStruct((n // s, vw, s), BF16)],
        compiler_params=_params("parallel"),
        name="mla_prep",
    )(mla_in, *consts, cos_t, sin_t)


def _attn_kernel(cfg, q_ref, k_ref, vt_ref, tab_ref, *rest):
    n_tab, qw, backs, sel_cfg, want_lse, n_qt, stride = cfg
    n_qs = n_qt // stride
    if sel_cfg is not None:
        sel_ref, rest = rest[0], rest[1:]
        sel_bpt, sel_stride, sel_pair_mul = sel_cfg
    o_ref = rest[0]
    lse_ref = rest[1] if want_lse else None
    st_scr, p_scr, acc_fin, m_fin = rest[-6:-4], rest[-4:-2], rest[-2], rest[-1]
    blk = pl.program_id(1)
    lane = lax.broadcasted_iota(jnp.int32, (1, LANES), 1)

    counts = tuple(sum(min(t % n_qs, bk) + 1 for t in range(n_qt)) for bk in backs)
    back = jnp.int32(backs[-1])
    n_tiles = jnp.int32(counts[-1])
    for bi in range(len(backs) - 2, -1, -1):
        back = jnp.where(blk == bi, jnp.int32(backs[bi]), back)
        n_tiles = jnp.where(blk == bi, jnp.int32(counts[bi]), n_tiles)

    ones_rows = jnp.ones((DEN_ROWS, TK), BF16)

    def first_key_tile(qi):
        return qi - jnp.minimum(qi & (n_qs - 1), back)

    def rows(t, size):
        return pl.ds(t * size if isinstance(t, int) else pl.multiple_of(t * size, size), size)

    def out_rows(t, size):
        if stride == 1:
            return rows(t, size)
        return pl.ds((t >> int(math.log2(n_qs))) + stride * size * (t & (n_qs - 1)), size, stride=stride)

    def logits_to(slot, qi, j, filler):
        d = jnp.where(filler, n_tab, jnp.minimum(qi - j, n_tab - 1))
        if qw == 1:
            q = q_ref[rows(qi, TQ), :]
            zero = jnp.zeros_like(q)
            qs = [jnp.where(lane < HEAD_DIM, q, zero), jnp.where(lane >= HEAD_DIM, q, zero)]
        else:
            qs = [q_ref[rows(qi, TQ), s * LANES:(s + 1) * LANES] for s in range(2)]
        for s in range(2):
            kj = k_ref[rows(j, TK), :] if qw == 1 else k_ref[rows(j, TK), s * LANES:(s + 1) * LANES]
            st = _dot_nt(kj, qs[s])
            if sel_cfg is not None:
                off = sel_stride * (s + sel_pair_mul * blk) + sel_bpt * j
                kb = TK // sel_bpt
                qcol = pl.ds(pl.multiple_of(qi * TQ, TQ), TQ)
                st = jnp.concatenate(
                    [st[bk * kb:(bk + 1) * kb] + sel_ref[pl.ds(off + bk, 1), qcol]
                     for bk in range(sel_bpt)], axis=0)
            st_scr[slot][s] = st + tab_ref[s, d]

    def accumulate(slot, j, alphas, accs):
        krow = pl.multiple_of(j * TK, TK)
        new = []
        for s in range(2):
            vt = jnp.concatenate([vt_ref[s * HEAD_DIM:(s + 1) * HEAD_DIM, pl.ds(krow, TK)], ones_rows], axis=0)
            new.append(alphas[s] * accs[s] + _dot(vt, p_scr[slot][s]))
        return tuple(new)

    def finalize(qi, ms, accs):
        for s in range(2):
            acc_fin[qi, s] = accs[s]
            if want_lse:
                m_fin[qi, s] = jnp.broadcast_to(ms[s], m_fin.shape[2:])

    def write_out(t):
        accs = [acc_fin[t, s] for s in range(2)]
        dens = [acc[HEAD_DIM:HEAD_DIM + 1] for acc in accs]
        out_t = jnp.concatenate([acc[:HEAD_DIM] / l for acc, l in zip(accs, dens)], axis=0)
        o_ref[out_rows(t, TQ), :] = out_t.T.astype(o_ref.dtype)
        if want_lse:
            lse_t = jnp.concatenate([jnp.broadcast_to((m_fin[t, s][0:1] + jnp.log2(dens[s])) * LN2, (HEAD_DIM, TQ))
                                     for s in range(2)], axis=0)
            lse_ref[out_rows(t, TQ), :] = lse_t.T

    def advance(q, j):
        last = j == q
        at_end = jnp.logical_and(last, q == n_qt - 1)
        starts = jnp.logical_and(last, jnp.logical_not(at_end))
        qn = jnp.where(starts, q + 1, q)
        jn = jnp.where(at_end, j, jnp.where(last, first_key_tile(q + 1), j + 1))
        return qn, jn, starts, at_end

    def softmax(slot, tile, stats):
        is_first = tile[2]
        keep = jnp.where(is_first, 0.0, 1.0)
        new_stats, alphas = [], []
        for s in range(2):
            m = jnp.where(is_first, NEG_INF, stats[s])
            st = st_scr[slot][s]
            m_new = jnp.maximum(m, jnp.max(st, axis=0, keepdims=True))
            alphas.append(jnp.exp2(m - m_new) * keep)
            new_stats.append(m_new)
            p_scr[slot][s] = jnp.exp2(st - m_new).astype(BF16)
        return tuple(new_stats), tuple(alphas)

    def body(u, carry):
        tile_a, (q2, j2), (q1, j1, first1), stats2, stats1, alphas2, alphas1, accs = carry
        qa, ja, first_a, _ = tile_a
        tile_b = advance(qa, ja)
        qb, jb, first_b, _ = tile_b
        tile_c = advance(qb, jb)
        logits_to(1, qb, jb, tile_b[3])
        accs_x = accumulate(0, j2, alphas2, accs)
        accs_y = accumulate(1, j1, alphas1, accs_x)
        stats_a, alphas_a = softmax(0, tile_a, stats1)
        logits_to(0, tile_c[0], tile_c[1], tile_c[3])
        stats_b, alphas_b = softmax(1, tile_b, stats_a)

        finalize(q2, stats2, accs_x)
        finalize(q1, stats1, accs_y)
        return tile_c, (qa, ja), (qb, jb, first_b), stats_a, stats_b, alphas_a, alphas_b, accs_y

    zero_i = jnp.int32(0)
    logits_to(0, zero_i, zero_i, zero_i != 0)
    for p_slot in p_scr:
        p_slot[...] = jnp.zeros(p_slot.shape, BF16)
    stats0 = tuple(jnp.full((1, TQ), NEG_INF, F32) for _ in range(2))
    ones = tuple(jnp.ones((1, TQ), F32) for _ in range(2))
    init = ((zero_i, zero_i, zero_i == 0, zero_i != 0), (zero_i, zero_i), (zero_i, zero_i, zero_i != 0),
            stats0, stats0, ones, ones, tuple(jnp.zeros((HEAD_DIM + DEN_ROWS, TQ), F32) for _ in range(2)))
    _, (q2, j2), (q1, j1, first1), stats2, stats1, alphas2, alphas1, accs = lax.fori_loop(
        0, (n_tiles + 1) // 2, body, init)
    accs_x = accumulate(0, j2, alphas2, accs)
    finalize(q2, stats2, accs_x)
    finalize(q1, stats1, accumulate(1, j1, alphas1, accs_x))
    for t in range(n_qt):
        write_out(t)


def _attn(q, k, vt, tab, sel, *, qc0, kc0, vb0, n_blk, qw, backs=None, kv_shared=False,
          tab_shared=False, sel_cfg=None, out_dtype=F32, want_lse=False, stride=1):
    b, s = q.shape[:2]
    n_tab = tab.shape[2] - 1
    n_qt = s // TQ
    if backs is None:
        backs = (n_qt,)
    cfg = (n_tab, qw, backs, sel_cfg, want_lse, n_qt, stride)
    kidx = (lambda bb, h: (bb, 0, kc0)) if kv_shared else (lambda bb, h: (bb, 0, kc0 + h))
    vidx = (lambda bb, h: (bb, vb0, 0)) if kv_shared else (lambda bb, h: (bb, vb0 + h, 0))
    tidx = (lambda bb, h: (0, 0, 0, 0, 0)) if tab_shared else (lambda bb, h: (h, 0, 0, 0, 0))
    in_specs = [pl.BlockSpec((None, s, qw * LANES), lambda bb, h: (bb, 0, qc0 + h)),
                pl.BlockSpec((None, s, qw * LANES), kidx),
                pl.BlockSpec((None, 2 * HEAD_DIM, s), vidx),
                pl.BlockSpec((None, 2, n_tab + 1, TK, TQ), tidx)]
    args = [q, k, vt, tab]
    if sel_cfg is not None:
        in_specs.append(pl.BlockSpec((None, LANES, s), lambda bb, h: (bb, 0, 0)))
        args.append(sel)
    ospec = pl.BlockSpec((None, s, LANES), lambda bb, h: (bb, 0, h))
    out_specs = [ospec]
    out_shape = [jax.ShapeDtypeStruct((b, s, n_blk * LANES), out_dtype)]
    if want_lse:
        out_specs.append(ospec)
        out_shape.append(jax.ShapeDtypeStruct((b, s, n_blk * LANES), F32))
    return pl.pallas_call(
        functools.partial(_attn_kernel, cfg),
        grid=(b, n_blk),
        in_specs=in_specs,
        out_specs=out_specs,
        out_shape=out_shape,
        scratch_shapes=([pltpu.VMEM((2, TK, TQ), F32)] * 2 + [pltpu.VMEM((2, TK, TQ), BF16)] * 2
                        + [pltpu.VMEM((n_qt, 2, HEAD_DIM + DEN_ROWS, TQ), F32), pltpu.VMEM((n_qt, 2, 8, TQ), F32)]),
        compiler_params=_params("parallel", "parallel"),
        name="attn",
    )(*args)


def _nsa_cmp_kernel(kc_ref, vc_ref, pe_ref, wlo_ref, whi_ref, w2_ref, gm_ref, gain_ref, kcmp_ref, vcmpt_ref):
    nch = kcmp_ref.shape[0]

    def hidden(c_ref, j):
        lo = hi = None
        for l in range(NSA_CMP_STRIDE):
            rows = c_ref[pl.ds(l, nch, stride=NSA_CMP_STRIDE), :]
            sl = slice(l * LANES, (l + 1) * LANES)
            t_lo = _dot((rows + pe_ref[j, 0][:, sl]).astype(BF16), wlo_ref[j][sl, :])
            t_hi = _dot((rows + pe_ref[j, 1][:, sl]).astype(BF16), whi_ref[j][sl, :])
            lo = t_lo if lo is None else lo + t_lo
            hi = t_hi if hi is None else hi + t_hi
        h = lo + pltpu.roll(hi, nch - 1, 0)
        return (h * _sigmoid(h)).astype(BF16)

    kz = _dot_nt(hidden(kc_ref, 0), w2_ref[0])
    msq = _dot_hilo(kz * kz, gm_ref[...])
    kcmp_ref[...] = (kz * lax.rsqrt(msq + RMS_EPS) * gain_ref[...]).astype(BF16)
    vcmpt_ref[...] = _dot_nt(w2_ref[1], hidden(vc_ref, 1)).astype(BF16)


def _nsa_cmp(kc3, vc3, pe, wlo, whi, w2, gm, gain):
    b, s, width = kc3.shape
    nch = s // NSA_CMP_STRIDE
    consts = [pe, wlo, whi, w2, gm, gain]
    blk = pl.BlockSpec((None, s, width), lambda i: (i, 0, 0))
    oblk = pl.BlockSpec((None, nch, LANES), lambda i: (i, 0, 0))
    return pl.pallas_call(
        _nsa_cmp_kernel,
        grid=(b,),
        in_specs=[blk, blk] + [_resident(a.shape) for a in consts],
        out_specs=[oblk, oblk],
        out_shape=[jax.ShapeDtypeStruct((b, nch, LANES), BF16)] * 2,
        compiler_params=_params("parallel"),
        name="nsa_cmp",
    )(kc3, vc3, *consts)


def _rank_keep(score, ids, top):
    cnt = jnp.zeros(score.shape, jnp.int32)
    for mp in range(score.shape[0]):
        other = score[mp:mp + 1, :]
        tie = jnp.where(mp < ids, 1, 0)
        cnt = cnt + jnp.where(other > score, 1, jnp.where(other == score, tie, 0))
    return cnt < top


def _nsa_sel_kernel(q_ref, kcmp_ref, vcmpt_ref, bias_ref, ovl_ref, oc_ref, sel_ref):
    qi = pl.program_id(1)
    lane = lax.broadcasted_iota(jnp.int32, (1, LANES), 1)
    row = lax.broadcasted_iota(jnp.int32, (LANES, 1), 0)
    t = qi * TQ + lax.broadcasted_iota(jnp.int32, (1, TQ), 1)
    mask_c = (NSA_CMP_STRIDE * row + NSA_CMP_LEN - 1) <= t
    kcmp = kcmp_ref[...]
    heads = [(p, g) for p in range(NSA_HPG) for g in range(NSA_GROUPS)]
    raw = []
    for p, g in heads:
        qp = q_ref[:, p * LANES:(p + 1) * LANES]
        raw.append(_dot_nt(kcmp, jnp.where((lane >> 6) == g, qp, jnp.zeros_like(qp))))
    pcs = []
    for (p, g), r in zip(heads, raw):
        lg = jnp.where(mask_c, r + bias_ref[p * NSA_GROUPS + g], NEG_INF)
        m = jnp.max(lg, axis=0, keepdims=True)
        e = jnp.where(mask_c, jnp.exp2(lg - m), 0.0)
        den = jnp.maximum(jnp.sum(e, axis=0, keepdims=True), 1e-30)
        pcs.append(e / den)
    imp = jnp.zeros((LANES, TQ), F32)
    ocs = []
    for (p, g), pc in zip(heads, pcs):
        hi, lo = _split(pc)
        ocs.append(_dot(vcmpt_ref[g * HEAD_DIM:(g + 1) * HEAD_DIM, :], hi))
        imp = imp + _dot(ovl_ref[g], hi) + _dot(ovl_ref[g], lo)
    for p in range(NSA_HPG):
        oc_ref[:, p * LANES:(p + 1) * LANES] = jnp.concatenate([ocs[2 * p], ocs[2 * p + 1]], axis=0).T.astype(BF16)

    n_slc = 32
    ids = row[0:n_slc]
    cur = t >> 6
    forced = (ids == 0) | (ids == cur) | (ids == cur - 1)
    masks = []
    for g in range(NSA_GROUPS):
        score = jnp.where(forced, NSA_FORCED, jnp.where(ids <= cur, imp[g * n_slc:(g + 1) * n_slc], NEG_INF))
        keep = _rank_keep(score, ids, NSA_SLC_TOP) & (score > 0.5 * NEG_INF)
        masks.append(jnp.where(keep, 0.0, NEG_INF))
    masks.append(jnp.full((LANES - NSA_GROUPS * n_slc, TQ), NEG_INF, F32))
    sel_ref[...] = jnp.concatenate(masks, axis=0)


def _nsa_sel(q, kcmp, vcmpt, bias_c, ovl):
    b, s, w = q.shape
    return pl.pallas_call(
        _nsa_sel_kernel,
        grid=(b, s // TQ),
        in_specs=[pl.BlockSpec((None, TQ, w), lambda bb, i: (bb, i, 0)),
                  pl.BlockSpec((None, LANES, LANES), lambda bb, i: (bb, 0, 0)),
                  pl.BlockSpec((None, LANES, LANES), lambda bb, i: (bb, 0, 0)),
                  pl.BlockSpec((NSA_HEADS, LANES, TQ), lambda bb, i: (0, 0, i)),
                  _resident(ovl.shape)],
        out_specs=[pl.BlockSpec((None, TQ, w), lambda bb, i: (bb, i, 0)),
                   pl.BlockSpec((None, LANES, TQ), lambda bb, i: (bb, 0, i))],
        out_shape=[jax.ShapeDtypeStruct((b, s, w), BF16), jax.ShapeDtypeStruct((b, LANES, s), F32)],
        compiler_params=_params("parallel", "parallel"),
        name="nsa_sel",
    )(q, kcmp, vcmpt, bias_c, ovl)


def _moba_gate_kernel(q_ref, k_ref, avg_ref, sel_ref, kh_ref, kl_ref):
    qi = pl.program_id(1)
    nb = 8

    @pl.when(qi == 0)
    def _():
        kmean = _dot(avg_ref[...], k_ref[...])
        r2 = lax.broadcasted_iota(jnp.int32, kmean.shape, 0)
        c2 = lax.broadcasted_iota(jnp.int32, kmean.shape, 1)
        kh_ref[...], kl_ref[...] = _split(jnp.where((r2 >> 3) == (c2 >> 6), kmean, 0.0))

    q = q_ref[...]
    gate = _dot_nt(kh_ref[...], q) + _dot_nt(kl_ref[...], q)
    ids = lax.broadcasted_iota(jnp.int32, (nb, 1), 0)
    past = ids < qi
    masks = []
    for h in range(MOBA_HEADS):
        score = jnp.where(past, gate[h * nb:(h + 1) * nb], NEG_INF)
        keep = (_rank_keep(score, ids, MOBA_TOP) & past) | (ids == qi)
        masks.append(jnp.where(keep, 0.0, NEG_INF))
    masks.append(jnp.full((LANES - MOBA_HEADS * nb, TQ), NEG_INF, F32))
    sel_ref[...] = jnp.concatenate(masks, axis=0)


def _moba_gate(p_arr, avg, qc, kc):
    b, s, _ = p_arr.shape
    w = MOBA_HEADS * HEAD_DIM
    return pl.pallas_call(
        _moba_gate_kernel,
        grid=(b, s // TQ),
        in_specs=[pl.BlockSpec((None, TQ, w), lambda bb, i: (bb, i, qc)),
                  pl.BlockSpec((None, s, w), lambda bb, i: (bb, 0, kc)),
                  _resident(avg.shape)],
        out_specs=pl.BlockSpec((None, LANES, TQ), lambda bb, i: (bb, 0, i)),
        out_shape=jax.ShapeDtypeStruct((b, LANES, s), F32),
        scratch_shapes=[pltpu.VMEM((LANES, w), BF16)] * 2,
        compiler_params=_params("parallel", "arbitrary"),
        name="moba_gate",
    )(p_arr, p_arr, avg)


def _even_out_kernel(x_ref, oa_ref, oc_ref, os_ref, ow_ref, gl_ref, eg_ref, wa_ref, wb_ref, gt_ref, o_ref):
    sg = _sigmoid(gl_ref[...])
    hi, lo = _split(sg)
    nsa = None
    for br, src in enumerate((oc_ref, os_ref, ow_ref)):
        gexp = _dot(hi, eg_ref[br]) + _dot(lo, eg_ref[br])
        term = gexp * src[...]
        nsa = term if nsa is None else nsa + term
    m = _dot(oa_ref[...], wa_ref[...]) + _dot(nsa.astype(BF16), wb_ref[...])
    o_ref[...] = x_ref[...] + gt_ref[...] * m


def _even_out(x, o_a, o_c, o_s, o_w, mla_in, eg, wa, wb, gate, s):
    n, d = x.shape
    tpb = s // TM
    row = lambda wd: pl.BlockSpec((TM, wd), lambda i: (i, 0))
    return pl.pallas_call(
        _even_out_kernel,
        grid=(n // TM,),
        in_specs=[row(d), row(o_a.shape[1]), row(o_c.shape[1]), row(o_s.shape[1]), row(o_w.shape[1]),
                  pl.BlockSpec((TM, LANES), lambda i: (i, 3)),
                  _resident(eg.shape), _resident(wa.shape), _resident(wb.shape), _mod_spec(tpb, d)],
        out_specs=row(d),
        out_shape=jax.ShapeDtypeStruct((n, d), F32),
        compiler_params=_params("parallel"),
        name="even_out",
    )(x, o_a, o_c, o_s, o_w, mla_in, eg, wa, wb, gate)


def _odd_out_kernel(x_ref, od0_ref, od1_ref, od2_ref, ls0_ref, ls1_ref, ls2_ref, om_ref, wd_ref, wm_ref, gt_ref, o_ref):
    ods = (od0_ref, od1_ref, od2_ref)
    ls = [r[...] for r in (ls0_ref, ls1_ref, ls2_ref)]
    mx = jnp.maximum(jnp.maximum(ls[0], ls[1]), ls[2])
    es = [jnp.exp(l - mx) for l in ls]
    den = es[0] + es[1] + es[2]
    merged = None
    for g in range(len(DIL_PAIRS)):
        term = (es[g] / den) * ods[g][...]
        merged = term if merged is None else merged + term
    m = _dot(merged.astype(BF16), wd_ref[...]) + _dot(om_ref[...], wm_ref[...])
    o_ref[...] = x_ref[...] + gt_ref[...] * m


def _odd_out(x, o_ds, lse_ds, o_m, wd, wm, gate, s):
    n, d = x.shape
    tpb = s // TM
    row = lambda wd_: pl.BlockSpec((TM, wd_), lambda i: (i, 0))
    return pl.pallas_call(
        _odd_out_kernel,
        grid=(n // TM,),
        in_specs=[row(d)] + [row(a.shape[1]) for a in (*o_ds, *lse_ds)] + [row(o_m.shape[1]),
                  _resident(wd.shape), _resident(wm.shape), _mod_spec(tpb, d)],
        out_specs=row(d),
        out_shape=jax.ShapeDtypeStruct((n, d), F32),
        compiler_params=_params("parallel"),
        name="odd_out",
    )(x, *o_ds, *lse_ds, o_m, wd, wm, gate)


def _t5_bucket(dist):
    n = jnp.maximum(jnp.asarray(dist, jnp.int32), 0)
    nf = jnp.maximum(n, 1).astype(F32)
    large = T5_MAX_EXACT + (jnp.log(nf / T5_MAX_EXACT) / math.log(T5_MAX_DIST / T5_MAX_EXACT)
                            * (NUM_BUCKETS - T5_MAX_EXACT)).astype(jnp.int32)
    return jnp.where(n < T5_MAX_EXACT, n, jnp.minimum(large, NUM_BUCKETS - 1))


TOEP_PERIOD = 4 * TQ


def _toeplitz_dist():
    j = np.arange(TOEP_PERIOD)
    return np.where(j < 3 * TQ, j, j - TOEP_PERIOD)


def _toeplitz_kernel(n_tab, u_ref, o_ref):
    x = jnp.broadcast_to(u_ref[...], (TK, TOEP_PERIOD))
    y = pltpu.roll(x, 0, 1, stride=1, stride_axis=0)
    for dlt in range(n_tab):
        o_ref[dlt] = y[:, dlt * TQ:(dlt + 1) * TQ]
    o_ref[n_tab] = jnp.full((TK, TQ), NEG_INF * LOG2E, F32)


def _toeplitz_tiles(u, n_tab):
    h = u.shape[0]
    return pl.pallas_call(
        functools.partial(_toeplitz_kernel, n_tab),
        grid=(h,),
        in_specs=[pl.BlockSpec((None, 1, TOEP_PERIOD), lambda i: (i, 0, 0))],
        out_specs=pl.BlockSpec((None, n_tab + 1, TK, TQ), lambda i: (i, 0, 0, 0)),
        out_shape=jax.ShapeDtypeStruct((h, n_tab + 1, TK, TQ), F32),
        compiler_params=_params("parallel"),
        name="toeplitz",
    )(u.reshape(h, 1, TOEP_PERIOD))


def _cmp_bias_kernel(u_ref, o_ref):
    x = jnp.broadcast_to(u_ref[...], o_ref.shape)
    o_ref[...] = pltpu.roll(x, 0, 1, stride=NSA_CMP_STRIDE, stride_axis=0)


def _cmp_bias(u, n_rows):
    h, s = u.shape
    return pl.pallas_call(
        _cmp_bias_kernel,
        grid=(h,),
        in_specs=[pl.BlockSpec((None, 1, s), lambda i: (i, 0, 0))],
        out_specs=pl.BlockSpec((None, n_rows, s), lambda i: (i, 0, 0)),
        out_shape=jax.ShapeDtypeStruct((h, n_rows, s), F32),
        compiler_params=_params("parallel"),
        name="cmp_bias",
    )(u.reshape(h, 1, s))


def _bias_tiles(t5_cols, ok, n_tab=3, dist_scale=1):
    dist = _toeplitz_dist()
    bias = jnp.transpose(t5_cols[_t5_bucket(dist * dist_scale)])
    u = jnp.where(jnp.asarray(ok)[None], bias, NEG_INF) * LOG2E
    return _toeplitz_tiles(u, n_tab)


def _pair(tiles):
    h = tiles.shape[0]
    return tiles.reshape(h // 2, 2, *tiles.shape[1:])


def _group_mean_np(sizes, width=LANES):
    gm = np.zeros((width, width), np.float32)
    o = 0
    for sz in sizes:
        gm[o:o + sz, o:o + sz] = 1.0 / sz
        o += sz
    return gm


def _group_mean_matrix(sizes, width=LANES):
    return jnp.asarray(_group_mean_np(sizes, width), BF16)


EV_META = ((0, 0, False), (0, 128, False), (0, 256, False), (0, 384, False),
           (1, 0, True), (1, 128, True), (1, 256, True), (1, 384, True),
           (2, 0, False), (3, 0, False), (4, 0, True), (5, 0, False), (6, 0, True), (7, 0, False))
EV_OUTS = [(512, F32, "rows"), (512, BF16, "rows"), (128, F32, "rows"), (128, F32, "rows"), (128, BF16, "rows"),
           (128, BF16, "cols"), (128, BF16, "rows"), (128, BF16, "cols")]
OD_DIL_STRIDES = (1, 4, 8)
OD_META = (((0, 0, True), (0, LANES, True)) + tuple((3 + g, h * LANES, True) for g in range(2) for h in range(2))
           + ((0, 2 * LANES, True), (0, 3 * LANES, True))
           + tuple((3 + g, (2 + h) * LANES, True) for g in range(2) for h in range(2))
           + ((1, 0, False), (1, LANES, False)) + tuple((0, (4 + c) * LANES, False) for c in range(4))
           + tuple((0, (8 + c) * LANES, True) for c in range(4)) + ((2, 0, False), (2, LANES, False)))
OD_OUTS = [(12 * LANES, BF16, "rows"), (2 * LANES, BF16, "cols"), (2 * LANES, BF16, "cols"),
           (4 * LANES, BF16, OD_DIL_STRIDES[1]), (4 * LANES, BF16, OD_DIL_STRIDES[2])]


def _even_w_in(w):
    jn, d, _ = w.shape
    z = lambda n_: jnp.zeros((jn, d, n_), w.dtype)
    nq = w[:, :, 416:928].reshape(jn, d, NSA_GROUPS, NSA_HPG, HEAD_DIM)
    nq = jnp.transpose(nq, (0, 1, 3, 2, 4)).reshape(jn, d, NSA_HEADS * HEAD_DIM)
    chunk3 = jnp.concatenate([w[:, :, 1696:1720], z(HEAD_DIM - 24), w[:, :, 384:416], z(LANES - 96)], axis=-1)
    return jnp.concatenate([w[:, :, 0:384], chunk3, nq, w[:, :, 928:1696]], axis=-1)


def kernel(x, c, t5_bias, ada_w, ada_b, norm_g, ffn_w_in, ffn_w_out, ev_w_in, ev_w_out, mla_q_norm_g,
           mla_kv_norm_g, mla_w_uq, mla_w_ukv, mla_qk_g, nsa_cmp_pe, nsa_cmp_w1, nsa_cmp_w2, nsa_qk_g,
           od_w_in, od_w_out, dil_qk_g, moba_qk_g):
    b, s, d = x.shape
    assert (s, d) == (2048, D_MODEL) and s % TM == 0 and TQ == MOBA_BLOCK and TQ == TK
    n = b * s
    hd = HEAD_DIM
    n_even = ev_w_in.shape[0]
    n_odd = od_w_in.shape[0]
    c64 = hd ** -0.5 * LOG2E
    c96 = (MLA_NOPE + MLA_ROPE) ** -0.5 * LOG2E

    mod = _ada(c, ada_w, ada_b).reshape(DEPTH, b, 3, 3, 1, d)

    dist = _toeplitz_dist()
    causal = dist >= 0
    gm64 = _group_mean_matrix((hd, hd))
    gm_proj = _group_mean_matrix((hd,) * (PROJ_CHUNK // hd), PROJ_CHUNK)
    per_chunk = PROJ_CHUNK // LANES
    chunked = lambda m: tuple(tuple(m[i:i + per_chunk]) for i in range(0, len(m), per_chunk))

    padc = FF_PAD - D_FF
    wa_all = jnp.pad(ffn_w_in[..., :D_FF], ((0, 0), (0, 0), (0, 0), (0, padc))).astype(BF16)
    wb_all = jnp.pad(ffn_w_in[..., D_FF:], ((0, 0), (0, 0), (0, 0), (0, padc))).astype(BF16)
    wo_all = jnp.pad(ffn_w_out, ((0, 0), (0, 0), (0, padc), (0, 0))).astype(BF16)

    nsa_tab = t5_bias[:, MLA_HEADS:MLA_HEADS + NSA_HEADS].reshape(NUM_BUCKETS, NSA_GROUPS, NSA_HPG)
    nsa_cols = jnp.transpose(nsa_tab, (0, 2, 1)).reshape(NUM_BUCKETS, NSA_HEADS)
    tab_sel = _pair(_bias_tiles(nsa_cols, causal))
    tab_win = _pair(_bias_tiles(nsa_cols, causal & (dist <= NSA_WINDOW - 1)))
    tab_mla = _toeplitz_tiles(jnp.where(jnp.asarray(causal), 0.0, NEG_INF).astype(F32)[None], 2)
    tab_mla = jnp.broadcast_to(tab_mla[None], (1, 2) + tab_mla.shape[1:])
    n_cmp_pad = s // NSA_CMP_STRIDE
    bias_c = _cmp_bias(jnp.transpose(nsa_cols[_t5_bucket(np.arange(s) - (NSA_CMP_LEN - 1))]) * LOG2E, n_cmp_pad)
    n_cmp = (s - NSA_CMP_LEN) // NSA_CMP_STRIDE + 1
    cstart = np.arange(n_cmp) * NSA_CMP_STRIDE
    sstart = np.arange(s // NSA_SLC_BLOCK) * NSA_SLC_BLOCK
    overlap = np.clip(np.minimum(cstart[:, None] + NSA_CMP_LEN, sstart[None, :] + NSA_SLC_BLOCK)
                      - np.maximum(cstart[:, None], sstart[None, :]), 0, None).astype(np.float32) / NSA_CMP_LEN
    ovl = np.zeros((NSA_GROUPS, LANES, LANES), np.float32)
    for g in range(NSA_GROUPS):
        ovl[g, 32 * g:32 * g + 32, :n_cmp] = overlap.T
    ovl = jnp.asarray(ovl, BF16)
    eg = np.zeros((3, LANES, NSA_HEADS * hd), np.float32)
    for g in range(NSA_GROUPS):
        for p in range(NSA_HPG):
            for br in range(3):
                eg[br, (g * NSA_HPG + p) * 3 + br, p * LANES + g * hd:p * LANES + (g + 1) * hd] = 1.0
    eg = jnp.asarray(eg, BF16)
    gm_mla = jnp.asarray(np.kron(np.eye(2, dtype=np.float32), _group_mean_np((MLA_NOPE, MLA_ROPE))), BF16)
    inv = ROPE_THETA ** (-jnp.arange(0, MLA_ROPE, 2, dtype=F32) / MLA_ROPE)
    ang = jnp.arange(s, dtype=F32)[:, None] * inv[None, :]
    ones = jnp.ones((s, MLA_NOPE), F32)
    tail = LANES - MLA_NOPE - MLA_ROPE
    cos_t = jnp.tile(jnp.concatenate([ones, jnp.cos(ang), jnp.cos(ang), jnp.ones((s, tail), F32)], axis=1), (1, 2))
    sin_t = jnp.tile(jnp.concatenate([0 * ones, jnp.sin(ang), jnp.sin(ang), jnp.zeros((s, tail), F32)], axis=1), (1, 2))

    ev_w = _even_w_in(ev_w_in).astype(BF16)
    ev_gain = jnp.ones((n_even, 14 * LANES), F32)
    ev_gain = ev_gain.at[:, 512:1024].set(jnp.tile(nsa_qk_g[:, 0], (1, 8)) * c64)
    ev_gain = ev_gain.at[:, 1280:1408].set(jnp.tile(nsa_qk_g[:, 1], (1, 2)))
    ev_gain = ev_gain.at[:, 1536:1664].set(jnp.tile(nsa_qk_g[:, 1], (1, 2)))
    gain_kc = jnp.tile(nsa_qk_g[:, 1], (1, 2))
    wuq = jnp.pad(mla_w_uq.reshape(n_even, MLA_Q_LORA, MLA_HEADS, MLA_NOPE + MLA_ROPE),
                  ((0, 0), (0, 0), (0, 0), (0, tail))).reshape(n_even, MLA_Q_LORA, MLA_HEADS * LANES).astype(BF16)
    ukv = mla_w_ukv.reshape(n_even, MLA_KV_LORA, MLA_HEADS, MLA_NOPE + MLA_V)
    wuk = jnp.pad(ukv[..., :MLA_NOPE], ((0, 0), (0, 0), (0, 0), (0, LANES - MLA_NOPE))
                  ).reshape(n_even, MLA_KV_LORA, MLA_HEADS * LANES).astype(BF16)
    wuv = jnp.swapaxes(ukv[..., MLA_NOPE:].reshape(n_even, MLA_KV_LORA, MLA_HEADS * MLA_V), 1, 2).astype(BF16)
    zt = jnp.zeros((n_even, tail), F32)
    gq = jnp.tile(jnp.concatenate([mla_qk_g[:, 0] * c96, zt], axis=1), (1, 2))
    gkn = jnp.tile(jnp.concatenate([mla_qk_g[:, 1, :MLA_NOPE], jnp.zeros((n_even, LANES - MLA_NOPE), F32)], axis=1), (1, 2))
    gkr = jnp.tile(jnp.concatenate([jnp.zeros((n_even, MLA_NOPE), F32), mla_qk_g[:, 1, MLA_NOPE:], zt], axis=1), (1, 2))
    pe2 = jnp.broadcast_to(nsa_cmp_pe.reshape(n_even, 2, 2, 16, 1, hd), (n_even, 2, 2, 16, NSA_GROUPS, hd)
                           ).reshape(n_even, 2, 2, 1, 16 * LANES)
    eye = jnp.eye(NSA_GROUPS, dtype=F32)
    w1 = nsa_cmp_w1.reshape(n_even, 2, 2, 16, hd, NSA_CMP_HID)
    w1x = jnp.einsum('ijaldc,gh->ijalgdhc', w1, eye).reshape(n_even, 2, 2, 16 * LANES, NSA_GROUPS * NSA_CMP_HID).astype(BF16)
    w2x = jnp.einsum('ijcd,gh->ijhdgc', nsa_cmp_w2, eye).reshape(n_even, 2, LANES, NSA_GROUPS * NSA_CMP_HID).astype(BF16)
    wa_o = ev_w_out[:, :MLA_HEADS * MLA_V].astype(BF16)
    wb_o = jnp.transpose(ev_w_out[:, MLA_HEADS * MLA_V:].reshape(n_even, NSA_GROUPS, NSA_HPG, hd, d),
                         (0, 2, 1, 3, 4)).reshape(n_even, NSA_HEADS * hd, d).astype(BF16)

    assert DIL_PAIRS == ((128, 1), (512, 4), (2048, 16))
    dil_cfg = tuple(zip(OD_DIL_STRIDES, (1, 1, 0)))
    dil_ok = (causal & (dist <= 128), causal & (dist <= 128), causal & (dist % 2 == 0))
    tab_dil = [_pair(_bias_tiles(t5_bias[:, gi * DIL_HPG:(gi + 1) * DIL_HPG], dil_ok[gi], dist_scale=dil_cfg[gi][0]))
               for gi in range(len(DIL_PAIRS))]
    tab_moba = _pair(_bias_tiles(t5_bias[:, DIL_SLOTS:DIL_SLOTS + MOBA_HEADS], causal))
    avg = np.zeros((LANES, s), np.float32)
    for h in range(MOBA_HEADS):
        for m in range(s // MOBA_BLOCK):
            avg[8 * h + m, m * MOBA_BLOCK:(m + 1) * MOBA_BLOCK] = 1.0 / MOBA_BLOCK
    avg = jnp.asarray(avg, BF16)
    od_w = od_w_in.astype(BF16)
    od_gain = jnp.concatenate([jnp.tile(dil_qk_g[:, 0], (1, 12)) * c64, jnp.tile(dil_qk_g[:, 1], (1, 12)),
                               jnp.ones((n_odd, 768), F32), jnp.tile(moba_qk_g[:, 0], (1, 4)) * c64,
                               jnp.tile(moba_qk_g[:, 1], (1, 4)), jnp.ones((n_odd, 256), F32)], axis=1)
    wd_o = od_w_out[:, :DIL_HPG * hd].astype(BF16)
    wm_o = od_w_out[:, DIL_HPG * hd:].astype(BF16)

    sh3 = lambda a: a.reshape(b, s, a.shape[-1])
    tr3 = lambda a: jnp.swapaxes(sh3(a), 1, 2)
    xf = x.reshape(n, d)
    for i in range(DEPTH):
        j = i // 2
        g_i = norm_g[i].reshape(3, 1, d)
        xf = _ffn(xf, g_i[0], mod[i, :, 0, 0], mod[i, :, 0, 1], mod[i, :, 0, 2],
                  wa_all[i, 0], wb_all[i, 0], wo_all[i, 0], s)
        if i % 2 == 0:
            mla_in, nsa_q, kc, vc, ks, vs, kw, vw = _proj(
                xf, g_i[1], mod[i, :, 1, 0], mod[i, :, 1, 1], ev_w[j], ev_gain[j][None], gm_proj,
                chunked(EV_META), EV_OUTS, s)
            qf, kf, vf = _mla_prep(mla_in, mla_q_norm_g[j][None], mla_kv_norm_g[j][None], wuq[j], wuk[j], wuv[j],
                                   gm_mla, gq[j][None], gkn[j][None], gkr[j][None], cos_t, sin_t, s)
            (o_a,) = _attn(sh3(qf), sh3(kf), vf, tab_mla, None, qc0=0, kc0=0, vb0=0, n_blk=MLA_HEADS // 2, qw=2,
                           tab_shared=True, out_dtype=BF16)
            kcmp, vcmpt = _nsa_cmp(sh3(kc), sh3(vc), pe2[j], w1x[j, :, 0], w1x[j, :, 1], w2x[j], gm64, gain_kc[j][None])
            o_c, sel = _nsa_sel(sh3(nsa_q), kcmp, vcmpt, bias_c, ovl)
            (o_s,) = _attn(sh3(nsa_q), sh3(ks), vs, tab_sel, sel, qc0=0, kc0=0, vb0=0, n_blk=NSA_HPG, qw=1,
                           kv_shared=True, sel_cfg=(TK // NSA_SLC_BLOCK, 32, 0), out_dtype=BF16)
            (o_w,) = _attn(sh3(nsa_q), sh3(kw), vw, tab_win, None, qc0=0, kc0=0, vb0=0, n_blk=NSA_HPG, qw=1,
                           kv_shared=True, backs=(2, 2, 2, 2), out_dtype=BF16)
            xf = _even_out(xf, o_a.reshape(n, -1), o_c.reshape(n, -1), o_s.reshape(n, -1), o_w.reshape(n, -1),
                           mla_in, eg, wa_o[j], wb_o[j], mod[i, :, 1, 2], s)
        else:
            pr, vd0t, vmt, qk1, qk2 = _proj(xf, g_i[1], mod[i, :, 1, 0], mod[i, :, 1, 1], od_w[j], od_gain[j][None],
                                            gm_proj, chunked(OD_META), OD_OUTS, s)
            pr3 = pr.reshape(b, s, pr.shape[-1])
            o_ds, lse_ds = [], []
            for gi, (r, bk) in enumerate(dil_cfg):
                if r == 1:
                    qk, vdt = pr3, vd0t
                else:
                    v = pr3[:, :, (2 + 2 * gi) * LANES:(4 + 2 * gi) * LANES].reshape(b, s // r, r, 2 * LANES)
                    qk, vdt = (qk1, qk2)[gi - 1], jnp.transpose(v, (0, 3, 2, 1)).reshape(b, 2 * LANES, s)
                o_g, lse_g = _attn(qk, qk, vdt, tab_dil[gi], None, qc0=0, kc0=2, vb0=0, n_blk=2,
                                   qw=1, backs=(bk,), want_lse=True, stride=r)
                o_ds.append(o_g.reshape(n, -1))
                lse_ds.append(lse_g.reshape(n, -1))
            selm = _moba_gate(pr3, avg, 4, 5)
            (o_m,) = _attn(pr3, pr3, vmt, tab_moba, selm, qc0=8, kc0=10, vb0=0, n_blk=2, qw=1,
                           sel_cfg=(1, 8, 2), out_dtype=BF16)
            xf = _odd_out(xf, o_ds, lse_ds, o_m.reshape(n, -1), wd_o[j], wm_o[j], mod[i, :, 1, 2], s)
        xf = _ffn(xf, g_i[2], mod[i, :, 2, 0], mod[i, :, 2, 1], mod[i, :, 2, 2],
                  wa_all[i, 1], wb_all[i, 1], wo_all[i, 1], s)
    return xf.reshape(b, s, d)
```

```python
import functools
import math

import numpy as np
import jax
import jax.numpy as jnp
from jax import lax
from jax.experimental import pallas as pl
from jax.experimental.pallas import tpu as pltpu

F32 = jnp.float32
BF16 = jnp.bfloat16

D_MODEL = 1024
DEPTH = 4
D_FF = 2752
HEAD_DIM = 64
NUM_BUCKETS = 32
T5_MAX_EXACT = 16
T5_MAX_DIST = 128
RMS_EPS = 1e-6
NEG_INF = -1e30
MLA_HEADS = 8
MLA_NOPE = 64
MLA_ROPE = 32
MLA_V = 64
MLA_Q_LORA = 256
MLA_KV_LORA = 128
ROPE_THETA = 10000.0
NSA_HEADS = 8
NSA_GROUPS = 2
NSA_HPG = 4
NSA_CMP_LEN = 32
NSA_CMP_STRIDE = 16
NSA_CMP_HID = 256
NSA_SLC_BLOCK = 64
NSA_SLC_TOP = 8
NSA_WINDOW = 512
NSA_FORCED = 1e6
DIL_PAIRS = ((128, 1), (512, 4), (2048, 16))
DIL_HPG = 4
DIL_SLOTS = len(DIL_PAIRS) * DIL_HPG
MOBA_HEADS = 4
MOBA_BLOCK = 256
MOBA_TOP = 3

LANES = 128
V7X_VMEM_BYTES = 64 * 1024 * 1024
VMEM_LIMIT = V7X_VMEM_BYTES * 7 // 8
TM = 512
TQ = 256
TK = 256
FF_CHUNK = 256
FF_PAD = -(-D_FF // FF_CHUNK) * FF_CHUNK
PROJ_CHUNK = 256
DEN_ROWS = 16
LOG2E = math.log2(math.e)
LN2 = math.log(2.0)


def _dot(a, b):
    return jnp.dot(a, b, preferred_element_type=F32)


def _dot_nt(a, b):
    return lax.dot_general(a, b, (((1,), (1,)), ((), ())), preferred_element_type=F32)


def _split(a):
    hi = a.astype(BF16)
    lo = (a - hi.astype(F32)).astype(BF16)
    return hi, lo


def _dot_hilo(a, b):
    hi, lo = _split(a)
    return _dot(hi, b) + _dot(lo, b)


def _sigmoid(x):
    return 1.0 / (1.0 + jnp.exp(-x))


def _modulated_norm(x, g, shift, scale):
    ms = jnp.mean(x * x, axis=-1, keepdims=True)
    y = x * lax.rsqrt(ms + RMS_EPS) * g
    return y * (1.0 + scale) + shift


def _params(*sem):
    return pltpu.CompilerParams(dimension_semantics=sem, vmem_limit_bytes=VMEM_LIMIT)


def _resident(shape):
    nd = len(shape)
    return pl.BlockSpec(shape, lambda *_: (0,) * nd, pipeline_mode=pl.Buffered(1))


def _ada_kernel(c_ref, w_ref, b_ref, o_ref):
    c = c_ref[...]
    ca = c * _sigmoid(c)
    o_ref[...] = jnp.dot(ca, w_ref[...], preferred_element_type=F32,
                         precision=lax.Precision.HIGHEST) + b_ref[...]


def _ada(c, ada_w, ada_b):
    depth, d, n = ada_w.shape
    b = c.shape[0]
    tn = 18 * LANES
    return pl.pallas_call(
        _ada_kernel,
        grid=(depth, n // tn),
        in_specs=[pl.BlockSpec((b, d), lambda l, j: (0, 0)),
                  pl.BlockSpec((None, d, tn), lambda l, j: (l, 0, j)),
                  pl.BlockSpec((None, 1, tn), lambda l, j: (l, 0, j))],
        out_specs=pl.BlockSpec((None, b, tn), lambda l, j: (l, 0, j)),
        out_shape=jax.ShapeDtypeStruct((depth, b, n), F32),
        compiler_params=_params("parallel", "parallel"),
        name="ada",
    )(c, ada_w, ada_b.reshape(depth, 1, n))


def _ffn_kernel(x_ref, g_ref, sh_ref, sc_ref, gt_ref, wa_ref, wb_ref, wo_ref, o_ref, y_ref, acc_ref):
    y_ref[...] = _modulated_norm(x_ref[...], g_ref[...], sh_ref[...], sc_ref[...]).astype(BF16)
    for c in range(FF_PAD // FF_CHUNK):
        sl = slice(c * FF_CHUNK, (c + 1) * FF_CHUNK)
        a = _dot(y_ref[...], wa_ref[:, sl])
        b = _dot(y_ref[...], wb_ref[:, sl])
        u = (a * _sigmoid(a) * b).astype(BF16)
        contrib = _dot(u, wo_ref[sl, :])
        if c == 0:
            acc_ref[...] = contrib
        else:
            acc_ref[...] += contrib
    o_ref[...] = x_ref[...] + 0.5 * gt_ref[...] * acc_ref[...]


def _mod_spec(tiles_per_batch, d):
    return pl.BlockSpec((None, 1, d), lambda i: (i // tiles_per_batch, 0, 0))


def _ffn_in_specs(ffn_args, tpb, d):
    _, _, _, _, wa, wb, wo = ffn_args
    return [_resident((1, d)), _mod_spec(tpb, d), _mod_spec(tpb, d), _mod_spec(tpb, d),
            _resident(wa.shape), _resident(wb.shape), _resident(wo.shape)]


def _ffn_scratch(d):
    return [pltpu.VMEM((TM, d), BF16), pltpu.VMEM((TM, d), F32)]


def _residual_then_ffn(x_new, ffn_refs):
    *refs, xm_ref = ffn_refs
    xm_ref[...] = x_new
    _ffn_kernel(xm_ref, *refs)


def _ffn(x, g, shift, scale, gate, wa, wb, wo, s):
    n, d = x.shape
    tpb = s // TM
    return pl.pallas_call(
        _ffn_kernel,
        grid=(n // TM,),
        in_specs=[pl.BlockSpec((TM, d), lambda i: (i, 0)),
                  _resident((1, d)),
                  _mod_spec(tpb, d), _mod_spec(tpb, d), _mod_spec(tpb, d),
                  _resident(wa.shape), _resident(wb.shape), _resident(wo.shape)],
        out_specs=pl.BlockSpec((TM, d), lambda i: (i, 0)),
        out_shape=jax.ShapeDtypeStruct((n, d), F32),
        scratch_shapes=[pltpu.VMEM((TM, d), BF16), pltpu.VMEM((TM, d), F32)],
        compiler_params=_params("parallel"),
        name="ffn",
    )(x, g, shift, scale, gate, wa, wb, wo)


def _proj_kernel(meta, kinds, seq, x_ref, g_ref, sh_ref, sc_ref, w_ref, gain_ref, gm_ref, *rest):
    n_out = len(kinds)
    outs = rest[:n_out]
    y_ref, z_ref, stage_ref = rest[n_out:n_out + 3]
    tile_in_seq = pl.program_id(0) % (seq // TM)
    y_ref[...] = _modulated_norm(x_ref[...], g_ref[...], sh_ref[...], sc_ref[...]).astype(BF16)
    for c in range(len(meta)):
        sl = slice(c * PROJ_CHUNK, (c + 1) * PROJ_CHUNK)
        z_ref[:, sl] = _dot(y_ref[...], w_ref[:, sl])
    for c, halves in enumerate(meta):
        sl = slice(c * PROJ_CHUNK, (c + 1) * PROJ_CHUNK)
        if any(normed for _, _, normed in halves):
            z = z_ref[:, sl]
            msq = _dot((z * z).astype(BF16), gm_ref[...])
            zn = z * lax.rsqrt(msq + RMS_EPS) * gain_ref[:, sl]
        for hf, (oi, off, normed) in enumerate(halves):
            hsl = slice(c * PROJ_CHUNK + hf * LANES, c * PROJ_CHUNK + (hf + 1) * LANES)
            kind = kinds[oi]
            if kind == "cols":
                src = (zn[:, hf * LANES:(hf + 1) * LANES] if normed else z_ref[:, hsl]) * gain_ref[:, hsl]
                outs[oi][off:off + LANES, :] = src.T.astype(outs[oi].dtype)
            elif kind == "rows":
                src = zn[:, hf * LANES:(hf + 1) * LANES] if normed else z_ref[:, hsl]
                outs[oi][:, off:off + LANES] = src.astype(outs[oi].dtype)
            else:
                stage_ref[...] = zn[:, hf * LANES:(hf + 1) * LANES] if normed else z_ref[:, hsl]
                per = TM // kind
                for cs in range(kind):
                    dest = pl.multiple_of(cs * (seq // kind) + tile_in_seq * per, per)
                    outs[oi][pl.ds(dest, per), off:off + LANES] = (
                        stage_ref[pl.ds(cs, per, stride=kind), :].astype(outs[oi].dtype))


def _proj(x, g, shift, scale, w, gain, gm, meta, out_defs, s):
    n, d = x.shape
    tpb = s // TM
    specs = {"rows": lambda wd: pl.BlockSpec((TM, wd), lambda i: (i, 0)),
             "cols": lambda wd: pl.BlockSpec((None, wd, TM), lambda i: (i // tpb, 0, i % tpb))}
    stream = lambda wd: pl.BlockSpec((None, s, wd), lambda i: (i // tpb, 0, 0))
    shapes = {"rows": lambda wd: (n, wd), "cols": lambda wd: (n // s, wd, s)}
    kinds = tuple(k for _, _, k in out_defs)
    return pl.pallas_call(
        functools.partial(_proj_kernel, meta, kinds, s),
        grid=(n // TM,),
        in_specs=[pl.BlockSpec((TM, d), lambda i: (i, 0)),
                  _resident((1, d)),
                  _mod_spec(tpb, d), _mod_spec(tpb, d),
                  _resident(w.shape), _resident(gain.shape), _resident(gm.shape)],
        out_specs=[specs.get(k, stream)(wd) for wd, _, k in out_defs],
        out_shape=[jax.ShapeDtypeStruct(shapes.get(k, lambda wd: (n // s, s, wd))(wd), dt) for wd, dt, k in out_defs],
        scratch_shapes=[pltpu.VMEM((TM, d), BF16), pltpu.VMEM((TM, w.shape[1]), F32), pltpu.VMEM((TM, LANES), F32)],
        compiler_params=_params("arbitrary"),
        name="proj",
    )(x, g, shift, scale, w, gain, gm)


def _mla_prep_kernel(in_ref, qg_ref, kvg_ref, wuq_ref, wuk_ref, wuv_ref, gm_ref,
                     gq_ref, gkn_ref, gkr_ref, cos_ref, sin_ref, q_out, k_out, v_out):
    def rms(z, g):
        return z * lax.rsqrt(jnp.mean(z * z, axis=-1, keepdims=True) + RMS_EPS) * g

    cqn = rms(in_ref[:, 0:MLA_Q_LORA], qg_ref[...]).astype(BF16)
    ckvn = rms(in_ref[:, MLA_Q_LORA:MLA_Q_LORA + MLA_KV_LORA], kvg_ref[...]).astype(BF16)
    c3 = in_ref[:, 3 * LANES:4 * LANES]
    cos = cos_ref[...]
    sin = sin_ref[...]
    gm = gm_ref[...]
    half = MLA_ROPE // 2
    slot_lane = lax.broadcasted_iota(jnp.int32, (1, 2 * LANES), 1) & (LANES - 1)
    first_half = slot_lane < MLA_NOPE + half

    def norm_rope(z, gain):
        msq = _dot((z * z).astype(BF16), gm)
        z = z * lax.rsqrt(msq + RMS_EPS) * gain
        rot = jnp.where(first_half, -pltpu.roll(z, 2 * LANES - half, 1), pltpu.roll(z, half, 1))
        return z * cos + rot * sin

    kr = norm_rope(jnp.concatenate([c3, c3], axis=1), gkr_ref[...])
    for h in range(MLA_HEADS // 2):
        sl = slice(h * 2 * LANES, (h + 1) * 2 * LANES)
        q_out[:, sl] = norm_rope(_dot(cqn, wuq_ref[:, sl]), gq_ref[...]).astype(BF16)
        k_out[:, sl] = (norm_rope(_dot(ckvn, wuk_ref[:, sl]), gkn_ref[...]) + kr).astype(BF16)
    v_out[...] = _dot_nt(wuv_ref[...], ckvn).astype(BF16)


def _mla_prep(mla_in, qg, kvg, wuq, wuk, wuv, gm, gq, gkn, gkr, cos_t, sin_t, s):
    n = mla_in.shape[0]
    tpb = s // TM
    hw = MLA_HEADS * LANES
    vw = MLA_HEADS * MLA_V
    tab = pl.BlockSpec((TM, 2 * LANES), lambda i: (i % tpb, 0))
    consts = [qg, kvg, wuq, wuk, wuv, gm, gq, gkn, gkr]
    return pl.pallas_call(
        _mla_prep_kernel,
        grid=(n // TM,),
        in_specs=[pl.BlockSpec((TM, 4 * LANES), lambda i: (i, 0))] + [_resident(a.shape) for a in consts] + [tab, tab],
        out_specs=[pl.BlockSpec((TM, hw), lambda i: (i, 0)), pl.BlockSpec((TM, hw), lambda i: (i, 0)),
                   pl.BlockSpec((None, vw, TM), lambda i: (i // tpb, 0, i % tpb))],
        out_shape=[jax.ShapeDtypeStruct((n, hw), BF16), jax.ShapeDtypeStruct((n, hw), BF16),
                   jax.ShapeDtypeStruct((n // s, vw, s), BF16)],
        compiler_params=_params("parallel"),
        name="mla_prep",
    )(mla_in, *consts, cos_t, sin_t)


def _attn_kernel(cfg, q_ref, k_ref, vt_ref, tab_ref, *rest):
    n_tab, qw, backs, sel_cfg, want_lse, n_qt, stride = cfg
    n_qs = n_qt // stride
    if sel_cfg is not None:
        sel_ref, rest = rest[0], rest[1:]
        sel_bpt, sel_stride, sel_pair_mul = sel_cfg
    o_ref = rest[0]
    lse_ref = rest[1] if want_lse else None
    st_scr, p_scr, acc_fin, m_fin = rest[-6:-4], rest[-4:-2], rest[-2], rest[-1]
    blk = pl.program_id(1)
    lane = lax.broadcasted_iota(jnp.int32, (1, LANES), 1)

    counts = tuple(sum(min(t % n_qs, bk) + 1 for t in range(n_qt)) for bk in backs)
    back = jnp.int32(backs[-1])
    n_tiles = jnp.int32(counts[-1])
    for bi in range(len(backs) - 2, -1, -1):
        back = jnp.where(blk == bi, jnp.int32(backs[bi]), back)
        n_tiles = jnp.where(blk == bi, jnp.int32(counts[bi]), n_tiles)

    ones_rows = jnp.ones((DEN_ROWS, TK), BF16)

    def first_key_tile(qi):
        return qi - jnp.minimum(qi & (n_qs - 1), back)

    def rows(t, size):
        return pl.ds(t * size if isinstance(t, int) else pl.multiple_of(t * size, size), size)

    def out_rows(t, size):
        if stride == 1:
            return rows(t, size)
        return pl.ds((t >> int(math.log2(n_qs))) + stride * size * (t & (n_qs - 1)), size, stride=stride)

    def logits_to(slot, qi, j, filler):
        d = jnp.where(filler, n_tab, jnp.minimum(qi - j, n_tab - 1))
        if qw == 1:
            q = q_ref[rows(qi, TQ), :]
            zero = jnp.zeros_like(q)
            qs = [jnp.where(lane < HEAD_DIM, q, zero), jnp.where(lane >= HEAD_DIM, q, zero)]
        else:
            qs = [q_ref[rows(qi, TQ), s * LANES:(s + 1) * LANES] for s in range(2)]
        for s in range(2):
            kj = k_ref[rows(j, TK), :] if qw == 1 else k_ref[rows(j, TK), s * LANES:(s + 1) * LANES]
            st = _dot_nt(kj, qs[s])
            if sel_cfg is not None:
                off = sel_stride * (s + sel_pair_mul * blk) + sel_bpt * j
                kb = TK // sel_bpt
                qcol = pl.ds(pl.multiple_of(qi * TQ, TQ), TQ)
                st = jnp.concatenate(
                    [st[bk * kb:(bk + 1) * kb] + sel_ref[pl.ds(off + bk, 1), qcol]
                     for bk in range(sel_bpt)], axis=0)
            st_scr[slot][s] = st + tab_ref[s, d]

    def accumulate(slot, j, alphas, accs):
        krow = pl.multiple_of(j * TK, TK)
        new = []
        for s in range(2):
            vt = jnp.concatenate([vt_ref[s * HEAD_DIM:(s + 1) * HEAD_DIM, pl.ds(krow, TK)], ones_rows], axis=0)
            new.append(alphas[s] * accs[s] + _dot(vt, p_scr[slot][s]))
        return tuple(new)

    def finalize(qi, ms, accs):
        for s in range(2):
            acc_fin[qi, s] = accs[s]
            if want_lse:
                m_fin[qi, s] = jnp.broadcast_to(ms[s], m_fin.shape[2:])

    def write_out(t):
        accs = [acc_fin[t, s] for s in range(2)]
        dens = [acc[HEAD_DIM:HEAD_DIM + 1] for acc in accs]
        out_t = jnp.concatenate([acc[:HEAD_DIM] / l for acc, l in zip(accs, dens)], axis=0)
        o_ref[out_rows(t, TQ), :] = out_t.T.astype(o_ref.dtype)
        if want_lse:
            lse_t = jnp.concatenate([jnp.broadcast_to((m_fin[t, s][0:1] + jnp.log2(dens[s])) * LN2, (HEAD_DIM, TQ))
                                     for s in range(2)], axis=0)
            lse_ref[out_rows(t, TQ), :] = lse_t.T

    def advance(q, j):
        last = j == q
        at_end = jnp.logical_and(last, q == n_qt - 1)
        starts = jnp.logical_and(last, jnp.logical_not(at_end))
        qn = jnp.where(starts, q + 1, q)
        jn = jnp.where(at_end, j, jnp.where(last, first_key_tile(q + 1), j + 1))
        return qn, jn, starts, at_end

    def softmax(slot, tile, stats):
        is_first = tile[2]
        keep = jnp.where(is_first, 0.0, 1.0)
        new_stats, alphas = [], []
        for s in range(2):
            m = jnp.where(is_first, NEG_INF, stats[s])
            st = st_scr[slot][s]
            m_new = jnp.maximum(m, jnp.max(st, axis=0, keepdims=True))
            alphas.append(jnp.exp2(m - m_new) * keep)
            new_stats.append(m_new)
            p_scr[slot][s] = jnp.exp2(st - m_new).astype(BF16)
        return tuple(new_stats), tuple(alphas)

    def body(u, carry):
        tile_a, (q2, j2), (q1, j1, first1), stats2, stats1, alphas2, alphas1, accs = carry
        qa, ja, first_a, _ = tile_a
        tile_b = advance(qa, ja)
        qb, jb, first_b, _ = tile_b
        tile_c = advance(qb, jb)
        logits_to(1, qb, jb, tile_b[3])
        accs_x = accumulate(0, j2, alphas2, accs)
        accs_y = accumulate(1, j1, alphas1, accs_x)
        stats_a, alphas_a = softmax(0, tile_a, stats1)
        logits_to(0, tile_c[0], tile_c[1], tile_c[3])
        stats_b, alphas_b = softmax(1, tile_b, stats_a)

        finalize(q2, stats2, accs_x)
        finalize(q1, stats1, accs_y)
        return tile_c, (qa, ja), (qb, jb, first_b), stats_a, stats_b, alphas_a, alphas_b, accs_y

    zero_i = jnp.int32(0)
    logits_to(0, zero_i, zero_i, zero_i != 0)
    for p_slot in p_scr:
        p_slot[...] = jnp.zeros(p_slot.shape, BF16)
    stats0 = tuple(jnp.full((1, TQ), NEG_INF, F32) for _ in range(2))
    ones = tuple(jnp.ones((1, TQ), F32) for _ in range(2))
    init = ((zero_i, zero_i, zero_i == 0, zero_i != 0), (zero_i, zero_i), (zero_i, zero_i, zero_i != 0),
            stats0, stats0, ones, ones, tuple(jnp.zeros((HEAD_DIM + DEN_ROWS, TQ), F32) for _ in range(2)))
    _, (q2, j2), (q1, j1, first1), stats2, stats1, alphas2, alphas1, accs = lax.fori_loop(
        0, (n_tiles + 1) // 2, body, init)
    accs_x = accumulate(0, j2, alphas2, accs)
    finalize(q2, stats2, accs_x)
    finalize(q1, stats1, accumulate(1, j1, alphas1, accs_x))
    for t in range(n_qt):
        write_out(t)


def _attn(q, k, vt, tab, sel, *, qc0, kc0, vb0, n_blk, qw, backs=None, kv_shared=False,
          tab_shared=False, sel_cfg=None, out_dtype=F32, want_lse=False, stride=1):
    b, s = q.shape[:2]
    n_tab = tab.shape[2] - 1
    n_qt = s // TQ
    if backs is None:
        backs = (n_qt,)
    cfg = (n_tab, qw, backs, sel_cfg, want_lse, n_qt, stride)
    kidx = (lambda bb, h: (bb, 0, kc0)) if kv_shared else (lambda bb, h: (bb, 0, kc0 + h))
    vidx = (lambda bb, h: (bb, vb0, 0)) if kv_shared else (lambda bb, h: (bb, vb0 + h, 0))
    tidx = (lambda bb, h: (0, 0, 0, 0, 0)) if tab_shared else (lambda bb, h: (h, 0, 0, 0, 0))
    in_specs = [pl.BlockSpec((None, s, qw * LANES), lambda bb, h: (bb, 0, qc0 + h)),
                pl.BlockSpec((None, s, qw * LANES), kidx),
                pl.BlockSpec((None, 2 * HEAD_DIM, s), vidx),
                pl.BlockSpec((None, 2, n_tab + 1, TK, TQ), tidx)]
    args = [q, k, vt, tab]
    if sel_cfg is not None:
        in_specs.append(pl.BlockSpec((None, LANES, s), lambda bb, h: (bb, 0, 0)))
        args.append(sel)
    ospec = pl.BlockSpec((None, s, LANES), lambda bb, h: (bb, 0, h))
    out_specs = [ospec]
    out_shape = [jax.ShapeDtypeStruct((b, s, n_blk * LANES), out_dtype)]
    if want_lse:
        out_specs.append(ospec)
        out_shape.append(jax.ShapeDtypeStruct((b, s, n_blk * LANES), F32))
    return pl.pallas_call(
        functools.partial(_attn_kernel, cfg),
        grid=(b, n_blk),
        in_specs=in_specs,
        out_specs=out_specs,
        out_shape=out_shape,
        scratch_shapes=([pltpu.VMEM((2, TK, TQ), F32)] * 2 + [pltpu.VMEM((2, TK, TQ), BF16)] * 2
                        + [pltpu.VMEM((n_qt, 2, HEAD_DIM + DEN_ROWS, TQ), F32), pltpu.VMEM((n_qt, 2, 8, TQ), F32)]),
        compiler_params=_params("parallel", "parallel"),
        name="attn",
    )(*args)


def _nsa_cmp_kernel(kc_ref, vc_ref, pe_ref, wlo_ref, whi_ref, w2_ref, gm_ref, gain_ref, kcmp_ref, vcmpt_ref):
    nch = kcmp_ref.shape[0]

    def hidden(c_ref, j):
        lo = hi = None
        for l in range(NSA_CMP_STRIDE):
            rows = c_ref[pl.ds(l, nch, stride=NSA_CMP_STRIDE), :]
            sl = slice(l * LANES, (l + 1) * LANES)
            t_lo = _dot((rows + pe_ref[j, 0][:, sl]).astype(BF16), wlo_ref[j][sl, :])
            t_hi = _dot((rows + pe_ref[j, 1][:, sl]).astype(BF16), whi_ref[j][sl, :])
            lo = t_lo if lo is None else lo + t_lo
            hi = t_hi if hi is None else hi + t_hi
        h = lo + pltpu.roll(hi, nch - 1, 0)
        return (h * _sigmoid(h)).astype(BF16)

    kz = _dot_nt(hidden(kc_ref, 0), w2_ref[0])
    msq = _dot_hilo(kz * kz, gm_ref[...])
    kcmp_ref[...] = (kz * lax.rsqrt(msq + RMS_EPS) * gain_ref[...]).astype(BF16)
    vcmpt_ref[...] = _dot_nt(w2_ref[1], hidden(vc_ref, 1)).astype(BF16)


def _nsa_cmp(kc3, vc3, pe, wlo, whi, w2, gm, gain):
    b, s, width = kc3.shape
    nch = s // NSA_CMP_STRIDE
    consts = [pe, wlo, whi, w2, gm, gain]
    blk = pl.BlockSpec((None, s, width), lambda i: (i, 0, 0))
    oblk = pl.BlockSpec((None, nch, LANES), lambda i: (i, 0, 0))
    return pl.pallas_call(
        _nsa_cmp_kernel,
        grid=(b,),
        in_specs=[blk, blk] + [_resident(a.shape) for a in consts],
        out_specs=[oblk, oblk],
        out_shape=[jax.ShapeDtypeStruct((b, nch, LANES), BF16)] * 2,
        compiler_params=_params("parallel"),
        name="nsa_cmp",
    )(kc3, vc3, *consts)


def _rank_keep(score, ids, top):
    cnt = jnp.zeros(score.shape, jnp.int32)
    for mp in range(score.shape[0]):
        other = score[mp:mp + 1, :]
        tie = jnp.where(mp < ids, 1, 0)
        cnt = cnt + jnp.where(other > score, 1, jnp.where(other == score, tie, 0))
    return cnt < top


def _nsa_sel_kernel(q_ref, kcmp_ref, vcmpt_ref, bias_ref, ovl_ref, oc_ref, sel_ref):
    qi = pl.program_id(1)
    lane = lax.broadcasted_iota(jnp.int32, (1, LANES), 1)
    row = lax.broadcasted_iota(jnp.int32, (LANES, 1), 0)
    t = qi * TQ + lax.broadcasted_iota(jnp.int32, (1, TQ), 1)
    mask_c = (NSA_CMP_STRIDE * row + NSA_CMP_LEN - 1) <= t
    kcmp = kcmp_ref[...]
    heads = [(p, g) for p in range(NSA_HPG) for g in range(NSA_GROUPS)]
    raw = []
    for p, g in heads:
        qp = q_ref[:, p * LANES:(p + 1) * LANES]
        raw.append(_dot_nt(kcmp, jnp.where((lane >> 6) == g, qp, jnp.zeros_like(qp))))
    pcs = []
    for (p, g), r in zip(heads, raw):
        lg = jnp.where(mask_c, r + bias_ref[p * NSA_GROUPS + g], NEG_INF)
        m = jnp.max(lg, axis=0, keepdims=True)
        e = jnp.where(mask_c, jnp.exp2(lg - m), 0.0)
        den = jnp.maximum(jnp.sum(e, axis=0, keepdims=True), 1e-30)
        pcs.append(e / den)
    imp = jnp.zeros((LANES, TQ), F32)
    ocs = []
    for (p, g), pc in zip(heads, pcs):
        hi, lo = _split(pc)
        ocs.append(_dot(vcmpt_ref[g * HEAD_DIM:(g + 1) * HEAD_DIM, :], hi))
        imp = imp + _dot(ovl_ref[g], hi) + _dot(ovl_ref[g], lo)
    for p in range(NSA_HPG):
        oc_ref[:, p * LANES:(p + 1) * LANES] = jnp.concatenate([ocs[2 * p], ocs[2 * p + 1]], axis=0).T.astype(BF16)

    n_slc = 32
    ids = row[0:n_slc]
    cur = t >> 6
    forced = (ids == 0) | (ids == cur) | (ids == cur - 1)
    masks = []
    for g in range(NSA_GROUPS):
        score = jnp.where(forced, NSA_FORCED, jnp.where(ids <= cur, imp[g * n_slc:(g + 1) * n_slc], NEG_INF))
        keep = _rank_keep(score, ids, NSA_SLC_TOP) & (score > 0.5 * NEG_INF)
        masks.append(jnp.where(keep, 0.0, NEG_INF))
    masks.append(jnp.full((LANES - NSA_GROUPS * n_slc, TQ), NEG_INF, F32))
    sel_ref[...] = jnp.concatenate(masks, axis=0)


def _nsa_sel(q, kcmp, vcmpt, bias_c, ovl):
    b, s, w = q.shape
    return pl.pallas_call(
        _nsa_sel_kernel,
        grid=(b, s // TQ),
        in_specs=[pl.BlockSpec((None, TQ, w), lambda bb, i: (bb, i, 0)),
                  pl.BlockSpec((None, LANES, LANES), lambda bb, i: (bb, 0, 0)),
                  pl.BlockSpec((None, LANES, LANES), lambda bb, i: (bb, 0, 0)),
                  pl.BlockSpec((NSA_HEADS, LANES, TQ), lambda bb, i: (0, 0, i)),
                  _resident(ovl.shape)],
        out_specs=[pl.BlockSpec((None, TQ, w), lambda bb, i: (bb, i, 0)),
                   pl.BlockSpec((None, LANES, TQ), lambda bb, i: (bb, 0, i))],
        out_shape=[jax.ShapeDtypeStruct((b, s, w), BF16), jax.ShapeDtypeStruct((b, LANES, s), F32)],
        compiler_params=_params("parallel", "parallel"),
        name="nsa_sel",
    )(q, kcmp, vcmpt, bias_c, ovl)


def _moba_gate_kernel(q_ref, k_ref, avg_ref, sel_ref, kh_ref, kl_ref):
    qi = pl.program_id(1)
    nb = 8

    @pl.when(qi == 0)
    def _():
        kmean = _dot(avg_ref[...], k_ref[...])
        r2 = lax.broadcasted_iota(jnp.int32, kmean.shape, 0)
        c2 = lax.broadcasted_iota(jnp.int32, kmean.shape, 1)
        kh_ref[...], kl_ref[...] = _split(jnp.where((r2 >> 3) == (c2 >> 6), kmean, 0.0))

    q = q_ref[...]
    gate = _dot_nt(kh_ref[...], q) + _dot_nt(kl_ref[...], q)
    ids = lax.broadcasted_iota(jnp.int32, (nb, 1), 0)
    past = ids < qi
    masks = []
    for h in range(MOBA_HEADS):
        score = jnp.where(past, gate[h * nb:(h + 1) * nb], NEG_INF)
        keep = (_rank_keep(score, ids, MOBA_TOP) & past) | (ids == qi)
        masks.append(jnp.where(keep, 0.0, NEG_INF))
    masks.append(jnp.full((LANES - MOBA_HEADS * nb, TQ), NEG_INF, F32))
    sel_ref[...] = jnp.concatenate(masks, axis=0)


def _moba_gate(p_arr, avg, qc, kc):
    b, s, _ = p_arr.shape
    w = MOBA_HEADS * HEAD_DIM
    return pl.pallas_call(
        _moba_gate_kernel,
        grid=(b, s // TQ),
        in_specs=[pl.BlockSpec((None, TQ, w), lambda bb, i: (bb, i, qc)),
                  pl.BlockSpec((None, s, w), lambda bb, i: (bb, 0, kc)),
                  _resident(avg.shape)],
        out_specs=pl.BlockSpec((None, LANES, TQ), lambda bb, i: (bb, 0, i)),
        out_shape=jax.ShapeDtypeStruct((b, LANES, s), F32),
        scratch_shapes=[pltpu.VMEM((LANES, w), BF16)] * 2,
        compiler_params=_params("parallel", "arbitrary"),
        name="moba_gate",
    )(p_arr, p_arr, avg)


def _even_out_kernel(x_ref, oa_ref, oc_ref, os_ref, ow_ref, gl_ref, eg_ref, wa_ref, wb_ref, gt_ref, *ffn_refs):
    sg = _sigmoid(gl_ref[...])
    hi, lo = _split(sg)
    nsa = None
    for br, src in enumerate((oc_ref, os_ref, ow_ref)):
        gexp = _dot(hi, eg_ref[br]) + _dot(lo, eg_ref[br])
        term = gexp * src[...]
        nsa = term if nsa is None else nsa + term
    m = _dot(oa_ref[...], wa_ref[...]) + _dot(nsa.astype(BF16), wb_ref[...])
    _residual_then_ffn(x_ref[...] + gt_ref[...] * m, ffn_refs)


def _even_out(x, o_a, o_c, o_s, o_w, mla_in, eg, wa, wb, gate, ffn_args, s):
    n, d = x.shape
    tpb = s // TM
    row = lambda wd: pl.BlockSpec((TM, wd), lambda i: (i, 0))
    return pl.pallas_call(
        _even_out_kernel,
        grid=(n // TM,),
        in_specs=[row(d), row(o_a.shape[1]), row(o_c.shape[1]), row(o_s.shape[1]), row(o_w.shape[1]),
                  pl.BlockSpec((TM, LANES), lambda i: (i, 3)),
                  _resident(eg.shape), _resident(wa.shape), _resident(wb.shape), _mod_spec(tpb, d)]
        + _ffn_in_specs(ffn_args, tpb, d),
        out_specs=row(d),
        out_shape=jax.ShapeDtypeStruct((n, d), F32),
        scratch_shapes=_ffn_scratch(d) + [pltpu.VMEM((TM, d), F32)],
        compiler_params=_params("parallel"),
        name="even_out",
    )(x, o_a, o_c, o_s, o_w, mla_in, eg, wa, wb, gate, *ffn_args)


def _odd_out_kernel(x_ref, od0_ref, od1_ref, od2_ref, ls0_ref, ls1_ref, ls2_ref, om_ref, wd_ref, wm_ref, gt_ref, *ffn_refs):
    ods = (od0_ref, od1_ref, od2_ref)
    ls = [r[...] for r in (ls0_ref, ls1_ref, ls2_ref)]
    mx = jnp.maximum(jnp.maximum(ls[0], ls[1]), ls[2])
    es = [jnp.exp(l - mx) for l in ls]
    den = es[0] + es[1] + es[2]
    merged = None
    for g in range(len(DIL_PAIRS)):
        term = (es[g] / den) * ods[g][...]
        merged = term if merged is None else merged + term
    m = _dot(merged.astype(BF16), wd_ref[...]) + _dot(om_ref[...], wm_ref[...])
    _residual_then_ffn(x_ref[...] + gt_ref[...] * m, ffn_refs)


def _odd_out(x, o_ds, lse_ds, o_m, wd, wm, gate, ffn_args, s):
    n, d = x.shape
    tpb = s // TM
    row = lambda wd_: pl.BlockSpec((TM, wd_), lambda i: (i, 0))
    return pl.pallas_call(
        _odd_out_kernel,
        grid=(n // TM,),
        in_specs=[row(d)] + [row(a.shape[1]) for a in (*o_ds, *lse_ds)] + [row(o_m.shape[1]),
                  _resident(wd.shape), _resident(wm.shape), _mod_spec(tpb, d)] + _ffn_in_specs(ffn_args, tpb, d),
        out_specs=row(d),
        out_shape=jax.ShapeDtypeStruct((n, d), F32),
        scratch_shapes=_ffn_scratch(d) + [pltpu.VMEM((TM, d), F32)],
        compiler_params=_params("parallel"),
        name="odd_out",
    )(x, *o_ds, *lse_ds, o_m, wd, wm, gate, *ffn_args)


def _t5_bucket(dist):
    n = jnp.maximum(jnp.asarray(dist, jnp.int32), 0)
    nf = jnp.maximum(n, 1).astype(F32)
    large = T5_MAX_EXACT + (jnp.log(nf / T5_MAX_EXACT) / math.log(T5_MAX_DIST / T5_MAX_EXACT)
                            * (NUM_BUCKETS - T5_MAX_EXACT)).astype(jnp.int32)
    return jnp.where(n < T5_MAX_EXACT, n, jnp.minimum(large, NUM_BUCKETS - 1))


TOEP_PERIOD = 4 * TQ


def _toeplitz_dist():
    j = np.arange(TOEP_PERIOD)
    return np.where(j < 3 * TQ, j, j - TOEP_PERIOD)


def _toeplitz_kernel(n_tab, u_ref, o_ref):
    x = jnp.broadcast_to(u_ref[...], (TK, TOEP_PERIOD))
    y = pltpu.roll(x, 0, 1, stride=1, stride_axis=0)
    for dlt in range(n_tab):
        o_ref[dlt] = y[:, dlt * TQ:(dlt + 1) * TQ]
    o_ref[n_tab] = jnp.full((TK, TQ), NEG_INF * LOG2E, F32)


def _toeplitz_tiles(u, n_tab):
    h = u.shape[0]
    return pl.pallas_call(
        functools.partial(_toeplitz_kernel, n_tab),
        grid=(h,),
        in_specs=[pl.BlockSpec((None, 1, TOEP_PERIOD), lambda i: (i, 0, 0))],
        out_specs=pl.BlockSpec((None, n_tab + 1, TK, TQ), lambda i: (i, 0, 0, 0)),
        out_shape=jax.ShapeDtypeStruct((h, n_tab + 1, TK, TQ), F32),
        compiler_params=_params("parallel"),
        name="toeplitz",
    )(u.reshape(h, 1, TOEP_PERIOD))


def _cmp_bias_kernel(u_ref, o_ref):
    x = jnp.broadcast_to(u_ref[...], o_ref.shape)
    o_ref[...] = pltpu.roll(x, 0, 1, stride=NSA_CMP_STRIDE, stride_axis=0)


def _cmp_bias(u, n_rows):
    h, s = u.shape
    return pl.pallas_call(
        _cmp_bias_kernel,
        grid=(h,),
        in_specs=[pl.BlockSpec((None, 1, s), lambda i: (i, 0, 0))],
        out_specs=pl.BlockSpec((None, n_rows, s), lambda i: (i, 0, 0)),
        out_shape=jax.ShapeDtypeStruct((h, n_rows, s), F32),
        compiler_params=_params("parallel"),
        name="cmp_bias",
    )(u.reshape(h, 1, s))


def _bias_tiles(t5_cols, ok, n_tab=3, dist_scale=1):
    dist = _toeplitz_dist()
    bias = jnp.transpose(t5_cols[_t5_bucket(dist * dist_scale)])
    u = jnp.where(jnp.asarray(ok)[None], bias, NEG_INF) * LOG2E
    return _toeplitz_tiles(u, n_tab)


def _pair(tiles):
    h = tiles.shape[0]
    return tiles.reshape(h // 2, 2, *tiles.shape[1:])


def _group_mean_np(sizes, width=LANES):
    gm = np.zeros((width, width), np.float32)
    o = 0
    for sz in sizes:
        gm[o:o + sz, o:o + sz] = 1.0 / sz
        o += sz
    return gm


def _group_mean_matrix(sizes, width=LANES):
    return jnp.asarray(_group_mean_np(sizes, width), BF16)


EV_META = ((0, 0, False), (0, 128, False), (0, 256, False), (0, 384, False),
           (1, 0, True), (1, 128, True), (1, 256, True), (1, 384, True),
           (2, 0, False), (3, 0, False), (4, 0, True), (5, 0, False), (6, 0, True), (7, 0, False))
EV_OUTS = [(512, F32, "rows"), (512, BF16, "rows"), (128, F32, "rows"), (128, F32, "rows"), (128, BF16, "rows"),
           (128, BF16, "cols"), (128, BF16, "rows"), (128, BF16, "cols")]
OD_DIL_STRIDES = (1, 4, 8)
OD_META = (((0, 0, True), (0, LANES, True)) + tuple((3 + g, h * LANES, True) for g in range(2) for h in range(2))
           + ((0, 2 * LANES, True), (0, 3 * LANES, True))
           + tuple((3 + g, (2 + h) * LANES, True) for g in range(2) for h in range(2))
           + ((1, 0, False), (1, LANES, False)) + tuple((0, (4 + c) * LANES, False) for c in range(4))
           + tuple((0, (8 + c) * LANES, True) for c in range(4)) + ((2, 0, False), (2, LANES, False)))
OD_OUTS = [(12 * LANES, BF16, "rows"), (2 * LANES, BF16, "cols"), (2 * LANES, BF16, "cols"),
           (4 * LANES, BF16, OD_DIL_STRIDES[1]), (4 * LANES, BF16, OD_DIL_STRIDES[2])]


def _even_w_in(w):
    jn, d, _ = w.shape
    z = lambda n_: jnp.zeros((jn, d, n_), w.dtype)
    nq = w[:, :, 416:928].reshape(jn, d, NSA_GROUPS, NSA_HPG, HEAD_DIM)
    nq = jnp.transpose(nq, (0, 1, 3, 2, 4)).reshape(jn, d, NSA_HEADS * HEAD_DIM)
    chunk3 = jnp.concatenate([w[:, :, 1696:1720], z(HEAD_DIM - 24), w[:, :, 384:416], z(LANES - 96)], axis=-1)
    return jnp.concatenate([w[:, :, 0:384], chunk3, nq, w[:, :, 928:1696]], axis=-1)


def kernel(x, c, t5_bias, ada_w, ada_b, norm_g, ffn_w_in, ffn_w_out, ev_w_in, ev_w_out, mla_q_norm_g,
           mla_kv_norm_g, mla_w_uq, mla_w_ukv, mla_qk_g, nsa_cmp_pe, nsa_cmp_w1, nsa_cmp_w2, nsa_qk_g,
           od_w_in, od_w_out, dil_qk_g, moba_qk_g):
    b, s, d = x.shape
    assert (s, d) == (2048, D_MODEL) and s % TM == 0 and TQ == MOBA_BLOCK and TQ == TK
    n = b * s
    hd = HEAD_DIM
    n_even = ev_w_in.shape[0]
    n_odd = od_w_in.shape[0]
    c64 = hd ** -0.5 * LOG2E
    c96 = (MLA_NOPE + MLA_ROPE) ** -0.5 * LOG2E

    mod = _ada(c, ada_w, ada_b).reshape(DEPTH, b, 3, 3, 1, d)

    dist = _toeplitz_dist()
    causal = dist >= 0
    gm64 = _group_mean_matrix((hd, hd))
    gm_proj = _group_mean_matrix((hd,) * (PROJ_CHUNK // hd), PROJ_CHUNK)
    per_chunk = PROJ_CHUNK // LANES
    chunked = lambda m: tuple(tuple(m[i:i + per_chunk]) for i in range(0, len(m), per_chunk))

    padc = FF_PAD - D_FF
    wa_all = jnp.pad(ffn_w_in[..., :D_FF], ((0, 0), (0, 0), (0, 0), (0, padc))).astype(BF16)
    wb_all = jnp.pad(ffn_w_in[..., D_FF:], ((0, 0), (0, 0), (0, 0), (0, padc))).astype(BF16)
    wo_all = jnp.pad(ffn_w_out, ((0, 0), (0, 0), (0, padc), (0, 0))).astype(BF16)

    nsa_tab = t5_bias[:, MLA_HEADS:MLA_HEADS + NSA_HEADS].reshape(NUM_BUCKETS, NSA_GROUPS, NSA_HPG)
    nsa_cols = jnp.transpose(nsa_tab, (0, 2, 1)).reshape(NUM_BUCKETS, NSA_HEADS)
    tab_sel = _pair(_bias_tiles(nsa_cols, causal))
    tab_win = _pair(_bias_tiles(nsa_cols, causal & (dist <= NSA_WINDOW - 1)))
    tab_mla = _toeplitz_tiles(jnp.where(jnp.asarray(causal), 0.0, NEG_INF).astype(F32)[None], 2)
    tab_mla = jnp.broadcast_to(tab_mla[None], (1, 2) + tab_mla.shape[1:])
    n_cmp_pad = s // NSA_CMP_STRIDE
    bias_c = _cmp_bias(jnp.transpose(nsa_cols[_t5_bucket(np.arange(s) - (NSA_CMP_LEN - 1))]) * LOG2E, n_cmp_pad)
    n_cmp = (s - NSA_CMP_LEN) // NSA_CMP_STRIDE + 1
    cstart = np.arange(n_cmp) * NSA_CMP_STRIDE
    sstart = np.arange(s // NSA_SLC_BLOCK) * NSA_SLC_BLOCK
    overlap = np.clip(np.minimum(cstart[:, None] + NSA_CMP_LEN, sstart[None, :] + NSA_SLC_BLOCK)
                      - np.maximum(cstart[:, None], sstart[None, :]), 0, None).astype(np.float32) / NSA_CMP_LEN
    ovl = np.zeros((NSA_GROUPS, LANES, LANES), np.float32)
    for g in range(NSA_GROUPS):
        ovl[g, 32 * g:32 * g + 32, :n_cmp] = overlap.T
    ovl = jnp.asarray(ovl, BF16)
    eg = np.zeros((3, LANES, NSA_HEADS * hd), np.float32)
    for g in range(NSA_GROUPS):
        for p in range(NSA_HPG):
            for br in range(3):
                eg[br, (g * NSA_HPG + p) * 3 + br, p * LANES + g * hd:p * LANES + (g + 1) * hd] = 1.0
    eg = jnp.asarray(eg, BF16)
    gm_mla = jnp.asarray(np.kron(np.eye(2, dtype=np.float32), _group_mean_np((MLA_NOPE, MLA_ROPE))), BF16)
    inv = ROPE_THETA ** (-jnp.arange(0, MLA_ROPE, 2, dtype=F32) / MLA_ROPE)
    ang = jnp.arange(s, dtype=F32)[:, None] * inv[None, :]
    ones = jnp.ones((s, MLA_NOPE), F32)
    tail = LANES - MLA_NOPE - MLA_ROPE
    cos_t = jnp.tile(jnp.concatenate([ones, jnp.cos(ang), jnp.cos(ang), jnp.ones((s, tail), F32)], axis=1), (1, 2))
    sin_t = jnp.tile(jnp.concatenate([0 * ones, jnp.sin(ang), jnp.sin(ang), jnp.zeros((s, tail), F32)], axis=1), (1, 2))

    ev_w = _even_w_in(ev_w_in).astype(BF16)
    ev_gain = jnp.ones((n_even, 14 * LANES), F32)
    ev_gain = ev_gain.at[:, 512:1024].set(jnp.tile(nsa_qk_g[:, 0], (1, 8)) * c64)
    ev_gain = ev_gain.at[:, 1280:1408].set(jnp.tile(nsa_qk_g[:, 1], (1, 2)))
    ev_gain = ev_gain.at[:, 1536:1664].set(jnp.tile(nsa_qk_g[:, 1], (1, 2)))
    gain_kc = jnp.tile(nsa_qk_g[:, 1], (1, 2))
    wuq = jnp.pad(mla_w_uq.reshape(n_even, MLA_Q_LORA, MLA_HEADS, MLA_NOPE + MLA_ROPE),
                  ((0, 0), (0, 0), (0, 0), (0, tail))).reshape(n_even, MLA_Q_LORA, MLA_HEADS * LANES).astype(BF16)
    ukv = mla_w_ukv.reshape(n_even, MLA_KV_LORA, MLA_HEADS, MLA_NOPE + MLA_V)
    wuk = jnp.pad(ukv[..., :MLA_NOPE], ((0, 0), (0, 0), (0, 0), (0, LANES - MLA_NOPE))
                  ).reshape(n_even, MLA_KV_LORA, MLA_HEADS * LANES).astype(BF16)
    wuv = jnp.swapaxes(ukv[..., MLA_NOPE:].reshape(n_even, MLA_KV_LORA, MLA_HEADS * MLA_V), 1, 2).astype(BF16)
    zt = jnp.zeros((n_even, tail), F32)
    gq = jnp.tile(jnp.concatenate([mla_qk_g[:, 0] * c96, zt], axis=1), (1, 2))
    gkn = jnp.tile(jnp.concatenate([mla_qk_g[:, 1, :MLA_NOPE], jnp.zeros((n_even, LANES - MLA_NOPE), F32)], axis=1), (1, 2))
    gkr = jnp.tile(jnp.concatenate([jnp.zeros((n_even, MLA_NOPE), F32), mla_qk_g[:, 1, MLA_NOPE:], zt], axis=1), (1, 2))
    pe2 = jnp.broadcast_to(nsa_cmp_pe.reshape(n_even, 2, 2, 16, 1, hd), (n_even, 2, 2, 16, NSA_GROUPS, hd)
                           ).reshape(n_even, 2, 2, 1, 16 * LANES)
    eye = jnp.eye(NSA_GROUPS, dtype=F32)
    w1 = nsa_cmp_w1.reshape(n_even, 2, 2, 16, hd, NSA_CMP_HID)
    w1x = jnp.einsum('ijaldc,gh->ijalgdhc', w1, eye).reshape(n_even, 2, 2, 16 * LANES, NSA_GROUPS * NSA_CMP_HID).astype(BF16)
    w2x = jnp.einsum('ijcd,gh->ijhdgc', nsa_cmp_w2, eye).reshape(n_even, 2, LANES, NSA_GROUPS * NSA_CMP_HID).astype(BF16)
    wa_o = ev_w_out[:, :MLA_HEADS * MLA_V].astype(BF16)
    wb_o = jnp.transpose(ev_w_out[:, MLA_HEADS * MLA_V:].reshape(n_even, NSA_GROUPS, NSA_HPG, hd, d),
                         (0, 2, 1, 3, 4)).reshape(n_even, NSA_HEADS * hd, d).astype(BF16)

    assert DIL_PAIRS == ((128, 1), (512, 4), (2048, 16))
    dil_cfg = tuple(zip(OD_DIL_STRIDES, (1, 1, 0)))
    dil_ok = (causal & (dist <= 128), causal & (dist <= 128), causal & (dist % 2 == 0))
    tab_dil = [_pair(_bias_tiles(t5_bias[:, gi * DIL_HPG:(gi + 1) * DIL_HPG], dil_ok[gi], dist_scale=dil_cfg[gi][0]))
               for gi in range(len(DIL_PAIRS))]
    tab_moba = _pair(_bias_tiles(t5_bias[:, DIL_SLOTS:DIL_SLOTS + MOBA_HEADS], causal))
    avg = np.zeros((LANES, s), np.float32)
    for h in range(MOBA_HEADS):
        for m in range(s // MOBA_BLOCK):
            avg[8 * h + m, m * MOBA_BLOCK:(m + 1) * MOBA_BLOCK] = 1.0 / MOBA_BLOCK
    avg = jnp.asarray(avg, BF16)
    od_w = od_w_in.astype(BF16)
    od_gain = jnp.concatenate([jnp.tile(dil_qk_g[:, 0], (1, 12)) * c64, jnp.tile(dil_qk_g[:, 1], (1, 12)),
                               jnp.ones((n_odd, 768), F32), jnp.tile(moba_qk_g[:, 0], (1, 4)) * c64,
                               jnp.tile(moba_qk_g[:, 1], (1, 4)), jnp.ones((n_odd, 256), F32)], axis=1)
    wd_o = od_w_out[:, :DIL_HPG * hd].astype(BF16)
    wm_o = od_w_out[:, DIL_HPG * hd:].astype(BF16)

    sh3 = lambda a: a.reshape(b, s, a.shape[-1])
    tr3 = lambda a: jnp.swapaxes(sh3(a), 1, 2)
    xf = x.reshape(n, d)
    for i in range(DEPTH):
        j = i // 2
        g_i = norm_g[i].reshape(3, 1, d)
        ffn2 = (g_i[2], mod[i, :, 2, 0], mod[i, :, 2, 1], mod[i, :, 2, 2], wa_all[i, 1], wb_all[i, 1], wo_all[i, 1])
        xf = _ffn(xf, g_i[0], mod[i, :, 0, 0], mod[i, :, 0, 1], mod[i, :, 0, 2],
                  wa_all[i, 0], wb_all[i, 0], wo_all[i, 0], s)
        if i % 2 == 0:
            mla_in, nsa_q, kc, vc, ks, vs, kw, vw = _proj(
                xf, g_i[1], mod[i, :, 1, 0], mod[i, :, 1, 1], ev_w[j], ev_gain[j][None], gm_proj,
                chunked(EV_META), EV_OUTS, s)
            qf, kf, vf = _mla_prep(mla_in, mla_q_norm_g[j][None], mla_kv_norm_g[j][None], wuq[j], wuk[j], wuv[j],
                                   gm_mla, gq[j][None], gkn[j][None], gkr[j][None], cos_t, sin_t, s)
            (o_a,) = _attn(sh3(qf), sh3(kf), vf, tab_mla, None, qc0=0, kc0=0, vb0=0, n_blk=MLA_HEADS // 2, qw=2,
                           tab_shared=True, out_dtype=BF16)
            kcmp, vcmpt = _nsa_cmp(sh3(kc), sh3(vc), pe2[j], w1x[j, :, 0], w1x[j, :, 1], w2x[j], gm64, gain_kc[j][None])
            o_c, sel = _nsa_sel(sh3(nsa_q), kcmp, vcmpt, bias_c, ovl)
            (o_s,) = _attn(sh3(nsa_q), sh3(ks), vs, tab_sel, sel, qc0=0, kc0=0, vb0=0, n_blk=NSA_HPG, qw=1,
                           kv_shared=True, sel_cfg=(TK // NSA_SLC_BLOCK, 32, 0), out_dtype=BF16)
            (o_w,) = _attn(sh3(nsa_q), sh3(kw), vw, tab_win, None, qc0=0, kc0=0, vb0=0, n_blk=NSA_HPG, qw=1,
                           kv_shared=True, backs=(2, 2, 2, 2), out_dtype=BF16)
            xf = _even_out(xf, o_a.reshape(n, -1), o_c.reshape(n, -1), o_s.reshape(n, -1), o_w.reshape(n, -1),
                           mla_in, eg, wa_o[j], wb_o[j], mod[i, :, 1, 2], ffn2, s)
        else:
            pr, vd0t, vmt, qk1, qk2 = _proj(xf, g_i[1], mod[i, :, 1, 0], mod[i, :, 1, 1], od_w[j], od_gain[j][None],
                                            gm_proj, chunked(OD_META), OD_OUTS, s)
            pr3 = pr.reshape(b, s, pr.shape[-1])
            o_ds, lse_ds = [], []
            for gi, (r, bk) in enumerate(dil_cfg):
                if r == 1:
                    qk, vdt = pr3, vd0t
                else:
                    v = pr3[:, :, (2 + 2 * gi) * LANES:(4 + 2 * gi) * LANES].reshape(b, s // r, r, 2 * LANES)
                    qk, vdt = (qk1, qk2)[gi - 1], jnp.transpose(v, (0, 3, 2, 1)).reshape(b, 2 * LANES, s)
                o_g, lse_g = _attn(qk, qk, vdt, tab_dil[gi], None, qc0=0, kc0=2, vb0=0, n_blk=2,
                                   qw=1, backs=(bk,), want_lse=True, stride=r)
                o_ds.append(o_g.reshape(n, -1))
                lse_ds.append(lse_g.reshape(n, -1))
            selm = _moba_gate(pr3, avg, 4, 5)
            (o_m,) = _attn(pr3, pr3, vmt, tab_moba, selm, qc0=8, kc0=10, vb0=0, n_blk=2, qw=1,
                           sel_cfg=(1, 8, 2), out_dtype=BF16)
            xf = _odd_out(xf, o_ds, lse_ds, o_m.reshape(n, -1), wd_o[j], wm_o[j], mod[i, :, 1, 2], ffn2, s)
    return xf.reshape(b, s, d)
```

```python
import functools
import math

import numpy as np
import jax
import jax.numpy as jnp
from jax import lax
from jax.experimental import pallas as pl
from jax.experimental.pallas import tpu as pltpu

F32 = jnp.float32
BF16 = jnp.bfloat16

D_MODEL = 1024
DEPTH = 4
D_FF = 2752
HEAD_DIM = 64
NUM_BUCKETS = 32
T5_MAX_EXACT = 16
T5_MAX_DIST = 128
RMS_EPS = 1e-6
NEG_INF = -1e30
MLA_HEADS = 8
MLA_NOPE = 64
MLA_ROPE = 32
MLA_V = 64
MLA_Q_LORA = 256
MLA_KV_LORA = 128
ROPE_THETA = 10000.0
NSA_HEADS = 8
NSA_GROUPS = 2
NSA_HPG = 4
NSA_CMP_LEN = 32
NSA_CMP_STRIDE = 16
NSA_CMP_HID = 256
NSA_SLC_BLOCK = 64
NSA_SLC_TOP = 8
NSA_WINDOW = 512
NSA_FORCED = 1e6
DIL_PAIRS = ((128, 1), (512, 4), (2048, 16))
DIL_HPG = 4
DIL_SLOTS = len(DIL_PAIRS) * DIL_HPG
MOBA_HEADS = 4
MOBA_BLOCK = 256
MOBA_TOP = 3

LANES = 128
V7X_VMEM_BYTES = 64 * 1024 * 1024
VMEM_LIMIT = V7X_VMEM_BYTES * 7 // 8
TM = 512
TQ = 256
TK = 256
FF_CHUNK = 256
FF_PAD = -(-D_FF // FF_CHUNK) * FF_CHUNK
PROJ_CHUNK = 256
DEN_ROWS = 16
LOG2E = math.log2(math.e)
LN2 = math.log(2.0)


def _dot(a, b):
    return jnp.dot(a, b, preferred_element_type=F32)


def _dot_nt(a, b):
    return lax.dot_general(a, b, (((1,), (1,)), ((), ())), preferred_element_type=F32)


def _split(a):
    hi = a.astype(BF16)
    lo = (a - hi.astype(F32)).astype(BF16)
    return hi, lo


def _dot_hilo(a, b):
    hi, lo = _split(a)
    return _dot(hi, b) + _dot(lo, b)


def _sigmoid(x):
    return 1.0 / (1.0 + jnp.exp(-x))


def _modulated_norm(x, g, shift, scale):
    ms = jnp.mean(x * x, axis=-1, keepdims=True)
    y = x * lax.rsqrt(ms + RMS_EPS) * g
    return y * (1.0 + scale) + shift


def _params(*sem):
    return pltpu.CompilerParams(dimension_semantics=sem, vmem_limit_bytes=VMEM_LIMIT)


def _resident(shape):
    nd = len(shape)
    return pl.BlockSpec(shape, lambda *_: (0,) * nd, pipeline_mode=pl.Buffered(1))


def _ada_kernel(c_ref, w_ref, b_ref, o_ref):
    c = c_ref[...]
    ca = c * _sigmoid(c)
    o_ref[...] = jnp.dot(ca, w_ref[...], preferred_element_type=F32,
                         precision=lax.Precision.HIGHEST) + b_ref[...]


def _ada(c, ada_w, ada_b):
    depth, d, n = ada_w.shape
    b = c.shape[0]
    tn = 18 * LANES
    return pl.pallas_call(
        _ada_kernel,
        grid=(depth, n // tn),
        in_specs=[pl.BlockSpec((b, d), lambda l, j: (0, 0)),
                  pl.BlockSpec((None, d, tn), lambda l, j: (l, 0, j)),
                  pl.BlockSpec((None, 1, tn), lambda l, j: (l, 0, j))],
        out_specs=pl.BlockSpec((None, b, tn), lambda l, j: (l, 0, j)),
        out_shape=jax.ShapeDtypeStruct((depth, b, n), F32),
        compiler_params=_params("parallel", "parallel"),
        name="ada",
    )(c, ada_w, ada_b.reshape(depth, 1, n))


def _ffn_kernel(x_ref, g_ref, sh_ref, sc_ref, gt_ref, wa_ref, wb_ref, wo_ref, o_ref, y_ref, acc_ref):
    y_ref[...] = _modulated_norm(x_ref[...], g_ref[...], sh_ref[...], sc_ref[...]).astype(BF16)
    for c in range(FF_PAD // FF_CHUNK):
        sl = slice(c * FF_CHUNK, (c + 1) * FF_CHUNK)
        a = _dot(y_ref[...], wa_ref[:, sl])
        b = _dot(y_ref[...], wb_ref[:, sl])
        u = (a * _sigmoid(a) * b).astype(BF16)
        contrib = _dot(u, wo_ref[sl, :])
        if c == 0:
            acc_ref[...] = contrib
        else:
            acc_ref[...] += contrib
    o_ref[...] = x_ref[...] + 0.5 * gt_ref[...] * acc_ref[...]


def _mod_spec(tiles_per_batch, d):
    return pl.BlockSpec((None, 1, d), lambda i: (i // tiles_per_batch, 0, 0))


def _ffn_in_specs(ffn_args, tpb, d):
    _, _, _, _, wa, wb, wo = ffn_args
    return [_resident((1, d)), _mod_spec(tpb, d), _mod_spec(tpb, d), _mod_spec(tpb, d),
            _resident(wa.shape), _resident(wb.shape), _resident(wo.shape)]


def _ffn_scratch(d):
    return [pltpu.VMEM((TM, d), BF16), pltpu.VMEM((TM, d), F32)]


def _residual_then_ffn(x_new, ffn_refs):
    *refs, xm_ref = ffn_refs
    xm_ref[...] = x_new
    _ffn_kernel(xm_ref, *refs)


def _ffn(x, g, shift, scale, gate, wa, wb, wo, s):
    n, d = x.shape
    tpb = s // TM
    return pl.pallas_call(
        _ffn_kernel,
        grid=(n // TM,),
        in_specs=[pl.BlockSpec((TM, d), lambda i: (i, 0)),
                  _resident((1, d)),
                  _mod_spec(tpb, d), _mod_spec(tpb, d), _mod_spec(tpb, d),
                  _resident(wa.shape), _resident(wb.shape), _resident(wo.shape)],
        out_specs=pl.BlockSpec((TM, d), lambda i: (i, 0)),
        out_shape=jax.ShapeDtypeStruct((n, d), F32),
        scratch_shapes=[pltpu.VMEM((TM, d), BF16), pltpu.VMEM((TM, d), F32)],
        compiler_params=_params("parallel"),
        name="ffn",
    )(x, g, shift, scale, gate, wa, wb, wo)


def _proj_kernel(meta, kinds, seq, n_mla, x_ref, g_ref, sh_ref, sc_ref, w_ref, gain_ref, gm_ref, *rest):
    mla_ins, rest = rest[:n_mla], rest[n_mla:]
    n_out = len(kinds)
    outs = rest[:n_out]
    mla_outs = rest[n_out:n_out + (3 if n_mla else 0)]
    y_ref, z_ref, stage_ref = rest[-3:]
    tile_in_seq = pl.program_id(0) % (seq // TM)
    y_ref[...] = _modulated_norm(x_ref[...], g_ref[...], sh_ref[...], sc_ref[...]).astype(BF16)
    for c in range(len(meta)):
        sl = slice(c * PROJ_CHUNK, (c + 1) * PROJ_CHUNK)
        z_ref[:, sl] = _dot(y_ref[...], w_ref[:, sl])
    for c, halves in enumerate(meta):
        sl = slice(c * PROJ_CHUNK, (c + 1) * PROJ_CHUNK)
        if any(normed for _, _, normed in halves):
            z = z_ref[:, sl]
            msq = _dot((z * z).astype(BF16), gm_ref[...])
            zn = z * lax.rsqrt(msq + RMS_EPS) * gain_ref[:, sl]
        for hf, (oi, off, normed) in enumerate(halves):
            hsl = slice(c * PROJ_CHUNK + hf * LANES, c * PROJ_CHUNK + (hf + 1) * LANES)
            kind = kinds[oi]
            if kind == "cols":
                src = (zn[:, hf * LANES:(hf + 1) * LANES] if normed else z_ref[:, hsl]) * gain_ref[:, hsl]
                outs[oi][off:off + LANES, :] = src.T.astype(outs[oi].dtype)
            elif kind == "rows":
                src = zn[:, hf * LANES:(hf + 1) * LANES] if normed else z_ref[:, hsl]
                outs[oi][:, off:off + LANES] = src.astype(outs[oi].dtype)
            else:
                stage_ref[...] = zn[:, hf * LANES:(hf + 1) * LANES] if normed else z_ref[:, hsl]
                per = TM // kind
                for cs in range(kind):
                    dest = pl.multiple_of(cs * (seq // kind) + tile_in_seq * per, per)
                    outs[oi][pl.ds(dest, per), off:off + LANES] = (
                        stage_ref[pl.ds(cs, per, stride=kind), :].astype(outs[oi].dtype))
    if n_mla:
        _mla_prep_kernel(z_ref, *mla_ins, *mla_outs)


def _proj(x, g, shift, scale, w, gain, gm, meta, out_defs, s, mla=()):
    n, d = x.shape
    tpb = s // TM
    specs = {"rows": lambda wd: pl.BlockSpec((TM, wd), lambda i: (i, 0)),
             "cols": lambda wd: pl.BlockSpec((None, wd, TM), lambda i: (i // tpb, 0, i % tpb))}
    stream = lambda wd: pl.BlockSpec((None, s, wd), lambda i: (i // tpb, 0, 0))
    shapes = {"rows": lambda wd: (n, wd), "cols": lambda wd: (n // s, wd, s)}
    kinds = tuple(k for _, _, k in out_defs)
    hw, vw = MLA_HEADS * LANES, MLA_HEADS * MLA_V
    tab = pl.BlockSpec((TM, 2 * LANES), lambda i: (i % tpb, 0))
    mla_specs = [_resident(a.shape) for a in mla[:-2]] + ([tab, tab] if mla else [])
    mla_out_specs = [specs["rows"](hw), specs["rows"](hw), specs["cols"](vw)] if mla else []
    mla_out_shapes = [jax.ShapeDtypeStruct(sh, BF16) for sh in ((n, hw), (n, hw), (n // s, vw, s))] if mla else []
    return pl.pallas_call(
        functools.partial(_proj_kernel, meta, kinds, s, len(mla)),
        grid=(n // TM,),
        in_specs=[pl.BlockSpec((TM, d), lambda i: (i, 0)),
                  _resident((1, d)),
                  _mod_spec(tpb, d), _mod_spec(tpb, d),
                  _resident(w.shape), _resident(gain.shape), _resident(gm.shape)] + mla_specs,
        out_specs=[specs.get(k, stream)(wd) for wd, _, k in out_defs] + mla_out_specs,
        out_shape=[jax.ShapeDtypeStruct(shapes.get(k, lambda wd: (n // s, s, wd))(wd), dt) for wd, dt, k in out_defs]
        + mla_out_shapes,
        scratch_shapes=[pltpu.VMEM((TM, d), BF16), pltpu.VMEM((TM, w.shape[1]), F32), pltpu.VMEM((TM, LANES), F32)],
        compiler_params=_params("arbitrary"),
        name="proj",
    )(x, g, shift, scale, w, gain, gm, *mla)


def _mla_prep_kernel(in_ref, qg_ref, kvg_ref, wuq_ref, wuk_ref, wuv_ref, gm_ref,
                     gq_ref, gkn_ref, gkr_ref, cos_ref, sin_ref, q_out, k_out, v_out):
    def rms(z, g):
        return z * lax.rsqrt(jnp.mean(z * z, axis=-1, keepdims=True) + RMS_EPS) * g

    cqn = rms(in_ref[:, 0:MLA_Q_LORA], qg_ref[...]).astype(BF16)
    ckvn = rms(in_ref[:, MLA_Q_LORA:MLA_Q_LORA + MLA_KV_LORA], kvg_ref[...]).astype(BF16)
    c3 = in_ref[:, 3 * LANES:4 * LANES]
    cos = cos_ref[...]
    sin = sin_ref[...]
    gm = gm_ref[...]
    half = MLA_ROPE // 2
    slot_lane = lax.broadcasted_iota(jnp.int32, (1, 2 * LANES), 1) & (LANES - 1)
    first_half = slot_lane < MLA_NOPE + half

    def norm_rope(z, gain):
        msq = _dot((z * z).astype(BF16), gm)
        z = z * lax.rsqrt(msq + RMS_EPS) * gain
        rot = jnp.where(first_half, -pltpu.roll(z, 2 * LANES - half, 1), pltpu.roll(z, half, 1))
        return z * cos + rot * sin

    kr = norm_rope(jnp.concatenate([c3, c3], axis=1), gkr_ref[...])
    for h in range(MLA_HEADS // 2):
        sl = slice(h * 2 * LANES, (h + 1) * 2 * LANES)
        q_out[:, sl] = norm_rope(_dot(cqn, wuq_ref[:, sl]), gq_ref[...]).astype(BF16)
        k_out[:, sl] = (norm_rope(_dot(ckvn, wuk_ref[:, sl]), gkn_ref[...]) + kr).astype(BF16)
    v_out[...] = _dot_nt(wuv_ref[...], ckvn).astype(BF16)


def _mla_prep(mla_in, qg, kvg, wuq, wuk, wuv, gm, gq, gkn, gkr, cos_t, sin_t, s):
    n = mla_in.shape[0]
    tpb = s // TM
    hw = MLA_HEADS * LANES
    vw = MLA_HEADS * MLA_V
    tab = pl.BlockSpec((TM, 2 * LANES), lambda i: (i % tpb, 0))
    consts = [qg, kvg, wuq, wuk, wuv, gm, gq, gkn, gkr]
    return pl.pallas_call(
        _mla_prep_kernel,
        grid=(n // TM,),
        in_specs=[pl.BlockSpec((TM, 4 * LANES), lambda i: (i, 0))] + [_resident(a.shape) for a in consts] + [tab, tab],
        out_specs=[pl.BlockSpec((TM, hw), lambda i: (i, 0)), pl.BlockSpec((TM, hw), lambda i: (i, 0)),
                   pl.BlockSpec((None, vw, TM), lambda i: (i // tpb, 0, i % tpb))],
        out_shape=[jax.ShapeDtypeStruct((n, hw), BF16), jax.ShapeDtypeStruct((n, hw), BF16),
                   jax.ShapeDtypeStruct((n // s, vw, s), BF16)],
        compiler_params=_params("parallel"),
        name="mla_prep",
    )(mla_in, *consts, cos_t, sin_t)


def _attn_kernel(cfg, q_ref, k_ref, vt_ref, tab_ref, *rest):
    n_tab, qw, backs, sel_cfg, want_lse, n_qt, stride = cfg
    n_qs = n_qt // stride
    if sel_cfg is not None:
        sel_ref, rest = rest[0], rest[1:]
        sel_bpt, sel_stride, sel_pair_mul = sel_cfg
    o_ref = rest[0]
    lse_ref = rest[1] if want_lse else None
    st_scr, p_scr, acc_fin, m_fin = rest[-6:-4], rest[-4:-2], rest[-2], rest[-1]
    blk = pl.program_id(1)
    lane = lax.broadcasted_iota(jnp.int32, (1, LANES), 1)

    counts = tuple(sum(min(t % n_qs, bk) + 1 for t in range(n_qt)) for bk in backs)
    back = jnp.int32(backs[-1])
    n_tiles = jnp.int32(counts[-1])
    for bi in range(len(backs) - 2, -1, -1):
        back = jnp.where(blk == bi, jnp.int32(backs[bi]), back)
        n_tiles = jnp.where(blk == bi, jnp.int32(counts[bi]), n_tiles)

    ones_rows = jnp.ones((DEN_ROWS, TK), BF16)

    def first_key_tile(qi):
        return qi - jnp.minimum(qi & (n_qs - 1), back)

    def rows(t, size):
        return pl.ds(t * size if isinstance(t, int) else pl.multiple_of(t * size, size), size)

    def out_rows(t, size):
        if stride == 1:
            return rows(t, size)
        return pl.ds((t >> int(math.log2(n_qs))) + stride * size * (t & (n_qs - 1)), size, stride=stride)

    def logits_to(slot, qi, j, filler):
        d = jnp.where(filler, n_tab, jnp.minimum(qi - j, n_tab - 1))
        if qw == 1:
            q = q_ref[rows(qi, TQ), :]
            zero = jnp.zeros_like(q)
            qs = [jnp.where(lane < HEAD_DIM, q, zero), jnp.where(lane >= HEAD_DIM, q, zero)]
        else:
            qs = [q_ref[rows(qi, TQ), s * LANES:(s + 1) * LANES] for s in range(2)]
        for s in range(2):
            kj = k_ref[rows(j, TK), :] if qw == 1 else k_ref[rows(j, TK), s * LANES:(s + 1) * LANES]
            st = _dot_nt(kj, qs[s])
            if sel_cfg is not None:
                off = sel_stride * (s + sel_pair_mul * blk) + sel_bpt * j
                kb = TK // sel_bpt
                qcol = pl.ds(pl.multiple_of(qi * TQ, TQ), TQ)
                st = jnp.concatenate(
                    [st[bk * kb:(bk + 1) * kb] + sel_ref[pl.ds(off + bk, 1), qcol]
                     for bk in range(sel_bpt)], axis=0)
            st_scr[slot][s] = st + tab_ref[s, d]

    def accumulate(slot, j, alphas, accs):
        krow = pl.multiple_of(j * TK, TK)
        new = []
        for s in range(2):
            vt = jnp.concatenate([vt_ref[s * HEAD_DIM:(s + 1) * HEAD_DIM, pl.ds(krow, TK)], ones_rows], axis=0)
            new.append(alphas[s] * accs[s] + _dot(vt, p_scr[slot][s]))
        return tuple(new)

    def finalize(qi, ms, accs):
        for s in range(2):
            acc_fin[qi, s] = accs[s]
            if want_lse:
                m_fin[qi, s] = jnp.broadcast_to(ms[s], m_fin.shape[2:])

    def write_out(t):
        accs = [acc_fin[t, s] for s in range(2)]
        dens = [acc[HEAD_DIM:HEAD_DIM + 1] for acc in accs]
        out_t = jnp.concatenate([acc[:HEAD_DIM] / l for acc, l in zip(accs, dens)], axis=0)
        o_ref[out_rows(t, TQ), :] = out_t.T.astype(o_ref.dtype)
        if want_lse:
            lse_t = jnp.concatenate([jnp.broadcast_to((m_fin[t, s][0:1] + jnp.log2(dens[s])) * LN2, (HEAD_DIM, TQ))
                                     for s in range(2)], axis=0)
            lse_ref[out_rows(t, TQ), :] = lse_t.T

    def advance(q, j):
        last = j == q
        at_end = jnp.logical_and(last, q == n_qt - 1)
        starts = jnp.logical_and(last, jnp.logical_not(at_end))
        qn = jnp.where(starts, q + 1, q)
        jn = jnp.where(at_end, j, jnp.where(last, first_key_tile(q + 1), j + 1))
        return qn, jn, starts, at_end

    def softmax(slot, tile, stats):
        is_first = tile[2]
        keep = jnp.where(is_first, 0.0, 1.0)
        new_stats, alphas = [], []
        for s in range(2):
            m = jnp.where(is_first, NEG_INF, stats[s])
            st = st_scr[slot][s]
            m_new = jnp.maximum(m, jnp.max(st, axis=0, keepdims=True))
            alphas.append(jnp.exp2(m - m_new) * keep)
            new_stats.append(m_new)
            p_scr[slot][s] = jnp.exp2(st - m_new).astype(BF16)
        return tuple(new_stats), tuple(alphas)

    def body(u, carry):
        tile_a, (q2, j2), (q1, j1, first1), stats2, stats1, alphas2, alphas1, accs = carry
        qa, ja, first_a, _ = tile_a
        tile_b = advance(qa, ja)
        qb, jb, first_b, _ = tile_b
        tile_c = advance(qb, jb)
        logits_to(1, qb, jb, tile_b[3])
        accs_x = accumulate(0, j2, alphas2, accs)
        accs_y = accumulate(1, j1, alphas1, accs_x)
        stats_a, alphas_a = softmax(0, tile_a, stats1)
        logits_to(0, tile_c[0], tile_c[1], tile_c[3])
        stats_b, alphas_b = softmax(1, tile_b, stats_a)

        finalize(q2, stats2, accs_x)
        finalize(q1, stats1, accs_y)
        return tile_c, (qa, ja), (qb, jb, first_b), stats_a, stats_b, alphas_a, alphas_b, accs_y

    zero_i = jnp.int32(0)
    logits_to(0, zero_i, zero_i, zero_i != 0)
    for p_slot in p_scr:
        p_slot[...] = jnp.zeros(p_slot.shape, BF16)
    stats0 = tuple(jnp.full((1, TQ), NEG_INF, F32) for _ in range(2))
    ones = tuple(jnp.ones((1, TQ), F32) for _ in range(2))
    init = ((zero_i, zero_i, zero_i == 0, zero_i != 0), (zero_i, zero_i), (zero_i, zero_i, zero_i != 0),
            stats0, stats0, ones, ones, tuple(jnp.zeros((HEAD_DIM + DEN_ROWS, TQ), F32) for _ in range(2)))
    _, (q2, j2), (q1, j1, first1), stats2, stats1, alphas2, alphas1, accs = lax.fori_loop(
        0, (n_tiles + 1) // 2, body, init)
    accs_x = accumulate(0, j2, alphas2, accs)
    finalize(q2, stats2, accs_x)
    finalize(q1, stats1, accumulate(1, j1, alphas1, accs_x))
    for t in range(n_qt):
        write_out(t)


def _attn(q, k, vt, tab, sel, *, qc0, kc0, vb0, n_blk, qw, backs=None, kv_shared=False,
          tab_shared=False, sel_cfg=None, out_dtype=F32, want_lse=False, stride=1):
    b, s = q.shape[:2]
    n_tab = tab.shape[2] - 1
    n_qt = s // TQ
    if backs is None:
        backs = (n_qt,)
    cfg = (n_tab, qw, backs, sel_cfg, want_lse, n_qt, stride)
    kidx = (lambda bb, h: (bb, 0, kc0)) if kv_shared else (lambda bb, h: (bb, 0, kc0 + h))
    vidx = (lambda bb, h: (bb, vb0, 0)) if kv_shared else (lambda bb, h: (bb, vb0 + h, 0))
    tidx = (lambda bb, h: (0, 0, 0, 0, 0)) if tab_shared else (lambda bb, h: (h, 0, 0, 0, 0))
    in_specs = [pl.BlockSpec((None, s, qw * LANES), lambda bb, h: (bb, 0, qc0 + h)),
                pl.BlockSpec((None, s, qw * LANES), kidx),
                pl.BlockSpec((None, 2 * HEAD_DIM, s), vidx),
                pl.BlockSpec((None, 2, n_tab + 1, TK, TQ), tidx)]
    args = [q, k, vt, tab]
    if sel_cfg is not None:
        in_specs.append(pl.BlockSpec((None, LANES, s), lambda bb, h: (bb, 0, 0)))
        args.append(sel)
    ospec = pl.BlockSpec((None, s, LANES), lambda bb, h: (bb, 0, h))
    out_specs = [ospec]
    out_shape = [jax.ShapeDtypeStruct((b, s, n_blk * LANES), out_dtype)]
    if want_lse:
        out_specs.append(ospec)
        out_shape.append(jax.ShapeDtypeStruct((b, s, n_blk * LANES), F32))
    return pl.pallas_call(
        functools.partial(_attn_kernel, cfg),
        grid=(b, n_blk),
        in_specs=in_specs,
        out_specs=out_specs,
        out_shape=out_shape,
        scratch_shapes=([pltpu.VMEM((2, TK, TQ), F32)] * 2 + [pltpu.VMEM((2, TK, TQ), BF16)] * 2
                        + [pltpu.VMEM((n_qt, 2, HEAD_DIM + DEN_ROWS, TQ), F32), pltpu.VMEM((n_qt, 2, 8, TQ), F32)]),
        compiler_params=_params("parallel", "parallel"),
        name="attn",
    )(*args)


def _nsa_cmp_kernel(kc_ref, vc_ref, pe_ref, wlo_ref, whi_ref, w2_ref, gm_ref, gain_ref, kcmp_ref, vcmpt_ref):
    nch = kcmp_ref.shape[0]

    def hidden(c_ref, j):
        lo = hi = None
        for l in range(NSA_CMP_STRIDE):
            rows = c_ref[pl.ds(l, nch, stride=NSA_CMP_STRIDE), :]
            sl = slice(l * LANES, (l + 1) * LANES)
            t_lo = _dot((rows + pe_ref[j, 0][:, sl]).astype(BF16), wlo_ref[j][sl, :])
            t_hi = _dot((rows + pe_ref[j, 1][:, sl]).astype(BF16), whi_ref[j][sl, :])
            lo = t_lo if lo is None else lo + t_lo
            hi = t_hi if hi is None else hi + t_hi
        h = lo + pltpu.roll(hi, nch - 1, 0)
        return (h * _sigmoid(h)).astype(BF16)

    kz = _dot_nt(hidden(kc_ref, 0), w2_ref[0])
    msq = _dot_hilo(kz * kz, gm_ref[...])
    kcmp_ref[...] = (kz * lax.rsqrt(msq + RMS_EPS) * gain_ref[...]).astype(BF16)
    vcmpt_ref[...] = _dot_nt(w2_ref[1], hidden(vc_ref, 1)).astype(BF16)


def _nsa_cmp(kc3, vc3, pe, wlo, whi, w2, gm, gain):
    b, s, width = kc3.shape
    nch = s // NSA_CMP_STRIDE
    consts = [pe, wlo, whi, w2, gm, gain]
    blk = pl.BlockSpec((None, s, width), lambda i: (i, 0, 0))
    oblk = pl.BlockSpec((None, nch, LANES), lambda i: (i, 0, 0))
    return pl.pallas_call(
        _nsa_cmp_kernel,
        grid=(b,),
        in_specs=[blk, blk] + [_resident(a.shape) for a in consts],
        out_specs=[oblk, oblk],
        out_shape=[jax.ShapeDtypeStruct((b, nch, LANES), BF16)] * 2,
        compiler_params=_params("parallel"),
        name="nsa_cmp",
    )(kc3, vc3, *consts)


def _rank_keep(score, ids, top):
    cnt = jnp.zeros(score.shape, jnp.int32)
    for mp in range(score.shape[0]):
        other = score[mp:mp + 1, :]
        tie = jnp.where(mp < ids, 1, 0)
        cnt = cnt + jnp.where(other > score, 1, jnp.where(other == score, tie, 0))
    return cnt < top


def _nsa_sel_kernel(q_ref, kcmp_ref, vcmpt_ref, bias_ref, ovl_ref, oc_ref, sel_ref):
    qi = pl.program_id(1)
    lane = lax.broadcasted_iota(jnp.int32, (1, LANES), 1)
    row = lax.broadcasted_iota(jnp.int32, (LANES, 1), 0)
    t = qi * TQ + lax.broadcasted_iota(jnp.int32, (1, TQ), 1)
    mask_c = (NSA_CMP_STRIDE * row + NSA_CMP_LEN - 1) <= t
    kcmp = kcmp_ref[...]
    heads = [(p, g) for p in range(NSA_HPG) for g in range(NSA_GROUPS)]
    raw = []
    for p, g in heads:
        qp = q_ref[:, p * LANES:(p + 1) * LANES]
        raw.append(_dot_nt(kcmp, jnp.where((lane >> 6) == g, qp, jnp.zeros_like(qp))))
    pcs = []
    for (p, g), r in zip(heads, raw):
        lg = jnp.where(mask_c, r + bias_ref[p * NSA_GROUPS + g], NEG_INF)
        m = jnp.max(lg, axis=0, keepdims=True)
        e = jnp.where(mask_c, jnp.exp2(lg - m), 0.0)
        den = jnp.maximum(jnp.sum(e, axis=0, keepdims=True), 1e-30)
        pcs.append(e / den)
    imp = jnp.zeros((LANES, TQ), F32)
    ocs = []
    for (p, g), pc in zip(heads, pcs):
        hi, lo = _split(pc)
        ocs.append(_dot(vcmpt_ref[g * HEAD_DIM:(g + 1) * HEAD_DIM, :], hi))
        imp = imp + _dot(ovl_ref[g], hi) + _dot(ovl_ref[g], lo)
    for p in range(NSA_HPG):
        oc_ref[:, p * LANES:(p + 1) * LANES] = jnp.concatenate([ocs[2 * p], ocs[2 * p + 1]], axis=0).T.astype(BF16)

    n_slc = 32
    ids = row[0:n_slc]
    cur = t >> 6
    forced = (ids == 0) | (ids == cur) | (ids == cur - 1)
    masks = []
    for g in range(NSA_GROUPS):
        score = jnp.where(forced, NSA_FORCED, jnp.where(ids <= cur, imp[g * n_slc:(g + 1) * n_slc], NEG_INF))
        keep = _rank_keep(score, ids, NSA_SLC_TOP) & (score > 0.5 * NEG_INF)
        masks.append(jnp.where(keep, 0.0, NEG_INF))
    masks.append(jnp.full((LANES - NSA_GROUPS * n_slc, TQ), NEG_INF, F32))
    sel_ref[...] = jnp.concatenate(masks, axis=0)


def _nsa_sel(q, kcmp, vcmpt, bias_c, ovl):
    b, s, w = q.shape
    return pl.pallas_call(
        _nsa_sel_kernel,
        grid=(b, s // TQ),
        in_specs=[pl.BlockSpec((None, TQ, w), lambda bb, i: (bb, i, 0)),
                  pl.BlockSpec((None, LANES, LANES), lambda bb, i: (bb, 0, 0)),
                  pl.BlockSpec((None, LANES, LANES), lambda bb, i: (bb, 0, 0)),
                  pl.BlockSpec((NSA_HEADS, LANES, TQ), lambda bb, i: (0, 0, i)),
                  _resident(ovl.shape)],
        out_specs=[pl.BlockSpec((None, TQ, w), lambda bb, i: (bb, i, 0)),
                   pl.BlockSpec((None, LANES, TQ), lambda bb, i: (bb, 0, i))],
        out_shape=[jax.ShapeDtypeStruct((b, s, w), BF16), jax.ShapeDtypeStruct((b, LANES, s), F32)],
        compiler_params=_params("parallel", "parallel"),
        name="nsa_sel",
    )(q, kcmp, vcmpt, bias_c, ovl)


def _moba_gate_kernel(q_ref, k_ref, avg_ref, sel_ref, kh_ref, kl_ref):
    qi = pl.program_id(1)
    nb = 8

    @pl.when(qi == 0)
    def _():
        kmean = _dot(avg_ref[...], k_ref[...])
        r2 = lax.broadcasted_iota(jnp.int32, kmean.shape, 0)
        c2 = lax.broadcasted_iota(jnp.int32, kmean.shape, 1)
        kh_ref[...], kl_ref[...] = _split(jnp.where((r2 >> 3) == (c2 >> 6), kmean, 0.0))

    q = q_ref[...]
    gate = _dot_nt(kh_ref[...], q) + _dot_nt(kl_ref[...], q)
    ids = lax.broadcasted_iota(jnp.int32, (nb, 1), 0)
    past = ids < qi
    masks = []
    for h in range(MOBA_HEADS):
        score = jnp.where(past, gate[h * nb:(h + 1) * nb], NEG_INF)
        keep = (_rank_keep(score, ids, MOBA_TOP) & past) | (ids == qi)
        masks.append(jnp.where(keep, 0.0, NEG_INF))
    masks.append(jnp.full((LANES - MOBA_HEADS * nb, TQ), NEG_INF, F32))
    sel_ref[...] = jnp.concatenate(masks, axis=0)


def _moba_gate(p_arr, avg, qc, kc):
    b, s, _ = p_arr.shape
    w = MOBA_HEADS * HEAD_DIM
    return pl.pallas_call(
        _moba_gate_kernel,
        grid=(b, s // TQ),
        in_specs=[pl.BlockSpec((None, TQ, w), lambda bb, i: (bb, i, qc)),
                  pl.BlockSpec((None, s, w), lambda bb, i: (bb, 0, kc)),
                  _resident(avg.shape)],
        out_specs=pl.BlockSpec((None, LANES, TQ), lambda bb, i: (bb, 0, i)),
        out_shape=jax.ShapeDtypeStruct((b, LANES, s), F32),
        scratch_shapes=[pltpu.VMEM((LANES, w), BF16)] * 2,
        compiler_params=_params("parallel", "arbitrary"),
        name="moba_gate",
    )(p_arr, p_arr, avg)


def _even_out_kernel(x_ref, oa_ref, oc_ref, os_ref, ow_ref, gl_ref, eg_ref, wa_ref, wb_ref, gt_ref, *ffn_refs):
    sg = _sigmoid(gl_ref[...])
    hi, lo = _split(sg)
    nsa = None
    for br, src in enumerate((oc_ref, os_ref, ow_ref)):
        gexp = _dot(hi, eg_ref[br]) + _dot(lo, eg_ref[br])
        term = gexp * src[...]
        nsa = term if nsa is None else nsa + term
    m = _dot(oa_ref[...], wa_ref[...]) + _dot(nsa.astype(BF16), wb_ref[...])
    _residual_then_ffn(x_ref[...] + gt_ref[...] * m, ffn_refs)


def _even_out(x, o_a, o_c, o_s, o_w, mla_in, eg, wa, wb, gate, ffn_args, s):
    n, d = x.shape
    tpb = s // TM
    row = lambda wd: pl.BlockSpec((TM, wd), lambda i: (i, 0))
    return pl.pallas_call(
        _even_out_kernel,
        grid=(n // TM,),
        in_specs=[row(d), row(o_a.shape[1]), row(o_c.shape[1]), row(o_s.shape[1]), row(o_w.shape[1]),
                  pl.BlockSpec((TM, LANES), lambda i: (i, 3)),
                  _resident(eg.shape), _resident(wa.shape), _resident(wb.shape), _mod_spec(tpb, d)]
        + _ffn_in_specs(ffn_args, tpb, d),
        out_specs=row(d),
        out_shape=jax.ShapeDtypeStruct((n, d), F32),
        scratch_shapes=_ffn_scratch(d) + [pltpu.VMEM((TM, d), F32)],
        compiler_params=_params("parallel"),
        name="even_out",
    )(x, o_a, o_c, o_s, o_w, mla_in, eg, wa, wb, gate, *ffn_args)


def _odd_out_kernel(x_ref, od0_ref, od1_ref, od2_ref, ls0_ref, ls1_ref, ls2_ref, om_ref, wd_ref, wm_ref, gt_ref, *ffn_refs):
    ods = (od0_ref, od1_ref, od2_ref)
    ls = [r[...] for r in (ls0_ref, ls1_ref, ls2_ref)]
    mx = jnp.maximum(jnp.maximum(ls[0], ls[1]), ls[2])
    es = [jnp.exp(l - mx) for l in ls]
    den = es[0] + es[1] + es[2]
    merged = None
    for g in range(len(DIL_PAIRS)):
        term = (es[g] / den) * ods[g][...]
        merged = term if merged is None else merged + term
    m = _dot(merged.astype(BF16), wd_ref[...]) + _dot(om_ref[...], wm_ref[...])
    _residual_then_ffn(x_ref[...] + gt_ref[...] * m, ffn_refs)


def _odd_out(x, o_ds, lse_ds, o_m, wd, wm, gate, ffn_args, s):
    n, d = x.shape
    tpb = s // TM
    row = lambda wd_: pl.BlockSpec((TM, wd_), lambda i: (i, 0))
    return pl.pallas_call(
        _odd_out_kernel,
        grid=(n // TM,),
        in_specs=[row(d)] + [row(a.shape[1]) for a in (*o_ds, *lse_ds)] + [row(o_m.shape[1]),
                  _resident(wd.shape), _resident(wm.shape), _mod_spec(tpb, d)] + _ffn_in_specs(ffn_args, tpb, d),
        out_specs=row(d),
        out_shape=jax.ShapeDtypeStruct((n, d), F32),
        scratch_shapes=_ffn_scratch(d) + [pltpu.VMEM((TM, d), F32)],
        compiler_params=_params("parallel"),
        name="odd_out",
    )(x, *o_ds, *lse_ds, o_m, wd, wm, gate, *ffn_args)


def _t5_bucket(dist):
    n = jnp.maximum(jnp.asarray(dist, jnp.int32), 0)
    nf = jnp.maximum(n, 1).astype(F32)
    large = T5_MAX_EXACT + (jnp.log(nf / T5_MAX_EXACT) / math.log(T5_MAX_DIST / T5_MAX_EXACT)
                            * (NUM_BUCKETS - T5_MAX_EXACT)).astype(jnp.int32)
    return jnp.where(n < T5_MAX_EXACT, n, jnp.minimum(large, NUM_BUCKETS - 1))


TOEP_PERIOD = 4 * TQ


def _toeplitz_dist():
    j = np.arange(TOEP_PERIOD)
    return np.where(j < 3 * TQ, j, j - TOEP_PERIOD)


def _toeplitz_kernel(n_tab, u_ref, o_ref):
    x = jnp.broadcast_to(u_ref[...], (TK, TOEP_PERIOD))
    y = pltpu.roll(x, 0, 1, stride=1, stride_axis=0)
    for dlt in range(n_tab):
        o_ref[dlt] = y[:, dlt * TQ:(dlt + 1) * TQ]
    o_ref[n_tab] = jnp.full((TK, TQ), NEG_INF * LOG2E, F32)


def _toeplitz_tiles(u, n_tab):
    h = u.shape[0]
    return pl.pallas_call(
        functools.partial(_toeplitz_kernel, n_tab),
        grid=(h,),
        in_specs=[pl.BlockSpec((None, 1, TOEP_PERIOD), lambda i: (i, 0, 0))],
        out_specs=pl.BlockSpec((None, n_tab + 1, TK, TQ), lambda i: (i, 0, 0, 0)),
        out_shape=jax.ShapeDtypeStruct((h, n_tab + 1, TK, TQ), F32),
        compiler_params=_params("parallel"),
        name="toeplitz",
    )(u.reshape(h, 1, TOEP_PERIOD))


def _cmp_bias_kernel(u_ref, o_ref):
    x = jnp.broadcast_to(u_ref[...], o_ref.shape)
    o_ref[...] = pltpu.roll(x, 0, 1, stride=NSA_CMP_STRIDE, stride_axis=0)


def _cmp_bias(u, n_rows):
    h, s = u.shape
    return pl.pallas_call(
        _cmp_bias_kernel,
        grid=(h,),
        in_specs=[pl.BlockSpec((None, 1, s), lambda i: (i, 0, 0))],
        out_specs=pl.BlockSpec((None, n_rows, s), lambda i: (i, 0, 0)),
        out_shape=jax.ShapeDtypeStruct((h, n_rows, s), F32),
        compiler_params=_params("parallel"),
        name="cmp_bias",
    )(u.reshape(h, 1, s))


def _bias_tiles(t5_cols, ok, n_tab=3, dist_scale=1):
    dist = _toeplitz_dist()
    bias = jnp.transpose(t5_cols[_t5_bucket(dist * dist_scale)])
    u = jnp.where(jnp.asarray(ok)[None], bias, NEG_INF) * LOG2E
    return _toeplitz_tiles(u, n_tab)


def _pair(tiles):
    h = tiles.shape[0]
    return tiles.reshape(h // 2, 2, *tiles.shape[1:])


def _group_mean_np(sizes, width=LANES):
    gm = np.zeros((width, width), np.float32)
    o = 0
    for sz in sizes:
        gm[o:o + sz, o:o + sz] = 1.0 / sz
        o += sz
    return gm


def _group_mean_matrix(sizes, width=LANES):
    return jnp.asarray(_group_mean_np(sizes, width), BF16)


EV_META = ((0, 0, False), (0, 128, False), (0, 256, False), (0, 384, False),
           (1, 0, True), (1, 128, True), (1, 256, True), (1, 384, True),
           (2, 0, False), (3, 0, False), (4, 0, True), (5, 0, False), (6, 0, True), (7, 0, False))
EV_OUTS = [(512, F32, "rows"), (512, BF16, "rows"), (128, F32, "rows"), (128, F32, "rows"), (128, BF16, "rows"),
           (128, BF16, "cols"), (128, BF16, "rows"), (128, BF16, "cols")]
OD_DIL_STRIDES = (1, 4, 8)
OD_META = (((0, 0, True), (0, LANES, True)) + tuple((3 + g, h * LANES, True) for g in range(2) for h in range(2))
           + ((0, 2 * LANES, True), (0, 3 * LANES, True))
           + tuple((3 + g, (2 + h) * LANES, True) for g in range(2) for h in range(2))
           + ((1, 0, False), (1, LANES, False)) + tuple((0, (4 + c) * LANES, False) for c in range(4))
           + tuple((0, (8 + c) * LANES, True) for c in range(4)) + ((2, 0, False), (2, LANES, False)))
OD_OUTS = [(12 * LANES, BF16, "rows"), (2 * LANES, BF16, "cols"), (2 * LANES, BF16, "cols"),
           (4 * LANES, BF16, OD_DIL_STRIDES[1]), (4 * LANES, BF16, OD_DIL_STRIDES[2])]


def _even_w_in(w):
    jn, d, _ = w.shape
    z = lambda n_: jnp.zeros((jn, d, n_), w.dtype)
    nq = w[:, :, 416:928].reshape(jn, d, NSA_GROUPS, NSA_HPG, HEAD_DIM)
    nq = jnp.transpose(nq, (0, 1, 3, 2, 4)).reshape(jn, d, NSA_HEADS * HEAD_DIM)
    chunk3 = jnp.concatenate([w[:, :, 1696:1720], z(HEAD_DIM - 24), w[:, :, 384:416], z(LANES - 96)], axis=-1)
    return jnp.concatenate([w[:, :, 0:384], chunk3, nq, w[:, :, 928:1696]], axis=-1)


def kernel(x, c, t5_bias, ada_w, ada_b, norm_g, ffn_w_in, ffn_w_out, ev_w_in, ev_w_out, mla_q_norm_g,
           mla_kv_norm_g, mla_w_uq, mla_w_ukv, mla_qk_g, nsa_cmp_pe, nsa_cmp_w1, nsa_cmp_w2, nsa_qk_g,
           od_w_in, od_w_out, dil_qk_g, moba_qk_g):
    b, s, d = x.shape
    assert (s, d) == (2048, D_MODEL) and s % TM == 0 and TQ == MOBA_BLOCK and TQ == TK
    n = b * s
    hd = HEAD_DIM
    n_even = ev_w_in.shape[0]
    n_odd = od_w_in.shape[0]
    c64 = hd ** -0.5 * LOG2E
    c96 = (MLA_NOPE + MLA_ROPE) ** -0.5 * LOG2E

    mod = _ada(c, ada_w, ada_b).reshape(DEPTH, b, 3, 3, 1, d)

    dist = _toeplitz_dist()
    causal = dist >= 0
    gm64 = _group_mean_matrix((hd, hd))
    gm_proj = _group_mean_matrix((hd,) * (PROJ_CHUNK // hd), PROJ_CHUNK)
    per_chunk = PROJ_CHUNK // LANES
    chunked = lambda m: tuple(tuple(m[i:i + per_chunk]) for i in range(0, len(m), per_chunk))

    padc = FF_PAD - D_FF
    wa_all = jnp.pad(ffn_w_in[..., :D_FF], ((0, 0), (0, 0), (0, 0), (0, padc))).astype(BF16)
    wb_all = jnp.pad(ffn_w_in[..., D_FF:], ((0, 0), (0, 0), (0, 0), (0, padc))).astype(BF16)
    wo_all = jnp.pad(ffn_w_out, ((0, 0), (0, 0), (0, padc), (0, 0))).astype(BF16)

    nsa_tab = t5_bias[:, MLA_HEADS:MLA_HEADS + NSA_HEADS].reshape(NUM_BUCKETS, NSA_GROUPS, NSA_HPG)
    nsa_cols = jnp.transpose(nsa_tab, (0, 2, 1)).reshape(NUM_BUCKETS, NSA_HEADS)
    tab_sel = _pair(_bias_tiles(nsa_cols, causal))
    tab_win = _pair(_bias_tiles(nsa_cols, causal & (dist <= NSA_WINDOW - 1)))
    tab_mla = _toeplitz_tiles(jnp.where(jnp.asarray(causal), 0.0, NEG_INF).astype(F32)[None], 2)
    tab_mla = jnp.broadcast_to(tab_mla[None], (1, 2) + tab_mla.shape[1:])
    n_cmp_pad = s // NSA_CMP_STRIDE
    bias_c = _cmp_bias(jnp.transpose(nsa_cols[_t5_bucket(np.arange(s) - (NSA_CMP_LEN - 1))]) * LOG2E, n_cmp_pad)
    n_cmp = (s - NSA_CMP_LEN) // NSA_CMP_STRIDE + 1
    cstart = np.arange(n_cmp) * NSA_CMP_STRIDE
    sstart = np.arange(s // NSA_SLC_BLOCK) * NSA_SLC_BLOCK
    overlap = np.clip(np.minimum(cstart[:, None] + NSA_CMP_LEN, sstart[None, :] + NSA_SLC_BLOCK)
                      - np.maximum(cstart[:, None], sstart[None, :]), 0, None).astype(np.float32) / NSA_CMP_LEN
    ovl = np.zeros((NSA_GROUPS, LANES, LANES), np.float32)
    for g in range(NSA_GROUPS):
        ovl[g, 32 * g:32 * g + 32, :n_cmp] = overlap.T
    ovl = jnp.asarray(ovl, BF16)
    eg = np.zeros((3, LANES, NSA_HEADS * hd), np.float32)
    for g in range(NSA_GROUPS):
        for p in range(NSA_HPG):
            for br in range(3):
                eg[br, (g * NSA_HPG + p) * 3 + br, p * LANES + g * hd:p * LANES + (g + 1) * hd] = 1.0
    eg = jnp.asarray(eg, BF16)
    gm_mla = jnp.asarray(np.kron(np.eye(2, dtype=np.float32), _group_mean_np((MLA_NOPE, MLA_ROPE))), BF16)
    inv = ROPE_THETA ** (-jnp.arange(0, MLA_ROPE, 2, dtype=F32) / MLA_ROPE)
    ang = jnp.arange(s, dtype=F32)[:, None] * inv[None, :]
    ones = jnp.ones((s, MLA_NOPE), F32)
    tail = LANES - MLA_NOPE - MLA_ROPE
    cos_t = jnp.tile(jnp.concatenate([ones, jnp.cos(ang), jnp.cos(ang), jnp.ones((s, tail), F32)], axis=1), (1, 2))
    sin_t = jnp.tile(jnp.concatenate([0 * ones, jnp.sin(ang), jnp.sin(ang), jnp.zeros((s, tail), F32)], axis=1), (1, 2))

    ev_w = _even_w_in(ev_w_in).astype(BF16)
    ev_gain = jnp.ones((n_even, 14 * LANES), F32)
    ev_gain = ev_gain.at[:, 512:1024].set(jnp.tile(nsa_qk_g[:, 0], (1, 8)) * c64)
    ev_gain = ev_gain.at[:, 1280:1408].set(jnp.tile(nsa_qk_g[:, 1], (1, 2)))
    ev_gain = ev_gain.at[:, 1536:1664].set(jnp.tile(nsa_qk_g[:, 1], (1, 2)))
    gain_kc = jnp.tile(nsa_qk_g[:, 1], (1, 2))
    wuq = jnp.pad(mla_w_uq.reshape(n_even, MLA_Q_LORA, MLA_HEADS, MLA_NOPE + MLA_ROPE),
                  ((0, 0), (0, 0), (0, 0), (0, tail))).reshape(n_even, MLA_Q_LORA, MLA_HEADS * LANES).astype(BF16)
    ukv = mla_w_ukv.reshape(n_even, MLA_KV_LORA, MLA_HEADS, MLA_NOPE + MLA_V)
    wuk = jnp.pad(ukv[..., :MLA_NOPE], ((0, 0), (0, 0), (0, 0), (0, LANES - MLA_NOPE))
                  ).reshape(n_even, MLA_KV_LORA, MLA_HEADS * LANES).astype(BF16)
    wuv = jnp.swapaxes(ukv[..., MLA_NOPE:].reshape(n_even, MLA_KV_LORA, MLA_HEADS * MLA_V), 1, 2).astype(BF16)
    zt = jnp.zeros((n_even, tail), F32)
    gq = jnp.tile(jnp.concatenate([mla_qk_g[:, 0] * c96, zt], axis=1), (1, 2))
    gkn = jnp.tile(jnp.concatenate([mla_qk_g[:, 1, :MLA_NOPE], jnp.zeros((n_even, LANES - MLA_NOPE), F32)], axis=1), (1, 2))
    gkr = jnp.tile(jnp.concatenate([jnp.zeros((n_even, MLA_NOPE), F32), mla_qk_g[:, 1, MLA_NOPE:], zt], axis=1), (1, 2))
    pe2 = jnp.broadcast_to(nsa_cmp_pe.reshape(n_even, 2, 2, 16, 1, hd), (n_even, 2, 2, 16, NSA_GROUPS, hd)
                           ).reshape(n_even, 2, 2, 1, 16 * LANES)
    eye = jnp.eye(NSA_GROUPS, dtype=F32)
    w1 = nsa_cmp_w1.reshape(n_even, 2, 2, 16, hd, NSA_CMP_HID)
    w1x = jnp.einsum('ijaldc,gh->ijalgdhc', w1, eye).reshape(n_even, 2, 2, 16 * LANES, NSA_GROUPS * NSA_CMP_HID).astype(BF16)
    w2x = jnp.einsum('ijcd,gh->ijhdgc', nsa_cmp_w2, eye).reshape(n_even, 2, LANES, NSA_GROUPS * NSA_CMP_HID).astype(BF16)
    wa_o = ev_w_out[:, :MLA_HEADS * MLA_V].astype(BF16)
    wb_o = jnp.transpose(ev_w_out[:, MLA_HEADS * MLA_V:].reshape(n_even, NSA_GROUPS, NSA_HPG, hd, d),
                         (0, 2, 1, 3, 4)).reshape(n_even, NSA_HEADS * hd, d).astype(BF16)

    assert DIL_PAIRS == ((128, 1), (512, 4), (2048, 16))
    dil_cfg = tuple(zip(OD_DIL_STRIDES, (1, 1, 0)))
    dil_ok = (causal & (dist <= 128), causal & (dist <= 128), causal & (dist % 2 == 0))
    tab_dil = [_pair(_bias_tiles(t5_bias[:, gi * DIL_HPG:(gi + 1) * DIL_HPG], dil_ok[gi], dist_scale=dil_cfg[gi][0]))
               for gi in range(len(DIL_PAIRS))]
    tab_moba = _pair(_bias_tiles(t5_bias[:, DIL_SLOTS:DIL_SLOTS + MOBA_HEADS], causal))
    avg = np.zeros((LANES, s), np.float32)
    for h in range(MOBA_HEADS):
        for m in range(s // MOBA_BLOCK):
            avg[8 * h + m, m * MOBA_BLOCK:(m + 1) * MOBA_BLOCK] = 1.0 / MOBA_BLOCK
    avg = jnp.asarray(avg, BF16)
    od_w = od_w_in.astype(BF16)
    od_gain = jnp.concatenate([jnp.tile(dil_qk_g[:, 0], (1, 12)) * c64, jnp.tile(dil_qk_g[:, 1], (1, 12)),
                               jnp.ones((n_odd, 768), F32), jnp.tile(moba_qk_g[:, 0], (1, 4)) * c64,
                               jnp.tile(moba_qk_g[:, 1], (1, 4)), jnp.ones((n_odd, 256), F32)], axis=1)
    wd_o = od_w_out[:, :DIL_HPG * hd].astype(BF16)
    wm_o = od_w_out[:, DIL_HPG * hd:].astype(BF16)

    sh3 = lambda a: a.reshape(b, s, a.shape[-1])
    tr3 = lambda a: jnp.swapaxes(sh3(a), 1, 2)
    xf = x.reshape(n, d)
    for i in range(DEPTH):
        j = i // 2
        g_i = norm_g[i].reshape(3, 1, d)
        ffn2 = (g_i[2], mod[i, :, 2, 0], mod[i, :, 2, 1], mod[i, :, 2, 2], wa_all[i, 1], wb_all[i, 1], wo_all[i, 1])
        xf = _ffn(xf, g_i[0], mod[i, :, 0, 0], mod[i, :, 0, 1], mod[i, :, 0, 2],
                  wa_all[i, 0], wb_all[i, 0], wo_all[i, 0], s)
        if i % 2 == 0:
            mla_in, nsa_q, kc, vc, ks, vs, kw, vw, qf, kf, vf = _proj(
                xf, g_i[1], mod[i, :, 1, 0], mod[i, :, 1, 1], ev_w[j], ev_gain[j][None], gm_proj,
                chunked(EV_META), EV_OUTS, s,
                mla=(mla_q_norm_g[j][None], mla_kv_norm_g[j][None], wuq[j], wuk[j], wuv[j], gm_mla,
                     gq[j][None], gkn[j][None], gkr[j][None], cos_t, sin_t))
            (o_a,) = _attn(sh3(qf), sh3(kf), vf, tab_mla, None, qc0=0, kc0=0, vb0=0, n_blk=MLA_HEADS // 2, qw=2,
                           tab_shared=True, out_dtype=BF16)
            kcmp, vcmpt = _nsa_cmp(sh3(kc), sh3(vc), pe2[j], w1x[j, :, 0], w1x[j, :, 1], w2x[j], gm64, gain_kc[j][None])
            o_c, sel = _nsa_sel(sh3(nsa_q), kcmp, vcmpt, bias_c, ovl)
            (o_s,) = _attn(sh3(nsa_q), sh3(ks), vs, tab_sel, sel, qc0=0, kc0=0, vb0=0, n_blk=NSA_HPG, qw=1,
                           kv_shared=True, sel_cfg=(TK // NSA_SLC_BLOCK, 32, 0), out_dtype=BF16)
            (o_w,) = _attn(sh3(nsa_q), sh3(kw), vw, tab_win, None, qc0=0, kc0=0, vb0=0, n_blk=NSA_HPG, qw=1,
                           kv_shared=True, backs=(2, 2, 2, 2), out_dtype=BF16)
            xf = _even_out(xf, o_a.reshape(n, -1), o_c.reshape(n, -1), o_s.reshape(n, -1), o_w.reshape(n, -1),
                           mla_in, eg, wa_o[j], wb_o[j], mod[i, :, 1, 2], ffn2, s)
        else:
            pr, vd0t, vmt, qk1, qk2 = _proj(xf, g_i[1], mod[i, :, 1, 0], mod[i, :, 1, 1], od_w[j], od_gain[j][None],
                                            gm_proj, chunked(OD_META), OD_OUTS, s)
            pr3 = pr.reshape(b, s, pr.shape[-1])
            o_ds, lse_ds = [], []
            for gi, (r, bk) in enumerate(dil_cfg):
                if r == 1:
                    qk, vdt = pr3, vd0t
                else:
                    v = pr3[:, :, (2 + 2 * gi) * LANES:(4 + 2 * gi) * LANES].reshape(b, s // r, r, 2 * LANES)
                    qk, vdt = (qk1, qk2)[gi - 1], jnp.transpose(v, (0, 3, 2, 1)).reshape(b, 2 * LANES, s)
                o_g, lse_g = _attn(qk, qk, vdt, tab_dil[gi], None, qc0=0, kc0=2, vb0=0, n_blk=2,
                                   qw=1, backs=(bk,), want_lse=True, stride=r)
                o_ds.append(o_g.reshape(n, -1))
                lse_ds.append(lse_g.reshape(n, -1))
            selm = _moba_gate(pr3, avg, 4, 5)
            (o_m,) = _attn(pr3, pr3, vmt, tab_moba, selm, qc0=8, kc0=10, vb0=0, n_blk=2, qw=1,
                           sel_cfg=(1, 8, 2), out_dtype=BF16)
            xf = _odd_out(xf, o_ds, lse_ds, o_m.reshape(n, -1), wd_o[j], wm_o[j], mod[i, :, 1, 2], ffn2, s)
    return xf.reshape(b, s, d)
```
